```python
import math
import jax, jax.numpy as jnp
from jax import lax
import numpy as np

D_MODEL = 1024
BATCH = 4
SEQ = 4096
DEPTH = 2

GRID_W = 64
CTX_LEN = 256
HEAD_DIM = 64
D_HYENA = 256
D_FNET = 256
D_NA = 512
N_NA_HEADS = D_NA // HEAD_DIM
FNET_GROUPS = 4
FNET_GROUP_DIM = D_FNET // FNET_GROUPS
HYENA_ORDER = 2
POS_EMB_DIM = 33
POS_BANDS = (POS_EMB_DIM - 1) // 2
FILTER_HIDDEN = 64
HYENA_DECAY_TARGET = 1e-2
HYENA_FAST_PCT = 0.3
HYENA_SLOW_PCT = 1.5
NA_KH = 8
NA_KW = 16
D_FF = -(-8 * D_MODEL // (3 * 256)) * 256
EPS = 1e-6

OFF_HY = 0
OFF_FN = OFF_HY + 3 * D_HYENA
OFF_Q = OFF_FN + D_FNET
OFF_K = OFF_Q + D_NA
OFF_V = OFF_K + D_NA
D_PROJ = OFF_V + D_NA

kernel_name = "hybrid_hyena_fnet_natten_dit"


def rmsnorm(x, g):
    x32 = x.astype(jnp.float32)
    y = x32 * lax.rsqrt(jnp.mean(x32 * x32, axis=-1, keepdims=True) + EPS)
    return (y * g.astype(jnp.float32)).astype(x.dtype)


def modulate(x, g, shift, scale):
    return rmsnorm(x, g) * (1 + scale) + shift


def split_heads(t):
    return t.reshape(t.shape[0], t.shape[1], N_NA_HEADS, HEAD_DIM)


def short_conv(u, w, b):
    L = u.shape[1]
    up = jnp.pad(u, ((0, 0), (1, 1), (0, 0)))
    return up[:, :L] * w[0] + up[:, 1:L + 1] * w[1] + up[:, 2:L + 2] * w[2] + b


def hyena_filters(L, w1, b1, freq, w2, b2, w3):
    f32 = jnp.float32
    t = jnp.linspace(0.0, 1.0, L, dtype=f32)[:, None]
    w = 2.0 * math.pi * jnp.arange(L, dtype=f32)[:, None] / L
    f = jnp.linspace(1e-4, POS_BANDS - 1, POS_BANDS, dtype=f32)[None]
    z = jnp.concatenate([t, jnp.cos(f * w), -jnp.sin(f * w)], axis=-1)
    h = jnp.sin(freq[0].astype(f32) * (z @ w1.astype(f32) + b1.astype(f32)))
    h = jnp.sin(freq[1].astype(f32) * (h @ w2.astype(f32) + b2.astype(f32)))
    k = (h @ w3.astype(f32)).reshape(L, HYENA_ORDER, 2, D_HYENA)
    max_decay = math.log(HYENA_DECAY_TARGET) / HYENA_FAST_PCT
    min_decay = math.log(HYENA_DECAY_TARGET) / HYENA_SLOW_PCT
    deltas = jnp.linspace(min_decay, max_decay, D_HYENA, dtype=f32)
    decay = jnp.exp(-t * jnp.abs(deltas))
    k = k * decay[:, None, None, :]
    k_fwd, k_bwd = k[:, :, 0], k[:, :, 1]
    k_circ = jnp.concatenate([k_fwd, jnp.zeros_like(k_fwd[:1]), k_bwd[:0:-1]], axis=0)
    k_circ = k_circ / jnp.sum(jnp.abs(k_circ), axis=0, keepdims=True)
    return jnp.fft.rfft(k_circ, axis=0)


def long_conv(u, k_f, skip):
    L = u.shape[1]
    u32 = u.astype(jnp.float32)
    y = jnp.fft.irfft(jnp.fft.rfft(u32, n=2 * L, axis=1) * k_f[None], n=2 * L, axis=1)[:, :L]
    return (y + u32 * skip.astype(jnp.float32)).astype(u.dtype)


def hyena_mixer(u, conv_w, conv_b, w1, b1, freq, w2, b2, w3, skip):
    u = short_conv(u, conv_w, conv_b)
    x1, x2, v = jnp.split(u, 3, axis=-1)
    k_f = hyena_filters(u.shape[1], w1, b1, freq, w2, b2, w3)
    z = x1 * long_conv(v, k_f[:, 0], skip[0])
    return x2 * long_conv(z, k_f[:, 1], skip[1])


def fourier_mix(g):
    B, L, _ = g.shape
    g4 = g.astype(jnp.float32).reshape(B, L, FNET_GROUPS, FNET_GROUP_DIM)
    y = jnp.real(jnp.fft.fft2(g4, axes=(1, 3), norm="ortho"))
    return y.reshape(B, L, D_FNET).astype(g.dtype)


def context_attention(q, k, v):
    s = jnp.einsum('bqhd,bkhd->bhqk', q, k).astype(jnp.float32) / math.sqrt(HEAD_DIM)
    p = jax.nn.softmax(s, axis=-1).astype(v.dtype)
    return jnp.einsum('bhqk,bkhd->bqhd', p, v)


def neighbourhood_attention(q, k, v, k_ctx, v_ctx, rpb):
    B, L, H, Dh = q.shape
    rows = L // GRID_W
    kh = min(NA_KH, rows)
    kw = NA_KW
    scale = 1.0 / math.sqrt(Dh)
    qg = q.reshape(B, rows, GRID_W, H, Dh)
    kg = k.reshape(B, rows, GRID_W, H, Dh)
    vg = v.reshape(B, rows, GRID_W, H, Dh)
    col = jnp.arange(GRID_W)
    col_start = jnp.clip(col - kw // 2, 0, GRID_W - kw)
    col_idx = col_start[:, None] + jnp.arange(kw)[None]
    col_bias_idx = (col_idx - col[:, None] + (NA_KW - 1))[:, None, :]

    def row_block(args):
        r, q_r = args
        r_start = jnp.clip(r - kh // 2, 0, rows - kh)
        k_rows = lax.dynamic_slice_in_dim(kg, r_start, kh, axis=1)
        v_rows = lax.dynamic_slice_in_dim(vg, r_start, kh, axis=1)
        k_win = jnp.take(k_rows, col_idx, axis=2)
        v_win = jnp.take(v_rows, col_idx, axis=2)
        row_bias_idx = (r_start + jnp.arange(kh) - r + (NA_KH - 1))[None, :, None]
        bias = rpb[:, row_bias_idx, col_bias_idx].astype(jnp.float32)
        s_loc = jnp.einsum('bqhd,biqjhd->bhqij', q_r, k_win).astype(jnp.float32) * scale + bias[None]
        s_ctx = jnp.einsum('bqhd,bchd->bhqc', q_r, k_ctx).astype(jnp.float32) * scale
        s = jnp.concatenate([s_loc.reshape(B, H, GRID_W, kh * kw), s_ctx], axis=-1)
        p = jax.nn.softmax(s, axis=-1).astype(v.dtype)
        p_loc = p[..., :kh * kw].reshape(B, H, GRID_W, kh, kw)
        p_ctx = p[..., kh * kw:]
        return (jnp.einsum('bhqij,biqjhd->bqhd', p_loc, v_win)
                + jnp.einsum('bhqc,bchd->bqhd', p_ctx, v_ctx))

    out = lax.map(row_block, (jnp.arange(rows), jnp.moveaxis(qg, 1, 0)))
    return jnp.moveaxis(out, 0, 1).reshape(B, L, H, Dh)


def swiglu(h, w_gate, w_up, w_down):
    return (jax.nn.silu(h @ w_gate) * (h @ w_up)) @ w_down


def setup_inputs(seed: int = 0) -> dict:
    key = jax.random.key(seed)
    ks = jax.random.split(key, 32)
    f32 = jnp.float32

    def nrm(k, shape, s):
        return jax.random.normal(k, shape, f32) * s

    L = DEPTH
    return {
        "x": nrm(ks[0], (BATCH, SEQ, D_MODEL), 1.0),
        "c": nrm(ks[1], (BATCH, D_MODEL), 1.0),
        "ctx": nrm(ks[2], (BATCH, CTX_LEN, D_MODEL), 1.0),
        "c_ctx": nrm(ks[3], (D_MODEL,), 1.0),
        "w_mod": nrm(ks[4], (L, D_MODEL, 6 * D_MODEL), D_MODEL ** -0.5),
        "b_mod": nrm(ks[5], (L, 6 * D_MODEL), 0.01),
        "norm1_g": 1.0 + nrm(ks[6], (L, D_MODEL), 0.01),
        "norm2_g": 1.0 + nrm(ks[7], (L, D_MODEL), 0.01),
        "w_in": nrm(ks[8], (L, D_MODEL, D_PROJ), D_MODEL ** -0.5),
        "w_out": nrm(ks[9], (L, D_MODEL, D_MODEL), D_MODEL ** -0.5),
        "hy_conv_w": nrm(ks[10], (L, 3, 3 * D_HYENA), 3 ** -0.5),
        "hy_conv_b": nrm(ks[11], (L, 3 * D_HYENA), 0.01),
        "filt_w1": nrm(ks[12], (L, POS_EMB_DIM, FILTER_HIDDEN), POS_EMB_DIM ** -0.5),
        "filt_b1": nrm(ks[13], (L, FILTER_HIDDEN), 0.01),
        "filt_freq": 1.0 + nrm(ks[14], (L, 2, FILTER_HIDDEN), 0.05),
        "filt_w2": nrm(ks[15], (L, FILTER_HIDDEN, FILTER_HIDDEN), FILTER_HIDDEN ** -0.5),
        "filt_b2": nrm(ks[16], (L, FILTER_HIDDEN), 0.01),
        "filt_w3": nrm(ks[17], (L, FILTER_HIDDEN, HYENA_ORDER * 2 * D_HYENA), FILTER_HIDDEN ** -0.5),
        "hy_skip": nrm(ks[18], (L, HYENA_ORDER, D_HYENA), 1.0),
        "na_rpb": nrm(ks[19], (L, N_NA_HEADS, 2 * NA_KH - 1, 2 * NA_KW - 1), 0.1),
        "w_gate": nrm(ks[20], (L, D_MODEL, D_FF), D_MODEL ** -0.5),
        "w_up": nrm(ks[21], (L, D_MODEL, D_FF), D_MODEL ** -0.5),
        "w_down": nrm(ks[22], (L, D_FF, D_MODEL), D_FF ** -0.5),
        "final_g": 1.0 + nrm(ks[23], (D_MODEL,), 0.01),
    }


def reference(x, c, ctx, c_ctx, w_mod, b_mod, norm1_g, norm2_g, w_in, w_out,
              hy_conv_w, hy_conv_b, filt_w1, filt_b1, filt_freq, filt_w2, filt_b2,
              filt_w3, hy_skip, na_rpb, w_gate, w_up, w_down, final_g):
    silu_c = jax.nn.silu(c)
    silu_cc = jax.nn.silu(c_ctx)
    for l in range(DEPTH):
        last = l == DEPTH - 1
        hy_params = (hy_conv_w[l], hy_conv_b[l], filt_w1[l], filt_b1[l], filt_freq[l],
                     filt_w2[l], filt_b2[l], filt_w3[l], hy_skip[l])
        mod = silu_c @ w_mod[l] + b_mod[l]
        sh1, sc1, g1, sh2, sc2, g2 = [m[:, None] for m in jnp.split(mod, 6, axis=-1)]
        mod_c = silu_cc @ w_mod[l] + b_mod[l]
        csh1, csc1, cg1, csh2, csc2, cg2 = jnp.split(mod_c, 6, axis=-1)

        h = modulate(x, norm1_g[l], sh1, sc1)
        hc = modulate(ctx, norm1_g[l], csh1, csc1)
        proj = h @ w_in[l]
        if last:
            proj_c_kv = hc @ w_in[l][:, OFF_K:]
            k_c = split_heads(proj_c_kv[..., :D_NA])
            v_c = split_heads(proj_c_kv[..., D_NA:])
        else:
            proj_c = hc @ w_in[l]
            k_c = split_heads(proj_c[..., OFF_K:OFF_V])
            v_c = split_heads(proj_c[..., OFF_V:])

        y_hy = hyena_mixer(proj[..., OFF_HY:OFF_FN], *hy_params)
        y_fn = fourier_mix(proj[..., OFF_FN:OFF_Q])
        y_na = neighbourhood_attention(split_heads(proj[..., OFF_Q:OFF_K]),
                                       split_heads(proj[..., OFF_K:OFF_V]),
                                       split_heads(proj[..., OFF_V:]), k_c, v_c, na_rpb[l])
        mix = jnp.concatenate([y_hy, y_fn, y_na.reshape(x.shape[0], x.shape[1], D_NA)], axis=-1) @ w_out[l]
        x = x + g1 * mix

        x = x + g2 * swiglu(modulate(x, norm2_g[l], sh2, sc2), w_gate[l], w_up[l], w_down[l])

        if not last:
            cy_hy = hyena_mixer(proj_c[..., OFF_HY:OFF_FN], *hy_params)
            cy_fn = fourier_mix(proj_c[..., OFF_FN:OFF_Q])
            cy_na = context_attention(split_heads(proj_c[..., OFF_Q:OFF_K]), k_c, v_c)
            mix_c = jnp.concatenate([cy_hy, cy_fn, cy_na.reshape(ctx.shape[0], ctx.shape[1], D_NA)], axis=-1) @ w_out[l]
            ctx = ctx + cg1 * mix_c
            ctx = ctx + cg2 * swiglu(modulate(ctx, norm2_g[l], csh2, csc2), w_gate[l], w_up[l], w_down[l])
    return rmsnorm(x, final_g)
```

```python
import functools
import math

import numpy as np
import jax
import jax.numpy as jnp
from jax import lax
from jax.experimental import pallas as pl
from jax.experimental.pallas import tpu as pltpu

F32 = jnp.float32
BF16 = jnp.bfloat16

EPS = 1e-6
GRID_W = 64
HEAD_DIM = 64
D_HYENA = 256
D_FNET = 256
D_NA = 512
FNET_GROUP_DIM = 64
NA_KH = 8
NA_KW = 16
POS_BANDS = 16
HYENA_DECAY_TARGET = 1e-2
HYENA_FAST_PCT = 0.3
HYENA_SLOW_PCT = 1.5
MASK_VALUE = -1e30
VMEM_LIMIT = 56 * 1024 * 1024
LANES = 128


def _params(*sem):
    return pltpu.CompilerParams(dimension_semantics=sem, vmem_limit_bytes=VMEM_LIMIT)


def _bdot(a, b):
    return jnp.dot(a, b, preferred_element_type=F32)


def _split_bf16(a):
    hi = a.astype(BF16)
    lo = (a - hi.astype(F32)).astype(BF16)
    return hi, lo


def _dot3(a, b):
    a_hi, a_lo = _split_bf16(a)
    b_hi, b_lo = _split_bf16(b)
    return _bdot(a_hi, b_hi) + _bdot(a_lo, b_hi) + _bdot(a_hi, b_lo)


def _mod_body(c_ref, w_ref, b_ref, o_ref):
    c = c_ref[...]
    o_ref[0] = _dot3(c * jax.nn.sigmoid(c), w_ref[0]) + b_ref[0]


def _modulation(c_all, w_mod, b_mod):
    depth, d, n = w_mod.shape
    rows = c_all.shape[0]
    tn = 1536
    return pl.pallas_call(
        _mod_body,
        grid=(depth, n // tn),
        in_specs=[
            pl.BlockSpec((rows, d), lambda l, j: (0, 0)),
            pl.BlockSpec((1, d, tn), lambda l, j: (l, 0, j)),
            pl.BlockSpec((1, 1, tn), lambda l, j: (l, 0, j)),
        ],
        out_specs=pl.BlockSpec((1, rows, tn), lambda l, j: (l, 0, j)),
        out_shape=jax.ShapeDtypeStruct((depth, rows, n), F32),
        compiler_params=_params("parallel", "parallel"),
        name="modulation",
    )(c_all, w_mod, b_mod.reshape(depth, 1, n))


def _modulated_norm(x, g, sc, sh):
    ms = jnp.mean(x * x, axis=-1, keepdims=True)
    y = x * lax.rsqrt(ms + EPS) * g
    return y * (1.0 + sc) + sh


def _norm_proj_body(splits, x_ref, g_ref, sc_ref, sh_ref, w_ref, *o_refs):
    h = _modulated_norm(x_ref[0], g_ref[...], sc_ref[0], sh_ref[0])
    p = _bdot(h.astype(BF16), w_ref[...])
    off = 0
    for o_ref, width in zip(o_refs, splits):
        o_ref[0] = p[:, off:off + width].astype(o_ref.dtype)
        off += width


def _norm_proj(x, g, sc, sh, w, splits, dtypes, tm):
    b, t, d = x.shape
    n = w.shape[1]
    tm = min(tm, t)
    return pl.pallas_call(
        functools.partial(_norm_proj_body, tuple(splits)),
        grid=(b, t // tm),
        in_specs=[
            pl.BlockSpec((1, tm, d), lambda i, j: (i, j, 0)),
            pl.BlockSpec((1, d), lambda i, j: (0, 0)),
            pl.BlockSpec((1, 1, d), lambda i, j: (i, 0, 0)),
            pl.BlockSpec((1, 1, d), lambda i, j: (i, 0, 0)),
            pl.BlockSpec((d, n), lambda i, j: (0, 0)),
        ],
        out_specs=[pl.BlockSpec((1, tm, wd), lambda i, j: (i, j, 0)) for wd in splits],
        out_shape=[jax.ShapeDtypeStruct((b, t, wd), dt) for wd, dt in zip(splits, dtypes)],
        compiler_params=_params("parallel", "parallel"),
        name="norm_proj",
    )(x, g, sc, sh, w)


def _proj_out_body(x_ref, gate_ref, hy_ref, fn_ref, na_ref, w_ref, o_ref):
    mix = (_bdot(hy_ref[0], w_ref[:D_HYENA])
           + _bdot(fn_ref[0], w_ref[D_HYENA:D_HYENA + D_FNET])
           + _bdot(na_ref[0], w_ref[D_HYENA + D_FNET:]))
    o_ref[0] = x_ref[0] + gate_ref[0] * mix


def _proj_out(x, gate, y_hy, y_fn, y_na, w, tm):
    b, t, d = x.shape
    tm = min(tm, t)
    tok = lambda wd: pl.BlockSpec((1, tm, wd), lambda i, j: (i, j, 0))
    return pl.pallas_call(
        _proj_out_body,
        grid=(b, t // tm),
        in_specs=[tok(d), pl.BlockSpec((1, 1, d), lambda i, j: (i, 0, 0)),
                  tok(D_HYENA), tok(D_FNET), tok(D_NA),
                  pl.BlockSpec(w.shape, lambda i, j: (0, 0))],
        out_specs=tok(d),
        out_shape=jax.ShapeDtypeStruct(x.shape, F32),
        compiler_params=_params("parallel", "parallel"),
        name="proj_out",
    )(x, gate, y_hy, y_fn, y_na, w)


def _ffn_body(final_norm, x_ref, g_ref, sc_ref, sh_ref, gate_ref, wg_ref, wu_ref, wd_ref,
              fg_ref, o_ref, h_scr, acc_scr):
    f = pl.program_id(2)

    @pl.when(f == 0)
    def _():
        h = _modulated_norm(x_ref[0], g_ref[...], sc_ref[0], sh_ref[0])
        h_scr[...] = h.astype(BF16)

    h = h_scr[...]
    a = _bdot(h, wg_ref[...])
    u = _bdot(h, wu_ref[...])
    act = (a * jax.nn.sigmoid(a) * u).astype(BF16)
    part = _bdot(act, wd_ref[...])

    @pl.when(f == 0)
    def _():
        acc_scr[...] = part

    @pl.when(f != 0)
    def _():
        acc_scr[...] += part

    @pl.when(f == pl.num_programs(2) - 1)
    def _():
        y = x_ref[0] + gate_ref[0] * acc_scr[...]
        if final_norm:
            ms = jnp.mean(y * y, axis=-1, keepdims=True)
            y = y * lax.rsqrt(ms + EPS) * fg_ref[...]
        o_ref[0] = y


def _ffn(x, g, sc, sh, gate, wg, wu, wd, final_g, final_norm, tm, tf):
    b, t, d = x.shape
    dff = wg.shape[1]
    tm = min(tm, t)
    vec = pl.BlockSpec((1, 1, d), lambda i, j, k: (i, 0, 0))
    return pl.pallas_call(
        functools.partial(_ffn_body, final_norm),
        grid=(b, t // tm, dff // tf),
        in_specs=[
            pl.BlockSpec((1, tm, d), lambda i, j, k: (i, j, 0)),
            pl.BlockSpec((1, d), lambda i, j, k: (0, 0)),
            vec, vec, vec,
            pl.BlockSpec((d, tf), lambda i, j, k: (0, k)),
            pl.BlockSpec((d, tf), lambda i, j, k: (0, k)),
            pl.BlockSpec((tf, d), lambda i, j, k: (k, 0)),
            pl.BlockSpec((1, d), lambda i, j, k: (0, 0)),
        ],
        out_specs=pl.BlockSpec((1, tm, d), lambda i, j, k: (i, j, 0)),
        out_shape=jax.ShapeDtypeStruct(x.shape, F32),
        scratch_shapes=[pltpu.VMEM((tm, d), BF16), pltpu.VMEM((tm, d), F32)],
        compiler_params=_params("parallel", "parallel", "arbitrary"),
        name="ffn",
    )(x, g, sc, sh, gate, wg, wu, wd, final_g)


def _split_seq(n_total):
    log = int(round(math.log2(n_total)))
    assert 2 ** log == n_total
    n1 = 2 ** (log // 2)
    return n1, n_total // n1


@functools.lru_cache(maxsize=None)
def _hyena_consts(seq):
    n = 2 * seq
    n1, n2 = _split_seq(n)
    h1 = n1 // 2
    f1 = np.arange(h1)[:, None]
    t1 = np.arange(h1)[None, :]
    th = np.pi * (2 * f1 + 1) * t1 / n1
    m1 = np.empty((n1, h1))
    m1[0::2] = np.cos(th)
    m1[1::2] = -np.sin(th)
    m3 = m1.T * (2.0 / n)
    f = (np.arange(h1)[:, None, None] + n1 * np.arange(n2)[None, :, None])
    t2 = np.arange(n2)[None, None, :]
    ph = np.pi * ((2 * f + 1) * t2 % (2 * n)) / n
    c, s = np.cos(ph), np.sin(ph)
    m2 = np.concatenate([np.concatenate([c, s], axis=2), np.concatenate([-s, c], axis=2)], axis=1)
    m2t = np.transpose(m2, (0, 2, 1))
    return n1, n2, m1, m2, m2t, m3


@functools.lru_cache(maxsize=None)
def _fnet_consts(seq):
    n1, n2 = _split_seq(seq)
    f1 = np.arange(n1)[:, None]
    t1 = np.arange(n1)[None, :]
    th = 2 * np.pi * (f1 * t1 % n1) / n1
    m1 = np.empty((2 * n1, n1))
    m1[0::2] = np.cos(th)
    m1[1::2] = -np.sin(th)
    f = (np.arange(n1)[:, None, None] + n1 * np.arange(n2)[None, :, None])
    t2 = np.arange(n2)[None, None, :]
    ph = 2 * np.pi * (f * t2 % seq) / seq
    c, s = np.cos(ph), np.sin(ph)
    m2 = np.concatenate([np.concatenate([c, s], axis=2), np.concatenate([-s, c], axis=2)], axis=1)
    gd = FNET_GROUP_DIM
    cc = np.arange(gd)
    pg = 2 * np.pi * (np.outer(cc, cc) % gd) / gd
    scale = 1.0 / math.sqrt(gd * seq)
    eye = np.eye(D_FNET // gd)
    bdc = np.kron(eye, np.cos(pg)) * scale
    bds = np.kron(eye, np.sin(pg)) * scale
    return n1, n2, m1, m2, bdc, bds


def _mxu_const(a):
    return jnp.asarray(a, dtype=F32).astype(BF16)


def _lane_tile(width):
    return min(width, 8192)


def _stage1_body(m_ref, x_ref, o_ref):
    o_ref[0] = _bdot(m_ref[...], x_ref[0, 0].astype(BF16)).astype(o_ref.dtype)


def _stage1(m, x, sel):
    g, _, r, w = x.shape
    r2 = m.shape[0]
    tl = _lane_tile(w)
    return pl.pallas_call(
        _stage1_body,
        grid=(g, w // tl),
        in_specs=[pl.BlockSpec((r2, r), lambda i, j: (0, 0)),
                  pl.BlockSpec((1, 1, r, tl), lambda i, j: (i, sel, 0, j))],
        out_specs=pl.BlockSpec((1, r2, tl), lambda i, j: (i, 0, j)),
        out_shape=jax.ShapeDtypeStruct((g, r2, w), BF16),
        compiler_params=_params("parallel", "parallel"),
        name="dft_stage1",
    )(m, x)


def _short_conv_body(u_ref, w_ref, b_ref, o_ref):
    u = u_ref[0]
    seq = u.shape[0]
    row = lax.broadcasted_iota(jnp.int32, u.shape, 0)
    prev = jnp.where(row == 0, 0.0, pltpu.roll(u, 1, axis=0))
    nxt = jnp.where(row == seq - 1, 0.0, pltpu.roll(u, seq - 1, axis=0))
    o_ref[0, 0] = prev * w_ref[0:1] + u * w_ref[1:2] + nxt * w_ref[2:3] + b_ref[...]


def _short_conv(u, w, bias):
    b, seq, c3 = u.shape
    c = D_HYENA
    return pl.pallas_call(
        _short_conv_body,
        grid=(b, c3 // c),
        in_specs=[pl.BlockSpec((1, seq, c), lambda i, j: (i, 0, j)),
                  pl.BlockSpec((3, c), lambda i, j: (0, j)),
                  pl.BlockSpec((1, c), lambda i, j: (0, j))],
        out_specs=pl.BlockSpec((1, 1, seq, c), lambda i, j: (i, j, 0, 0)),
        out_shape=jax.ShapeDtypeStruct((b, c3 // c, seq, c), F32),
        compiler_params=_params("parallel", "parallel"),
        name="short_conv",
    )(u, w, bias.reshape(1, c3))


def _filter_body(z_ref, w1_ref, b1_ref, fr_ref, w2_ref, b2_ref, w3_ref, dl_ref, o_ref):
    z = z_ref[...]
    h = jnp.sin(fr_ref[0:1] * (_dot3(z, w1_ref[...]) + b1_ref[...]))
    h = jnp.sin(fr_ref[1:2] * (_dot3(h, w2_ref[...]) + b2_ref[...]))
    k = _dot3(h, w3_ref[0])
    c = D_HYENA
    t = z[:, 0:1]
    decay = jnp.exp(-t * jnp.abs(dl_ref[...]))
    kf = k[:, :c] * decay
    row = lax.broadcasted_iota(jnp.int32, kf.shape, 0)
    kb = jnp.where(row == 0, 0.0, k[:, c:] * decay)
    norm = jnp.sum(jnp.abs(kf), axis=0, keepdims=True) + jnp.sum(jnp.abs(kb), axis=0, keepdims=True)
    inv = 1.0 / norm
    o_ref[0, 0] = kf * inv
    o_ref[1, 0] = kb * inv


def _hyena_filters_time(seq, w1, b1, freq, w2, b2, w3):
    c = D_HYENA
    t = np.linspace(0.0, 1.0, seq)[:, None]
    w = 2.0 * np.pi * np.arange(seq)[:, None] / seq
    f = np.linspace(1e-4, POS_BANDS - 1, POS_BANDS)[None]
    z = np.concatenate([t, np.cos(f * w), -np.sin(f * w)], axis=-1).astype(np.float32)
    pe = z.shape[1]
    z = jnp.asarray(np.pad(z, ((0, 0), (0, LANES - pe))))
    w1p = jnp.pad(w1, ((0, LANES - pe), (0, 0)))
    max_decay = math.log(HYENA_DECAY_TARGET) / HYENA_FAST_PCT
    min_decay = math.log(HYENA_DECAY_TARGET) / HYENA_SLOW_PCT
    deltas = jnp.asarray(np.linspace(min_decay, max_decay, c, dtype=np.float32)[None])
    hid = w1.shape[1]
    w3r = jnp.transpose(w3.reshape(hid, 2, 2 * c), (1, 0, 2))
    full = lambda a: pl.BlockSpec(a.shape, lambda o: (0,) * a.ndim)
    b1r, b2r = b1.reshape(1, hid), b2.reshape(1, hid)
    return pl.pallas_call(
        _filter_body,
        grid=(2,),
        in_specs=[full(z), full(w1p), full(b1r), full(freq), full(w2), full(b2r),
                  pl.BlockSpec((1, hid, 2 * c), lambda o: (o, 0, 0)), full(deltas)],
        out_specs=pl.BlockSpec((2, 1, seq, c), lambda o: (o, 0, 0, 0)),
        out_shape=jax.ShapeDtypeStruct((4, 1, seq, c), F32),
        compiler_params=_params("parallel"),
        name="hyena_filter_time",
    )(z, w1p, b1r, freq, w2, b2r, w3r, deltas)


def _filter_spec_body(m2_ref, af_ref, ab_ref, kr_ref, ki_ref):
    n2 = kr_ref.shape[2]
    for i in range(m2_ref.shape[0]):
        xf = _bdot(m2_ref[i], af_ref[0, i])
        xb = _bdot(m2_ref[i], ab_ref[0, i])
        kr_ref[0, i] = xf[:n2] + xb[:n2]
        ki_ref[0, i] = xf[n2:] - xb[n2:]


def _filter_spectrum(m2, a, fb):
    _, h1, n22, c = a.shape
    n2 = n22 // 2
    out = jax.ShapeDtypeStruct((2, h1, n2, c), F32)
    return pl.pallas_call(
        _filter_spec_body,
        grid=(2, h1 // fb),
        in_specs=[pl.BlockSpec((fb, n22, n22), lambda o, j: (j, 0, 0)),
                  pl.BlockSpec((1, fb, n22, c), lambda o, j: (2 * o, j, 0, 0)),
                  pl.BlockSpec((1, fb, n22, c), lambda o, j: (2 * o + 1, j, 0, 0))],
        out_specs=[pl.BlockSpec((1, fb, n2, c), lambda o, j: (o, j, 0, 0))] * 2,
        out_shape=[out, out],
        compiler_params=_params("parallel", "parallel"),
        name="hyena_filter_spectrum",
    )(m2, a, a)


def _conv_stage2_body(m2_ref, m2t_ref, a_ref, kr_ref, ki_ref, o_ref):
    n2 = kr_ref.shape[2]
    for i in range(m2_ref.shape[0]):
        x = _bdot(m2_ref[i], a_ref[0, i])
        xr, xi = x[:n2], x[n2:]
        kr, ki = kr_ref[0, i], ki_ref[0, i]
        y = jnp.concatenate([kr * xr - ki * xi, kr * xi + ki * xr], axis=0).astype(BF16)
        o_ref[0, i] = _bdot(m2t_ref[i], y).astype(BF16)


def _conv_stage2(m2, m2t, a, kr, ki, order, fb):
    b, h1, n22, c = a.shape
    n2 = n22 // 2
    mat = pl.BlockSpec((fb, n22, n22), lambda i, j: (j, 0, 0))
    kspec = pl.BlockSpec((1, fb, n2, c), lambda i, j: (order, j, 0, 0))
    blk = pl.BlockSpec((1, fb, n22, c), lambda i, j: (i, j, 0, 0))
    return pl.pallas_call(
        _conv_stage2_body,
        grid=(b, h1 // fb),
        in_specs=[mat, mat, blk, kspec, kspec],
        out_specs=blk,
        out_shape=jax.ShapeDtypeStruct(a.shape, BF16),
        compiler_params=_params("parallel", "parallel"),
        name="hyena_conv_stage2",
    )(m2, m2t, a, kr, ki)


def _conv_stage3_body(with_next, m3_ref, m1_ref, b_ref, gate_ref, skipin_ref, skip_ref, *o_refs):
    y = _bdot(m3_ref[...], b_ref[0])
    z = gate_ref[0, 0] * (y + skipin_ref[0, 0] * skip_ref[...])
    o_refs[0][0, 0] = z.astype(o_refs[0].dtype)
    if with_next:
        o_refs[1][0] = _bdot(m1_ref[...], z.astype(BF16)).astype(BF16)


def _conv_stage3(m3, m1, bm, gates, gate_sel, skipin, skipin_sel, skip_row, with_next, out_dtype):
    b, n1, w = bm.shape
    h1 = n1 // 2
    tl = _lane_tile(w)
    out_specs = [pl.BlockSpec((1, 1, h1, tl), lambda i, j: (i, 0, 0, j))]
    out_shape = [jax.ShapeDtypeStruct((b, 1, h1, w), out_dtype)]
    if with_next:
        out_specs.append(pl.BlockSpec((1, n1, tl), lambda i, j: (i, 0, j)))
        out_shape.append(jax.ShapeDtypeStruct((b, n1, w), BF16))
    return pl.pallas_call(
        functools.partial(_conv_stage3_body, with_next),
        grid=(b, w // tl),
        in_specs=[pl.BlockSpec((h1, n1), lambda i, j: (0, 0)),
                  pl.BlockSpec((n1, h1), lambda i, j: (0, 0)),
                  pl.BlockSpec((1, n1, tl), lambda i, j: (i, 0, j)),
                  pl.BlockSpec((1, 1, h1, tl), lambda i, j: (i, gate_sel, 0, j)),
                  pl.BlockSpec((1, 1, h1, tl), lambda i, j: (i, skipin_sel, 0, j)),
                  pl.BlockSpec((1, tl), lambda i, j: (0, 0))],
        out_specs=out_specs,
        out_shape=out_shape,
        compiler_params=_params("parallel", "parallel"),
        name="hyena_conv_stage3",
    )(m3, m1, bm, gates, skipin, skip_row)


def _hyena_mixer(u, conv_w, conv_b, w1, b1, freq, w2, b2, w3, skip):
    b, seq, _ = u.shape
    c = D_HYENA
    n1, n2, m1, m2, m2t, m3 = _hyena_consts(seq)
    h1 = n1 // 2
    w = n2 * c
    fb = min(h1, 8)
    m1, m2, m2t, m3 = (_mxu_const(a) for a in (m1, m2, m2t, m3))

    kt = _hyena_filters_time(seq, w1, b1, freq, w2, b2, w3)
    ka = _stage1(m1, kt.reshape(4, 1, h1, w), 0)
    kr, ki = _filter_spectrum(m2, ka.reshape(4, h1, 2 * n2, c), fb)

    u3 = _short_conv(u, conv_w, conv_b).reshape(b, 3, h1, w)
    tl = _lane_tile(w)
    skip_rows = jnp.tile(skip, (1, tl // c))

    a = _stage1(m1, u3, 2)
    bm = _conv_stage2(m2, m2t, a.reshape(b, h1, 2 * n2, c), kr, ki, 0, fb)
    z, a = _conv_stage3(m3, m1, bm.reshape(b, n1, w), u3, 0, u3, 2, skip_rows[0:1], True, F32)
    bm = _conv_stage2(m2, m2t, a.reshape(b, h1, 2 * n2, c), kr, ki, 1, fb)
    (y,) = _conv_stage3(m3, m1, bm.reshape(b, n1, w), u3, 1, z, 0, skip_rows[1:2], False, BF16)
    return y.reshape(b, seq, c)


def _fnet_stage2_body(m2_ref, bdc_ref, bds_ref, a_ref, o_ref):
    n2 = m2_ref.shape[1] // 2
    c = a_ref.shape[3]
    for i in range(m2_ref.shape[0]):
        x = _bdot(m2_ref[i], a_ref[0, i])
        o = _bdot(x[:n2].astype(BF16), bdc_ref[...]) + _bdot(x[n2:].astype(BF16), bds_ref[...])
        o_ref[0, :, i * c:(i + 1) * c] = o.astype(o_ref.dtype)


def _fourier_mix(g):
    b, seq, c = g.shape
    n1, n2, m1, m2, bdc, bds = _fnet_consts(seq)
    m1, m2, bdc, bds = (_mxu_const(a) for a in (m1, m2, bdc, bds))
    a = _stage1(m1, g.reshape(b, 1, n1, n2 * c), 0)
    fb = min(n1, 8)
    y = pl.pallas_call(
        _fnet_stage2_body,
        grid=(b, n1 // fb),
        in_specs=[pl.BlockSpec((fb, 2 * n2, 2 * n2), lambda i, j: (j, 0, 0)),
                  pl.BlockSpec((c, c), lambda i, j: (0, 0)),
                  pl.BlockSpec((c, c), lambda i, j: (0, 0)),
                  pl.BlockSpec((1, fb, 2 * n2, c), lambda i, j: (i, j, 0, 0))],
        out_specs=pl.BlockSpec((1, n2, fb * c), lambda i, j: (i, 0, j)),
        out_shape=jax.ShapeDtypeStruct((b, n2, n1 * c), BF16),
        compiler_params=_params("parallel", "parallel"),
        name="fnet_stage2",
    )(m2, bdc, bds, a.reshape(b, n1, 2 * n2, c))
    return y.reshape(b, seq, c)


def _dot_nt(a, b):
    return lax.dot_general(a, b, (((1,), (1,)), ((), ())), preferred_element_type=F32)


def _head_pair_attention(q2, keys, values, biases):
    lane = lax.broadcasted_iota(jnp.int32, q2.shape, 1)
    scale = 1.0 / math.sqrt(HEAD_DIM)
    outs = []
    for hh in range(2):
        in_head = (lane >= hh * HEAD_DIM) & (lane < (hh + 1) * HEAD_DIM)
        qm = jnp.where(in_head, q2, jnp.zeros_like(q2))
        scores = []
        for kk, bias in zip(keys, biases[hh]):
            s = _dot_nt(qm, kk) * scale
            scores.append(s if bias is None else s + bias)
        m = scores[0].max(axis=-1, keepdims=True)
        for s in scores[1:]:
            m = jnp.maximum(m, s.max(axis=-1, keepdims=True))
        es = [jnp.exp(s - m) for s in scores]
        denom = es[0].sum(axis=-1, keepdims=True)
        for e in es[1:]:
            denom = denom + e.sum(axis=-1, keepdims=True)
        acc = _bdot(es[0].astype(BF16), values[0])
        for e, vv in zip(es[1:], values[1:]):
            acc = acc + _bdot(e.astype(BF16), vv)
        outs.append(acc / denom)
    return jnp.where(lane < HEAD_DIM, outs[0], outs[1])


def _na_body(rows, q_ref, k_ref, v_ref, kc_ref, vc_ref, bias_ref, o_ref):
    r = pl.program_id(1)
    kh = NA_KH
    r_start = jnp.clip(r - kh // 2, 0, rows - kh)
    start = pl.multiple_of(r_start * GRID_W, GRID_W)
    n_pairs = q_ref.shape[2] // LANES
    for p in range(n_pairs):
        sl = slice(p * LANES, (p + 1) * LANES)
        q2 = q_ref[0, :, sl]
        k2 = k_ref[0, pl.ds(start, kh * GRID_W), sl]
        v2 = v_ref[0, pl.ds(start, kh * GRID_W), sl]
        kc2 = kc_ref[0, :, sl]
        vc2 = vc_ref[0, :, sl]
        biases = [[bias_ref[2 * p + hh, 0], None] for hh in range(2)]
        o = _head_pair_attention(q2, [k2, kc2], [v2, vc2], biases)
        o_ref[0, :, sl] = o.astype(o_ref.dtype)


def _na_bias_table(rpb, rows):
    kh, kw, w = min(NA_KH, rows), NA_KW, GRID_W
    col = np.arange(w)
    col_start = np.clip(col - kw // 2, 0, w - kw)
    key = np.arange(w)
    valid = (key[None, :] >= col_start[:, None]) & (key[None, :] < col_start[:, None] + kw)
    col_idx = np.clip(key[None, :] - col[:, None] + (NA_KW - 1), 0, 2 * NA_KW - 2)
    shift = np.arange(kh)[:, None]
    row_idx = np.arange(kh)[None, :] - shift + (NA_KH - 1)
    g = rpb[:, row_idx[:, None, :, None], col_idx[None, :, None, :]]
    g = jnp.where(valid[None, None, :, None, :], g.astype(F32), MASK_VALUE)
    return g.reshape(rpb.shape[0], kh, w, kh * w)


def _neighbourhood_attention(qkv, kv_ctx, rpb):
    b, seq, _ = qkv.shape
    rows = seq // GRID_W
    lc = kv_ctx.shape[1]
    kh = NA_KH
    assert rows >= kh
    bias = _na_bias_table(rpb, rows)
    nh = rpb.shape[0]

    def bias_map(i, r):
        return (0, r - jnp.clip(r - kh // 2, 0, rows - kh), 0, 0)

    return pl.pallas_call(
        functools.partial(_na_body, rows),
        grid=(b, rows),
        in_specs=[pl.BlockSpec((1, GRID_W, D_NA), lambda i, r: (i, r, 0)),
                  pl.BlockSpec((1, seq, D_NA), lambda i, r: (i, 0, 1)),
                  pl.BlockSpec((1, seq, D_NA), lambda i, r: (i, 0, 2)),
                  pl.BlockSpec((1, lc, D_NA), lambda i, r: (i, 0, 0)),
                  pl.BlockSpec((1, lc, D_NA), lambda i, r: (i, 0, 1)),
                  pl.BlockSpec((nh, 1, GRID_W, kh * GRID_W), bias_map)],
        out_specs=pl.BlockSpec((1, GRID_W, D_NA), lambda i, r: (i, r, 0)),
        out_shape=jax.ShapeDtypeStruct((b, seq, D_NA), BF16),
        compiler_params=_params("parallel", "arbitrary"),
        name="neighbourhood_attention",
    )(qkv, qkv, qkv, kv_ctx, kv_ctx, bias)


def _ctx_attn_body(q_ref, k_ref, v_ref, o_ref):
    o = _head_pair_attention(q_ref[0], [k_ref[0]], [v_ref[0]], [[None], [None]])
    o_ref[0] = o.astype(o_ref.dtype)


def _context_attention(qkv):
    b, lc, _ = qkv.shape
    n_pairs = D_NA // LANES
    return pl.pallas_call(
        _ctx_attn_body,
        grid=(b, n_pairs),
        in_specs=[pl.BlockSpec((1, lc, LANES), lambda i, p: (i, 0, p)),
                  pl.BlockSpec((1, lc, LANES), lambda i, p: (i, 0, n_pairs + p)),
                  pl.BlockSpec((1, lc, LANES), lambda i, p: (i, 0, 2 * n_pairs + p))],
        out_specs=pl.BlockSpec((1, lc, LANES), lambda i, p: (i, 0, p)),
        out_shape=jax.ShapeDtypeStruct((b, lc, D_NA), BF16),
        compiler_params=_params("parallel", "parallel"),
        name="context_attention",
    )(qkv, qkv, qkv)


def kernel(x, c, ctx, c_ctx, w_mod, b_mod, norm1_g, norm2_g, w_in, w_out, hy_conv_w, hy_conv_b,
           filt_w1, filt_b1, filt_freq, filt_w2, filt_b2, filt_w3, hy_skip, na_rpb,
           w_gate, w_up, w_down, final_g):
    depth = w_mod.shape[0]
    b, seq, d = x.shape
    off_q = 3 * D_HYENA + D_FNET
    off_k = off_q + D_NA
    splits = (3 * D_HYENA, D_FNET, 3 * D_NA)
    dtypes = (F32, BF16, BF16)
    tm, tf = 512, 1408

    pad = (-(b + 1)) % 8
    c_all = jnp.concatenate([c, c_ctx[None], jnp.zeros((pad, d), F32)], axis=0)
    mod = _modulation(c_all, w_mod, b_mod)

    w_in_b, w_out_b = w_in.astype(BF16), w_out.astype(BF16)
    w_gate_b, w_up_b, w_down_b = w_gate.astype(BF16), w_up.astype(BF16), w_down.astype(BF16)
    fg = final_g.reshape(1, d)

    for l in range(depth):
        last = l == depth - 1
        m_lat = mod[l, :b].reshape(b, 1, 6, d)
        m_ctx = jnp.broadcast_to(mod[l, b].reshape(1, 1, 6, d), (b, 1, 6, d))
        sh1, sc1, g1, sh2, sc2, g2 = (m_lat[:, :, i] for i in range(6))
        csh1, csc1, cg1, csh2, csc2, cg2 = (m_ctx[:, :, i] for i in range(6))
        n1g, n2g = norm1_g[l].reshape(1, d), norm2_g[l].reshape(1, d)
        hy_params = (hy_conv_w[l], hy_conv_b[l], filt_w1[l], filt_b1[l], filt_freq[l],
                     filt_w2[l], filt_b2[l], filt_w3[l], hy_skip[l])

        hy, fn, qkv = _norm_proj(x, n1g, sc1, sh1, w_in_b[l], splits, dtypes, tm)
        if last:
            (kv_c,) = _norm_proj(ctx, n1g, csc1, csh1, w_in_b[l][:, off_k:], (2 * D_NA,), (BF16,), tm)
        else:
            chy, cfn, cqkv = _norm_proj(ctx, n1g, csc1, csh1, w_in_b[l], splits, dtypes, tm)
            kv_c = cqkv[:, :, D_NA:]

        y_hy = _hyena_mixer(hy, *hy_params)
        y_fn = _fourier_mix(fn)
        y_na = _neighbourhood_attention(qkv, kv_c, na_rpb[l])
        x = _proj_out(x, g1, y_hy, y_fn, y_na, w_out_b[l], tm)

        x = _ffn(x, n2g, sc2, sh2, g2, w_gate_b[l], w_up_b[l], w_down_b[l], fg, last, tm, tf)

        if not last:
            cy_hy = _hyena_mixer(chy, *hy_params)
            cy_fn = _fourier_mix(cfn)
            cy_na = _context_attention(cqkv)
            ctx = _proj_out(ctx, cg1, cy_hy, cy_fn, cy_na, w_out_b[l], tm)
            ctx = _ffn(ctx, n2g, csc2, csh2, cg2, w_gate_b[l], w_up_b[l], w_down_b[l], fg, False, tm, tf)
    return x
```

```python
import functools
import math

import numpy as np
import jax
import jax.numpy as jnp
from jax import lax
from jax.experimental import pallas as pl
from jax.experimental.pallas import tpu as pltpu

F32 = jnp.float32
BF16 = jnp.bfloat16

EPS = 1e-6
GRID_W = 64
HEAD_DIM = 64
D_HYENA = 256
D_FNET = 256
D_NA = 512
FNET_GROUP_DIM = 64
NA_KH = 8
NA_KW = 16
NA_QROWS = 4
NA_KROWS = NA_QROWS + NA_KH - 1
POS_BANDS = 16
HYENA_DECAY_TARGET = 1e-2
HYENA_FAST_PCT = 0.3
HYENA_SLOW_PCT = 1.5
MASK_VALUE = -1e30
VMEM_LIMIT = 56 * 1024 * 1024
LANES = 128
F32_SUBLANES = 8
BF16_SUBLANES = 16


def _params(*sem):
    return pltpu.CompilerParams(dimension_semantics=sem, vmem_limit_bytes=VMEM_LIMIT)


def _bdot(a, b):
    return jnp.dot(a, b, preferred_element_type=F32)


def _split_bf16(a):
    hi = a.astype(BF16)
    lo = (a - hi.astype(F32)).astype(BF16)
    return hi, lo


def _dot3(a, b):
    a_hi, a_lo = _split_bf16(a)
    b_hi, b_lo = _split_bf16(b)
    return _bdot(a_hi, b_hi) + _bdot(a_lo, b_hi) + _bdot(a_hi, b_lo)


def _mod_body(c_ref, w_ref, b_ref, o_ref):
    c = c_ref[...]
    o_ref[0] = _dot3(c * jax.nn.sigmoid(c), w_ref[0]) + b_ref[0]


def _modulation(c_all, w_mod, b_mod):
    depth, d, n = w_mod.shape
    rows = c_all.shape[0]
    tn = 1536
    return pl.pallas_call(
        _mod_body,
        grid=(depth, n // tn),
        in_specs=[
            pl.BlockSpec((rows, d), lambda l, j: (0, 0)),
            pl.BlockSpec((1, d, tn), lambda l, j: (l, 0, j)),
            pl.BlockSpec((1, 1, tn), lambda l, j: (l, 0, j)),
        ],
        out_specs=pl.BlockSpec((1, rows, tn), lambda l, j: (l, 0, j)),
        out_shape=jax.ShapeDtypeStruct((depth, rows, n), F32),
        compiler_params=_params("parallel", "parallel"),
        name="modulation",
    )(c_all, w_mod, b_mod.reshape(depth, 1, n))


def _modulated_norm(x, g, sc, sh):
    ms = jnp.mean(x * x, axis=-1, keepdims=True)
    y = x * lax.rsqrt(ms + EPS) * g
    return y * (1.0 + sc) + sh


def _norm_proj_body(splits, x_ref, g_ref, sc_ref, sh_ref, w_ref, *o_refs):
    h = _modulated_norm(x_ref[0], g_ref[...], sc_ref[0], sh_ref[0])
    p = _bdot(h.astype(BF16), w_ref[...])
    off = 0
    for o_ref, width in zip(o_refs, splits):
        o_ref[0] = p[:, off:off + width].astype(o_ref.dtype)
        off += width


def _norm_proj(x, g, sc, sh, w, splits, dtypes, tm):
    b, t, d = x.shape
    n = w.shape[1]
    tm = min(tm, t)
    return pl.pallas_call(
        functools.partial(_norm_proj_body, tuple(splits)),
        grid=(b, t // tm),
        in_specs=[
            pl.BlockSpec((1, tm, d), lambda i, j: (i, j, 0)),
            pl.BlockSpec((1, d), lambda i, j: (0, 0)),
            pl.BlockSpec((1, 1, d), lambda i, j: (i, 0, 0)),
            pl.BlockSpec((1, 1, d), lambda i, j: (i, 0, 0)),
            pl.BlockSpec((d, n), lambda i, j: (0, 0)),
        ],
        out_specs=[pl.BlockSpec((1, tm, wd), lambda i, j: (i, j, 0)) for wd in splits],
        out_shape=[jax.ShapeDtypeStruct((b, t, wd), dt) for wd, dt in zip(splits, dtypes)],
        compiler_params=_params("parallel", "parallel"),
        name="norm_proj",
    )(x, g, sc, sh, w)


def _proj_out_body(x_ref, gate_ref, hy_ref, fn_ref, na_ref, w_ref, o_ref):
    mix = (_bdot(hy_ref[0], w_ref[:D_HYENA])
           + _bdot(fn_ref[0], w_ref[D_HYENA:D_HYENA + D_FNET])
           + _bdot(na_ref[0], w_ref[D_HYENA + D_FNET:]))
    o_ref[0] = x_ref[0] + gate_ref[0] * mix


def _proj_out(x, gate, y_hy, y_fn, y_na, w, tm):
    b, t, d = x.shape
    tm = min(tm, t)
    tok = lambda wd: pl.BlockSpec((1, tm, wd), lambda i, j: (i, j, 0))
    return pl.pallas_call(
        _proj_out_body,
        grid=(b, t // tm),
        in_specs=[tok(d), pl.BlockSpec((1, 1, d), lambda i, j: (i, 0, 0)),
                  tok(D_HYENA), tok(D_FNET), tok(D_NA),
                  pl.BlockSpec(w.shape, lambda i, j: (0, 0))],
        out_specs=tok(d),
        out_shape=jax.ShapeDtypeStruct(x.shape, F32),
        compiler_params=_params("parallel", "parallel"),
        name="proj_out",
    )(x, gate, y_hy, y_fn, y_na, w)


def _ffn_body(final_norm, x_ref, g_ref, sc_ref, sh_ref, gate_ref, wg_ref, wu_ref, wd_ref,
              fg_ref, o_ref, h_scr, acc_scr):
    f = pl.program_id(2)

    @pl.when(f == 0)
    def _():
        h = _modulated_norm(x_ref[0], g_ref[...], sc_ref[0], sh_ref[0])
        h_scr[...] = h.astype(BF16)

    h = h_scr[...]
    a = _bdot(h, wg_ref[...])
    u = _bdot(h, wu_ref[...])
    act = (a * jax.nn.sigmoid(a) * u).astype(BF16)
    part = _bdot(act, wd_ref[...])

    @pl.when(f == 0)
    def _():
        acc_scr[...] = part

    @pl.when(f != 0)
    def _():
        acc_scr[...] += part

    @pl.when(f == pl.num_programs(2) - 1)
    def _():
        y = x_ref[0] + gate_ref[0] * acc_scr[...]
        if final_norm:
            ms = jnp.mean(y * y, axis=-1, keepdims=True)
            y = y * lax.rsqrt(ms + EPS) * fg_ref[...]
        o_ref[0] = y


def _ffn(x, g, sc, sh, gate, wg, wu, wd, final_g, final_norm, tm, tf):
    b, t, d = x.shape
    dff = wg.shape[1]
    tm = min(tm, t)
    vec = pl.BlockSpec((1, 1, d), lambda i, j, k: (i, 0, 0))
    return pl.pallas_call(
        functools.partial(_ffn_body, final_norm),
        grid=(b, t // tm, dff // tf),
        in_specs=[
            pl.BlockSpec((1, tm, d), lambda i, j, k: (i, j, 0)),
            pl.BlockSpec((1, d), lambda i, j, k: (0, 0)),
            vec, vec, vec,
            pl.BlockSpec((d, tf), lambda i, j, k: (0, k)),
            pl.BlockSpec((d, tf), lambda i, j, k: (0, k)),
            pl.BlockSpec((tf, d), lambda i, j, k: (k, 0)),
            pl.BlockSpec((1, d), lambda i, j, k: (0, 0)),
        ],
        out_specs=pl.BlockSpec((1, tm, d), lambda i, j, k: (i, j, 0)),
        out_shape=jax.ShapeDtypeStruct(x.shape, F32),
        scratch_shapes=[pltpu.VMEM((tm, d), BF16), pltpu.VMEM((tm, d), F32)],
        compiler_params=_params("parallel", "parallel", "arbitrary"),
        name="ffn",
    )(x, g, sc, sh, gate, wg, wu, wd, final_g)


def _split_seq(n_total):
    log = int(round(math.log2(n_total)))
    assert 2 ** log == n_total
    n1 = 2 ** (log // 2)
    return n1, n_total // n1


@functools.lru_cache(maxsize=None)
def _hyena_consts(seq):
    n = 2 * seq
    n1, n2 = _split_seq(n)
    h1 = n1 // 2
    f1 = np.arange(h1)[:, None]
    t1 = np.arange(h1)[None, :]
    th = np.pi * (2 * f1 + 1) * t1 / n1
    m1 = np.empty((n1, h1))
    m1[0::2] = np.cos(th)
    m1[1::2] = -np.sin(th)
    m3 = m1.T * (2.0 / n)
    f = (np.arange(h1)[:, None, None] + n1 * np.arange(n2)[None, :, None])
    t2 = np.arange(n2)[None, None, :]
    ph = np.pi * ((2 * f + 1) * t2 % (2 * n)) / n
    c, s = np.cos(ph), np.sin(ph)
    m2 = np.concatenate([np.concatenate([c, s], axis=2), np.concatenate([-s, c], axis=2)], axis=1)
    m2t = np.transpose(m2, (0, 2, 1))
    m1k = np.kron(m1, np.eye(F32_SUBLANES))
    m3k = np.kron(m3, np.eye(BF16_SUBLANES))
    return n1, n2, m1k, m2, m2t, m3k


@functools.lru_cache(maxsize=None)
def _fnet_consts(seq):
    n1, n2 = _split_seq(seq)
    n1, n2 = n1 // 2, n2 * 2
    f1 = np.arange(n1)[:, None]
    t1 = np.arange(n1)[None, :]
    th = 2 * np.pi * (f1 * t1 % n1) / n1
    m1 = np.empty((2 * n1, n1))
    m1[0::2] = np.cos(th)
    m1[1::2] = -np.sin(th)
    f = (np.arange(n1)[:, None, None] + n1 * np.arange(n2)[None, :, None])
    t2 = np.arange(n2)[None, None, :]
    ph = 2 * np.pi * (f * t2 % seq) / seq
    c, s = np.cos(ph), np.sin(ph)
    m2 = np.concatenate([np.concatenate([c, s], axis=2), np.concatenate([-s, c], axis=2)], axis=1)
    gd = FNET_GROUP_DIM
    cc = np.arange(gd)
    pg = 2 * np.pi * (np.outer(cc, cc) % gd) / gd
    scale = 1.0 / math.sqrt(gd * seq)
    eye = np.eye(D_FNET // gd)
    bdc = np.kron(eye, np.cos(pg)) * scale
    bds = np.kron(eye, np.sin(pg)) * scale
    m1k = np.kron(m1, np.eye(BF16_SUBLANES))
    g = F32_SUBLANES
    pre = np.zeros((g * n1, 2 * n1 * g))
    pim = np.zeros((g * n1, 2 * n1 * g))
    for f2 in range(g):
        for ff in range(n1):
            pre[f2 * n1 + ff, (2 * ff) * g + f2] = 1.0
            pim[f2 * n1 + ff, (2 * ff + 1) * g + f2] = 1.0
    return n1, n2, m1k, m2, pre, pim, bdc, bds


def _mxu_const(a):
    return jnp.asarray(a, dtype=F32).astype(BF16)


def _stage1_into(mk_ref, x_at, a_scr, group):
    n_out, n2, c = a_scr.shape

    def body(s, carry):
        r0 = pl.multiple_of(s * group, group)
        xs = x_at(r0)
        xs = xs.reshape(xs.shape[0] * group, c).astype(BF16)
        a_scr[:, pl.ds(r0, group), :] = _bdot(mk_ref[...], xs).reshape(n_out, group, c)
        return carry

    lax.fori_loop(0, n2 // group, body, 0)


def _short_conv_body(u_ref, w_ref, b_ref, o_ref):
    u = u_ref[0]
    seq = u.shape[0]
    row = lax.broadcasted_iota(jnp.int32, u.shape, 0)
    prev = jnp.where(row == 0, 0.0, pltpu.roll(u, 1, axis=0))
    nxt = jnp.where(row == seq - 1, 0.0, pltpu.roll(u, seq - 1, axis=0))
    o_ref[0, 0] = prev * w_ref[0:1] + u * w_ref[1:2] + nxt * w_ref[2:3] + b_ref[...]


def _short_conv(u, w, bias):
    b, seq, c3 = u.shape
    c = D_HYENA
    return pl.pallas_call(
        _short_conv_body,
        grid=(b, c3 // c),
        in_specs=[pl.BlockSpec((1, seq, c), lambda i, j: (i, 0, j)),
                  pl.BlockSpec((3, c), lambda i, j: (0, j)),
                  pl.BlockSpec((1, c), lambda i, j: (0, j))],
        out_specs=pl.BlockSpec((1, 1, seq, c), lambda i, j: (i, j, 0, 0)),
        out_shape=jax.ShapeDtypeStruct((b, c3 // c, seq, c), F32),
        compiler_params=_params("parallel", "parallel"),
        name="short_conv",
    )(u, w, bias.reshape(1, c3))


def _filter_body(z_ref, w1_ref, b1_ref, fr_ref, w2_ref, b2_ref, w3_ref, dl_ref, o_ref):
    z = z_ref[...]
    h = jnp.sin(fr_ref[0:1] * (_dot3(z, w1_ref[...]) + b1_ref[...]))
    h = jnp.sin(fr_ref[1:2] * (_dot3(h, w2_ref[...]) + b2_ref[...]))
    k = _dot3(h, w3_ref[0])
    c = D_HYENA
    t = z[:, 0:1]
    decay = jnp.exp(-t * jnp.abs(dl_ref[...]))
    kf = k[:, :c] * decay
    row = lax.broadcasted_iota(jnp.int32, kf.shape, 0)
    kb = jnp.where(row == 0, 0.0, k[:, c:] * decay)
    norm = jnp.sum(jnp.abs(kf), axis=0, keepdims=True) + jnp.sum(jnp.abs(kb), axis=0, keepdims=True)
    inv = 1.0 / norm
    o_ref[0, 0] = kf * inv
    o_ref[1, 0] = kb * inv


def _hyena_filters_time(seq, w1, b1, freq, w2, b2, w3):
    c = D_HYENA
    t = np.linspace(0.0, 1.0, seq)[:, None]
    w = 2.0 * np.pi * np.arange(seq)[:, None] / seq
    f = np.linspace(1e-4, POS_BANDS - 1, POS_BANDS)[None]
    z = np.concatenate([t, np.cos(f * w), -np.sin(f * w)], axis=-1).astype(np.float32)
    pe = z.shape[1]
    z = jnp.asarray(np.pad(z, ((0, 0), (0, LANES - pe))))
    w1p = jnp.pad(w1, ((0, LANES - pe), (0, 0)))
    max_decay = math.log(HYENA_DECAY_TARGET) / HYENA_FAST_PCT
    min_decay = math.log(HYENA_DECAY_TARGET) / HYENA_SLOW_PCT
    deltas = jnp.asarray(np.linspace(min_decay, max_decay, c, dtype=np.float32)[None])
    hid = w1.shape[1]
    w3r = jnp.transpose(w3.reshape(hid, 2, 2 * c), (1, 0, 2))
    full = lambda a: pl.BlockSpec(a.shape, lambda o: (0,) * a.ndim)
    b1r, b2r = b1.reshape(1, hid), b2.reshape(1, hid)
    return pl.pallas_call(
        _filter_body,
        grid=(2,),
        in_specs=[full(z), full(w1p), full(b1r), full(freq), full(w2), full(b2r),
                  pl.BlockSpec((1, hid, 2 * c), lambda o: (o, 0, 0)), full(deltas)],
        out_specs=pl.BlockSpec((2, 1, seq, c), lambda o: (o, 0, 0, 0)),
        out_shape=jax.ShapeDtypeStruct((4, 1, seq, c), F32),
        compiler_params=_params("parallel"),
        name="hyena_filter_time",
    )(z, w1p, b1r, freq, w2, b2r, w3r, deltas)


def _stage2_rows(a_scr, f1):
    _, n2, c = a_scr.shape
    return a_scr[pl.ds(2 * f1, 2)].reshape(2 * n2, c).astype(BF16)


def _filter_spec_body(m1k_ref, m2_ref, kf_ref, kb_ref, kr_ref, ki_ref, af_scr, ab_scr):
    j = pl.program_id(1)
    fb = m2_ref.shape[0]
    n2 = kr_ref.shape[2]
    g = F32_SUBLANES

    @pl.when(j == 0)
    def _():
        _stage1_into(m1k_ref, lambda r0: kf_ref[0, 0, :, pl.ds(r0, g), :], af_scr, g)
        _stage1_into(m1k_ref, lambda r0: kb_ref[0, 0, :, pl.ds(r0, g), :], ab_scr, g)

    for i in range(fb):
        f1 = j * fb + i
        xf = _bdot(m2_ref[i], _stage2_rows(af_scr, f1))
        xb = _bdot(m2_ref[i], _stage2_rows(ab_scr, f1))
        kr_ref[0, i] = xf[:n2] + xb[:n2]
        ki_ref[0, i] = xf[n2:] - xb[n2:]


def _filter_spectrum(m1k, m2, kt, fb):
    _, _, h1, n2, c = kt.shape
    out = jax.ShapeDtypeStruct((2, h1, n2, c), F32)
    return pl.pallas_call(
        _filter_spec_body,
        grid=(2, h1 // fb),
        in_specs=[pl.BlockSpec(m1k.shape, lambda o, j: (0, 0)),
                  pl.BlockSpec((fb, 2 * n2, 2 * n2), lambda o, j: (j, 0, 0)),
                  pl.BlockSpec((1, 1, h1, n2, c), lambda o, j: (2 * o, 0, 0, 0, 0)),
                  pl.BlockSpec((1, 1, h1, n2, c), lambda o, j: (2 * o + 1, 0, 0, 0, 0))],
        out_specs=[pl.BlockSpec((1, fb, n2, c), lambda o, j: (o, j, 0, 0))] * 2,
        out_shape=[out, out],
        scratch_shapes=[pltpu.VMEM((2 * h1, n2, c), F32)] * 2,
        compiler_params=_params("parallel", "arbitrary"),
        name="hyena_filter_spectrum",
    )(m1k, m2, kt, kt)


def _conv_fwd_body(m1k_ref, m2_ref, m2t_ref, x_ref, kr_ref, ki_ref, o_ref, a_scr):
    j = pl.program_id(1)
    fb = m2_ref.shape[0]
    _, n2, c = a_scr.shape
    g = F32_SUBLANES

    @pl.when(j == 0)
    def _():
        _stage1_into(m1k_ref, lambda r0: x_ref[0, 0, :, pl.ds(r0, g), :], a_scr, g)

    for i in range(fb):
        x = _bdot(m2_ref[i], _stage2_rows(a_scr, j * fb + i))
        xr, xi = x[:n2], x[n2:]
        kr, ki = kr_ref[0, i], ki_ref[0, i]
        y = jnp.concatenate([kr * xr - ki * xi, kr * xi + ki * xr], axis=0).astype(BF16)
        o_ref[0, 2 * i:2 * i + 2] = _bdot(m2t_ref[i], y).astype(BF16).reshape(2, n2, c)


def _conv_fwd(m1k, m2, m2t, x, sel, kr, ki, order, fb):
    b, _, h1, n2, c = x.shape
    mat = pl.BlockSpec((fb, 2 * n2, 2 * n2), lambda i, j: (j, 0, 0))
    kspec = pl.BlockSpec((1, fb, n2, c), lambda i, j: (order, j, 0, 0))
    return pl.pallas_call(
        _conv_fwd_body,
        grid=(b, h1 // fb),
        in_specs=[pl.BlockSpec(m1k.shape, lambda i, j: (0, 0)), mat, mat,
                  pl.BlockSpec((1, 1, h1, n2, c), lambda i, j: (i, sel, 0, 0, 0)),
                  kspec, kspec],
        out_specs=pl.BlockSpec((1, 2 * fb, n2, c), lambda i, j: (i, j, 0, 0)),
        out_shape=jax.ShapeDtypeStruct((b, 2 * h1, n2, c), BF16),
        scratch_shapes=[pltpu.VMEM((2 * h1, n2, c), F32)],
        compiler_params=_params("parallel", "arbitrary"),
        name="hyena_conv_fwd",
    )(m1k, m2, m2t, x, kr, ki)


def _conv_inv_body(m3k_ref, b_ref, gate_ref, skipin_ref, skip_ref, o_ref):
    n1, ts, c = b_ref.shape[1:]
    h1 = n1 // 2
    g = BF16_SUBLANES
    for s in range(ts // g):
        rows = slice(s * g, (s + 1) * g)
        bs = b_ref[0, :, rows, :].reshape(n1 * g, c)
        y = _bdot(m3k_ref[...], bs).reshape(h1, g, c)
        z = gate_ref[0, 0, :, rows, :] * (y + skipin_ref[0, 0, :, rows, :] * skip_ref[...])
        o_ref[0, 0, :, rows, :] = z.astype(o_ref.dtype)


def _conv_inv(m3k, bm, gates, gate_sel, skipin, skipin_sel, skip_row, out_dtype):
    b, n1, n2, c = bm.shape
    h1 = n1 // 2
    ts = min(n2, 2 * BF16_SUBLANES)
    tok = lambda sel: pl.BlockSpec((1, 1, h1, ts, c), lambda i, j: (i, sel, 0, j, 0))
    return pl.pallas_call(
        _conv_inv_body,
        grid=(b, n2 // ts),
        in_specs=[pl.BlockSpec(m3k.shape, lambda i, j: (0, 0)),
                  pl.BlockSpec((1, n1, ts, c), lambda i, j: (i, 0, j, 0)),
                  tok(gate_sel), tok(skipin_sel),
                  pl.BlockSpec((1, c), lambda i, j: (0, 0))],
        out_specs=tok(0),
        out_shape=jax.ShapeDtypeStruct((b, 1, h1, n2, c), out_dtype),
        compiler_params=_params("parallel", "parallel"),
        name="hyena_conv_inv",
    )(m3k, bm, gates, skipin, skip_row)


def _hyena_mixer(u, conv_w, conv_b, w1, b1, freq, w2, b2, w3, skip):
    b, seq, _ = u.shape
    c = D_HYENA
    n1, n2, m1k, m2, m2t, m3k = _hyena_consts(seq)
    h1 = n1 // 2
    fb = min(h1, 8)
    m1k, m2, m2t, m3k = (_mxu_const(a) for a in (m1k, m2, m2t, m3k))

    kt = _hyena_filters_time(seq, w1, b1, freq, w2, b2, w3)
    kr, ki = _filter_spectrum(m1k, m2, kt.reshape(4, 1, h1, n2, c), fb)

    u3 = _short_conv(u, conv_w, conv_b).reshape(b, 3, h1, n2, c)
    bm = _conv_fwd(m1k, m2, m2t, u3, 2, kr, ki, 0, fb)
    z = _conv_inv(m3k, bm, u3, 0, u3, 2, skip[0:1], F32)
    bm = _conv_fwd(m1k, m2, m2t, z, 0, kr, ki, 1, fb)
    y = _conv_inv(m3k, bm, u3, 1, z, 0, skip[1:2], BF16)
    return y.reshape(b, seq, c)


def _fnet_body(m1k_ref, m2_ref, pre_ref, pim_ref, bdc_ref, bds_ref, g_ref, o_ref, a_scr):
    j = pl.program_id(1)
    fb = m2_ref.shape[0]
    n1x2, n2, c = a_scr.shape
    n1 = n1x2 // 2

    @pl.when(j == 0)
    def _():
        gi = BF16_SUBLANES
        _stage1_into(m1k_ref, lambda r0: g_ref[0, :, pl.ds(r0, gi), :], a_scr, gi)

    for i in range(fb):
        f1 = j * fb + i
        a_scr[pl.ds(2 * f1, 2)] = _bdot(m2_ref[i], _stage2_rows(a_scr, f1)).reshape(2, n2, c)

    @pl.when(j == pl.num_programs(1) - 1)
    def _():
        go = F32_SUBLANES

        def body(s, carry):
            r0 = pl.multiple_of(s * go, go)
            spec = a_scr[:, pl.ds(r0, go), :].reshape(n1x2 * go, c).astype(BF16)
            xr = _bdot(pre_ref[...], spec).astype(BF16)
            xi = _bdot(pim_ref[...], spec).astype(BF16)
            out = _bdot(xr, bdc_ref[...]) + _bdot(xi, bds_ref[...])
            o_ref[0, pl.ds(pl.multiple_of(s * go * n1, go * n1), go * n1), :] = out.astype(o_ref.dtype)
            return carry

        lax.fori_loop(0, n2 // go, body, 0)


def _fourier_mix(g):
    b, seq, c = g.shape
    n1, n2, m1k, m2, pre, pim, bdc, bds = _fnet_consts(seq)
    m1k, m2, pre, pim, bdc, bds = (_mxu_const(a) for a in (m1k, m2, pre, pim, bdc, bds))
    fb = min(n1, 8)
    const = lambda a: pl.BlockSpec(a.shape, lambda i, j: (0,) * a.ndim)
    return pl.pallas_call(
        _fnet_body,
        grid=(b, n1 // fb),
        in_specs=[const(m1k),
                  pl.BlockSpec((fb, 2 * n2, 2 * n2), lambda i, j: (j, 0, 0)),
                  const(pre), const(pim), const(bdc), const(bds),
                  pl.BlockSpec((1, n1, n2, c), lambda i, j: (i, 0, 0, 0))],
        out_specs=pl.BlockSpec((1, seq, c), lambda i, j: (i, 0, 0)),
        out_shape=jax.ShapeDtypeStruct((b, seq, c), BF16),
        scratch_shapes=[pltpu.VMEM((2 * n1, n2, c), F32)],
        compiler_params=_params("parallel", "arbitrary"),
        name="fnet",
    )(m1k, m2, pre, pim, bdc, bds, g.reshape(b, n1, n2, c))


def _dot_nt(a, b):
    return lax.dot_general(a, b, (((1,), (1,)), ((), ())), preferred_element_type=F32)


def _head_pair_attention(q2, keys, values, biases):
    lane = lax.broadcasted_iota(jnp.int32, q2.shape, 1)
    scale = 1.0 / math.sqrt(HEAD_DIM)
    outs = []
    for hh in range(2):
        in_head = (lane >= hh * HEAD_DIM) & (lane < (hh + 1) * HEAD_DIM)
        qm = jnp.where(in_head, q2, jnp.zeros_like(q2))
        scores = []
        for kk, bias in zip(keys, biases[hh]):
            s = _dot_nt(qm, kk) * scale
            scores.append(s if bias is None else s + bias)
        m = scores[0].max(axis=-1, keepdims=True)
        for s in scores[1:]:
            m = jnp.maximum(m, s.max(axis=-1, keepdims=True))
        es = [jnp.exp(s - m) for s in scores]
        denom = es[0].sum(axis=-1, keepdims=True)
        for e in es[1:]:
            denom = denom + e.sum(axis=-1, keepdims=True)
        acc = _bdot(es[0].astype(BF16), values[0])
        for e, vv in zip(es[1:], values[1:]):
            acc = acc + _bdot(e.astype(BF16), vv)
        outs.append(acc / denom)
    return jnp.where(lane < HEAD_DIM, outs[0], outs[1])


def _na_window_base(row_block, rows):
    return jnp.clip(row_block * NA_QROWS - NA_KH // 2, 0, rows - NA_KROWS)


def _na_body(rows, q_ref, k_ref, v_ref, kc_ref, vc_ref, bias_ref, o_ref):
    base = _na_window_base(pl.program_id(1), rows)
    start = pl.multiple_of(base * GRID_W, GRID_W)
    n_keys = NA_KROWS * GRID_W
    n_pairs = q_ref.shape[2] // LANES
    for p in range(n_pairs):
        sl = slice(p * LANES, (p + 1) * LANES)
        q2 = q_ref[0, :, sl]
        k2 = k_ref[0, pl.ds(start, n_keys), sl]
        v2 = v_ref[0, pl.ds(start, n_keys), sl]
        kc2 = kc_ref[0, :, sl]
        vc2 = vc_ref[0, :, sl]
        biases = [[bias_ref[2 * p + hh, 0], None] for hh in range(2)]
        o = _head_pair_attention(q2, [k2, kc2], [v2, vc2], biases)
        o_ref[0, :, sl] = o.astype(o_ref.dtype)


def _na_bias_table(rpb, rows):
    kh, kw, w = NA_KH, NA_KW, GRID_W
    nh = rpb.shape[0]
    col = np.arange(w)
    col_start = np.clip(col - kw // 2, 0, w - kw)
    key = np.arange(w)
    valid = (key[None, :] >= col_start[:, None]) & (key[None, :] < col_start[:, None] + kw)
    lo = w - NA_KW
    padded = jnp.pad(rpb.astype(F32), ((0, 0), (0, 0), (lo, lo)))
    t1 = jnp.stack([padded[:, :, w - 1 - q:2 * w - 1 - q] for q in range(w)], axis=2)
    t1 = jnp.where(valid[None, None], t1, MASK_VALUE)
    pad_rows = NA_KROWS + kh
    t1 = jnp.pad(t1, ((0, 0), (pad_rows, pad_rows), (0, 0), (0, 0)), constant_values=MASK_VALUE)
    cases = []
    for r0 in (0, NA_QROWS, rows - NA_QROWS):
        base = int(np.clip(r0 - kh // 2, 0, rows - NA_KROWS))
        blocks = []
        for qr in range(NA_QROWS):
            q_abs = r0 + qr
            r_start = int(np.clip(q_abs - kh // 2, 0, rows - kh))
            a0 = base - q_abs + (kh - 1) + pad_rows
            k_abs = base + np.arange(NA_KROWS)
            row_valid = (k_abs >= r_start) & (k_abs < r_start + kh)
            blk = jnp.where(row_valid[None, :, None, None], t1[:, a0:a0 + NA_KROWS], MASK_VALUE)
            blocks.append(jnp.transpose(blk, (0, 2, 1, 3)).reshape(nh, w, NA_KROWS * w))
        cases.append(jnp.concatenate(blocks, axis=1))
    return jnp.stack(cases, axis=1)


def _neighbourhood_attention(qkv, kv_ctx, rpb):
    b, seq, _ = qkv.shape
    rows = seq // GRID_W
    lc = kv_ctx.shape[1]
    assert rows % NA_QROWS == 0 and rows >= NA_KROWS + 1 and NA_QROWS == NA_KH // 2
    n_blocks = rows // NA_QROWS
    bias = _na_bias_table(rpb, rows)
    nh = rpb.shape[0]
    tq = NA_QROWS * GRID_W

    def bias_map(i, r):
        return (0, jnp.where(r == 0, 0, jnp.where(r == n_blocks - 1, 2, 1)), 0, 0)

    return pl.pallas_call(
        functools.partial(_na_body, rows),
        grid=(b, n_blocks),
        in_specs=[pl.BlockSpec((1, tq, D_NA), lambda i, r: (i, r, 0)),
                  pl.BlockSpec((1, seq, D_NA), lambda i, r: (i, 0, 1)),
                  pl.BlockSpec((1, seq, D_NA), lambda i, r: (i, 0, 2)),
                  pl.BlockSpec((1, lc, D_NA), lambda i, r: (i, 0, 0)),
                  pl.BlockSpec((1, lc, D_NA), lambda i, r: (i, 0, 1)),
                  pl.BlockSpec((nh, 1, tq, NA_KROWS * GRID_W), bias_map)],
        out_specs=pl.BlockSpec((1, tq, D_NA), lambda i, r: (i, r, 0)),
        out_shape=jax.ShapeDtypeStruct((b, seq, D_NA), BF16),
        compiler_params=_params("parallel", "arbitrary"),
        name="neighbourhood_attention",
    )(qkv, qkv, qkv, kv_ctx, kv_ctx, bias)


def _ctx_attn_body(q_ref, k_ref, v_ref, o_ref):
    o = _head_pair_attention(q_ref[0], [k_ref[0]], [v_ref[0]], [[None], [None]])
    o_ref[0] = o.astype(o_ref.dtype)


def _context_attention(qkv):
    b, lc, _ = qkv.shape
    n_pairs = D_NA // LANES
    return pl.pallas_call(
        _ctx_attn_body,
        grid=(b, n_pairs),
        in_specs=[pl.BlockSpec((1, lc, LANES), lambda i, p: (i, 0, p)),
                  pl.BlockSpec((1, lc, LANES), lambda i, p: (i, 0, n_pairs + p)),
                  pl.BlockSpec((1, lc, LANES), lambda i, p: (i, 0, 2 * n_pairs + p))],
        out_specs=pl.BlockSpec((1, lc, LANES), lambda i, p: (i, 0, p)),
        out_shape=jax.ShapeDtypeStruct((b, lc, D_NA), BF16),
        compiler_params=_params("parallel", "parallel"),
        name="context_attention",
    )(qkv, qkv, qkv)


def kernel(x, c, ctx, c_ctx, w_mod, b_mod, norm1_g, norm2_g, w_in, w_out, hy_conv_w, hy_conv_b,
           filt_w1, filt_b1, filt_freq, filt_w2, filt_b2, filt_w3, hy_skip, na_rpb,
           w_gate, w_up, w_down, final_g):
    depth = w_mod.shape[0]
    b, seq, d = x.shape
    off_q = 3 * D_HYENA + D_FNET
    off_k = off_q + D_NA
    splits = (3 * D_HYENA, D_FNET, 3 * D_NA)
    dtypes = (F32, BF16, BF16)
    tm, tf = 512, 1408

    pad = (-(b + 1)) % 8
    c_all = jnp.concatenate([c, c_ctx[None], jnp.zeros((pad, d), F32)], axis=0)
    mod = _modulation(c_all, w_mod, b_mod)

    w_in_b, w_out_b = w_in.astype(BF16), w_out.astype(BF16)
    w_gate_b, w_up_b, w_down_b = w_gate.astype(BF16), w_up.astype(BF16), w_down.astype(BF16)
    fg = final_g.reshape(1, d)

    for l in range(depth):
        last = l == depth - 1
        m_lat = mod[l, :b].reshape(b, 1, 6, d)
        m_ctx = jnp.broadcast_to(mod[l, b].reshape(1, 1, 6, d), (b, 1, 6, d))
        sh1, sc1, g1, sh2, sc2, g2 = (m_lat[:, :, i] for i in range(6))
        csh1, csc1, cg1, csh2, csc2, cg2 = (m_ctx[:, :, i] for i in range(6))
        n1g, n2g = norm1_g[l].reshape(1, d), norm2_g[l].reshape(1, d)
        hy_params = (hy_conv_w[l], hy_conv_b[l], filt_w1[l], filt_b1[l], filt_freq[l],
                     filt_w2[l], filt_b2[l], filt_w3[l], hy_skip[l])

        hy, fn, qkv = _norm_proj(x, n1g, sc1, sh1, w_in_b[l], splits, dtypes, tm)
        if last:
            (kv_c,) = _norm_proj(ctx, n1g, csc1, csh1, w_in_b[l][:, off_k:], (2 * D_NA,), (BF16,), tm)
        else:
            chy, cfn, cqkv = _norm_proj(ctx, n1g, csc1, csh1, w_in_b[l], splits, dtypes, tm)
            kv_c = cqkv[:, :, D_NA:]

        y_hy = _hyena_mixer(hy, *hy_params)
        y_fn = _fourier_mix(fn)
        y_na = _neighbourhood_attention(qkv, kv_c, na_rpb[l])
        x = _proj_out(x, g1, y_hy, y_fn, y_na, w_out_b[l], tm)

        x = _ffn(x, n2g, sc2, sh2, g2, w_gate_b[l], w_up_b[l], w_down_b[l], fg, last, tm, tf)

        if not last:
            cy_hy = _hyena_mixer(chy, *hy_params)
            cy_fn = _fourier_mix(cfn)
            cy_na = _context_attention(cqkv)
            ctx = _proj_out(ctx, cg1, cy_hy, cy_fn, cy_na, w_out_b[l], tm)
            ctx = _ffn(ctx, n2g, csc2, csh2, cg2, w_gate_b[l], w_up_b[l], w_down_b[l], fg, False, tm, tf)
    return x
```

```python
import functools
import math

import numpy as np
import jax
import jax.numpy as jnp
from jax import lax
from jax.experimental import pallas as pl
from jax.experimental.pallas import tpu as pltpu

F32 = jnp.float32
BF16 = jnp.bfloat16

EPS = 1e-6
GRID_W = 64
HEAD_DIM = 64
D_HYENA = 256
D_FNET = 256
D_NA = 512
FNET_GROUP_DIM = 64
NA_KH = 8
NA_KW = 16
NA_QROWS = 4
NA_KROWS = NA_QROWS + NA_KH - 1
POS_BANDS = 16
HYENA_DECAY_TARGET = 1e-2
HYENA_FAST_PCT = 0.3
HYENA_SLOW_PCT = 1.5
MASK_VALUE = -1e30
VMEM_LIMIT = 56 * 1024 * 1024
LANES = 128
MXU_WIDTH = 256
LOG2E = math.log2(math.e)
F32_SUBLANES = 8
BF16_SUBLANES = 16


def _params(*sem):
    return pltpu.CompilerParams(dimension_semantics=sem, vmem_limit_bytes=VMEM_LIMIT)


def _bdot(a, b):
    return jnp.dot(a, b, preferred_element_type=F32)


def _split_bf16(a):
    hi = a.astype(BF16)
    lo = (a - hi.astype(F32)).astype(BF16)
    return hi, lo


def _dot3(a, b):
    a_hi, a_lo = _split_bf16(a)
    b_hi, b_lo = _split_bf16(b)
    return _bdot(a_hi, b_hi) + _bdot(a_lo, b_hi) + _bdot(a_hi, b_lo)


def _mod_body(c_ref, w_ref, b_ref, o_ref):
    c = c_ref[...]
    o_ref[0] = _dot3(c * jax.nn.sigmoid(c), w_ref[0]) + b_ref[0]


def _modulation(c_all, w_mod, b_mod):
    depth, d, n = w_mod.shape
    rows = c_all.shape[0]
    tn = 1536
    return pl.pallas_call(
        _mod_body,
        grid=(depth, n // tn),
        in_specs=[
            pl.BlockSpec((rows, d), lambda l, j: (0, 0)),
            pl.BlockSpec((1, d, tn), lambda l, j: (l, 0, j)),
            pl.BlockSpec((1, 1, tn), lambda l, j: (l, 0, j)),
        ],
        out_specs=pl.BlockSpec((1, rows, tn), lambda l, j: (l, 0, j)),
        out_shape=jax.ShapeDtypeStruct((depth, rows, n), F32),
        compiler_params=_params("parallel", "parallel"),
        name="modulation",
    )(c_all, w_mod, b_mod.reshape(depth, 1, n))


def _modulated_norm(x, g, sc, sh):
    ms = jnp.mean(x * x, axis=-1, keepdims=True)
    y = x * lax.rsqrt(ms + EPS) * g
    return y * (1.0 + sc) + sh


def _mod_spec(layer, row_of, chunk, d):
    return pl.BlockSpec((1, 1, 1, 1, d), lambda i, *_: (layer, row_of(i), chunk, 0, 0))


def _norm_proj_body(splits, x_ref, g_ref, sc_ref, sh_ref, w_ref, *o_refs):
    h = _modulated_norm(x_ref[0], g_ref[0], sc_ref[0, 0, 0], sh_ref[0, 0, 0])
    p = _bdot(h.astype(BF16), w_ref[0])
    off = 0
    for o_ref, (width, _, scale) in zip(o_refs, splits):
        part = p[:, off:off + width]
        o_ref[0] = (part if scale is None else part * scale).astype(o_ref.dtype)
        off += width


def _norm_proj(x, norm_g, mod, row_of, layer, w, w_idx, splits, tm):
    b, t, d = x.shape
    n = w.shape[2]
    tm = min(tm, t)
    return pl.pallas_call(
        functools.partial(_norm_proj_body, tuple(splits)),
        grid=(b, t // tm),
        in_specs=[
            pl.BlockSpec((1, tm, d), lambda i, j: (i, j, 0)),
            pl.BlockSpec((1, 1, d), lambda i, j: (layer, 0, 0)),
            _mod_spec(layer, row_of, 1, d),
            _mod_spec(layer, row_of, 0, d),
            pl.BlockSpec((1, d, n), lambda i, j: (w_idx, 0, 0)),
        ],
        out_specs=[pl.BlockSpec((1, tm, wd), lambda i, j: (i, j, 0)) for wd, _, _ in splits],
        out_shape=[jax.ShapeDtypeStruct((b, t, wd), dt) for wd, dt, _ in splits],
        compiler_params=_params("parallel", "parallel"),
        name="norm_proj",
    )(x, norm_g, mod, mod, w)


def _mix_ffn_body(final_norm, x_ref, hy_ref, fn_ref, na_ref, wo_ref, g1_ref, ng_ref, sc_ref, sh_ref, g2_ref,
                  wg_ref, wu_ref, wd_ref, fg_ref, o_ref):
    mix = (_bdot(hy_ref[0], wo_ref[0, :D_HYENA])
           + _bdot(fn_ref[0], wo_ref[0, D_HYENA:D_HYENA + D_FNET])
           + _bdot(na_ref[0], wo_ref[0, D_HYENA + D_FNET:]))
    x1 = x_ref[0] + g1_ref[0, 0, 0] * mix
    h = _modulated_norm(x1, ng_ref[0], sc_ref[0, 0, 0], sh_ref[0, 0, 0]).astype(BF16)
    acc = None
    for c0 in range(0, wg_ref.shape[2], MXU_WIDTH):
        cols = slice(c0, c0 + MXU_WIDTH)
        a = _bdot(h, wg_ref[0, :, cols])
        u = _bdot(h, wu_ref[0, :, cols])
        act = (a * jax.nn.sigmoid(a) * u).astype(BF16)
        part = _bdot(act, wd_ref[0, cols, :])
        acc = part if acc is None else acc + part
    y = x1 + g2_ref[0, 0, 0] * acc
    if final_norm:
        ms = jnp.mean(y * y, axis=-1, keepdims=True)
        y = y * lax.rsqrt(ms + EPS) * fg_ref[...]
    o_ref[0] = y


def _mix_ffn(x, y_hy, y_fn, y_na, w_out, norm_g, mod, row_of, layer, wg, wu, wd, final_g, final_norm, tm):
    b, t, d = x.shape
    dff = wg.shape[2]
    assert dff % MXU_WIDTH == 0
    tm = min(tm, t)
    tok = lambda wd_: pl.BlockSpec((1, tm, wd_), lambda i, j: (i, j, 0))
    resident = lambda shape: pl.BlockSpec((1,) + shape, lambda i, j: (layer, 0, 0), pipeline_mode=pl.Buffered(1))
    return pl.pallas_call(
        functools.partial(_mix_ffn_body, final_norm),
        grid=(b, t // tm),
        in_specs=[
            tok(d), tok(D_HYENA), tok(D_FNET), tok(D_NA),
            resident((d, d)),
            _mod_spec(layer, row_of, 2, d),
            pl.BlockSpec((1, 1, d), lambda i, j: (layer, 0, 0)),
            _mod_spec(layer, row_of, 4, d),
            _mod_spec(layer, row_of, 3, d),
            _mod_spec(layer, row_of, 5, d),
            resident((d, dff)), resident((d, dff)), resident((dff, d)),
            pl.BlockSpec((1, d), lambda i, j: (0, 0)),
        ],
        out_specs=tok(d),
        out_shape=jax.ShapeDtypeStruct(x.shape, F32),
        compiler_params=_params("parallel", "parallel"),
        name="mix_ffn",
    )(x, y_hy, y_fn, y_na, w_out, mod, norm_g, mod, mod, mod, wg, wu, wd, final_g)


def _split_seq(n_total):
    log = int(round(math.log2(n_total)))
    assert 2 ** log == n_total
    n1 = 2 ** (log // 2)
    return n1, n_total // n1


@functools.lru_cache(maxsize=None)
def _hyena_consts(seq):
    n = 2 * seq
    n1, n2 = _split_seq(n)
    h1 = n1 // 2
    f1 = np.arange(h1)[:, None]
    t1 = np.arange(h1)[None, :]
    th = np.pi * (2 * f1 + 1) * t1 / n1
    m1 = np.empty((n1, h1))
    m1[0::2] = np.cos(th)
    m1[1::2] = -np.sin(th)
    m3 = m1.T * (2.0 / n)
    f = (np.arange(h1)[:, None, None] + n1 * np.arange(n2)[None, :, None])
    t2 = np.arange(n2)[None, None, :]
    ph = np.pi * ((2 * f + 1) * t2 % (2 * n)) / n
    c, s = np.cos(ph), np.sin(ph)
    m2 = np.concatenate([np.concatenate([c, s], axis=2), np.concatenate([-s, c], axis=2)], axis=1)
    m2t = np.transpose(m2, (0, 2, 1))
    m1k = np.kron(m1, np.eye(F32_SUBLANES))
    m3k = np.kron(m3, np.eye(BF16_SUBLANES))
    return n1, n2, m1k, m2, m2t, m3k


@functools.lru_cache(maxsize=None)
def _fnet_consts(seq):
    n1, n2 = _split_seq(seq)
    n1, n2 = n1 // 2, n2 * 2
    f1 = np.arange(n1)[:, None]
    t1 = np.arange(n1)[None, :]
    th = 2 * np.pi * (f1 * t1 % n1) / n1
    m1 = np.empty((2 * n1, n1))
    m1[0::2] = np.cos(th)
    m1[1::2] = -np.sin(th)
    f = (np.arange(n1)[:, None, None] + n1 * np.arange(n2)[None, :, None])
    t2 = np.arange(n2)[None, None, :]
    ph = 2 * np.pi * (f * t2 % seq) / seq
    c, s = np.cos(ph), np.sin(ph)
    m2 = np.concatenate([np.concatenate([c, s], axis=2), np.concatenate([-s, c], axis=2)], axis=1)
    gd = FNET_GROUP_DIM
    cc = np.arange(gd)
    pg = 2 * np.pi * (np.outer(cc, cc) % gd) / gd
    scale = 1.0 / math.sqrt(gd * seq)
    eye = np.eye(D_FNET // gd)
    bdc = np.kron(eye, np.cos(pg)) * scale
    bds = np.kron(eye, np.sin(pg)) * scale
    m1k = np.kron(m1, np.eye(BF16_SUBLANES))
    g = F32_SUBLANES
    pre = np.zeros((g * n1, 2 * n1 * g))
    pim = np.zeros((g * n1, 2 * n1 * g))
    for f2 in range(g):
        for ff in range(n1):
            pre[f2 * n1 + ff, (2 * ff) * g + f2] = 1.0
            pim[f2 * n1 + ff, (2 * ff + 1) * g + f2] = 1.0
    return n1, n2, m1k, m2, pre, pim, bdc, bds


def _mxu_const(a):
    return jnp.asarray(a, dtype=F32).astype(BF16)


def _stage1_into(mk_ref, x_at, a_scr, group):
    n_out, n2, c = a_scr.shape

    def body(s, carry):
        r0 = pl.multiple_of(s * group, group)
        xs = x_at(r0)
        xs = xs.reshape(xs.shape[0] * group, c).astype(BF16)
        a_scr[:, pl.ds(r0, group), :] = _bdot(mk_ref[...], xs).reshape(n_out, group, c)
        return carry

    lax.fori_loop(0, n2 // group, body, 0, unroll=2)


def _short_conv_body(u_ref, w_ref, b_ref, o_ref):
    u = u_ref[0]
    seq = u.shape[0]
    row = lax.broadcasted_iota(jnp.int32, u.shape, 0)
    prev = jnp.where(row == 0, 0.0, pltpu.roll(u, 1, axis=0))
    nxt = jnp.where(row == seq - 1, 0.0, pltpu.roll(u, seq - 1, axis=0))
    o_ref[0, 0] = prev * w_ref[0:1] + u * w_ref[1:2] + nxt * w_ref[2:3] + b_ref[...]


def _short_conv(u, w, bias):
    b, seq, c3 = u.shape
    c = D_HYENA
    return pl.pallas_call(
        _short_conv_body,
        grid=(b, c3 // c),
        in_specs=[pl.BlockSpec((1, seq, c), lambda i, j: (i, 0, j)),
                  pl.BlockSpec((3, c), lambda i, j: (0, j)),
                  pl.BlockSpec((1, c), lambda i, j: (0, j))],
        out_specs=pl.BlockSpec((1, 1, seq, c), lambda i, j: (i, j, 0, 0)),
        out_shape=jax.ShapeDtypeStruct((b, c3 // c, seq, c), F32),
        compiler_params=_params("parallel", "parallel"),
        name="short_conv",
    )(u, w, bias.reshape(1, c3))


def _filter_body(z_ref, w1_ref, b1_ref, fr_ref, w2_ref, b2_ref, w3_ref, dl_ref, o_ref):
    z = z_ref[...]
    h = jnp.sin(fr_ref[0:1] * (_dot3(z, w1_ref[...]) + b1_ref[...]))
    h = jnp.sin(fr_ref[1:2] * (_dot3(h, w2_ref[...]) + b2_ref[...]))
    k = _dot3(h, w3_ref[0])
    c = D_HYENA
    t = z[:, 0:1]
    decay = jnp.exp(-t * jnp.abs(dl_ref[...]))
    kf = k[:, :c] * decay
    row = lax.broadcasted_iota(jnp.int32, kf.shape, 0)
    kb = jnp.where(row == 0, 0.0, k[:, c:] * decay)
    norm = jnp.sum(jnp.abs(kf), axis=0, keepdims=True) + jnp.sum(jnp.abs(kb), axis=0, keepdims=True)
    inv = 1.0 / norm
    o_ref[0, 0] = kf * inv
    o_ref[1, 0] = kb * inv


def _hyena_filters_time(seq, w1, b1, freq, w2, b2, w3):
    c = D_HYENA
    t = np.linspace(0.0, 1.0, seq)[:, None]
    w = 2.0 * np.pi * np.arange(seq)[:, None] / seq
    f = np.linspace(1e-4, POS_BANDS - 1, POS_BANDS)[None]
    z = np.concatenate([t, np.cos(f * w), -np.sin(f * w)], axis=-1).astype(np.float32)
    pe = z.shape[1]
    z = jnp.asarray(np.pad(z, ((0, 0), (0, LANES - pe))))
    w1p = jnp.pad(w1, ((0, LANES - pe), (0, 0)))
    max_decay = math.log(HYENA_DECAY_TARGET) / HYENA_FAST_PCT
    min_decay = math.log(HYENA_DECAY_TARGET) / HYENA_SLOW_PCT
    deltas = jnp.asarray(np.linspace(min_decay, max_decay, c, dtype=np.float32)[None])
    hid = w1.shape[1]
    w3r = jnp.transpose(w3.reshape(hid, 2, 2 * c), (1, 0, 2))
    full = lambda a: pl.BlockSpec(a.shape, lambda o: (0,) * a.ndim)
    b1r, b2r = b1.reshape(1, hid), b2.reshape(1, hid)
    return pl.pallas_call(
        _filter_body,
        grid=(2,),
        in_specs=[full(z), full(w1p), full(b1r), full(freq), full(w2), full(b2r),
                  pl.BlockSpec((1, hid, 2 * c), lambda o: (o, 0, 0)), full(deltas)],
        out_specs=pl.BlockSpec((2, 1, seq, c), lambda o: (o, 0, 0, 0)),
        out_shape=jax.ShapeDtypeStruct((4, 1, seq, c), F32),
        compiler_params=_params("parallel"),
        name="hyena_filter_time",
    )(z, w1p, b1r, freq, w2, b2r, w3r, deltas)


def _stage2_rows(a_scr, f1):
    _, n2, c = a_scr.shape
    return a_scr[pl.ds(2 * f1, 2)].reshape(2 * n2, c).astype(BF16)


def _filter_spec_body(m1k_ref, m2_ref, kf_ref, kb_ref, kr_ref, ki_ref, af_scr, ab_scr):
    j = pl.program_id(1)
    fb = m2_ref.shape[0]
    n2 = kr_ref.shape[2]
    g = F32_SUBLANES

    @pl.when(j == 0)
    def _():
        _stage1_into(m1k_ref, lambda r0: kf_ref[0, 0, :, pl.ds(r0, g), :], af_scr, g)
        _stage1_into(m1k_ref, lambda r0: kb_ref[0, 0, :, pl.ds(r0, g), :], ab_scr, g)

    for i in range(fb):
        f1 = j * fb + i
        xf = _bdot(m2_ref[i], _stage2_rows(af_scr, f1))
        xb = _bdot(m2_ref[i], _stage2_rows(ab_scr, f1))
        kr_ref[0, i] = xf[:n2] + xb[:n2]
        ki_ref[0, i] = xf[n2:] - xb[n2:]


def _filter_spectrum(m1k, m2, kt, fb):
    _, _, h1, n2, c = kt.shape
    out = jax.ShapeDtypeStruct((2, h1, n2, c), F32)
    return pl.pallas_call(
        _filter_spec_body,
        grid=(2, h1 // fb),
        in_specs=[pl.BlockSpec(m1k.shape, lambda o, j: (0, 0)),
                  pl.BlockSpec((fb, 2 * n2, 2 * n2), lambda o, j: (j, 0, 0)),
                  pl.BlockSpec((1, 1, h1, n2, c), lambda o, j: (2 * o, 0, 0, 0, 0)),
                  pl.BlockSpec((1, 1, h1, n2, c), lambda o, j: (2 * o + 1, 0, 0, 0, 0))],
        out_specs=[pl.BlockSpec((1, fb, n2, c), lambda o, j: (o, j, 0, 0))] * 2,
        out_shape=[out, out],
        scratch_shapes=[pltpu.VMEM((2 * h1, n2, c), F32)] * 2,
        compiler_params=_params("parallel", "arbitrary"),
        name="hyena_filter_spectrum",
    )(m1k, m2, kt, kt)


def _conv_fwd_body(m1k_ref, m2_ref, m2t_ref, x_ref, kr_ref, ki_ref, o_ref, a_scr):
    n1, n2, c = a_scr.shape
    g = F32_SUBLANES
    _stage1_into(m1k_ref, lambda r0: x_ref[0, 0, :, pl.ds(r0, g), :], a_scr, g)

    def per_f1(f1, carry):
        x = _bdot(m2_ref[f1], _stage2_rows(a_scr, f1))
        xr, xi = x[:n2], x[n2:]
        kr, ki = kr_ref[0, f1], ki_ref[0, f1]
        y = jnp.concatenate([kr * xr - ki * xi, kr * xi + ki * xr], axis=0).astype(BF16)
        o_ref[0, pl.ds(2 * f1, 2)] = _bdot(m2t_ref[f1], y).astype(BF16).reshape(2, n2, c)
        return carry

    lax.fori_loop(0, n1 // 2, per_f1, 0, unroll=8)


def _conv_fwd(m1k, m2, m2t, x, sel, kr, ki, order):
    b, _, h1, n2, c = x.shape
    resident = lambda a, idx: pl.BlockSpec(a.shape, lambda i: idx, pipeline_mode=pl.Buffered(1))
    kspec = pl.BlockSpec((1, h1, n2, c), lambda i: (order, 0, 0, 0), pipeline_mode=pl.Buffered(1))
    return pl.pallas_call(
        _conv_fwd_body,
        grid=(b,),
        in_specs=[resident(m1k, (0, 0)), resident(m2, (0, 0, 0)), resident(m2t, (0, 0, 0)),
                  pl.BlockSpec((1, 1, h1, n2, c), lambda i: (i, sel, 0, 0, 0)),
                  kspec, kspec],
        out_specs=pl.BlockSpec((1, 2 * h1, n2, c), lambda i: (i, 0, 0, 0)),
        out_shape=jax.ShapeDtypeStruct((b, 2 * h1, n2, c), BF16),
        scratch_shapes=[pltpu.VMEM((2 * h1, n2, c), F32)],
        compiler_params=_params("parallel"),
        name="hyena_conv_fwd",
    )(m1k, m2, m2t, x, kr, ki)


def _conv_inv_body(m3k_ref, b_ref, gate_ref, skipin_ref, skip_ref, o_ref):
    n1, ts, c = b_ref.shape[1:]
    h1 = n1 // 2
    g = BF16_SUBLANES
    for s in range(ts // g):
        rows = slice(s * g, (s + 1) * g)
        bs = b_ref[0, :, rows, :].reshape(n1 * g, c)
        y = _bdot(m3k_ref[...], bs).reshape(h1, g, c)
        z = gate_ref[0, 0, :, rows, :] * (y + skipin_ref[0, 0, :, rows, :] * skip_ref[...])
        o_ref[0, 0, :, rows, :] = z.astype(o_ref.dtype)


def _conv_inv(m3k, bm, gates, gate_sel, skipin, skipin_sel, skip_row, out_dtype):
    b, n1, n2, c = bm.shape
    h1 = n1 // 2
    ts = min(n2, 2 * BF16_SUBLANES)
    tok = lambda sel: pl.BlockSpec((1, 1, h1, ts, c), lambda i, j: (i, sel, 0, j, 0))
    return pl.pallas_call(
        _conv_inv_body,
        grid=(b, n2 // ts),
        in_specs=[pl.BlockSpec(m3k.shape, lambda i, j: (0, 0)),
                  pl.BlockSpec((1, n1, ts, c), lambda i, j: (i, 0, j, 0)),
                  tok(gate_sel), tok(skipin_sel),
                  pl.BlockSpec((1, c), lambda i, j: (0, 0))],
        out_specs=tok(0),
        out_shape=jax.ShapeDtypeStruct((b, 1, h1, n2, c), out_dtype),
        compiler_params=_params("parallel", "parallel"),
        name="hyena_conv_inv",
    )(m3k, bm, gates, skipin, skip_row)


def _hyena_mixer(u, conv_w, conv_b, w1, b1, freq, w2, b2, w3, skip):
    b, seq, _ = u.shape
    c = D_HYENA
    n1, n2, m1k, m2, m2t, m3k = _hyena_consts(seq)
    h1 = n1 // 2
    fb = min(h1, 8)
    m1k, m2, m2t, m3k = (_mxu_const(a) for a in (m1k, m2, m2t, m3k))

    kt = _hyena_filters_time(seq, w1, b1, freq, w2, b2, w3)
    kr, ki = _filter_spectrum(m1k, m2, kt.reshape(4, 1, h1, n2, c), fb)

    u3 = _short_conv(u, conv_w, conv_b).reshape(b, 3, h1, n2, c)
    bm = _conv_fwd(m1k, m2, m2t, u3, 2, kr, ki, 0)
    z = _conv_inv(m3k, bm, u3, 0, u3, 2, skip[0:1], F32)
    bm = _conv_fwd(m1k, m2, m2t, z, 0, kr, ki, 1)
    y = _conv_inv(m3k, bm, u3, 1, z, 0, skip[1:2], BF16)
    return y.reshape(b, seq, c)


def _fnet_body(m1k_ref, m2_ref, pre_ref, pim_ref, bdc_ref, bds_ref, g_ref, o_ref, a_scr):
    j = pl.program_id(1)
    fb = m2_ref.shape[0]
    n1x2, n2, c = a_scr.shape
    n1 = n1x2 // 2

    @pl.when(j == 0)
    def _():
        gi = BF16_SUBLANES
        _stage1_into(m1k_ref, lambda r0: g_ref[0, :, pl.ds(r0, gi), :], a_scr, gi)

    for i in range(fb):
        f1 = j * fb + i
        a_scr[pl.ds(2 * f1, 2)] = _bdot(m2_ref[i], _stage2_rows(a_scr, f1)).reshape(2, n2, c)

    @pl.when(j == pl.num_programs(1) - 1)
    def _():
        go = F32_SUBLANES

        def body(s, carry):
            r0 = pl.multiple_of(s * go, go)
            spec = a_scr[:, pl.ds(r0, go), :].reshape(n1x2 * go, c).astype(BF16)
            xr = _bdot(pre_ref[...], spec).astype(BF16)
            xi = _bdot(pim_ref[...], spec).astype(BF16)
            out = _bdot(xr, bdc_ref[...]) + _bdot(xi, bds_ref[...])
            o_ref[0, pl.ds(pl.multiple_of(s * go * n1, go * n1), go * n1), :] = out.astype(o_ref.dtype)
            return carry

        lax.fori_loop(0, n2 // go, body, 0)


def _fourier_mix(g):
    b, seq, c = g.shape
    n1, n2, m1k, m2, pre, pim, bdc, bds = _fnet_consts(seq)
    m1k, m2, pre, pim, bdc, bds = (_mxu_const(a) for a in (m1k, m2, pre, pim, bdc, bds))
    fb = min(n1, 8)
    const = lambda a: pl.BlockSpec(a.shape, lambda i, j: (0,) * a.ndim)
    return pl.pallas_call(
        _fnet_body,
        grid=(b, n1 // fb),
        in_specs=[const(m1k),
                  pl.BlockSpec((fb, 2 * n2, 2 * n2), lambda i, j: (j, 0, 0)),
                  const(pre), const(pim), const(bdc), const(bds),
                  pl.BlockSpec((1, n1, n2, c), lambda i, j: (i, 0, 0, 0))],
        out_specs=pl.BlockSpec((1, seq, c), lambda i, j: (i, 0, 0)),
        out_shape=jax.ShapeDtypeStruct((b, seq, c), BF16),
        scratch_shapes=[pltpu.VMEM((2 * n1, n2, c), F32)],
        compiler_params=_params("parallel", "arbitrary"),
        name="fnet",
    )(m1k, m2, pre, pim, bdc, bds, g.reshape(b, n1, n2, c))


def _dot_nt(a, b):
    return lax.dot_general(a, b, (((1,), (1,)), ((), ())), preferred_element_type=F32)


def _head_pair_attention(q2, keys, values, biases):
    lane = lax.broadcasted_iota(jnp.int32, q2.shape, 1)
    outs = []
    for hh in range(2):
        in_head = (lane >= hh * HEAD_DIM) & (lane < (hh + 1) * HEAD_DIM)
        qm = jnp.where(in_head, q2, jnp.zeros_like(q2))
        scores = []
        for kk, bias in zip(keys, biases[hh]):
            s = _dot_nt(qm, kk)
            scores.append(s if bias is None else s + bias)
        m = scores[0].max(axis=-1, keepdims=True)
        for s in scores[1:]:
            m = jnp.maximum(m, s.max(axis=-1, keepdims=True))
        es = [jnp.exp2(s - m) for s in scores]
        denom = es[0].sum(axis=-1, keepdims=True)
        for e in es[1:]:
            denom = denom + e.sum(axis=-1, keepdims=True)
        acc = _bdot(es[0].astype(BF16), values[0])
        for e, vv in zip(es[1:], values[1:]):
            acc = acc + _bdot(e.astype(BF16), vv)
        outs.append(acc / denom)
    return jnp.where(lane < HEAD_DIM, outs[0], outs[1])


def _na_window_base(row_block, rows):
    return jnp.clip(row_block * NA_QROWS - NA_KH // 2, 0, rows - NA_KROWS)


def _na_body(rows, q_ref, k_ref, v_ref, kc_ref, vc_ref, bias_ref, o_ref):
    base = _na_window_base(pl.program_id(1), rows)
    start = pl.multiple_of(base * GRID_W, GRID_W)
    n_keys = NA_KROWS * GRID_W
    n_pairs = q_ref.shape[2] // LANES
    for p in range(n_pairs):
        sl = slice(p * LANES, (p + 1) * LANES)
        q2 = q_ref[0, :, sl]
        k2 = k_ref[0, pl.ds(start, n_keys), sl]
        v2 = v_ref[0, pl.ds(start, n_keys), sl]
        kc2 = kc_ref[0, :, sl]
        vc2 = vc_ref[0, :, sl]
        biases = [[bias_ref[0, 2 * p + hh, 0], None] for hh in range(2)]
        o = _head_pair_attention(q2, [k2, kc2], [v2, vc2], biases)
        o_ref[0, :, sl] = o.astype(o_ref.dtype)


def _na_bias_tables(rpb, rows):
    kh, kw, w = NA_KH, NA_KW, GRID_W
    depth, nh = rpb.shape[:2]
    col = np.arange(w)
    col_start = np.clip(col - kw // 2, 0, w - kw)
    key = np.arange(w)
    valid = (key[None, :] >= col_start[:, None]) & (key[None, :] < col_start[:, None] + kw)
    lo = w - NA_KW
    padded = jnp.pad(rpb.astype(F32) * LOG2E, ((0, 0), (0, 0), (0, 0), (lo, lo)))
    t1 = jnp.stack([padded[..., w - 1 - q:2 * w - 1 - q] for q in range(w)], axis=2)
    t1 = jnp.where(valid[:, None, :], t1, MASK_VALUE)
    pad_rows = NA_KROWS + kh
    t1 = jnp.pad(t1, ((0, 0), (0, 0), (0, 0), (pad_rows, pad_rows), (0, 0)), constant_values=MASK_VALUE)
    blocks = []
    for r0 in (0, NA_QROWS, rows - NA_QROWS):
        base = int(np.clip(r0 - kh // 2, 0, rows - NA_KROWS))
        for qr in range(NA_QROWS):
            q_abs = r0 + qr
            r_start = int(np.clip(q_abs - kh // 2, 0, rows - kh))
            a0 = base - q_abs + (kh - 1) + pad_rows
            k_abs = base + np.arange(NA_KROWS)
            row_valid = (k_abs >= r_start) & (k_abs < r_start + kh)
            blk = jnp.where(row_valid[:, None], t1[:, :, :, a0:a0 + NA_KROWS], MASK_VALUE)
            blocks.append(blk.reshape(depth, nh, w, NA_KROWS * w))
    return jnp.stack(blocks, axis=2).reshape(depth, nh, 3, NA_QROWS * w, NA_KROWS * w)


def _neighbourhood_attention(q, kv, kv_ctx, bias, layer):
    b, seq, _ = q.shape
    rows = seq // GRID_W
    lc = kv_ctx.shape[1]
    assert rows % NA_QROWS == 0 and rows >= NA_KROWS + 1 and NA_QROWS == NA_KH // 2
    n_blocks = rows // NA_QROWS
    nh = bias.shape[1]
    tq = NA_QROWS * GRID_W

    def bias_map(i, r):
        return (layer, 0, jnp.where(r == 0, 0, jnp.where(r == n_blocks - 1, 2, 1)), 0, 0)

    return pl.pallas_call(
        functools.partial(_na_body, rows),
        grid=(b, n_blocks),
        in_specs=[pl.BlockSpec((1, tq, D_NA), lambda i, r: (i, r, 0)),
                  pl.BlockSpec((1, seq, D_NA), lambda i, r: (i, 0, 0)),
                  pl.BlockSpec((1, seq, D_NA), lambda i, r: (i, 0, 1)),
                  pl.BlockSpec((1, lc, D_NA), lambda i, r: (i, 0, 0)),
                  pl.BlockSpec((1, lc, D_NA), lambda i, r: (i, 0, 1)),
                  pl.BlockSpec((1, nh, 1, tq, NA_KROWS * GRID_W), bias_map)],
        out_specs=pl.BlockSpec((1, tq, D_NA), lambda i, r: (i, r, 0)),
        out_shape=jax.ShapeDtypeStruct((b, seq, D_NA), BF16),
        compiler_params=_params("parallel", "arbitrary"),
        name="neighbourhood_attention",
    )(q, kv, kv, kv_ctx, kv_ctx, bias)


def _ctx_attn_body(q_ref, k_ref, v_ref, o_ref):
    o = _head_pair_attention(q_ref[0], [k_ref[0]], [v_ref[0]], [[None], [None]])
    o_ref[0] = o.astype(o_ref.dtype)


def _context_attention(q, kv):
    b, lc, _ = q.shape
    n_pairs = D_NA // LANES
    return pl.pallas_call(
        _ctx_attn_body,
        grid=(b, n_pairs),
        in_specs=[pl.BlockSpec((1, lc, LANES), lambda i, p: (i, 0, p)),
                  pl.BlockSpec((1, lc, LANES), lambda i, p: (i, 0, p)),
                  pl.BlockSpec((1, lc, LANES), lambda i, p: (i, 0, n_pairs + p))],
        out_specs=pl.BlockSpec((1, lc, LANES), lambda i, p: (i, 0, p)),
        out_shape=jax.ShapeDtypeStruct((b, lc, D_NA), BF16),
        compiler_params=_params("parallel", "parallel"),
        name="context_attention",
    )(q, kv, kv)


def kernel(x, c, ctx, c_ctx, w_mod, b_mod, norm1_g, norm2_g, w_in, w_out, hy_conv_w, hy_conv_b,
           filt_w1, filt_b1, filt_freq, filt_w2, filt_b2, filt_w3, hy_skip, na_rpb,
           w_gate, w_up, w_down, final_g):
    depth = w_mod.shape[0]
    b, seq, d = x.shape
    off_k = 3 * D_HYENA + D_FNET + D_NA
    qk_scale = LOG2E / math.sqrt(HEAD_DIM)
    splits = ((3 * D_HYENA, F32, None), (D_FNET, BF16, None), (D_NA, BF16, qk_scale), (2 * D_NA, BF16, None))
    tm = 512

    pad = (-(b + 1)) % 8
    c_all = jnp.concatenate([c, c_ctx[None], jnp.zeros((pad, d), F32)], axis=0)
    mod = _modulation(c_all, w_mod, b_mod).reshape(depth, b + 1 + pad, 6, 1, d)
    lat_row = lambda i: i
    ctx_row = lambda i: b

    w_in_b, w_out_b = w_in.astype(BF16), w_out.astype(BF16)
    w_gate_b, w_up_b, w_down_b = w_gate.astype(BF16), w_up.astype(BF16), w_down.astype(BF16)
    w_kv_last = w_in_b[depth - 1:, :, off_k:]
    n1g, n2g = norm1_g.reshape(depth, 1, d), norm2_g.reshape(depth, 1, d)
    fg = final_g.reshape(1, d)
    na_bias = _na_bias_tables(na_rpb, seq // GRID_W)

    for l in range(depth):
        last = l == depth - 1
        hy_params = (hy_conv_w[l], hy_conv_b[l], filt_w1[l], filt_b1[l], filt_freq[l],
                     filt_w2[l], filt_b2[l], filt_w3[l], hy_skip[l])

        hy, fn, q, kv = _norm_proj(x, n1g, mod, lat_row, l, w_in_b, l, splits, tm)
        if last:
            (kv_c,) = _norm_proj(ctx, n1g, mod, ctx_row, l, w_kv_last, 0, ((2 * D_NA, BF16, None),), tm)
        else:
            chy, cfn, cq, kv_c = _norm_proj(ctx, n1g, mod, ctx_row, l, w_in_b, l, splits, tm)

        y_hy = _hyena_mixer(hy, *hy_params)
        y_fn = _fourier_mix(fn)
        y_na = _neighbourhood_attention(q, kv, kv_c, na_bias, l)

        x = _mix_ffn(x, y_hy, y_fn, y_na, w_out_b, n2g, mod, lat_row, l, w_gate_b, w_up_b, w_down_b,
                     fg, last, tm)

        if not last:
            cy_hy = _hyena_mixer(chy, *hy_params)
            cy_fn = _fourier_mix(cfn)
            cy_na = _context_attention(cq, kv_c)
            ctx = _mix_ffn(ctx, cy_hy, cy_fn, cy_na, w_out_b, n2g, mod, ctx_row, l, w_gate_b, w_up_b, w_down_b,
                           fg, False, tm)
    return x
```

```python
import functools
import math

import numpy as np
import jax
import jax.numpy as jnp
from jax import lax
from jax.experimental import pallas as pl
from jax.experimental.pallas import tpu as pltpu

F32 = jnp.float32
BF16 = jnp.bfloat16

EPS = 1e-6
GRID_W = 64
HEAD_DIM = 64
D_HYENA = 256
D_FNET = 256
D_NA = 512
FNET_GROUP_DIM = 64
NA_KH = 8
NA_KW = 16
NA_QROWS = 4
NA_KROWS = NA_QROWS + NA_KH
NA_KEY_GROUP = 256
POS_BANDS = 16
HYENA_DECAY_TARGET = 1e-2
HYENA_FAST_PCT = 0.3
HYENA_SLOW_PCT = 1.5
MASK_VALUE = -1e30
VMEM_LIMIT = 56 * 1024 * 1024
LANES = 128
MXU_WIDTH = 256
LOG2E = math.log2(math.e)
F32_SUBLANES = 8
BF16_SUBLANES = 16


def _params(*sem):
    return pltpu.CompilerParams(dimension_semantics=sem, vmem_limit_bytes=VMEM_LIMIT)


def _bdot(a, b):
    return jnp.dot(a, b, preferred_element_type=F32)


def _split_bf16(a):
    hi = a.astype(BF16)
    lo = (a - hi.astype(F32)).astype(BF16)
    return hi, lo


def _dot3(a, b):
    a_hi, a_lo = _split_bf16(a)
    b_hi, b_lo = _split_bf16(b)
    return _bdot(a_hi, b_hi) + _bdot(a_lo, b_hi) + _bdot(a_hi, b_lo)


def _mod_body(c_ref, w_ref, b_ref, o_ref):
    c = c_ref[...]
    o_ref[0] = _dot3(c * jax.nn.sigmoid(c), w_ref[0]) + b_ref[0]


def _modulation(c_all, w_mod, b_mod):
    depth, d, n = w_mod.shape
    rows = c_all.shape[0]
    tn = 1536
    return pl.pallas_call(
        _mod_body,
        grid=(depth, n // tn),
        in_specs=[
            pl.BlockSpec((rows, d), lambda l, j: (0, 0)),
            pl.BlockSpec((1, d, tn), lambda l, j: (l, 0, j)),
            pl.BlockSpec((1, 1, tn), lambda l, j: (l, 0, j)),
        ],
        out_specs=pl.BlockSpec((1, rows, tn), lambda l, j: (l, 0, j)),
        out_shape=jax.ShapeDtypeStruct((depth, rows, n), F32),
        compiler_params=_params("parallel", "parallel"),
        name="modulation",
    )(c_all, w_mod, b_mod.reshape(depth, 1, n))


def _modulated_norm(x, g, sc, sh):
    ms = jnp.mean(x * x, axis=-1, keepdims=True)
    y = x * lax.rsqrt(ms + EPS) * g
    return y * (1.0 + sc) + sh


def _mod_spec(layer, row_of, chunk, d):
    return pl.BlockSpec((1, 1, 1, 1, d), lambda i, *_: (layer, row_of(i), chunk, 0, 0))


def _norm_proj_body(splits, x_ref, g_ref, sc_ref, sh_ref, w_ref, *o_refs):
    h = _modulated_norm(x_ref[0], g_ref[0], sc_ref[0, 0, 0], sh_ref[0, 0, 0])
    p = _bdot(h.astype(BF16), w_ref[0])
    off = 0
    for o_ref, (width, _, scale) in zip(o_refs, splits):
        part = p[:, off:off + width]
        o_ref[0] = (part if scale is None else part * scale).astype(o_ref.dtype)
        off += width


def _norm_proj(x, norm_g, mod, row_of, layer, w, w_idx, splits, tm):
    b, t, d = x.shape
    n = w.shape[2]
    tm = min(tm, t)
    return pl.pallas_call(
        functools.partial(_norm_proj_body, tuple(splits)),
        grid=(b, t // tm),
        in_specs=[
            pl.BlockSpec((1, tm, d), lambda i, j: (i, j, 0)),
            pl.BlockSpec((1, 1, d), lambda i, j: (layer, 0, 0)),
            _mod_spec(layer, row_of, 1, d),
            _mod_spec(layer, row_of, 0, d),
            pl.BlockSpec((1, d, n), lambda i, j: (w_idx, 0, 0)),
        ],
        out_specs=[pl.BlockSpec((1, tm, wd), lambda i, j: (i, j, 0)) for wd, _, _ in splits],
        out_shape=[jax.ShapeDtypeStruct((b, t, wd), dt) for wd, dt, _ in splits],
        compiler_params=_params("parallel", "parallel"),
        name="norm_proj",
    )(x, norm_g, mod, mod, w)


def _mix_ffn_body(final_norm, x_ref, hy_ref, fn_ref, na_ref, wo_ref, g1_ref, ng_ref, sc_ref, sh_ref, g2_ref,
                  wg_ref, wu_ref, wd_ref, fg_ref, o_ref):
    mix = (_bdot(hy_ref[0], wo_ref[0, :D_HYENA])
           + _bdot(fn_ref[0], wo_ref[0, D_HYENA:D_HYENA + D_FNET])
           + _bdot(na_ref[0], wo_ref[0, D_HYENA + D_FNET:]))
    x1 = x_ref[0] + g1_ref[0, 0, 0] * mix
    h = _modulated_norm(x1, ng_ref[0], sc_ref[0, 0, 0], sh_ref[0, 0, 0]).astype(BF16)
    acc = None
    for c0 in range(0, wg_ref.shape[2], MXU_WIDTH):
        cols = slice(c0, c0 + MXU_WIDTH)
        a = _bdot(h, wg_ref[0, :, cols])
        u = _bdot(h, wu_ref[0, :, cols])
        act = (a * jax.nn.sigmoid(a) * u).astype(BF16)
        part = _bdot(act, wd_ref[0, cols, :])
        acc = part if acc is None else acc + part
    y = x1 + g2_ref[0, 0, 0] * acc
    if final_norm:
        ms = jnp.mean(y * y, axis=-1, keepdims=True)
        y = y * lax.rsqrt(ms + EPS) * fg_ref[...]
    o_ref[0] = y


def _mix_ffn(x, y_hy, y_fn, y_na, w_out, norm_g, mod, row_of, layer, wg, wu, wd, final_g, final_norm, tm):
    b, t, d = x.shape
    dff = wg.shape[2]
    assert dff % MXU_WIDTH == 0
    tm = min(tm, t)
    tok = lambda wd_: pl.BlockSpec((1, tm, wd_), lambda i, j: (i, j, 0))
    resident = lambda shape: pl.BlockSpec((1,) + shape, lambda i, j: (layer, 0, 0), pipeline_mode=pl.Buffered(1))
    return pl.pallas_call(
        functools.partial(_mix_ffn_body, final_norm),
        grid=(b, t // tm),
        in_specs=[
            tok(d), tok(D_HYENA), tok(D_FNET), tok(D_NA),
            resident((d, d)),
            _mod_spec(layer, row_of, 2, d),
            pl.BlockSpec((1, 1, d), lambda i, j: (layer, 0, 0)),
            _mod_spec(layer, row_of, 4, d),
            _mod_spec(layer, row_of, 3, d),
            _mod_spec(layer, row_of, 5, d),
            resident((d, dff)), resident((d, dff)), resident((dff, d)),
            pl.BlockSpec((1, d), lambda i, j: (0, 0)),
        ],
        out_specs=tok(d),
        out_shape=jax.ShapeDtypeStruct(x.shape, F32),
        compiler_params=_params("parallel", "parallel"),
        name="mix_ffn",
    )(x, y_hy, y_fn, y_na, w_out, mod, norm_g, mod, mod, mod, wg, wu, wd, final_g)


def _split_seq(n_total):
    log = int(round(math.log2(n_total)))
    assert 2 ** log == n_total
    n1 = 2 ** (log // 2)
    return n1, n_total // n1


@functools.lru_cache(maxsize=None)
def _hyena_consts(seq):
    n = 2 * seq
    n1, n2 = _split_seq(n)
    h1 = n1 // 2
    f1 = np.arange(h1)[:, None]
    t1 = np.arange(h1)[None, :]
    th = np.pi * (2 * f1 + 1) * t1 / n1
    m1 = np.empty((n1, h1))
    m1[0::2] = np.cos(th)
    m1[1::2] = -np.sin(th)
    m3 = m1.T * (2.0 / n)
    f = (np.arange(h1)[:, None, None] + n1 * np.arange(n2)[None, :, None])
    t2 = np.arange(n2)[None, None, :]
    ph = np.pi * ((2 * f + 1) * t2 % (2 * n)) / n
    c, s = np.cos(ph), np.sin(ph)
    m2 = np.concatenate([np.concatenate([c, s], axis=2), np.concatenate([-s, c], axis=2)], axis=1)
    m2t = np.transpose(m2, (0, 2, 1))
    m1k = np.kron(m1, np.eye(F32_SUBLANES))
    m3k = np.kron(m3, np.eye(BF16_SUBLANES))
    return n1, n2, m1k, m2, m2t, m3k


@functools.lru_cache(maxsize=None)
def _fnet_consts(seq):
    n1, n2 = _split_seq(seq)
    n1, n2 = n1 // 2, n2 * 2
    f1 = np.arange(n1)[:, None]
    t1 = np.arange(n1)[None, :]
    th = 2 * np.pi * (f1 * t1 % n1) / n1
    m1 = np.empty((2 * n1, n1))
    m1[0::2] = np.cos(th)
    m1[1::2] = -np.sin(th)
    f = (np.arange(n1)[:, None, None] + n1 * np.arange(n2)[None, :, None])
    t2 = np.arange(n2)[None, None, :]
    ph = 2 * np.pi * (f * t2 % seq) / seq
    c, s = np.cos(ph), np.sin(ph)
    m2 = np.concatenate([np.concatenate([c, s], axis=2), np.concatenate([-s, c], axis=2)], axis=1)
    gd = FNET_GROUP_DIM
    cc = np.arange(gd)
    pg = 2 * np.pi * (np.outer(cc, cc) % gd) / gd
    scale = 1.0 / math.sqrt(gd * seq)
    eye = np.eye(D_FNET // gd)
    bdc = np.kron(eye, np.cos(pg)) * scale
    bds = np.kron(eye, np.sin(pg)) * scale
    m1k = np.kron(m1, np.eye(BF16_SUBLANES))
    g = F32_SUBLANES
    pre = np.zeros((g * n1, 2 * n1 * g))
    pim = np.zeros((g * n1, 2 * n1 * g))
    for f2 in range(g):
        for ff in range(n1):
            pre[f2 * n1 + ff, (2 * ff) * g + f2] = 1.0
            pim[f2 * n1 + ff, (2 * ff + 1) * g + f2] = 1.0
    return n1, n2, m1k, m2, pre, pim, bdc, bds


def _mxu_const(a):
    return jnp.asarray(a, dtype=F32).astype(BF16)


def _stage1_into(mk_ref, x_at, a_scr, group):
    n_out, n2, c = a_scr.shape

    def body(s, carry):
        r0 = pl.multiple_of(s * group, group)
        xs = x_at(r0)
        xs = xs.reshape(xs.shape[0] * group, c).astype(BF16)
        a_scr[:, pl.ds(r0, group), :] = _bdot(mk_ref[...], xs).reshape(n_out, group, c)
        return carry

    lax.fori_loop(0, n2 // group, body, 0, unroll=2)


def _short_conv_body(u_ref, w_ref, b_ref, o_ref):
    u = u_ref[0]
    seq = u.shape[0]
    row = lax.broadcasted_iota(jnp.int32, u.shape, 0)
    prev = jnp.where(row == 0, 0.0, pltpu.roll(u, 1, axis=0))
    nxt = jnp.where(row == seq - 1, 0.0, pltpu.roll(u, seq - 1, axis=0))
    o_ref[0, 0] = prev * w_ref[0:1] + u * w_ref[1:2] + nxt * w_ref[2:3] + b_ref[...]


def _short_conv(u, w, bias):
    b, seq, c3 = u.shape
    c = D_HYENA
    return pl.pallas_call(
        _short_conv_body,
        grid=(b, c3 // c),
        in_specs=[pl.BlockSpec((1, seq, c), lambda i, j: (i, 0, j)),
                  pl.BlockSpec((3, c), lambda i, j: (0, j)),
                  pl.BlockSpec((1, c), lambda i, j: (0, j))],
        out_specs=pl.BlockSpec((1, 1, seq, c), lambda i, j: (i, j, 0, 0)),
        out_shape=jax.ShapeDtypeStruct((b, c3 // c, seq, c), F32),
        compiler_params=_params("parallel", "parallel"),
        name="short_conv",
    )(u, w, bias.reshape(1, c3))


def _filter_body(zt_ref, t_ref, w1t_ref, b1_ref, fr_ref, w2t_ref, b2_ref, w3_ref, dl_ref, o_ref, h_scr):
    @pl.when(pl.program_id(0) == 0)
    def _():
        h = jnp.sin(fr_ref[:, 0:1] * (_dot3(w1t_ref[...], zt_ref[...]) + b1_ref[...]))
        h = jnp.sin(fr_ref[:, 1:2] * (_dot3(w2t_ref[...], h) + b2_ref[...]))
        h_scr[...] = h.T

    k = _dot3(h_scr[...], w3_ref[0])
    c = D_HYENA
    t = t_ref[...]
    decay = jnp.exp(-t * jnp.abs(dl_ref[...]))
    kf = k[:, :c] * decay
    row = lax.broadcasted_iota(jnp.int32, kf.shape, 0)
    kb = jnp.where(row == 0, 0.0, k[:, c:] * decay)
    norm = jnp.sum(jnp.abs(kf), axis=0, keepdims=True) + jnp.sum(jnp.abs(kb), axis=0, keepdims=True)
    inv = 1.0 / norm
    o_ref[0, 0] = kf * inv
    o_ref[1, 0] = kb * inv


def _hyena_filters_time(seq, w1, b1, freq, w2, b2, w3):
    c = D_HYENA
    t = np.linspace(0.0, 1.0, seq)[:, None]
    w = 2.0 * np.pi * np.arange(seq)[:, None] / seq
    f = np.linspace(1e-4, POS_BANDS - 1, POS_BANDS)[None]
    z = np.concatenate([t, np.cos(f * w), -np.sin(f * w)], axis=-1).astype(np.float32)
    pe = z.shape[1]
    zt = jnp.asarray(np.pad(z, ((0, 0), (0, LANES - pe))).T)
    tcol = jnp.asarray(t.astype(np.float32))
    w1t = jnp.pad(w1, ((0, LANES - pe), (0, 0))).T
    max_decay = math.log(HYENA_DECAY_TARGET) / HYENA_FAST_PCT
    min_decay = math.log(HYENA_DECAY_TARGET) / HYENA_SLOW_PCT
    deltas = jnp.asarray(np.linspace(min_decay, max_decay, c, dtype=np.float32)[None])
    hid = w1.shape[1]
    w3r = jnp.transpose(w3.reshape(hid, 2, 2 * c), (1, 0, 2))
    full = lambda a: pl.BlockSpec(a.shape, lambda o: (0,) * a.ndim)
    b1c, b2c, frt = b1.reshape(hid, 1), b2.reshape(hid, 1), freq.T
    return pl.pallas_call(
        _filter_body,
        grid=(2,),
        in_specs=[full(zt), full(tcol), full(w1t), full(b1c), full(frt), full(w2.T), full(b2c),
                  pl.BlockSpec((1, hid, 2 * c), lambda o: (o, 0, 0)), full(deltas)],
        out_specs=pl.BlockSpec((2, 1, seq, c), lambda o: (o, 0, 0, 0)),
        out_shape=jax.ShapeDtypeStruct((4, 1, seq, c), F32),
        scratch_shapes=[pltpu.VMEM((seq, hid), F32)],
        compiler_params=_params("arbitrary"),
        name="hyena_filter_time",
    )(zt, tcol, w1t, b1c, frt, w2.T, b2c, w3r, deltas)


def _stage2_rows(a_scr, f1):
    _, n2, c = a_scr.shape
    return a_scr[pl.ds(2 * f1, 2)].reshape(2 * n2, c).astype(BF16)


def _filter_spec_body(m1k_ref, m2_ref, kf_ref, kb_ref, kr_ref, ki_ref, af_scr, ab_scr):
    j = pl.program_id(1)
    fb = m2_ref.shape[0]
    n2 = kr_ref.shape[2]
    g = F32_SUBLANES

    @pl.when(j == 0)
    def _():
        _stage1_into(m1k_ref, lambda r0: kf_ref[0, 0, :, pl.ds(r0, g), :], af_scr, g)
        _stage1_into(m1k_ref, lambda r0: kb_ref[0, 0, :, pl.ds(r0, g), :], ab_scr, g)

    for i in range(fb):
        f1 = j * fb + i
        xf = _bdot(m2_ref[i], _stage2_rows(af_scr, f1))
        xb = _bdot(m2_ref[i], _stage2_rows(ab_scr, f1))
        kr_ref[0, i] = xf[:n2] + xb[:n2]
        ki_ref[0, i] = xf[n2:] - xb[n2:]


def _filter_spectrum(m1k, m2, kt, fb):
    _, _, h1, n2, c = kt.shape
    out = jax.ShapeDtypeStruct((2, h1, n2, c), F32)
    return pl.pallas_call(
        _filter_spec_body,
        grid=(2, h1 // fb),
        in_specs=[pl.BlockSpec(m1k.shape, lambda o, j: (0, 0)),
                  pl.BlockSpec((fb, 2 * n2, 2 * n2), lambda o, j: (j, 0, 0)),
                  pl.BlockSpec((1, 1, h1, n2, c), lambda o, j: (2 * o, 0, 0, 0, 0)),
                  pl.BlockSpec((1, 1, h1, n2, c), lambda o, j: (2 * o + 1, 0, 0, 0, 0))],
        out_specs=[pl.BlockSpec((1, fb, n2, c), lambda o, j: (o, j, 0, 0))] * 2,
        out_shape=[out, out],
        scratch_shapes=[pltpu.VMEM((2 * h1, n2, c), F32)] * 2,
        compiler_params=_params("parallel", "arbitrary"),
        name="hyena_filter_spectrum",
    )(m1k, m2, kt, kt)


def _conv_fwd_body(m1k_ref, m2_ref, m2t_ref, x_ref, kr_ref, ki_ref, o_ref, a_scr):
    n1, n2, c = a_scr.shape
    g = F32_SUBLANES
    _stage1_into(m1k_ref, lambda r0: x_ref[0, 0, :, pl.ds(r0, g), :], a_scr, g)

    def per_f1(f1, carry):
        x = _bdot(m2_ref[f1], _stage2_rows(a_scr, f1))
        xr, xi = x[:n2], x[n2:]
        kr, ki = kr_ref[0, f1], ki_ref[0, f1]
        y = jnp.concatenate([kr * xr - ki * xi, kr * xi + ki * xr], axis=0).astype(BF16)
        o_ref[0, pl.ds(2 * f1, 2)] = _bdot(m2t_ref[f1], y).astype(BF16).reshape(2, n2, c)
        return carry

    lax.fori_loop(0, n1 // 2, per_f1, 0, unroll=8)


def _conv_fwd(m1k, m2, m2t, x, sel, kr, ki, order):
    b, _, h1, n2, c = x.shape
    resident = lambda a, idx: pl.BlockSpec(a.shape, lambda i: idx, pipeline_mode=pl.Buffered(1))
    kspec = pl.BlockSpec((1, h1, n2, c), lambda i: (order, 0, 0, 0), pipeline_mode=pl.Buffered(1))
    return pl.pallas_call(
        _conv_fwd_body,
        grid=(b,),
        in_specs=[resident(m1k, (0, 0)), resident(m2, (0, 0, 0)), resident(m2t, (0, 0, 0)),
                  pl.BlockSpec((1, 1, h1, n2, c), lambda i: (i, sel, 0, 0, 0)),
                  kspec, kspec],
        out_specs=pl.BlockSpec((1, 2 * h1, n2, c), lambda i: (i, 0, 0, 0)),
        out_shape=jax.ShapeDtypeStruct((b, 2 * h1, n2, c), BF16),
        scratch_shapes=[pltpu.VMEM((2 * h1, n2, c), F32)],
        compiler_params=_params("parallel"),
        name="hyena_conv_fwd",
    )(m1k, m2, m2t, x, kr, ki)


def _conv_inv_body(m3k_ref, b_ref, gate_ref, skipin_ref, skip_ref, o_ref):
    n1, ts, c = b_ref.shape[1:]
    h1 = n1 // 2
    g = BF16_SUBLANES
    for s in range(ts // g):
        rows = slice(s * g, (s + 1) * g)
        bs = b_ref[0, :, rows, :].reshape(n1 * g, c)
        y = _bdot(m3k_ref[...], bs).reshape(h1, g, c)
        z = gate_ref[0, 0, :, rows, :] * (y + skipin_ref[0, 0, :, rows, :] * skip_ref[...])
        o_ref[0, 0, :, rows, :] = z.astype(o_ref.dtype)


def _conv_inv(m3k, bm, gates, gate_sel, skipin, skipin_sel, skip_row, out_dtype):
    b, n1, n2, c = bm.shape
    h1 = n1 // 2
    ts = min(n2, 2 * BF16_SUBLANES)
    tok = lambda sel: pl.BlockSpec((1, 1, h1, ts, c), lambda i, j: (i, sel, 0, j, 0))
    return pl.pallas_call(
        _conv_inv_body,
        grid=(b, n2 // ts),
        in_specs=[pl.BlockSpec(m3k.shape, lambda i, j: (0, 0)),
                  pl.BlockSpec((1, n1, ts, c), lambda i, j: (i, 0, j, 0)),
                  tok(gate_sel), tok(skipin_sel),
                  pl.BlockSpec((1, c), lambda i, j: (0, 0))],
        out_specs=tok(0),
        out_shape=jax.ShapeDtypeStruct((b, 1, h1, n2, c), out_dtype),
        compiler_params=_params("parallel", "parallel"),
        name="hyena_conv_inv",
    )(m3k, bm, gates, skipin, skip_row)


def _hyena_mixer(u, conv_w, conv_b, w1, b1, freq, w2, b2, w3, skip):
    b, seq, _ = u.shape
    c = D_HYENA
    n1, n2, m1k, m2, m2t, m3k = _hyena_consts(seq)
    h1 = n1 // 2
    fb = min(h1, 8)
    m1k, m2, m2t, m3k = (_mxu_const(a) for a in (m1k, m2, m2t, m3k))

    kt = _hyena_filters_time(seq, w1, b1, freq, w2, b2, w3)
    kr, ki = _filter_spectrum(m1k, m2, kt.reshape(4, 1, h1, n2, c), fb)

    u3 = _short_conv(u, conv_w, conv_b).reshape(b, 3, h1, n2, c)
    bm = _conv_fwd(m1k, m2, m2t, u3, 2, kr, ki, 0)
    z = _conv_inv(m3k, bm, u3, 0, u3, 2, skip[0:1], F32)
    bm = _conv_fwd(m1k, m2, m2t, z, 0, kr, ki, 1)
    y = _conv_inv(m3k, bm, u3, 1, z, 0, skip[1:2], BF16)
    return y.reshape(b, seq, c)


def _fnet_body(m1k_ref, m2_ref, pre_ref, pim_ref, bdc_ref, bds_ref, g_ref, o_ref, a_scr):
    j = pl.program_id(1)
    fb = m2_ref.shape[0]
    n1x2, n2, c = a_scr.shape
    n1 = n1x2 // 2

    @pl.when(j == 0)
    def _():
        gi = BF16_SUBLANES
        _stage1_into(m1k_ref, lambda r0: g_ref[0, :, pl.ds(r0, gi), :], a_scr, gi)

    for i in range(fb):
        f1 = j * fb + i
        a_scr[pl.ds(2 * f1, 2)] = _bdot(m2_ref[i], _stage2_rows(a_scr, f1)).reshape(2, n2, c)

    @pl.when(j == pl.num_programs(1) - 1)
    def _():
        go = F32_SUBLANES

        def body(s, carry):
            r0 = pl.multiple_of(s * go, go)
            spec = a_scr[:, pl.ds(r0, go), :].reshape(n1x2 * go, c).astype(BF16)
            xr = _bdot(pre_ref[...], spec).astype(BF16)
            xi = _bdot(pim_ref[...], spec).astype(BF16)
            out = _bdot(xr, bdc_ref[...]) + _bdot(xi, bds_ref[...])
            o_ref[0, pl.ds(pl.multiple_of(s * go * n1, go * n1), go * n1), :] = out.astype(o_ref.dtype)
            return carry

        lax.fori_loop(0, n2 // go, body, 0, unroll=2)


def _fourier_mix(g):
    b, seq, c = g.shape
    n1, n2, m1k, m2, pre, pim, bdc, bds = _fnet_consts(seq)
    m1k, m2, pre, pim, bdc, bds = (_mxu_const(a) for a in (m1k, m2, pre, pim, bdc, bds))
    fb = min(n1, 8)
    const = lambda a: pl.BlockSpec(a.shape, lambda i, j: (0,) * a.ndim)
    return pl.pallas_call(
        _fnet_body,
        grid=(b, n1 // fb),
        in_specs=[const(m1k),
                  pl.BlockSpec((fb, 2 * n2, 2 * n2), lambda i, j: (j, 0, 0)),
                  const(pre), const(pim), const(bdc), const(bds),
                  pl.BlockSpec((1, n1, n2, c), lambda i, j: (i, 0, 0, 0))],
        out_specs=pl.BlockSpec((1, seq, c), lambda i, j: (i, 0, 0)),
        out_shape=jax.ShapeDtypeStruct((b, seq, c), BF16),
        scratch_shapes=[pltpu.VMEM((2 * n1, n2, c), F32)],
        compiler_params=_params("parallel", "arbitrary"),
        name="fnet",
    )(m1k, m2, pre, pim, bdc, bds, g.reshape(b, n1, n2, c))


def _dot_nt(a, b):
    return lax.dot_general(a, b, (((1,), (1,)), ((), ())), preferred_element_type=F32)


def _head_pair_attention(q2, keys, values, biases):
    lane = lax.broadcasted_iota(jnp.int32, q2.shape, 1)
    outs = []
    for hh in range(2):
        in_head = (lane >= hh * HEAD_DIM) & (lane < (hh + 1) * HEAD_DIM)
        qm = jnp.where(in_head, q2, jnp.zeros_like(q2))
        m = denom = acc = None
        for kk, vv, bias in zip(keys, values, biases[hh]):
            s = _dot_nt(qm, kk)
            if bias is not None:
                s = s + bias
            m_grp = s.max(axis=-1, keepdims=True)
            m_new = m_grp if m is None else jnp.maximum(m, m_grp)
            e = jnp.exp2(s - m_new)
            e_part = e[:, :LANES]
            for c0 in range(LANES, e.shape[1], LANES):
                e_part = e_part + e[:, c0:c0 + LANES]
            pv = _bdot(e.astype(BF16), vv)
            if m is None:
                denom, acc = e_part, pv
            else:
                alpha = jnp.exp2(m - m_new)
                denom = alpha * denom + e_part
                acc = alpha * acc + pv
            m = m_new
        outs.append(acc / denom.sum(axis=-1, keepdims=True))
    return jnp.where(lane < HEAD_DIM, outs[0], outs[1])


def _na_window_base(row_block, rows):
    return jnp.clip(row_block * NA_QROWS - NA_KH // 2, 0, rows - NA_KROWS)


def _na_body(rows, q_ref, k_ref, v_ref, kc_ref, vc_ref, bias_ref, o_ref):
    base = _na_window_base(pl.program_id(1), rows)
    start = pl.multiple_of(base * GRID_W, GRID_W)
    n_keys = NA_KROWS * GRID_W
    n_pairs = q_ref.shape[2] // LANES
    groups = [(c0, min(NA_KEY_GROUP, n_keys - c0)) for c0 in range(0, n_keys, NA_KEY_GROUP)]
    for p in range(n_pairs):
        sl = slice(p * LANES, (p + 1) * LANES)
        q2 = q_ref[0, :, sl]
        keys = [kc_ref[0, :, sl]] + [k_ref[0, pl.ds(start + c0, n), sl] for c0, n in groups]
        values = [vc_ref[0, :, sl]] + [v_ref[0, pl.ds(start + c0, n), sl] for c0, n in groups]
        biases = [[None] + [bias_ref[0, 2 * p + hh, 0, :, c0:c0 + n] for c0, n in groups] for hh in range(2)]
        o = _head_pair_attention(q2, keys, values, biases)
        o_ref[0, :, sl] = o.astype(o_ref.dtype)


def _na_bias_body(rows, p_ref, o_ref):
    kh, kw, w = NA_KH, NA_KW, GRID_W
    shape = (w, 2 * w)
    qc = lax.broadcasted_iota(jnp.int32, shape, 0)
    lane = lax.broadcasted_iota(jnp.int32, shape, 1)
    kc = lane & (w - 1)
    col_start = jnp.clip(qc - kw // 2, 0, w - kw)
    col_valid = (kc >= col_start) & (kc < col_start + kw)
    low_half = lane < w
    masked = jnp.full(shape, MASK_VALUE, F32)
    tiles = []
    for a in range(2 * kh - 1):
        row = jnp.broadcast_to(p_ref[0, 0, a:a + 1, :], shape)
        t = pltpu.roll(row, w + 1, 1, stride=1, stride_axis=0)
        t = jnp.where(low_half, t, pltpu.roll(t, w, 1))
        tiles.append(jnp.where(col_valid, t, masked))
    for case, r0 in enumerate((0, NA_QROWS, rows - NA_QROWS)):
        base = int(np.clip(r0 - kh // 2, 0, rows - NA_KROWS))
        for qr in range(NA_QROWS):
            q_abs = r0 + qr
            r_start = int(np.clip(q_abs - kh // 2, 0, rows - kh))

            def tile_for(kr):
                k_abs = base + kr
                if kr >= NA_KROWS or not (r_start <= k_abs < r_start + kh):
                    return masked
                return tiles[k_abs - q_abs + kh - 1]

            q_rows = slice(qr * w, (qr + 1) * w)
            for pair in range((NA_KROWS + 1) // 2):
                chunk = jnp.where(low_half, tile_for(2 * pair), tile_for(2 * pair + 1))
                width = min(2 * w, NA_KROWS * w - pair * 2 * w)
                o_ref[0, 0, case, q_rows, pair * 2 * w:pair * 2 * w + width] = chunk[:, :width]


def _na_bias_tables(rpb, rows):
    w = GRID_W
    depth, nh, na, nb = rpb.shape
    assert 2 * w == LANES and nb == 2 * NA_KW - 1
    lo = w - NA_KW
    padded = jnp.pad(rpb.astype(F32) * LOG2E, ((0, 0), (0, 0), (0, (-na) % 8), (lo, 2 * w - nb - lo)))
    return pl.pallas_call(
        functools.partial(_na_bias_body, rows),
        grid=(depth, nh),
        in_specs=[pl.BlockSpec((1, 1) + padded.shape[2:], lambda l, h: (l, h, 0, 0))],
        out_specs=pl.BlockSpec((1, 1, 3, NA_QROWS * w, NA_KROWS * w), lambda l, h: (l, h, 0, 0, 0)),
        out_shape=jax.ShapeDtypeStruct((depth, nh, 3, NA_QROWS * w, NA_KROWS * w), F32),
        compiler_params=_params("parallel", "parallel"),
        name="na_bias_tables",
    )(padded)


def _neighbourhood_attention(q, kv, kv_ctx, bias, layer):
    b, seq, _ = q.shape
    rows = seq // GRID_W
    lc = kv_ctx.shape[1]
    assert rows % NA_QROWS == 0 and rows >= NA_KROWS + 1 and NA_QROWS == NA_KH // 2
    n_blocks = rows // NA_QROWS
    nh = bias.shape[1]
    tq = NA_QROWS * GRID_W

    def bias_map(i, r):
        return (layer, 0, jnp.where(r == 0, 0, jnp.where(r == n_blocks - 1, 2, 1)), 0, 0)

    return pl.pallas_call(
        functools.partial(_na_body, rows),
        grid=(b, n_blocks),
        in_specs=[pl.BlockSpec((1, tq, D_NA), lambda i, r: (i, r, 0)),
                  pl.BlockSpec((1, seq, D_NA), lambda i, r: (i, 0, 0)),
                  pl.BlockSpec((1, seq, D_NA), lambda i, r: (i, 0, 1)),
                  pl.BlockSpec((1, lc, D_NA), lambda i, r: (i, 0, 0)),
                  pl.BlockSpec((1, lc, D_NA), lambda i, r: (i, 0, 1)),
                  pl.BlockSpec((1, nh, 1, tq, NA_KROWS * GRID_W), bias_map)],
        out_specs=pl.BlockSpec((1, tq, D_NA), lambda i, r: (i, r, 0)),
        out_shape=jax.ShapeDtypeStruct((b, seq, D_NA), BF16),
        compiler_params=_params("parallel", "arbitrary"),
        name="neighbourhood_attention",
    )(q, kv, kv, kv_ctx, kv_ctx, bias)


def _ctx_attn_body(q_ref, k_ref, v_ref, o_ref):
    o = _head_pair_attention(q_ref[0], [k_ref[0]], [v_ref[0]], [[None], [None]])
    o_ref[0] = o.astype(o_ref.dtype)


def _context_attention(q, kv):
    b, lc, _ = q.shape
    n_pairs = D_NA // LANES
    return pl.pallas_call(
        _ctx_attn_body,
        grid=(b, n_pairs),
        in_specs=[pl.BlockSpec((1, lc, LANES), lambda i, p: (i, 0, p)),
                  pl.BlockSpec((1, lc, LANES), lambda i, p: (i, 0, p)),
                  pl.BlockSpec((1, lc, LANES), lambda i, p: (i, 0, n_pairs + p))],
        out_specs=pl.BlockSpec((1, lc, LANES), lambda i, p: (i, 0, p)),
        out_shape=jax.ShapeDtypeStruct((b, lc, D_NA), BF16),
        compiler_params=_params("parallel", "parallel"),
        name="context_attention",
    )(q, kv, kv)


def kernel(x, c, ctx, c_ctx, w_mod, b_mod, norm1_g, norm2_g, w_in, w_out, hy_conv_w, hy_conv_b,
           filt_w1, filt_b1, filt_freq, filt_w2, filt_b2, filt_w3, hy_skip, na_rpb,
           w_gate, w_up, w_down, final_g):
    depth = w_mod.shape[0]
    b, seq, d = x.shape
    off_k = 3 * D_HYENA + D_FNET + D_NA
    qk_scale = LOG2E / math.sqrt(HEAD_DIM)
    splits = ((3 * D_HYENA, F32, None), (D_FNET, BF16, None), (D_NA, BF16, qk_scale), (2 * D_NA, BF16, None))
    tm = 512

    pad = (-(b + 1)) % 8
    c_all = jnp.concatenate([c, c_ctx[None], jnp.zeros((pad, d), F32)], axis=0)
    mod = _modulation(c_all, w_mod, b_mod).reshape(depth, b + 1 + pad, 6, 1, d)
    lat_row = lambda i: i
    ctx_row = lambda i: b

    w_in_b, w_out_b = w_in.astype(BF16), w_out.astype(BF16)
    w_gate_b, w_up_b, w_down_b = w_gate.astype(BF16), w_up.astype(BF16), w_down.astype(BF16)
    w_kv_last = w_in_b[depth - 1:, :, off_k:]
    n1g, n2g = norm1_g.reshape(depth, 1, d), norm2_g.reshape(depth, 1, d)
    fg = final_g.reshape(1, d)
    na_bias = _na_bias_tables(na_rpb, seq // GRID_W)

    for l in range(depth):
        last = l == depth - 1
        hy_params = (hy_conv_w[l], hy_conv_b[l], filt_w1[l], filt_b1[l], filt_freq[l],
                     filt_w2[l], filt_b2[l], filt_w3[l], hy_skip[l])

        hy, fn, q, kv = _norm_proj(x, n1g, mod, lat_row, l, w_in_b, l, splits, tm)
        if last:
            (kv_c,) = _norm_proj(ctx, n1g, mod, ctx_row, l, w_kv_last, 0, ((2 * D_NA, BF16, None),), tm)
        else:
            chy, cfn, cq, kv_c = _norm_proj(ctx, n1g, mod, ctx_row, l, w_in_b, l, splits, tm)

        y_hy = _hyena_mixer(hy, *hy_params)
        y_fn = _fourier_mix(fn)
        y_na = _neighbourhood_attention(q, kv, kv_c, na_bias, l)

        x = _mix_ffn(x, y_hy, y_fn, y_na, w_out_b, n2g, mod, lat_row, l, w_gate_b, w_up_b, w_down_b,
                     fg, last, tm)

        if not last:
            cy_hy = _hyena_mixer(chy, *hy_params)
            cy_fn = _fourier_mix(cfn)
            cy_na = _context_attention(cq, kv_c)
            ctx = _mix_ffn(ctx, cy_hy, cy_fn, cy_na, w_out_b, n2g, mod, ctx_row, l, w_gate_b, w_up_b, w_down_b,
                           fg, False, tm)
    return x
```

```python
import functools
import math

import numpy as np
import jax
import jax.numpy as jnp
from jax import lax
from jax.experimental import pallas as pl
from jax.experimental.pallas import tpu as pltpu

F32 = jnp.float32
BF16 = jnp.bfloat16

EPS = 1e-6
GRID_W = 64
HEAD_DIM = 64
D_HYENA = 256
D_FNET = 256
D_NA = 512
FNET_GROUP_DIM = 64
NA_KH = 8
NA_KW = 16
NA_QROWS = 4
NA_KROWS = NA_QROWS + NA_KH
NA_KEY_GROUP = 256
POS_BANDS = 16
HYENA_DECAY_TARGET = 1e-2
HYENA_FAST_PCT = 0.3
HYENA_SLOW_PCT = 1.5
MASK_VALUE = -1e30
VMEM_LIMIT = 56 * 1024 * 1024
LANES = 128
MXU_WIDTH = 256
LOG2E = math.log2(math.e)
F32_SUBLANES = 8
BF16_SUBLANES = 16


def _params(*sem):
    return pltpu.CompilerParams(dimension_semantics=sem, vmem_limit_bytes=VMEM_LIMIT)


def _bdot(a, b):
    return jnp.dot(a, b, preferred_element_type=F32)


def _split_bf16(a):
    hi = a.astype(BF16)
    lo = (a - hi.astype(F32)).astype(BF16)
    return hi, lo


def _dot3(a, b):
    a_hi, a_lo = _split_bf16(a)
    b_hi, b_lo = _split_bf16(b)
    return _bdot(a_hi, b_hi) + _bdot(a_lo, b_hi) + _bdot(a_hi, b_lo)


def _mod_body(c_ref, w_ref, b_ref, o_ref):
    c = c_ref[...]
    o_ref[0] = _dot3(c * jax.nn.sigmoid(c), w_ref[0]) + b_ref[0]


def _modulation(c_all, w_mod, b_mod):
    depth, d, n = w_mod.shape
    rows = c_all.shape[0]
    tn = 1536
    return pl.pallas_call(
        _mod_body,
        grid=(depth, n // tn),
        in_specs=[
            pl.BlockSpec((rows, d), lambda l, j: (0, 0)),
            pl.BlockSpec((1, d, tn), lambda l, j: (l, 0, j)),
            pl.BlockSpec((1, 1, tn), lambda l, j: (l, 0, j)),
        ],
        out_specs=pl.BlockSpec((1, rows, tn), lambda l, j: (l, 0, j)),
        out_shape=jax.ShapeDtypeStruct((depth, rows, n), F32),
        compiler_params=_params("parallel", "parallel"),
        name="modulation",
    )(c_all, w_mod, b_mod.reshape(depth, 1, n))


def _modulated_norm(x, g, sc, sh):
    ms = jnp.mean(x * x, axis=-1, keepdims=True)
    y = x * lax.rsqrt(ms + EPS) * g
    return y * (1.0 + sc) + sh


def _mod_spec(layer, row_of, chunk, d):
    return pl.BlockSpec((1, 1, 1, 1, d), lambda i, *_: (layer, row_of(i), chunk, 0, 0))


def _short_conv_tile(u, prev_row, next_row, cw_ref, cb_ref):
    rows = u.shape[0]
    row = lax.broadcasted_iota(jnp.int32, u.shape, 0)
    prev = jnp.where(row == 0, prev_row, pltpu.roll(u, 1, axis=0))
    nxt = jnp.where(row == rows - 1, next_row, pltpu.roll(u, rows - 1, axis=0))
    return prev * cw_ref[0:1] + u * cw_ref[1:2] + nxt * cw_ref[2:3] + cb_ref[...]


def _norm_proj_body(splits, conv, x_ref, g_ref, sc_ref, sh_ref, w_ref, *rest):
    o_refs = rest[4:] if conv else rest
    g, sc, sh = g_ref[0], sc_ref[0, 0, 0], sh_ref[0, 0, 0]
    p = _bdot(_modulated_norm(x_ref[0], g, sc, sh).astype(BF16), w_ref[0])
    off = 0
    for idx, (o_ref, (width, _, scale)) in enumerate(zip(o_refs, splits)):
        part = p[:, off:off + width]
        if conv and idx == 0:
            xp_ref, xn_ref, cw_ref, cb_ref = rest[:4]
            j, nj = pl.program_id(1), pl.num_programs(1)
            wc = w_ref[0, :, off:off + width]
            edge = F32_SUBLANES
            pp = _bdot(_modulated_norm(xp_ref[0], g, sc, sh).astype(BF16), wc)
            pn = _bdot(_modulated_norm(xn_ref[0], g, sc, sh).astype(BF16), wc)
            prev_row = jnp.where(j == 0, 0.0, pp[edge - 1:edge])
            next_row = jnp.where(j == nj - 1, 0.0, pn[0:1])
            u = _short_conv_tile(part, prev_row, next_row, cw_ref, cb_ref)
            c = o_ref.shape[3]
            for gi in range(width // c):
                o_ref[0, gi] = u[:, gi * c:(gi + 1) * c]
        else:
            o_ref[0] = (part if scale is None else part * scale).astype(o_ref.dtype)
        off += width


def _norm_proj(x, norm_g, mod, row_of, layer, w, w_idx, splits, tm, conv=None):
    b, t, d = x.shape
    n = w.shape[2]
    tm = min(tm, t)
    edge = F32_SUBLANES
    tok = lambda wd: pl.BlockSpec((1, tm, wd), lambda i, j: (i, j, 0))
    in_specs = [
        tok(d),
        pl.BlockSpec((1, 1, d), lambda i, j: (layer, 0, 0)),
        _mod_spec(layer, row_of, 1, d),
        _mod_spec(layer, row_of, 0, d),
        pl.BlockSpec((1, d, n), lambda i, j: (w_idx, 0, 0)),
    ]
    args = [x, norm_g, mod, mod, w]
    out_specs = [tok(wd) for wd, _, _ in splits]
    out_shape = [jax.ShapeDtypeStruct((b, t, wd), dt) for wd, dt, _ in splits]
    if conv is not None:
        cw, cb = conv
        w0, c = splits[0][0], D_HYENA
        per_tile = tm // edge
        in_specs += [
            pl.BlockSpec((1, edge, d), lambda i, j: (i, jnp.maximum(j * per_tile - 1, 0), 0)),
            pl.BlockSpec((1, edge, d), lambda i, j: (i, jnp.minimum((j + 1) * per_tile, t // edge - 1), 0)),
            pl.BlockSpec(cw.shape, lambda i, j: (0, 0)),
            pl.BlockSpec(cb.shape, lambda i, j: (0, 0)),
        ]
        args += [x, x, cw, cb]
        out_specs[0] = pl.BlockSpec((1, w0 // c, tm, c), lambda i, j: (i, 0, j, 0))
        out_shape[0] = jax.ShapeDtypeStruct((b, w0 // c, t, c), splits[0][1])
    return pl.pallas_call(
        functools.partial(_norm_proj_body, tuple(splits), conv is not None),
        grid=(b, t // tm),
        in_specs=in_specs,
        out_specs=out_specs,
        out_shape=out_shape,
        compiler_params=_params("parallel", "parallel"),
        name="norm_proj",
    )(*args)


def _mix_ffn_body(final_norm, x_ref, hy_ref, fn_ref, na_ref, wo_ref, g1_ref, ng_ref, sc_ref, sh_ref, g2_ref,
                  wg_ref, wu_ref, wd_ref, fg_ref, o_ref):
    mix = (_bdot(hy_ref[0], wo_ref[0, :D_HYENA])
           + _bdot(fn_ref[0], wo_ref[0, D_HYENA:D_HYENA + D_FNET])
           + _bdot(na_ref[0], wo_ref[0, D_HYENA + D_FNET:]))
    x1 = x_ref[0] + g1_ref[0, 0, 0] * mix
    h = _modulated_norm(x1, ng_ref[0], sc_ref[0, 0, 0], sh_ref[0, 0, 0]).astype(BF16)
    acc = None
    for c0 in range(0, wg_ref.shape[2], MXU_WIDTH):
        cols = slice(c0, c0 + MXU_WIDTH)
        a = _bdot(h, wg_ref[0, :, cols])
        u = _bdot(h, wu_ref[0, :, cols])
        act = (a * jax.nn.sigmoid(a) * u).astype(BF16)
        part = _bdot(act, wd_ref[0, cols, :])
        acc = part if acc is None else acc + part
    y = x1 + g2_ref[0, 0, 0] * acc
    if final_norm:
        ms = jnp.mean(y * y, axis=-1, keepdims=True)
        y = y * lax.rsqrt(ms + EPS) * fg_ref[...]
    o_ref[0] = y


def _mix_ffn(x, y_hy, y_fn, y_na, w_out, norm_g, mod, row_of, layer, wg, wu, wd, final_g, final_norm, tm):
    b, t, d = x.shape
    dff = wg.shape[2]
    assert dff % MXU_WIDTH == 0
    tm = min(tm, t)
    tok = lambda wd_: pl.BlockSpec((1, tm, wd_), lambda i, j: (i, j, 0))
    resident = lambda shape: pl.BlockSpec((1,) + shape, lambda i, j: (layer, 0, 0), pipeline_mode=pl.Buffered(1))
    return pl.pallas_call(
        functools.partial(_mix_ffn_body, final_norm),
        grid=(b, t // tm),
        in_specs=[
            tok(d), tok(D_HYENA), tok(D_FNET), tok(D_NA),
            resident((d, d)),
            _mod_spec(layer, row_of, 2, d),
            pl.BlockSpec((1, 1, d), lambda i, j: (layer, 0, 0)),
            _mod_spec(layer, row_of, 4, d),
            _mod_spec(layer, row_of, 3, d),
            _mod_spec(layer, row_of, 5, d),
            resident((d, dff)), resident((d, dff)), resident((dff, d)),
            pl.BlockSpec((1, d), lambda i, j: (0, 0)),
        ],
        out_specs=tok(d),
        out_shape=jax.ShapeDtypeStruct(x.shape, F32),
        compiler_params=_params("parallel", "parallel"),
        name="mix_ffn",
    )(x, y_hy, y_fn, y_na, w_out, mod, norm_g, mod, mod, mod, wg, wu, wd, final_g)


def _split_seq(n_total):
    log = int(round(math.log2(n_total)))
    assert 2 ** log == n_total
    n1 = 2 ** (log // 2)
    return n1, n_total // n1


@functools.lru_cache(maxsize=None)
def _hyena_consts(seq):
    n = 2 * seq
    n1, n2 = _split_seq(n)
    h1 = n1 // 2
    f1 = np.arange(h1)[:, None]
    t1 = np.arange(h1)[None, :]
    th = np.pi * (2 * f1 + 1) * t1 / n1
    m1 = np.empty((n1, h1))
    m1[0::2] = np.cos(th)
    m1[1::2] = -np.sin(th)
    m3 = m1.T * (2.0 / n)
    f = (np.arange(h1)[:, None, None] + n1 * np.arange(n2)[None, :, None])
    t2 = np.arange(n2)[None, None, :]
    ph = np.pi * ((2 * f + 1) * t2 % (2 * n)) / n
    c, s = np.cos(ph), np.sin(ph)
    m2 = np.concatenate([np.concatenate([c, s], axis=2), np.concatenate([-s, c], axis=2)], axis=1)
    m2t = np.transpose(m2, (0, 2, 1))
    m1k = np.kron(m1, np.eye(F32_SUBLANES))
    m3k = np.kron(m3, np.eye(BF16_SUBLANES))
    return n1, n2, m1k, m2, m2t, m3k


@functools.lru_cache(maxsize=None)
def _fnet_consts(seq):
    n1, n2 = _split_seq(seq)
    n1, n2 = n1 // 2, n2 * 2
    f1 = np.arange(n1)[:, None]
    t1 = np.arange(n1)[None, :]
    th = 2 * np.pi * (f1 * t1 % n1) / n1
    m1 = np.empty((2 * n1, n1))
    m1[0::2] = np.cos(th)
    m1[1::2] = -np.sin(th)
    f = (np.arange(n1)[:, None, None] + n1 * np.arange(n2)[None, :, None])
    t2 = np.arange(n2)[None, None, :]
    ph = 2 * np.pi * (f * t2 % seq) / seq
    c, s = np.cos(ph), np.sin(ph)
    m2 = np.concatenate([np.concatenate([c, s], axis=2), np.concatenate([-s, c], axis=2)], axis=1)
    gd = FNET_GROUP_DIM
    cc = np.arange(gd)
    pg = 2 * np.pi * (np.outer(cc, cc) % gd) / gd
    scale = 1.0 / math.sqrt(gd * seq)
    eye = np.eye(D_FNET // gd)
    bdc = np.kron(eye, np.cos(pg)) * scale
    bds = np.kron(eye, np.sin(pg)) * scale
    m1k = np.kron(m1, np.eye(BF16_SUBLANES))
    g = F32_SUBLANES
    perm = np.zeros((g * n1, n1 * g))
    for f2 in range(g):
        for ff in range(n1):
            perm[f2 * n1 + ff, ff * g + f2] = 1.0
    return n1, n2, m1k, m2, perm, bdc, bds


def _mxu_const(a):
    return jnp.asarray(a, dtype=F32).astype(BF16)


def _stage1_into(mk_ref, x_at, a_scr, group):
    nf, _, n2, c = a_scr.shape

    def body(s, carry):
        r0 = pl.multiple_of(s * group, group)
        xs = x_at(r0)
        xs = xs.reshape(xs.shape[0] * group, c).astype(BF16)
        a_scr[:, :, pl.ds(r0, group), :] = _bdot(mk_ref[...], xs).reshape(nf, 2, group, c)
        return carry

    lax.fori_loop(0, n2 // group, body, 0, unroll=2)


def _filter_body(zt_ref, t_ref, w1t_ref, b1_ref, fr_ref, w2t_ref, b2_ref, w3_ref, dl_ref, o_ref, h_scr):
    @pl.when(pl.program_id(0) == 0)
    def _():
        h = jnp.sin(fr_ref[:, 0:1] * (_dot3(w1t_ref[...], zt_ref[...]) + b1_ref[...]))
        h = jnp.sin(fr_ref[:, 1:2] * (_dot3(w2t_ref[...], h) + b2_ref[...]))
        h_scr[...] = h.T

    k = _dot3(h_scr[...], w3_ref[0])
    c = D_HYENA
    t = t_ref[...]
    decay = jnp.exp(-t * jnp.abs(dl_ref[...]))
    kf = k[:, :c] * decay
    row = lax.broadcasted_iota(jnp.int32, kf.shape, 0)
    kb = jnp.where(row == 0, 0.0, k[:, c:] * decay)
    norm = jnp.sum(jnp.abs(kf), axis=0, keepdims=True) + jnp.sum(jnp.abs(kb), axis=0, keepdims=True)
    inv = 1.0 / norm
    o_ref[0, 0] = kf * inv
    o_ref[1, 0] = kb * inv


def _hyena_filters_time(seq, w1, b1, freq, w2, b2, w3):
    c = D_HYENA
    t = np.linspace(0.0, 1.0, seq)[:, None]
    w = 2.0 * np.pi * np.arange(seq)[:, None] / seq
    f = np.linspace(1e-4, POS_BANDS - 1, POS_BANDS)[None]
    z = np.concatenate([t, np.cos(f * w), -np.sin(f * w)], axis=-1).astype(np.float32)
    pe = z.shape[1]
    zt = jnp.asarray(np.pad(z, ((0, 0), (0, LANES - pe))).T)
    tcol = jnp.asarray(t.astype(np.float32))
    w1t = jnp.pad(w1, ((0, LANES - pe), (0, 0))).T
    max_decay = math.log(HYENA_DECAY_TARGET) / HYENA_FAST_PCT
    min_decay = math.log(HYENA_DECAY_TARGET) / HYENA_SLOW_PCT
    deltas = jnp.asarray(np.linspace(min_decay, max_decay, c, dtype=np.float32)[None])
    hid = w1.shape[1]
    w3r = jnp.transpose(w3.reshape(hid, 2, 2 * c), (1, 0, 2))
    full = lambda a: pl.BlockSpec(a.shape, lambda o: (0,) * a.ndim)
    b1c, b2c, frt = b1.reshape(hid, 1), b2.reshape(hid, 1), freq.T
    return pl.pallas_call(
        _filter_body,
        grid=(2,),
        in_specs=[full(zt), full(tcol), full(w1t), full(b1c), full(frt), full(w2.T), full(b2c),
                  pl.BlockSpec((1, hid, 2 * c), lambda o: (o, 0, 0)), full(deltas)],
        out_specs=pl.BlockSpec((2, 1, seq, c), lambda o: (o, 0, 0, 0)),
        out_shape=jax.ShapeDtypeStruct((4, 1, seq, c), F32),
        scratch_shapes=[pltpu.VMEM((seq, hid), F32)],
        compiler_params=_params("arbitrary"),
        name="hyena_filter_time",
    )(zt, tcol, w1t, b1c, frt, w2.T, b2c, w3r, deltas)


def _stage2_rows(a_scr, f1):
    _, _, n2, c = a_scr.shape
    return a_scr[f1].reshape(2 * n2, c).astype(BF16)


def _filter_spec_body(m1k_ref, m2_ref, kf_ref, kb_ref, kr_ref, ki_ref, af_scr, ab_scr):
    j = pl.program_id(1)
    fb = m2_ref.shape[0]
    n2 = kr_ref.shape[2]
    g = F32_SUBLANES

    @pl.when(j == 0)
    def _():
        _stage1_into(m1k_ref, lambda r0: kf_ref[0, 0, :, pl.ds(r0, g), :], af_scr, g)
        _stage1_into(m1k_ref, lambda r0: kb_ref[0, 0, :, pl.ds(r0, g), :], ab_scr, g)

    for i in range(fb):
        f1 = j * fb + i
        xf = _bdot(m2_ref[i], _stage2_rows(af_scr, f1))
        xb = _bdot(m2_ref[i], _stage2_rows(ab_scr, f1))
        kr_ref[0, i] = xf[:n2] + xb[:n2]
        ki_ref[0, i] = xf[n2:] - xb[n2:]


def _filter_spectrum(m1k, m2, kt, fb):
    _, _, h1, n2, c = kt.shape
    out = jax.ShapeDtypeStruct((2, h1, n2, c), F32)
    return pl.pallas_call(
        _filter_spec_body,
        grid=(2, h1 // fb),
        in_specs=[pl.BlockSpec(m1k.shape, lambda o, j: (0, 0)),
                  pl.BlockSpec((fb, 2 * n2, 2 * n2), lambda o, j: (j, 0, 0)),
                  pl.BlockSpec((1, 1, h1, n2, c), lambda o, j: (2 * o, 0, 0, 0, 0)),
                  pl.BlockSpec((1, 1, h1, n2, c), lambda o, j: (2 * o + 1, 0, 0, 0, 0))],
        out_specs=[pl.BlockSpec((1, fb, n2, c), lambda o, j: (o, j, 0, 0))] * 2,
        out_shape=[out, out],
        scratch_shapes=[pltpu.VMEM((h1, 2, n2, c), F32)] * 2,
        compiler_params=_params("parallel", "arbitrary"),
        name="hyena_filter_spectrum",
    )(m1k, m2, kt, kt)


def _conv_fwd_body(m1k_ref, m2_ref, m2t_ref, x_ref, kr_ref, ki_ref, o_ref, a_scr):
    h1, _, n2, c = a_scr.shape
    g = F32_SUBLANES
    _stage1_into(m1k_ref, lambda r0: x_ref[0, 0, :, pl.ds(r0, g), :], a_scr, g)

    def per_f1(f1, carry):
        x = _bdot(m2_ref[f1], _stage2_rows(a_scr, f1))
        xr, xi = x[:n2], x[n2:]
        kr, ki = kr_ref[0, f1], ki_ref[0, f1]
        y = jnp.concatenate([kr * xr - ki * xi, kr * xi + ki * xr], axis=0).astype(BF16)
        o_ref[0, pl.ds(2 * f1, 2)] = _bdot(m2t_ref[f1], y).astype(BF16).reshape(2, n2, c)
        return carry

    lax.fori_loop(0, h1, per_f1, 0, unroll=8)


def _conv_fwd(m1k, m2, m2t, x, sel, kr, ki, order):
    b, _, h1, n2, c = x.shape
    resident = lambda a, idx: pl.BlockSpec(a.shape, lambda i: idx, pipeline_mode=pl.Buffered(1))
    kspec = pl.BlockSpec((1, h1, n2, c), lambda i: (order, 0, 0, 0), pipeline_mode=pl.Buffered(1))
    return pl.pallas_call(
        _conv_fwd_body,
        grid=(b,),
        in_specs=[resident(m1k, (0, 0)), resident(m2, (0, 0, 0)), resident(m2t, (0, 0, 0)),
                  pl.BlockSpec((1, 1, h1, n2, c), lambda i: (i, sel, 0, 0, 0)),
                  kspec, kspec],
        out_specs=pl.BlockSpec((1, 2 * h1, n2, c), lambda i: (i, 0, 0, 0)),
        out_shape=jax.ShapeDtypeStruct((b, 2 * h1, n2, c), BF16),
        scratch_shapes=[pltpu.VMEM((h1, 2, n2, c), F32)],
        compiler_params=_params("parallel"),
        name="hyena_conv_fwd",
    )(m1k, m2, m2t, x, kr, ki)


def _conv_inv_body(m3k_ref, b_ref, gate_ref, skipin_ref, skip_ref, o_ref):
    n1, ts, c = b_ref.shape[1:]
    h1 = n1 // 2
    g = BF16_SUBLANES
    for s in range(ts // g):
        rows = slice(s * g, (s + 1) * g)
        bs = b_ref[0, :, rows, :].reshape(n1 * g, c)
        y = _bdot(m3k_ref[...], bs).reshape(h1, g, c)
        z = gate_ref[0, 0, :, rows, :] * (y + skipin_ref[0, 0, :, rows, :] * skip_ref[...])
        o_ref[0, 0, :, rows, :] = z.astype(o_ref.dtype)


def _conv_inv(m3k, bm, gates, gate_sel, skipin, skipin_sel, skip_row, out_dtype):
    b, n1, n2, c = bm.shape
    h1 = n1 // 2
    ts = min(n2, 2 * BF16_SUBLANES)
    tok = lambda sel: pl.BlockSpec((1, 1, h1, ts, c), lambda i, j: (i, sel, 0, j, 0))
    return pl.pallas_call(
        _conv_inv_body,
        grid=(b, n2 // ts),
        in_specs=[pl.BlockSpec(m3k.shape, lambda i, j: (0, 0)),
                  pl.BlockSpec((1, n1, ts, c), lambda i, j: (i, 0, j, 0)),
                  tok(gate_sel), tok(skipin_sel),
                  pl.BlockSpec((1, c), lambda i, j: (0, 0))],
        out_specs=tok(0),
        out_shape=jax.ShapeDtypeStruct((b, 1, h1, n2, c), out_dtype),
        compiler_params=_params("parallel", "parallel"),
        name="hyena_conv_inv",
    )(m3k, bm, gates, skipin, skip_row)


def _hyena_mixer(u3, w1, b1, freq, w2, b2, w3, skip):
    b, _, seq, c = u3.shape
    n1, n2, m1k, m2, m2t, m3k = _hyena_consts(seq)
    h1 = n1 // 2
    fb = min(h1, 8)
    m1k, m2, m2t, m3k = (_mxu_const(a) for a in (m1k, m2, m2t, m3k))

    kt = _hyena_filters_time(seq, w1, b1, freq, w2, b2, w3)
    kr, ki = _filter_spectrum(m1k, m2, kt.reshape(4, 1, h1, n2, c), fb)

    u3 = u3.reshape(b, 3, h1, n2, c)
    bm = _conv_fwd(m1k, m2, m2t, u3, 2, kr, ki, 0)
    z = _conv_inv(m3k, bm, u3, 0, u3, 2, skip[0:1], F32)
    bm = _conv_fwd(m1k, m2, m2t, z, 0, kr, ki, 1)
    y = _conv_inv(m3k, bm, u3, 1, z, 0, skip[1:2], BF16)
    return y.reshape(b, seq, c)


def _fnet_body(m1k_ref, m2_ref, perm_ref, bdc_ref, bds_ref, g_ref, o_ref, a_scr):
    j = pl.program_id(1)
    fb = m2_ref.shape[0]
    n1, _, n2, c = a_scr.shape

    @pl.when(j == 0)
    def _():
        gi = BF16_SUBLANES
        _stage1_into(m1k_ref, lambda r0: g_ref[0, :, pl.ds(r0, gi), :], a_scr, gi)

    for i in range(fb):
        f1 = j * fb + i
        a_scr[f1] = _bdot(m2_ref[i], _stage2_rows(a_scr, f1)).reshape(2, n2, c)

    @pl.when(j == pl.num_programs(1) - 1)
    def _():
        go = F32_SUBLANES

        def body(s, carry):
            r0 = pl.multiple_of(s * go, go)
            re = a_scr[:, 0, pl.ds(r0, go), :].reshape(n1 * go, c).astype(BF16)
            im = a_scr[:, 1, pl.ds(r0, go), :].reshape(n1 * go, c).astype(BF16)
            xr = _bdot(perm_ref[...], re).astype(BF16)
            xi = _bdot(perm_ref[...], im).astype(BF16)
            out = _bdot(xr, bdc_ref[...]) + _bdot(xi, bds_ref[...])
            o_ref[0, pl.ds(pl.multiple_of(s * go * n1, go * n1), go * n1), :] = out.astype(o_ref.dtype)
            return carry

        lax.fori_loop(0, n2 // go, body, 0, unroll=2)


def _fourier_mix(g):
    b, seq, c = g.shape
    n1, n2, m1k, m2, perm, bdc, bds = _fnet_consts(seq)
    m1k, m2, perm, bdc, bds = (_mxu_const(a) for a in (m1k, m2, perm, bdc, bds))
    fb = min(n1, 8)
    const = lambda a: pl.BlockSpec(a.shape, lambda i, j: (0,) * a.ndim)
    return pl.pallas_call(
        _fnet_body,
        grid=(b, n1 // fb),
        in_specs=[const(m1k),
                  pl.BlockSpec((fb, 2 * n2, 2 * n2), lambda i, j: (j, 0, 0)),
                  const(perm), const(bdc), const(bds),
                  pl.BlockSpec((1, n1, n2, c), lambda i, j: (i, 0, 0, 0))],
        out_specs=pl.BlockSpec((1, seq, c), lambda i, j: (i, 0, 0)),
        out_shape=jax.ShapeDtypeStruct((b, seq, c), BF16),
        scratch_shapes=[pltpu.VMEM((n1, 2, n2, c), F32)],
        compiler_params=_params("parallel", "arbitrary"),
        name="fnet",
    )(m1k, m2, perm, bdc, bds, g.reshape(b, n1, n2, c))


def _dot_nt(a, b):
    return lax.dot_general(a, b, (((1,), (1,)), ((), ())), preferred_element_type=F32)


def _head_pair_attention(q2, keys, values, biases):
    lane = lax.broadcasted_iota(jnp.int32, q2.shape, 1)
    outs = []
    for hh in range(2):
        in_head = (lane >= hh * HEAD_DIM) & (lane < (hh + 1) * HEAD_DIM)
        qm = jnp.where(in_head, q2, jnp.zeros_like(q2))
        m = denom = acc = None
        for kk, vv, bias in zip(keys, values, biases[hh]):
            s = _dot_nt(qm, kk)
            if bias is not None:
                s = s + bias
            m_grp = s.max(axis=-1, keepdims=True)
            m_new = m_grp if m is None else jnp.maximum(m, m_grp)
            e = jnp.exp2(s - m_new)
            e_part = e[:, :LANES]
            for c0 in range(LANES, e.shape[1], LANES):
                e_part = e_part + e[:, c0:c0 + LANES]
            pv = _bdot(e.astype(BF16), vv)
            if m is None:
                denom, acc = e_part, pv
            else:
                alpha = jnp.exp2(m - m_new)
                denom = alpha * denom + e_part
                acc = alpha * acc + pv
            m = m_new
        outs.append(acc / denom.sum(axis=-1, keepdims=True))
    return jnp.where(lane < HEAD_DIM, outs[0], outs[1])


def _na_window_base(row_block, rows):
    return jnp.clip(row_block * NA_QROWS - NA_KH // 2, 0, rows - NA_KROWS)


def _na_body(rows, q_ref, k_ref, v_ref, kc_ref, vc_ref, bias_ref, o_ref):
    base = _na_window_base(pl.program_id(1), rows)
    start = pl.multiple_of(base * GRID_W, GRID_W)
    n_keys = NA_KROWS * GRID_W
    n_pairs = q_ref.shape[2] // LANES
    groups = [(c0, min(NA_KEY_GROUP, n_keys - c0)) for c0 in range(0, n_keys, NA_KEY_GROUP)]
    for p in range(n_pairs):
        sl = slice(p * LANES, (p + 1) * LANES)
        q2 = q_ref[0, :, sl]
        keys = [kc_ref[0, :, sl]] + [k_ref[0, pl.ds(start + c0, n), sl] for c0, n in groups]
        values = [vc_ref[0, :, sl]] + [v_ref[0, pl.ds(start + c0, n), sl] for c0, n in groups]
        biases = [[None] + [bias_ref[0, 2 * p + hh, 0, :, c0:c0 + n] for c0, n in groups] for hh in range(2)]
        o = _head_pair_attention(q2, keys, values, biases)
        o_ref[0, :, sl] = o.astype(o_ref.dtype)


def _na_bias_body(rows, p_ref, o_ref):
    kh, kw, w = NA_KH, NA_KW, GRID_W
    shape = (w, 2 * w)
    qc = lax.broadcasted_iota(jnp.int32, shape, 0)
    lane = lax.broadcasted_iota(jnp.int32, shape, 1)
    kc = lane & (w - 1)
    col_start = jnp.clip(qc - kw // 2, 0, w - kw)
    col_valid = (kc >= col_start) & (kc < col_start + kw)
    low_half = lane < w
    masked = jnp.full(shape, MASK_VALUE, F32)
    tiles = []
    for a in range(2 * kh - 1):
        row = jnp.broadcast_to(p_ref[0, 0, a:a + 1, :], shape)
        t = pltpu.roll(row, w + 1, 1, stride=1, stride_axis=0)
        t = jnp.where(low_half, t, pltpu.roll(t, w, 1))
        tiles.append(jnp.where(col_valid, t, masked))
    for case, r0 in enumerate((0, NA_QROWS, rows - NA_QROWS)):
        base = int(np.clip(r0 - kh // 2, 0, rows - NA_KROWS))
        for qr in range(NA_QROWS):
            q_abs = r0 + qr
            r_start = int(np.clip(q_abs - kh // 2, 0, rows - kh))

            def tile_for(kr):
                k_abs = base + kr
                if kr >= NA_KROWS or not (r_start <= k_abs < r_start + kh):
                    return masked
                return tiles[k_abs - q_abs + kh - 1]

            q_rows = slice(qr * w, (qr + 1) * w)
            for pair in range((NA_KROWS + 1) // 2):
                chunk = jnp.where(low_half, tile_for(2 * pair), tile_for(2 * pair + 1))
                width = min(2 * w, NA_KROWS * w - pair * 2 * w)
                o_ref[0, 0, case, q_rows, pair * 2 * w:pair * 2 * w + width] = chunk[:, :width]


def _na_bias_tables(rpb, rows):
    w = GRID_W
    depth, nh, na, nb = rpb.shape
    assert 2 * w == LANES and nb == 2 * NA_KW - 1
    lo = w - NA_KW
    padded = jnp.pad(rpb.astype(F32) * LOG2E, ((0, 0), (0, 0), (0, (-na) % 8), (lo, 2 * w - nb - lo)))
    return pl.pallas_call(
        functools.partial(_na_bias_body, rows),
        grid=(depth, nh),
        in_specs=[pl.BlockSpec((1, 1) + padded.shape[2:], lambda l, h: (l, h, 0, 0))],
        out_specs=pl.BlockSpec((1, 1, 3, NA_QROWS * w, NA_KROWS * w), lambda l, h: (l, h, 0, 0, 0)),
        out_shape=jax.ShapeDtypeStruct((depth, nh, 3, NA_QROWS * w, NA_KROWS * w), F32),
        compiler_params=_params("parallel", "parallel"),
        name="na_bias_tables",
    )(padded)


def _neighbourhood_attention(q, kv, kv_ctx, bias, layer):
    b, seq, _ = q.shape
    rows = seq // GRID_W
    lc = kv_ctx.shape[1]
    assert rows % NA_QROWS == 0 and rows >= NA_KROWS + 1 and NA_QROWS == NA_KH // 2
    n_blocks = rows // NA_QROWS
    nh = bias.shape[1]
    tq = NA_QROWS * GRID_W

    def bias_map(i, r):
        return (layer, 0, jnp.where(r == 0, 0, jnp.where(r == n_blocks - 1, 2, 1)), 0, 0)

    return pl.pallas_call(
        functools.partial(_na_body, rows),
        grid=(b, n_blocks),
        in_specs=[pl.BlockSpec((1, tq, D_NA), lambda i, r: (i, r, 0)),
                  pl.BlockSpec((1, seq, D_NA), lambda i, r: (i, 0, 0)),
                  pl.BlockSpec((1, seq, D_NA), lambda i, r: (i, 0, 1)),
                  pl.BlockSpec((1, lc, D_NA), lambda i, r: (i, 0, 0)),
                  pl.BlockSpec((1, lc, D_NA), lambda i, r: (i, 0, 1)),
                  pl.BlockSpec((1, nh, 1, tq, NA_KROWS * GRID_W), bias_map)],
        out_specs=pl.BlockSpec((1, tq, D_NA), lambda i, r: (i, r, 0)),
        out_shape=jax.ShapeDtypeStruct((b, seq, D_NA), BF16),
        compiler_params=_params("parallel", "arbitrary"),
        name="neighbourhood_attention",
    )(q, kv, kv, kv_ctx, kv_ctx, bias)


def _ctx_attn_body(q_ref, k_ref, v_ref, o_ref):
    o = _head_pair_attention(q_ref[0], [k_ref[0]], [v_ref[0]], [[None], [None]])
    o_ref[0] = o.astype(o_ref.dtype)


def _context_attention(q, kv):
    b, lc, _ = q.shape
    n_pairs = D_NA // LANES
    return pl.pallas_call(
        _ctx_attn_body,
        grid=(b, n_pairs),
        in_specs=[pl.BlockSpec((1, lc, LANES), lambda i, p: (i, 0, p)),
                  pl.BlockSpec((1, lc, LANES), lambda i, p: (i, 0, p)),
                  pl.BlockSpec((1, lc, LANES), lambda i, p: (i, 0, n_pairs + p))],
        out_specs=pl.BlockSpec((1, lc, LANES), lambda i, p: (i, 0, p)),
        out_shape=jax.ShapeDtypeStruct((b, lc, D_NA), BF16),
        compiler_params=_params("parallel", "parallel"),
        name="context_attention",
    )(q, kv, kv)


def kernel(x, c, ctx, c_ctx, w_mod, b_mod, norm1_g, norm2_g, w_in, w_out, hy_conv_w, hy_conv_b,
           filt_w1, filt_b1, filt_freq, filt_w2, filt_b2, filt_w3, hy_skip, na_rpb,
           w_gate, w_up, w_down, final_g):
    depth = w_mod.shape[0]
    b, seq, d = x.shape
    off_k = 3 * D_HYENA + D_FNET + D_NA
    qk_scale = LOG2E / math.sqrt(HEAD_DIM)
    splits = ((3 * D_HYENA, F32, None), (D_FNET, BF16, None), (D_NA, BF16, qk_scale), (2 * D_NA, BF16, None))
    tm = 512

    pad = (-(b + 1)) % 8
    c_all = jnp.concatenate([c, c_ctx[None], jnp.zeros((pad, d), F32)], axis=0)
    mod = _modulation(c_all, w_mod, b_mod).reshape(depth, b + 1 + pad, 6, 1, d)
    lat_row = lambda i: i
    ctx_row = lambda i: b

    w_in_b, w_out_b = w_in.astype(BF16), w_out.astype(BF16)
    w_gate_b, w_up_b, w_down_b = w_gate.astype(BF16), w_up.astype(BF16), w_down.astype(BF16)
    w_kv_last = w_in_b[depth - 1:, :, off_k:]
    n1g, n2g = norm1_g.reshape(depth, 1, d), norm2_g.reshape(depth, 1, d)
    fg = final_g.reshape(1, d)
    na_bias = _na_bias_tables(na_rpb, seq // GRID_W)

    for l in range(depth):
        last = l == depth - 1
        hy_params = (filt_w1[l], filt_b1[l], filt_freq[l], filt_w2[l], filt_b2[l], filt_w3[l], hy_skip[l])
        conv = (hy_conv_w[l], hy_conv_b[l].reshape(1, 3 * D_HYENA))

        hy, fn, q, kv = _norm_proj(x, n1g, mod, lat_row, l, w_in_b, l, splits, tm, conv)
        if last:
            (kv_c,) = _norm_proj(ctx, n1g, mod, ctx_row, l, w_kv_last, 0, ((2 * D_NA, BF16, None),), tm)
        else:
            chy, cfn, cq, kv_c = _norm_proj(ctx, n1g, mod, ctx_row, l, w_in_b, l, splits, tm, conv)

        y_hy = _hyena_mixer(hy, *hy_params)
        y_fn = _fourier_mix(fn)
        y_na = _neighbourhood_attention(q, kv, kv_c, na_bias, l)

        x = _mix_ffn(x, y_hy, y_fn, y_na, w_out_b, n2g, mod, lat_row, l, w_gate_b, w_up_b, w_down_b,
                     fg, last, tm)

        if not last:
            cy_hy = _hyena_mixer(chy, *hy_params)
            cy_fn = _fourier_mix(cfn)
            cy_na = _context_attention(cq, kv_c)
            ctx = _mix_ffn(ctx, cy_hy, cy_fn, cy_na, w_out_b, n2g, mod, ctx_row, l, w_gate_b, w_up_b, w_down_b,
                           fg, False, tm)
    return x
```

```python
import functools
import math

import numpy as np
import jax
import jax.numpy as jnp
from jax import lax
from jax.experimental import pallas as pl
from jax.experimental.pallas import tpu as pltpu

F32 = jnp.float32
BF16 = jnp.bfloat16

EPS = 1e-6
GRID_W = 64
HEAD_DIM = 64
D_HYENA = 256
D_FNET = 256
D_NA = 512
FNET_GROUP_DIM = 64
NA_KH = 8
NA_KW = 16
NA_QROWS = 4
NA_KROWS = NA_QROWS + NA_KH
NA_KEY_GROUP = 256
POS_BANDS = 16
HYENA_DECAY_TARGET = 1e-2
HYENA_FAST_PCT = 0.3
HYENA_SLOW_PCT = 1.5
MASK_VALUE = -1e30
VMEM_LIMIT = 56 * 1024 * 1024
LANES = 128
MXU_WIDTH = 256
LOG2E = math.log2(math.e)
F32_SUBLANES = 8
BF16_SUBLANES = 16


def _params(*sem):
    return pltpu.CompilerParams(dimension_semantics=sem, vmem_limit_bytes=VMEM_LIMIT)


def _bdot(a, b):
    return jnp.dot(a, b, preferred_element_type=F32)


def _split_bf16(a):
    hi = a.astype(BF16)
    lo = (a - hi.astype(F32)).astype(BF16)
    return hi, lo


def _dot3(a, b):
    a_hi, a_lo = _split_bf16(a)
    b_hi, b_lo = _split_bf16(b)
    return _bdot(a_hi, b_hi) + _bdot(a_lo, b_hi) + _bdot(a_hi, b_lo)


def _mod_body(c_ref, w_ref, b_ref, o_ref):
    c = c_ref[...]
    o_ref[0] = _dot3(c * jax.nn.sigmoid(c), w_ref[0]) + b_ref[0]


def _modulation(c_all, w_mod, b_mod):
    depth, d, n = w_mod.shape
    rows = c_all.shape[0]
    tn = 1536
    return pl.pallas_call(
        _mod_body,
        grid=(depth, n // tn),
        in_specs=[
            pl.BlockSpec((rows, d), lambda l, j: (0, 0)),
            pl.BlockSpec((1, d, tn), lambda l, j: (l, 0, j)),
            pl.BlockSpec((1, 1, tn), lambda l, j: (l, 0, j)),
        ],
        out_specs=pl.BlockSpec((1, rows, tn), lambda l, j: (l, 0, j)),
        out_shape=jax.ShapeDtypeStruct((depth, rows, n), F32),
        compiler_params=_params("parallel", "parallel"),
        name="modulation",
    )(c_all, w_mod, b_mod.reshape(depth, 1, n))


def _modulated_norm(x, g, sc, sh):
    ms = jnp.mean(x * x, axis=-1, keepdims=True)
    y = x * lax.rsqrt(ms + EPS) * g
    return y * (1.0 + sc) + sh


def _mod_spec(layer, row_of, chunk, d):
    return pl.BlockSpec((1, 1, 1, 1, d), lambda i, *_: (layer, row_of(i), chunk, 0, 0))


def _short_conv_tile(u, prev_row, next_row, cw_ref, cb_ref):
    rows = u.shape[0]
    row = lax.broadcasted_iota(jnp.int32, u.shape, 0)
    prev = jnp.where(row == 0, prev_row, pltpu.roll(u, 1, axis=0))
    nxt = jnp.where(row == rows - 1, next_row, pltpu.roll(u, rows - 1, axis=0))
    return prev * cw_ref[0:1] + u * cw_ref[1:2] + nxt * cw_ref[2:3] + cb_ref[...]


def _norm_proj_body(splits, conv, x_ref, g_ref, sc_ref, sh_ref, w_ref, *rest):
    o_refs = ((rest[4:6],) + rest[6:]) if conv else rest
    g, sc, sh = g_ref[0], sc_ref[0, 0, 0], sh_ref[0, 0, 0]
    p = _bdot(_modulated_norm(x_ref[0], g, sc, sh).astype(BF16), w_ref[0])
    off = 0
    for idx, (o_ref, (width, _, scale)) in enumerate(zip(o_refs, splits)):
        part = p[:, off:off + width]
        if conv and idx == 0:
            xp_ref, xn_ref, cw_ref, cb_ref = rest[:4]
            gate_ref, v_ref = o_ref
            j, nj = pl.program_id(1), pl.num_programs(1)
            wc = w_ref[0, :, off:off + width]
            edge = F32_SUBLANES
            pp = _bdot(_modulated_norm(xp_ref[0], g, sc, sh).astype(BF16), wc)
            pn = _bdot(_modulated_norm(xn_ref[0], g, sc, sh).astype(BF16), wc)
            prev_row = jnp.where(j == 0, 0.0, pp[edge - 1:edge])
            next_row = jnp.where(j == nj - 1, 0.0, pn[0:1])
            u = _short_conv_tile(part, prev_row, next_row, cw_ref, cb_ref)
            c = v_ref.shape[3]
            n_gates = gate_ref.shape[1]
            for gi in range(n_gates):
                gate_ref[0, gi] = u[:, gi * c:(gi + 1) * c].astype(gate_ref.dtype)
            v_ref[0, 0] = u[:, n_gates * c:]
        else:
            o_ref[0] = (part if scale is None else part * scale).astype(o_ref.dtype)
        off += width


def _norm_proj(x, norm_g, mod, row_of, layer, w, w_idx, splits, tm, conv=None):
    b, t, d = x.shape
    n = w.shape[2]
    tm = min(tm, t)
    edge = F32_SUBLANES
    tok = lambda wd: pl.BlockSpec((1, tm, wd), lambda i, j: (i, j, 0))
    in_specs = [
        tok(d),
        pl.BlockSpec((1, 1, d), lambda i, j: (layer, 0, 0)),
        _mod_spec(layer, row_of, 1, d),
        _mod_spec(layer, row_of, 0, d),
        pl.BlockSpec((1, d, n), lambda i, j: (w_idx, 0, 0)),
    ]
    args = [x, norm_g, mod, mod, w]
    out_specs = [tok(wd) for wd, _, _ in splits]
    out_shape = [jax.ShapeDtypeStruct((b, t, wd), dt) for wd, dt, _ in splits]
    if conv is not None:
        cw, cb = conv
        w0, c = splits[0][0], D_HYENA
        per_tile = tm // edge
        in_specs += [
            pl.BlockSpec((1, edge, d), lambda i, j: (i, jnp.maximum(j * per_tile - 1, 0), 0)),
            pl.BlockSpec((1, edge, d), lambda i, j: (i, jnp.minimum((j + 1) * per_tile, t // edge - 1), 0)),
            pl.BlockSpec(cw.shape, lambda i, j: (0, 0)),
            pl.BlockSpec(cb.shape, lambda i, j: (0, 0)),
        ]
        args += [x, x, cw, cb]
        n_gates = w0 // c - 1
        out_specs[0:1] = [pl.BlockSpec((1, n_gates, tm, c), lambda i, j: (i, 0, j, 0)),
                          pl.BlockSpec((1, 1, tm, c), lambda i, j: (i, 0, j, 0))]
        out_shape[0:1] = [jax.ShapeDtypeStruct((b, n_gates, t, c), BF16),
                          jax.ShapeDtypeStruct((b, 1, t, c), splits[0][1])]
    return pl.pallas_call(
        functools.partial(_norm_proj_body, tuple(splits), conv is not None),
        grid=(b, t // tm),
        in_specs=in_specs,
        out_specs=out_specs,
        out_shape=out_shape,
        compiler_params=_params("parallel", "parallel"),
        name="norm_proj",
    )(*args)


def _mix_ffn_body(final_norm, x_ref, hy_ref, fn_ref, na_ref, wo_ref, g1_ref, ng_ref, sc_ref, sh_ref, g2_ref,
                  wg_ref, wu_ref, wd_ref, fg_ref, o_ref):
    mix = (_bdot(hy_ref[0], wo_ref[0, :D_HYENA])
           + _bdot(fn_ref[0], wo_ref[0, D_HYENA:D_HYENA + D_FNET])
           + _bdot(na_ref[0], wo_ref[0, D_HYENA + D_FNET:]))
    x1 = x_ref[0] + g1_ref[0, 0, 0] * mix
    h = _modulated_norm(x1, ng_ref[0], sc_ref[0, 0, 0], sh_ref[0, 0, 0]).astype(BF16)
    acc = None
    for c0 in range(0, wg_ref.shape[2], MXU_WIDTH):
        cols = slice(c0, c0 + MXU_WIDTH)
        a = _bdot(h, wg_ref[0, :, cols])
        u = _bdot(h, wu_ref[0, :, cols])
        act = (a * jax.nn.sigmoid(a) * u).astype(BF16)
        part = _bdot(act, wd_ref[0, cols, :])
        acc = part if acc is None else acc + part
    y = x1 + g2_ref[0, 0, 0] * acc
    if final_norm:
        ms = jnp.mean(y * y, axis=-1, keepdims=True)
        y = y * lax.rsqrt(ms + EPS) * fg_ref[...]
    o_ref[0] = y


def _mix_ffn(x, y_hy, y_fn, y_na, w_out, norm_g, mod, row_of, layer, wg, wu, wd, final_g, final_norm, tm):
    b, t, d = x.shape
    dff = wg.shape[2]
    assert dff % MXU_WIDTH == 0
    tm = min(tm, t)
    tok = lambda wd_: pl.BlockSpec((1, tm, wd_), lambda i, j: (i, j, 0))
    resident = lambda shape: pl.BlockSpec((1,) + shape, lambda i, j: (layer, 0, 0), pipeline_mode=pl.Buffered(1))
    return pl.pallas_call(
        functools.partial(_mix_ffn_body, final_norm),
        grid=(b, t // tm),
        in_specs=[
            tok(d), tok(D_HYENA), tok(D_FNET), tok(D_NA),
            resident((d, d)),
            _mod_spec(layer, row_of, 2, d),
            pl.BlockSpec((1, 1, d), lambda i, j: (layer, 0, 0)),
            _mod_spec(layer, row_of, 4, d),
            _mod_spec(layer, row_of, 3, d),
            _mod_spec(layer, row_of, 5, d),
            resident((d, dff)), resident((d, dff)), resident((dff, d)),
            pl.BlockSpec((1, d), lambda i, j: (0, 0)),
        ],
        out_specs=tok(d),
        out_shape=jax.ShapeDtypeStruct(x.shape, F32),
        compiler_params=_params("parallel", "parallel"),
        name="mix_ffn",
    )(x, y_hy, y_fn, y_na, w_out, mod, norm_g, mod, mod, mod, wg, wu, wd, final_g)


def _split_seq(n_total):
    log = int(round(math.log2(n_total)))
    assert 2 ** log == n_total
    n1 = 2 ** (log // 2)
    return n1, n_total // n1


@functools.lru_cache(maxsize=None)
def _hyena_consts(seq):
    n = 2 * seq
    n1, n2 = _split_seq(n)
    h1 = n1 // 2
    f1 = np.arange(h1)[:, None]
    t1 = np.arange(h1)[None, :]
    th = np.pi * (2 * f1 + 1) * t1 / n1
    m1 = np.empty((n1, h1))
    m1[0::2] = np.cos(th)
    m1[1::2] = -np.sin(th)
    m3 = m1.T * (2.0 / n)
    f = (np.arange(h1)[:, None, None] + n1 * np.arange(n2)[None, :, None])
    t2 = np.arange(n2)[None, None, :]
    ph = np.pi * ((2 * f + 1) * t2 % (2 * n)) / n
    c, s = np.cos(ph), np.sin(ph)
    m2 = np.concatenate([np.concatenate([c, s], axis=2), np.concatenate([-s, c], axis=2)], axis=1)
    m2t = np.transpose(m2, (0, 2, 1))
    m1k = np.kron(m1, np.eye(F32_SUBLANES))
    m3k = np.kron(m3, np.eye(BF16_SUBLANES))
    return n1, n2, m1k, m2, m2t, m3k


@functools.lru_cache(maxsize=None)
def _fnet_consts(seq):
    n1, n2 = _split_seq(seq)
    n1, n2 = n1 // 2, n2 * 2
    f1 = np.arange(n1)[:, None]
    t1 = np.arange(n1)[None, :]
    th = 2 * np.pi * (f1 * t1 % n1) / n1
    m1 = np.empty((2 * n1, n1))
    m1[0::2] = np.cos(th)
    m1[1::2] = -np.sin(th)
    f = (np.arange(n1)[:, None, None] + n1 * np.arange(n2)[None, :, None])
    t2 = np.arange(n2)[None, None, :]
    ph = 2 * np.pi * (f * t2 % seq) / seq
    c, s = np.cos(ph), np.sin(ph)
    m2 = np.concatenate([np.concatenate([c, s], axis=2), np.concatenate([-s, c], axis=2)], axis=1)
    gd = FNET_GROUP_DIM
    cc = np.arange(gd)
    pg = 2 * np.pi * (np.outer(cc, cc) % gd) / gd
    scale = 1.0 / math.sqrt(gd * seq)
    eye = np.eye(D_FNET // gd)
    bdc = np.kron(eye, np.cos(pg)) * scale
    bds = np.kron(eye, np.sin(pg)) * scale
    m1k = np.kron(m1, np.eye(BF16_SUBLANES))
    g = F32_SUBLANES
    perm = np.zeros((g * n1, n1 * g))
    for f2 in range(g):
        for ff in range(n1):
            perm[f2 * n1 + ff, ff * g + f2] = 1.0
    return n1, n2, m1k, m2, perm, bdc, bds


def _mxu_const(a):
    return jnp.asarray(a, dtype=F32).astype(BF16)


def _stage1_into(mk_ref, x_at, a_scr, group):
    nf, _, n2, c = a_scr.shape

    def body(s, carry):
        r0 = pl.multiple_of(s * group, group)
        xs = x_at(r0)
        xs = xs.reshape(xs.shape[0] * group, c).astype(BF16)
        a_scr[:, :, pl.ds(r0, group), :] = _bdot(mk_ref[...], xs).reshape(nf, 2, group, c)
        return carry

    lax.fori_loop(0, n2 // group, body, 0, unroll=2)


def _stage2_rows(a_scr, f1):
    _, _, n2, c = a_scr.shape
    return a_scr[f1].reshape(2 * n2, c).astype(BF16)


def _filter_body(zt_ref, t_ref, w1t_ref, b1_ref, fr_ref, w2t_ref, b2_ref, w3_ref, dl_ref, m1k_ref, m2_ref,
                 kr_ref, ki_ref, h_scr, kt_scr, af_scr, ab_scr):
    o, j = pl.program_id(0), pl.program_id(1)
    fb = m2_ref.shape[0]
    h1, _, n2, c = af_scr.shape
    g = F32_SUBLANES

    @pl.when((o == 0) & (j == 0))
    def _():
        h = jnp.sin(fr_ref[:, 0:1] * (_dot3(w1t_ref[...], zt_ref[...]) + b1_ref[...]))
        h = jnp.sin(fr_ref[:, 1:2] * (_dot3(w2t_ref[...], h) + b2_ref[...]))
        h_scr[...] = h.T

    @pl.when(j == 0)
    def _():
        k = _dot3(h_scr[...], w3_ref[0])
        decay = jnp.exp(-t_ref[...] * jnp.abs(dl_ref[...]))
        kf = k[:, :c] * decay
        row = lax.broadcasted_iota(jnp.int32, kf.shape, 0)
        kb = jnp.where(row == 0, 0.0, k[:, c:] * decay)
        norm = jnp.sum(jnp.abs(kf), axis=0, keepdims=True) + jnp.sum(jnp.abs(kb), axis=0, keepdims=True)
        inv = 1.0 / norm
        kt_scr[0] = (kf * inv).reshape(h1, n2, c)
        kt_scr[1] = (kb * inv).reshape(h1, n2, c)
        _stage1_into(m1k_ref, lambda r0: kt_scr[0, :, pl.ds(r0, g), :], af_scr, g)
        _stage1_into(m1k_ref, lambda r0: kt_scr[1, :, pl.ds(r0, g), :], ab_scr, g)

    for i in range(fb):
        f1 = j * fb + i
        xf = _bdot(m2_ref[i], _stage2_rows(af_scr, f1))
        xb = _bdot(m2_ref[i], _stage2_rows(ab_scr, f1))
        kr_ref[0, i] = xf[:n2] + xb[:n2]
        ki_ref[0, i] = xf[n2:] - xb[n2:]


def _hyena_filter_spectrum(seq, w1, b1, freq, w2, b2, w3, m1k, m2, fb):
    c = D_HYENA
    h1, n2 = m2.shape[0], m2.shape[1] // 2
    t = np.linspace(0.0, 1.0, seq)[:, None]
    w = 2.0 * np.pi * np.arange(seq)[:, None] / seq
    f = np.linspace(1e-4, POS_BANDS - 1, POS_BANDS)[None]
    z = np.concatenate([t, np.cos(f * w), -np.sin(f * w)], axis=-1).astype(np.float32)
    pe = z.shape[1]
    zt = jnp.asarray(np.pad(z, ((0, 0), (0, LANES - pe))).T)
    tcol = jnp.asarray(t.astype(np.float32))
    w1t = jnp.pad(w1, ((0, LANES - pe), (0, 0))).T
    max_decay = math.log(HYENA_DECAY_TARGET) / HYENA_FAST_PCT
    min_decay = math.log(HYENA_DECAY_TARGET) / HYENA_SLOW_PCT
    deltas = jnp.asarray(np.linspace(min_decay, max_decay, c, dtype=np.float32)[None])
    hid = w1.shape[1]
    w3r = jnp.transpose(w3.reshape(hid, 2, 2 * c), (1, 0, 2))
    full = lambda a: pl.BlockSpec(a.shape, lambda o, j: (0,) * a.ndim)
    b1c, b2c, frt = b1.reshape(hid, 1), b2.reshape(hid, 1), freq.T
    out = jax.ShapeDtypeStruct((2, h1, n2, c), F32)
    return pl.pallas_call(
        _filter_body,
        grid=(2, h1 // fb),
        in_specs=[full(zt), full(tcol), full(w1t), full(b1c), full(frt), full(w2.T), full(b2c),
                  pl.BlockSpec((1, hid, 2 * c), lambda o, j: (o, 0, 0)), full(deltas), full(m1k),
                  pl.BlockSpec((fb, 2 * n2, 2 * n2), lambda o, j: (j, 0, 0))],
        out_specs=[pl.BlockSpec((1, fb, n2, c), lambda o, j: (o, j, 0, 0))] * 2,
        out_shape=[out, out],
        scratch_shapes=[pltpu.VMEM((seq, hid), F32), pltpu.VMEM((2, h1, n2, c), F32),
                        pltpu.VMEM((h1, 2, n2, c), F32), pltpu.VMEM((h1, 2, n2, c), F32)],
        compiler_params=_params("arbitrary", "arbitrary"),
        name="hyena_filter_spectrum",
    )(zt, tcol, w1t, b1c, frt, w2.T, b2c, w3r, deltas, m1k, m2)


def _conv_fwd_body(m1k_ref, m2_ref, m2t_ref, x_ref, kr_ref, ki_ref, o_ref, a_scr):
    h1, _, n2, c = a_scr.shape
    g = F32_SUBLANES
    _stage1_into(m1k_ref, lambda r0: x_ref[0, 0, :, pl.ds(r0, g), :], a_scr, g)

    def per_f1(f1, carry):
        x = _bdot(m2_ref[f1], _stage2_rows(a_scr, f1))
        xr, xi = x[:n2], x[n2:]
        kr, ki = kr_ref[0, f1], ki_ref[0, f1]
        y = jnp.concatenate([kr * xr - ki * xi, kr * xi + ki * xr], axis=0).astype(BF16)
        o_ref[0, pl.ds(2 * f1, 2)] = _bdot(m2t_ref[f1], y).astype(BF16).reshape(2, n2, c)
        return carry

    lax.fori_loop(0, h1, per_f1, 0, unroll=8)


def _conv_fwd(m1k, m2, m2t, x, sel, kr, ki, order):
    b, _, h1, n2, c = x.shape
    resident = lambda a, idx: pl.BlockSpec(a.shape, lambda i: idx, pipeline_mode=pl.Buffered(1))
    kspec = pl.BlockSpec((1, h1, n2, c), lambda i: (order, 0, 0, 0), pipeline_mode=pl.Buffered(1))
    return pl.pallas_call(
        _conv_fwd_body,
        grid=(b,),
        in_specs=[resident(m1k, (0, 0)), resident(m2, (0, 0, 0)), resident(m2t, (0, 0, 0)),
                  pl.BlockSpec((1, 1, h1, n2, c), lambda i: (i, sel, 0, 0, 0)),
                  kspec, kspec],
        out_specs=pl.BlockSpec((1, 2 * h1, n2, c), lambda i: (i, 0, 0, 0)),
        out_shape=jax.ShapeDtypeStruct((b, 2 * h1, n2, c), BF16),
        scratch_shapes=[pltpu.VMEM((h1, 2, n2, c), F32)],
        compiler_params=_params("parallel"),
        name="hyena_conv_fwd",
    )(m1k, m2, m2t, x, kr, ki)


def _conv_inv_body(m3k_ref, b_ref, gate_ref, skipin_ref, skip_ref, o_ref):
    n1, ts, c = b_ref.shape[1:]
    h1 = n1 // 2
    g = BF16_SUBLANES
    for s in range(ts // g):
        rows = slice(s * g, (s + 1) * g)
        bs = b_ref[0, :, rows, :].reshape(n1 * g, c)
        y = _bdot(m3k_ref[...], bs).reshape(h1, g, c)
        z = gate_ref[0, 0, :, rows, :] * (y + skipin_ref[0, 0, :, rows, :] * skip_ref[...])
        o_ref[0, 0, :, rows, :] = z.astype(o_ref.dtype)


def _conv_inv(m3k, bm, gates, gate_sel, skipin, skipin_sel, skip_row, out_dtype):
    b, n1, n2, c = bm.shape
    h1 = n1 // 2
    ts = min(n2, 2 * BF16_SUBLANES)
    tok = lambda sel: pl.BlockSpec((1, 1, h1, ts, c), lambda i, j: (i, sel, 0, j, 0))
    return pl.pallas_call(
        _conv_inv_body,
        grid=(b, n2 // ts),
        in_specs=[pl.BlockSpec(m3k.shape, lambda i, j: (0, 0)),
                  pl.BlockSpec((1, n1, ts, c), lambda i, j: (i, 0, j, 0)),
                  tok(gate_sel), tok(skipin_sel),
                  pl.BlockSpec((1, c), lambda i, j: (0, 0))],
        out_specs=tok(0),
        out_shape=jax.ShapeDtypeStruct((b, 1, h1, n2, c), out_dtype),
        compiler_params=_params("parallel", "parallel"),
        name="hyena_conv_inv",
    )(m3k, bm, gates, skipin, skip_row)


def _hyena_mixer(gates, v, w1, b1, freq, w2, b2, w3, skip):
    b, _, seq, c = v.shape
    n1, n2, m1k, m2, m2t, m3k = _hyena_consts(seq)
    h1 = n1 // 2
    m1k, m2, m2t, m3k = (_mxu_const(a) for a in (m1k, m2, m2t, m3k))

    kr, ki = _hyena_filter_spectrum(seq, w1, b1, freq, w2, b2, w3, m1k, m2, min(h1, 8))

    gates = gates.reshape(b, 2, h1, n2, c)
    v = v.reshape(b, 1, h1, n2, c)
    bm = _conv_fwd(m1k, m2, m2t, v, 0, kr, ki, 0)
    z = _conv_inv(m3k, bm, gates, 0, v, 0, skip[0:1], F32)
    bm = _conv_fwd(m1k, m2, m2t, z, 0, kr, ki, 1)
    y = _conv_inv(m3k, bm, gates, 1, z, 0, skip[1:2], BF16)
    return y.reshape(b, seq, c)


def _fnet_body(m1k_ref, m2_ref, perm_ref, bdc_ref, bds_ref, g_ref, o_ref, a_scr):
    j = pl.program_id(1)
    fb = m2_ref.shape[0]
    n1, _, n2, c = a_scr.shape

    @pl.when(j == 0)
    def _():
        gi = BF16_SUBLANES
        _stage1_into(m1k_ref, lambda r0: g_ref[0, :, pl.ds(r0, gi), :], a_scr, gi)

    for i in range(fb):
        f1 = j * fb + i
        a_scr[f1] = _bdot(m2_ref[i], _stage2_rows(a_scr, f1)).reshape(2, n2, c)

    @pl.when(j == pl.num_programs(1) - 1)
    def _():
        go = F32_SUBLANES

        def body(s, carry):
            r0 = pl.multiple_of(s * go, go)
            re = a_scr[:, 0, pl.ds(r0, go), :].reshape(n1 * go, c).astype(BF16)
            im = a_scr[:, 1, pl.ds(r0, go), :].reshape(n1 * go, c).astype(BF16)
            xr = _bdot(perm_ref[...], re).astype(BF16)
            xi = _bdot(perm_ref[...], im).astype(BF16)
            out = _bdot(xr, bdc_ref[...]) + _bdot(xi, bds_ref[...])
            o_ref[0, pl.ds(pl.multiple_of(s * go * n1, go * n1), go * n1), :] = out.astype(o_ref.dtype)
            return carry

        lax.fori_loop(0, n2 // go, body, 0, unroll=2)


def _fourier_mix(g):
    b, seq, c = g.shape
    n1, n2, m1k, m2, perm, bdc, bds = _fnet_consts(seq)
    m1k, m2, perm, bdc, bds = (_mxu_const(a) for a in (m1k, m2, perm, bdc, bds))
    fb = min(n1, 8)
    const = lambda a: pl.BlockSpec(a.shape, lambda i, j: (0,) * a.ndim)
    return pl.pallas_call(
        _fnet_body,
        grid=(b, n1 // fb),
        in_specs=[const(m1k),
                  pl.BlockSpec((fb, 2 * n2, 2 * n2), lambda i, j: (j, 0, 0)),
                  const(perm), const(bdc), const(bds),
                  pl.BlockSpec((1, n1, n2, c), lambda i, j: (i, 0, 0, 0))],
        out_specs=pl.BlockSpec((1, seq, c), lambda i, j: (i, 0, 0)),
        out_shape=jax.ShapeDtypeStruct((b, seq, c), BF16),
        scratch_shapes=[pltpu.VMEM((n1, 2, n2, c), F32)],
        compiler_params=_params("parallel", "arbitrary"),
        name="fnet",
    )(m1k, m2, perm, bdc, bds, g.reshape(b, n1, n2, c))


def _dot_nt(a, b):
    return lax.dot_general(a, b, (((1,), (1,)), ((), ())), preferred_element_type=F32)


def _head_pair_attention(q2, keys, values, biases):
    lane = lax.broadcasted_iota(jnp.int32, q2.shape, 1)
    outs = []
    for hh in range(2):
        in_head = (lane >= hh * HEAD_DIM) & (lane < (hh + 1) * HEAD_DIM)
        qm = jnp.where(in_head, q2, jnp.zeros_like(q2))
        m = denom = acc = None
        for kk, vv, bias in zip(keys, values, biases[hh]):
            s = _dot_nt(qm, kk)
            if bias is not None:
                s = s + bias
            m_grp = s.max(axis=-1, keepdims=True)
            m_new = m_grp if m is None else jnp.maximum(m, m_grp)
            e = jnp.exp2(s - m_new)
            e_part = e[:, :LANES]
            for c0 in range(LANES, e.shape[1], LANES):
                e_part = e_part + e[:, c0:c0 + LANES]
            pv = _bdot(e.astype(BF16), vv)
            if m is None:
                denom, acc = e_part, pv
            else:
                alpha = jnp.exp2(m - m_new)
                denom = alpha * denom + e_part
                acc = alpha * acc + pv
            m = m_new
        outs.append(acc / denom.sum(axis=-1, keepdims=True))
    return jnp.where(lane < HEAD_DIM, outs[0], outs[1])


def _na_window_base(row_block, rows):
    return jnp.clip(row_block * NA_QROWS - NA_KH // 2, 0, rows - NA_KROWS)


def _na_body(rows, q_ref, k_ref, v_ref, kc_ref, vc_ref, bias_ref, o_ref):
    base = _na_window_base(pl.program_id(1), rows)
    start = pl.multiple_of(base * GRID_W, GRID_W)
    n_keys = NA_KROWS * GRID_W
    n_pairs = q_ref.shape[2] // LANES
    groups = [(c0, min(NA_KEY_GROUP, n_keys - c0)) for c0 in range(0, n_keys, NA_KEY_GROUP)]
    for p in range(n_pairs):
        sl = slice(p * LANES, (p + 1) * LANES)
        q2 = q_ref[0, :, sl]
        keys = [kc_ref[0, :, sl]] + [k_ref[0, pl.ds(start + c0, n), sl] for c0, n in groups]
        values = [vc_ref[0, :, sl]] + [v_ref[0, pl.ds(start + c0, n), sl] for c0, n in groups]
        biases = [[None] + [bias_ref[0, 2 * p + hh, 0, :, c0:c0 + n] for c0, n in groups] for hh in range(2)]
        o = _head_pair_attention(q2, keys, values, biases)
        o_ref[0, :, sl] = o.astype(o_ref.dtype)


def _na_bias_body(rows, p_ref, o_ref):
    kh, kw, w = NA_KH, NA_KW, GRID_W
    shape = (w, 2 * w)
    qc = lax.broadcasted_iota(jnp.int32, shape, 0)
    lane = lax.broadcasted_iota(jnp.int32, shape, 1)
    kc = lane & (w - 1)
    col_start = jnp.clip(qc - kw // 2, 0, w - kw)
    col_valid = (kc >= col_start) & (kc < col_start + kw)
    low_half = lane < w
    masked = jnp.full(shape, MASK_VALUE, F32)
    tiles = []
    for a in range(2 * kh - 1):
        row = jnp.broadcast_to(p_ref[0, 0, a:a + 1, :], shape)
        t = pltpu.roll(row, w + 1, 1, stride=1, stride_axis=0)
        t = jnp.where(low_half, t, pltpu.roll(t, w, 1))
        tiles.append(jnp.where(col_valid, t, masked))
    for case, r0 in enumerate((0, NA_QROWS, rows - NA_QROWS)):
        base = int(np.clip(r0 - kh // 2, 0, rows - NA_KROWS))
        for qr in range(NA_QROWS):
            q_abs = r0 + qr
            r_start = int(np.clip(q_abs - kh // 2, 0, rows - kh))

            def tile_for(kr):
                k_abs = base + kr
                if kr >= NA_KROWS or not (r_start <= k_abs < r_start + kh):
                    return masked
                return tiles[k_abs - q_abs + kh - 1]

            q_rows = slice(qr * w, (qr + 1) * w)
            for pair in range((NA_KROWS + 1) // 2):
                chunk = jnp.where(low_half, tile_for(2 * pair), tile_for(2 * pair + 1))
                width = min(2 * w, NA_KROWS * w - pair * 2 * w)
                o_ref[0, 0, case, q_rows, pair * 2 * w:pair * 2 * w + width] = chunk[:, :width]


def _na_bias_tables(rpb, rows):
    w = GRID_W
    depth, nh, na, nb = rpb.shape
    assert 2 * w == LANES and nb == 2 * NA_KW - 1
    lo = w - NA_KW
    padded = jnp.pad(rpb.astype(F32) * LOG2E, ((0, 0), (0, 0), (0, (-na) % 8), (lo, 2 * w - nb - lo)))
    return pl.pallas_call(
        functools.partial(_na_bias_body, rows),
        grid=(depth, nh),
        in_specs=[pl.BlockSpec((1, 1) + padded.shape[2:], lambda l, h: (l, h, 0, 0))],
        out_specs=pl.BlockSpec((1, 1, 3, NA_QROWS * w, NA_KROWS * w), lambda l, h: (l, h, 0, 0, 0)),
        out_shape=jax.ShapeDtypeStruct((depth, nh, 3, NA_QROWS * w, NA_KROWS * w), F32),
        compiler_params=_params("parallel", "parallel"),
        name="na_bias_tables",
    )(padded)


def _neighbourhood_attention(q, kv, kv_ctx, bias, layer):
    b, seq, _ = q.shape
    rows = seq // GRID_W
    lc = kv_ctx.shape[1]
    assert rows % NA_QROWS == 0 and rows >= NA_KROWS + 1 and NA_QROWS == NA_KH // 2
    n_blocks = rows // NA_QROWS
    nh = bias.shape[1]
    tq = NA_QROWS * GRID_W

    def bias_map(i, r):
        return (layer, 0, jnp.where(r == 0, 0, jnp.where(r == n_blocks - 1, 2, 1)), 0, 0)

    return pl.pallas_call(
        functools.partial(_na_body, rows),
        grid=(b, n_blocks),
        in_specs=[pl.BlockSpec((1, tq, D_NA), lambda i, r: (i, r, 0)),
                  pl.BlockSpec((1, seq, D_NA), lambda i, r: (i, 0, 0)),
                  pl.BlockSpec((1, seq, D_NA), lambda i, r: (i, 0, 1)),
                  pl.BlockSpec((1, lc, D_NA), lambda i, r: (i, 0, 0)),
                  pl.BlockSpec((1, lc, D_NA), lambda i, r: (i, 0, 1)),
                  pl.BlockSpec((1, nh, 1, tq, NA_KROWS * GRID_W), bias_map)],
        out_specs=pl.BlockSpec((1, tq, D_NA), lambda i, r: (i, r, 0)),
        out_shape=jax.ShapeDtypeStruct((b, seq, D_NA), BF16),
        compiler_params=_params("parallel", "arbitrary"),
        name="neighbourhood_attention",
    )(q, kv, kv, kv_ctx, kv_ctx, bias)


def _ctx_attn_body(q_ref, k_ref, v_ref, o_ref):
    o = _head_pair_attention(q_ref[0], [k_ref[0]], [v_ref[0]], [[None], [None]])
    o_ref[0] = o.astype(o_ref.dtype)


def _context_attention(q, kv):
    b, lc, _ = q.shape
    n_pairs = D_NA // LANES
    return pl.pallas_call(
        _ctx_attn_body,
        grid=(b, n_pairs),
        in_specs=[pl.BlockSpec((1, lc, LANES), lambda i, p: (i, 0, p)),
                  pl.BlockSpec((1, lc, LANES), lambda i, p: (i, 0, p)),
                  pl.BlockSpec((1, lc, LANES), lambda i, p: (i, 0, n_pairs + p))],
        out_specs=pl.BlockSpec((1, lc, LANES), lambda i, p: (i, 0, p)),
        out_shape=jax.ShapeDtypeStruct((b, lc, D_NA), BF16),
        compiler_params=_params("parallel", "parallel"),
        name="context_attention",
    )(q, kv, kv)


def kernel(x, c, ctx, c_ctx, w_mod, b_mod, norm1_g, norm2_g, w_in, w_out, hy_conv_w, hy_conv_b,
           filt_w1, filt_b1, filt_freq, filt_w2, filt_b2, filt_w3, hy_skip, na_rpb,
           w_gate, w_up, w_down, final_g):
    depth = w_mod.shape[0]
    b, seq, d = x.shape
    off_k = 3 * D_HYENA + D_FNET + D_NA
    qk_scale = LOG2E / math.sqrt(HEAD_DIM)
    splits = ((3 * D_HYENA, F32, None), (D_FNET, BF16, None), (D_NA, BF16, qk_scale), (2 * D_NA, BF16, None))
    tm = 512

    pad = (-(b + 1)) % 8
    c_all = jnp.concatenate([c, c_ctx[None], jnp.zeros((pad, d), F32)], axis=0)
    mod = _modulation(c_all, w_mod, b_mod).reshape(depth, b + 1 + pad, 6, 1, d)
    lat_row = lambda i: i
    ctx_row = lambda i: b

    w_in_b, w_out_b = w_in.astype(BF16), w_out.astype(BF16)
    w_gate_b, w_up_b, w_down_b = w_gate.astype(BF16), w_up.astype(BF16), w_down.astype(BF16)
    w_kv_last = w_in_b[depth - 1:, :, off_k:]
    n1g, n2g = norm1_g.reshape(depth, 1, d), norm2_g.reshape(depth, 1, d)
    fg = final_g.reshape(1, d)
    na_bias = _na_bias_tables(na_rpb, seq // GRID_W)

    for l in range(depth):
        last = l == depth - 1
        hy_params = (filt_w1[l], filt_b1[l], filt_freq[l], filt_w2[l], filt_b2[l], filt_w3[l], hy_skip[l])
        conv = (hy_conv_w[l], hy_conv_b[l].reshape(1, 3 * D_HYENA))

        gates, v, fn, q, kv = _norm_proj(x, n1g, mod, lat_row, l, w_in_b, l, splits, tm, conv)
        if last:
            (kv_c,) = _norm_proj(ctx, n1g, mod, ctx_row, l, w_kv_last, 0, ((2 * D_NA, BF16, None),), tm)
        else:
            cgates, cv, cfn, cq, kv_c = _norm_proj(ctx, n1g, mod, ctx_row, l, w_in_b, l, splits, tm, conv)

        y_hy = _hyena_mixer(gates, v, *hy_params)
        y_fn = _fourier_mix(fn)
        y_na = _neighbourhood_attention(q, kv, kv_c, na_bias, l)

        x = _mix_ffn(x, y_hy, y_fn, y_na, w_out_b, n2g, mod, lat_row, l, w_gate_b, w_up_b, w_down_b,
                     fg, last, tm)

        if not last:
            cy_hy = _hyena_mixer(cgates, cv, *hy_params)
            cy_fn = _fourier_mix(cfn)
            cy_na = _context_attention(cq, kv_c)
            ctx = _mix_ffn(ctx, cy_hy, cy_fn, cy_na, w_out_b, n2g, mod, ctx_row, l, w_gate_b, w_up_b, w_down_b,
                           fg, False, tm)
    return x
```

```python
import functools
import math

import numpy as np
import jax
import jax.numpy as jnp
from jax import lax
from jax.experimental import pallas as pl
from jax.experimental.pallas import tpu as pltpu

F32 = jnp.float32
BF16 = jnp.bfloat16

EPS = 1e-6
GRID_W = 64
HEAD_DIM = 64
D_HYENA = 256
D_FNET = 256
D_NA = 512
FNET_GROUP_DIM = 64
NA_KH = 8
NA_KW = 16
NA_QROWS = 4
NA_KROWS = NA_QROWS + NA_KH
NA_KEY_GROUP = 256
POS_BANDS = 16
HYENA_DECAY_TARGET = 1e-2
HYENA_FAST_PCT = 0.3
HYENA_SLOW_PCT = 1.5
MASK_VALUE = -1e30
VMEM_LIMIT = 56 * 1024 * 1024
LANES = 128
MXU_WIDTH = 256
LOG2E = math.log2(math.e)
F32_SUBLANES = 8
BF16_SUBLANES = 16


def _params(*sem):
    return pltpu.CompilerParams(dimension_semantics=sem, vmem_limit_bytes=VMEM_LIMIT)


def _bdot(a, b):
    return jnp.dot(a, b, preferred_element_type=F32)


def _split_bf16(a):
    hi = a.astype(BF16)
    lo = (a - hi.astype(F32)).astype(BF16)
    return hi, lo


def _dot3(a, b):
    a_hi, a_lo = _split_bf16(a)
    b_hi, b_lo = _split_bf16(b)
    return _bdot(a_hi, b_hi) + _bdot(a_lo, b_hi) + _bdot(a_hi, b_lo)


def _mod_body(c_ref, w_ref, b_ref, o_ref):
    c = c_ref[...]
    o_ref[0] = _dot3(c * jax.nn.sigmoid(c), w_ref[0]) + b_ref[0]


def _modulation(c_all, w_mod, b_mod):
    depth, d, n = w_mod.shape
    rows = c_all.shape[0]
    tn = 1536
    return pl.pallas_call(
        _mod_body,
        grid=(depth, n // tn),
        in_specs=[
            pl.BlockSpec((rows, d), lambda l, j: (0, 0)),
            pl.BlockSpec((1, d, tn), lambda l, j: (l, 0, j)),
            pl.BlockSpec((1, 1, tn), lambda l, j: (l, 0, j)),
        ],
        out_specs=pl.BlockSpec((1, rows, tn), lambda l, j: (l, 0, j)),
        out_shape=jax.ShapeDtypeStruct((depth, rows, n), F32),
        compiler_params=_params("parallel", "parallel"),
        name="modulation",
    )(c_all, w_mod, b_mod.reshape(depth, 1, n))


def _modulated_norm(x, g, sc, sh):
    ms = jnp.mean(x * x, axis=-1, keepdims=True)
    y = x * lax.rsqrt(ms + EPS) * g
    return y * (1.0 + sc) + sh


def _mod_spec(layer, row_of, chunk, d):
    return pl.BlockSpec((1, 1, 1, 1, d), lambda i, *_: (layer, row_of(i), chunk, 0, 0))


def _short_conv_tile(u, prev_row, next_row, cw_ref, cb_ref):
    rows = u.shape[0]
    row = lax.broadcasted_iota(jnp.int32, u.shape, 0)
    prev = jnp.where(row == 0, prev_row, pltpu.roll(u, 1, axis=0))
    nxt = jnp.where(row == rows - 1, next_row, pltpu.roll(u, rows - 1, axis=0))
    return prev * cw_ref[0:1] + u * cw_ref[1:2] + nxt * cw_ref[2:3] + cb_ref[...]


def _norm_proj_body(splits, conv, x_ref, g_ref, sc_ref, sh_ref, w_ref, *rest):
    o_refs = ((rest[4:6],) + rest[6:]) if conv else rest
    g, sc, sh = g_ref[0], sc_ref[0, 0, 0], sh_ref[0, 0, 0]
    p = _bdot(_modulated_norm(x_ref[0], g, sc, sh).astype(BF16), w_ref[0])
    off = 0
    for idx, (o_ref, (width, _, scale)) in enumerate(zip(o_refs, splits)):
        part = p[:, off:off + width]
        if conv and idx == 0:
            xp_ref, xn_ref, cw_ref, cb_ref = rest[:4]
            gate_ref, v_ref = o_ref
            j, nj = pl.program_id(1), pl.num_programs(1)
            wc = w_ref[0, :, off:off + width]
            edge = F32_SUBLANES
            pp = _bdot(_modulated_norm(xp_ref[0], g, sc, sh).astype(BF16), wc)
            pn = _bdot(_modulated_norm(xn_ref[0], g, sc, sh).astype(BF16), wc)
            prev_row = jnp.where(j == 0, 0.0, pp[edge - 1:edge])
            next_row = jnp.where(j == nj - 1, 0.0, pn[0:1])
            u = _short_conv_tile(part, prev_row, next_row, cw_ref, cb_ref)
            c = v_ref.shape[3]
            n_gates = gate_ref.shape[1]
            for gi in range(n_gates):
                gate_ref[0, gi] = u[:, gi * c:(gi + 1) * c].astype(gate_ref.dtype)
            v_ref[0, 0] = u[:, n_gates * c:]
        else:
            o_ref[0] = (part if scale is None else part * scale).astype(o_ref.dtype)
        off += width


def _norm_proj(x, norm_g, mod, row_of, layer, w, w_idx, splits, tm, conv=None):
    b, t, d = x.shape
    n = w.shape[2]
    tm = min(tm, t)
    edge = F32_SUBLANES
    tok = lambda wd: pl.BlockSpec((1, tm, wd), lambda i, j: (i, j, 0))
    in_specs = [
        tok(d),
        pl.BlockSpec((1, 1, d), lambda i, j: (layer, 0, 0)),
        _mod_spec(layer, row_of, 1, d),
        _mod_spec(layer, row_of, 0, d),
        pl.BlockSpec((1, d, n), lambda i, j: (w_idx, 0, 0)),
    ]
    args = [x, norm_g, mod, mod, w]
    out_specs = [tok(wd) for wd, _, _ in splits]
    out_shape = [jax.ShapeDtypeStruct((b, t, wd), dt) for wd, dt, _ in splits]
    if conv is not None:
        cw, cb = conv
        w0, c = splits[0][0], D_HYENA
        per_tile = tm // edge
        in_specs += [
            pl.BlockSpec((1, edge, d), lambda i, j: (i, jnp.maximum(j * per_tile - 1, 0), 0)),
            pl.BlockSpec((1, edge, d), lambda i, j: (i, jnp.minimum((j + 1) * per_tile, t // edge - 1), 0)),
            pl.BlockSpec(cw.shape, lambda i, j: (0, 0)),
            pl.BlockSpec(cb.shape, lambda i, j: (0, 0)),
        ]
        args += [x, x, cw, cb]
        n_gates = w0 // c - 1
        out_specs[0:1] = [pl.BlockSpec((1, n_gates, tm, c), lambda i, j: (i, 0, j, 0)),
                          pl.BlockSpec((1, 1, tm, c), lambda i, j: (i, 0, j, 0))]
        out_shape[0:1] = [jax.ShapeDtypeStruct((b, n_gates, t, c), BF16),
                          jax.ShapeDtypeStruct((b, 1, t, c), splits[0][1])]
    return pl.pallas_call(
        functools.partial(_norm_proj_body, tuple(splits), conv is not None),
        grid=(b, t // tm),
        in_specs=in_specs,
        out_specs=out_specs,
        out_shape=out_shape,
        compiler_params=_params("parallel", "parallel"),
        name="norm_proj",
    )(*args)


def _mix_ffn_body(final_norm, x_ref, hy_ref, fn_ref, na_ref, wo_ref, g1_ref, ng_ref, sc_ref, sh_ref, g2_ref,
                  wg_ref, wu_ref, wd_ref, fg_ref, o_ref):
    mix = (_bdot(hy_ref[0], wo_ref[0, :D_HYENA])
           + _bdot(fn_ref[0], wo_ref[0, D_HYENA:D_HYENA + D_FNET])
           + _bdot(na_ref[0], wo_ref[0, D_HYENA + D_FNET:]))
    x1 = x_ref[0] + g1_ref[0, 0, 0] * mix
    h = _modulated_norm(x1, ng_ref[0], sc_ref[0, 0, 0], sh_ref[0, 0, 0]).astype(BF16)
    acc = None
    for c0 in range(0, wg_ref.shape[2], MXU_WIDTH):
        cols = slice(c0, c0 + MXU_WIDTH)
        a = _bdot(h, wg_ref[0, :, cols])
        u = _bdot(h, wu_ref[0, :, cols])
        act = (a * jax.nn.sigmoid(a) * u).astype(BF16)
        part = _bdot(act, wd_ref[0, cols, :])
        acc = part if acc is None else acc + part
    y = x1 + g2_ref[0, 0, 0] * acc
    if final_norm:
        ms = jnp.mean(y * y, axis=-1, keepdims=True)
        y = y * lax.rsqrt(ms + EPS) * fg_ref[...]
    o_ref[0] = y


def _mix_ffn(x, y_hy, y_fn, y_na, w_out, norm_g, mod, row_of, layer, wg, wu, wd, final_g, final_norm, tm):
    b, t, d = x.shape
    dff = wg.shape[2]
    assert dff % MXU_WIDTH == 0
    tm = min(tm, t)
    tok = lambda wd_: pl.BlockSpec((1, tm, wd_), lambda i, j: (i, j, 0))
    resident = lambda shape: pl.BlockSpec((1,) + shape, lambda i, j: (layer, 0, 0), pipeline_mode=pl.Buffered(1))
    return pl.pallas_call(
        functools.partial(_mix_ffn_body, final_norm),
        grid=(b, t // tm),
        in_specs=[
            tok(d), tok(D_HYENA), tok(D_FNET), tok(D_NA),
            resident((d, d)),
            _mod_spec(layer, row_of, 2, d),
            pl.BlockSpec((1, 1, d), lambda i, j: (layer, 0, 0)),
            _mod_spec(layer, row_of, 4, d),
            _mod_spec(layer, row_of, 3, d),
            _mod_spec(layer, row_of, 5, d),
            resident((d, dff)), resident((d, dff)), resident((dff, d)),
            pl.BlockSpec((1, d), lambda i, j: (0, 0)),
        ],
        out_specs=tok(d),
        out_shape=jax.ShapeDtypeStruct(x.shape, F32),
        compiler_params=_params("parallel", "parallel"),
        name="mix_ffn",
    )(x, y_hy, y_fn, y_na, w_out, mod, norm_g, mod, mod, mod, wg, wu, wd, final_g)


def _split_seq(n_total):
    log = int(round(math.log2(n_total)))
    assert 2 ** log == n_total
    n1 = 2 ** (log // 2)
    return n1, n_total // n1


@functools.lru_cache(maxsize=None)
def _hyena_consts(seq):
    n = 2 * seq
    n1, n2 = _split_seq(n)
    h1 = n1 // 2
    f1 = np.arange(h1)[:, None]
    t1 = np.arange(h1)[None, :]
    th = np.pi * (2 * f1 + 1) * t1 / n1
    m1 = np.empty((n1, h1))
    m1[0::2] = np.cos(th)
    m1[1::2] = -np.sin(th)
    m3 = m1.T * (2.0 / n)
    f = (np.arange(h1)[:, None, None] + n1 * np.arange(n2)[None, :, None])
    t2 = np.arange(n2)[None, None, :]
    ph = np.pi * ((2 * f + 1) * t2 % (2 * n)) / n
    c, s = np.cos(ph), np.sin(ph)
    m2 = np.concatenate([np.concatenate([c, s], axis=2), np.concatenate([-s, c], axis=2)], axis=1)
    m2t = np.transpose(m2, (0, 2, 1))
    m1k = np.kron(m1, np.eye(F32_SUBLANES))
    m3k = np.kron(m3, np.eye(BF16_SUBLANES))
    return n1, n2, m1k, m2, m2t, m3k


@functools.lru_cache(maxsize=None)
def _fnet_consts(seq):
    n1, n2 = _split_seq(seq)
    n1, n2 = n1 // 2, n2 * 2
    f1 = np.arange(n1)[:, None]
    t1 = np.arange(n1)[None, :]
    th = 2 * np.pi * (f1 * t1 % n1) / n1
    m1 = np.empty((2 * n1, n1))
    m1[0::2] = np.cos(th)
    m1[1::2] = -np.sin(th)
    f = (np.arange(n1)[:, None, None] + n1 * np.arange(n2)[None, :, None])
    t2 = np.arange(n2)[None, None, :]
    ph = 2 * np.pi * (f * t2 % seq) / seq
    c, s = np.cos(ph), np.sin(ph)
    m2 = np.concatenate([np.concatenate([c, s], axis=2), np.concatenate([-s, c], axis=2)], axis=1)
    gd = FNET_GROUP_DIM
    cc = np.arange(gd)
    pg = 2 * np.pi * (np.outer(cc, cc) % gd) / gd
    scale = 1.0 / math.sqrt(gd * seq)
    eye = np.eye(D_FNET // gd)
    bdc = np.kron(eye, np.cos(pg)) * scale
    bds = np.kron(eye, np.sin(pg)) * scale
    m1k = np.kron(m1, np.eye(BF16_SUBLANES))
    g = F32_SUBLANES
    perm = np.zeros((g * n1, n1 * g))
    for f2 in range(g):
        for ff in range(n1):
            perm[f2 * n1 + ff, ff * g + f2] = 1.0
    return n1, n2, m1k, m2, perm, bdc, bds


def _mxu_const(a):
    return jnp.asarray(a, dtype=F32).astype(BF16)


def _stage1_into(mk_ref, x_at, a_scr, group):
    nf, _, n2, c = a_scr.shape

    def body(s, carry):
        r0 = pl.multiple_of(s * group, group)
        xs = x_at(r0)
        xs = xs.reshape(xs.shape[0] * group, c).astype(BF16)
        a_scr[:, :, pl.ds(r0, group), :] = _bdot(mk_ref[...], xs).reshape(nf, 2, group, c)
        return carry

    lax.fori_loop(0, n2 // group, body, 0, unroll=True)


def _stage2_rows(a_scr, f1):
    _, _, n2, c = a_scr.shape
    return a_scr[f1].reshape(2 * n2, c).astype(BF16)


def _filter_body(zt_ref, t_ref, w1t_ref, b1_ref, fr_ref, w2t_ref, b2_ref, w3_ref, dl_ref, m1k_ref, m2_ref,
                 kr_ref, ki_ref, h_scr, kt_scr, af_scr, ab_scr):
    o, j = pl.program_id(0), pl.program_id(1)
    fb = m2_ref.shape[0]
    h1, _, n2, c = af_scr.shape
    g = F32_SUBLANES

    @pl.when((o == 0) & (j == 0))
    def _():
        h = jnp.sin(fr_ref[:, 0:1] * (_dot3(w1t_ref[...], zt_ref[...]) + b1_ref[...]))
        h = jnp.sin(fr_ref[:, 1:2] * (_dot3(w2t_ref[...], h) + b2_ref[...]))
        h_scr[...] = h.T

    @pl.when(j == 0)
    def _():
        k = _dot3(h_scr[...], w3_ref[0])
        decay = jnp.exp(-t_ref[...] * jnp.abs(dl_ref[...]))
        kf = k[:, :c] * decay
        row = lax.broadcasted_iota(jnp.int32, kf.shape, 0)
        kb = jnp.where(row == 0, 0.0, k[:, c:] * decay)
        norm = jnp.sum(jnp.abs(kf), axis=0, keepdims=True) + jnp.sum(jnp.abs(kb), axis=0, keepdims=True)
        inv = 1.0 / norm
        kt_scr[0] = (kf * inv).reshape(h1, n2, c)
        kt_scr[1] = (kb * inv).reshape(h1, n2, c)
        _stage1_into(m1k_ref, lambda r0: kt_scr[0, :, pl.ds(r0, g), :], af_scr, g)
        _stage1_into(m1k_ref, lambda r0: kt_scr[1, :, pl.ds(r0, g), :], ab_scr, g)

    for i in range(fb):
        f1 = j * fb + i
        xf = _bdot(m2_ref[i], _stage2_rows(af_scr, f1))
        xb = _bdot(m2_ref[i], _stage2_rows(ab_scr, f1))
        kr_ref[0, i] = xf[:n2] + xb[:n2]
        ki_ref[0, i] = xf[n2:] - xb[n2:]


def _hyena_filter_spectrum(seq, w1, b1, freq, w2, b2, w3, m1k, m2, fb):
    c = D_HYENA
    h1, n2 = m2.shape[0], m2.shape[1] // 2
    t = np.linspace(0.0, 1.0, seq)[:, None]
    w = 2.0 * np.pi * np.arange(seq)[:, None] / seq
    f = np.linspace(1e-4, POS_BANDS - 1, POS_BANDS)[None]
    z = np.concatenate([t, np.cos(f * w), -np.sin(f * w)], axis=-1).astype(np.float32)
    pe = z.shape[1]
    zt = jnp.asarray(np.pad(z, ((0, 0), (0, LANES - pe))).T)
    tcol = jnp.asarray(t.astype(np.float32))
    w1t = jnp.pad(w1, ((0, LANES - pe), (0, 0))).T
    max_decay = math.log(HYENA_DECAY_TARGET) / HYENA_FAST_PCT
    min_decay = math.log(HYENA_DECAY_TARGET) / HYENA_SLOW_PCT
    deltas = jnp.asarray(np.linspace(min_decay, max_decay, c, dtype=np.float32)[None])
    hid = w1.shape[1]
    w3r = jnp.transpose(w3.reshape(hid, 2, 2 * c), (1, 0, 2))
    full = lambda a: pl.BlockSpec(a.shape, lambda o, j: (0,) * a.ndim)
    b1c, b2c, frt = b1.reshape(hid, 1), b2.reshape(hid, 1), freq.T
    out = jax.ShapeDtypeStruct((2, h1, n2, c), F32)
    return pl.pallas_call(
        _filter_body,
        grid=(2, h1 // fb),
        in_specs=[full(zt), full(tcol), full(w1t), full(b1c), full(frt), full(w2.T), full(b2c),
                  pl.BlockSpec((1, hid, 2 * c), lambda o, j: (o, 0, 0)), full(deltas), full(m1k),
                  pl.BlockSpec((fb, 2 * n2, 2 * n2), lambda o, j: (j, 0, 0))],
        out_specs=[pl.BlockSpec((1, fb, n2, c), lambda o, j: (o, j, 0, 0))] * 2,
        out_shape=[out, out],
        scratch_shapes=[pltpu.VMEM((seq, hid), F32), pltpu.VMEM((2, h1, n2, c), F32),
                        pltpu.VMEM((h1, 2, n2, c), F32), pltpu.VMEM((h1, 2, n2, c), F32)],
        compiler_params=_params("arbitrary", "arbitrary"),
        name="hyena_filter_spectrum",
    )(zt, tcol, w1t, b1c, frt, w2.T, b2c, w3r, deltas, m1k, m2)


def _conv_fwd_body(m1k_ref, m2_ref, m2t_ref, x_ref, kr_ref, ki_ref, o_ref, a_scr):
    h1, _, n2, c = a_scr.shape
    g = F32_SUBLANES
    _stage1_into(m1k_ref, lambda r0: x_ref[0, 0, :, pl.ds(r0, g), :], a_scr, g)

    def per_f1(f1, carry):
        x = _bdot(m2_ref[f1], _stage2_rows(a_scr, f1))
        xr, xi = x[:n2], x[n2:]
        kr, ki = kr_ref[0, f1], ki_ref[0, f1]
        y = jnp.concatenate([kr * xr - ki * xi, kr * xi + ki * xr], axis=0).astype(BF16)
        o_ref[0, pl.ds(2 * f1, 2)] = _bdot(m2t_ref[f1], y).astype(BF16).reshape(2, n2, c)
        return carry

    lax.fori_loop(0, h1, per_f1, 0, unroll=True)


def _conv_fwd(m1k, m2, m2t, x, sel, kr, ki, order):
    b, _, h1, n2, c = x.shape
    resident = lambda a, idx: pl.BlockSpec(a.shape, lambda i: idx, pipeline_mode=pl.Buffered(1))
    kspec = pl.BlockSpec((1, h1, n2, c), lambda i: (order, 0, 0, 0), pipeline_mode=pl.Buffered(1))
    return pl.pallas_call(
        _conv_fwd_body,
        grid=(b,),
        in_specs=[resident(m1k, (0, 0)), resident(m2, (0, 0, 0)), resident(m2t, (0, 0, 0)),
                  pl.BlockSpec((1, 1, h1, n2, c), lambda i: (i, sel, 0, 0, 0)),
                  kspec, kspec],
        out_specs=pl.BlockSpec((1, 2 * h1, n2, c), lambda i: (i, 0, 0, 0)),
        out_shape=jax.ShapeDtypeStruct((b, 2 * h1, n2, c), BF16),
        scratch_shapes=[pltpu.VMEM((h1, 2, n2, c), F32)],
        compiler_params=_params("parallel"),
        name="hyena_conv_fwd",
    )(m1k, m2, m2t, x, kr, ki)


def _conv_inv_body(m3k_ref, b_ref, gate_ref, skipin_ref, skip_ref, o_ref):
    n1, ts, c = b_ref.shape[1:]
    h1 = n1 // 2
    g = BF16_SUBLANES
    for s in range(ts // g):
        rows = slice(s * g, (s + 1) * g)
        bs = b_ref[0, :, rows, :].reshape(n1 * g, c)
        y = _bdot(m3k_ref[...], bs).reshape(h1, g, c)
        z = gate_ref[0, 0, :, rows, :] * (y + skipin_ref[0, 0, :, rows, :] * skip_ref[...])
        o_ref[0, 0, :, rows, :] = z.astype(o_ref.dtype)


def _conv_inv(m3k, bm, gates, gate_sel, skipin, skipin_sel, skip_row, out_dtype):
    b, n1, n2, c = bm.shape
    h1 = n1 // 2
    ts = min(n2, 2 * BF16_SUBLANES)
    tok = lambda sel: pl.BlockSpec((1, 1, h1, ts, c), lambda i, j: (i, sel, 0, j, 0))
    return pl.pallas_call(
        _conv_inv_body,
        grid=(b, n2 // ts),
        in_specs=[pl.BlockSpec(m3k.shape, lambda i, j: (0, 0)),
                  pl.BlockSpec((1, n1, ts, c), lambda i, j: (i, 0, j, 0)),
                  tok(gate_sel), tok(skipin_sel),
                  pl.BlockSpec((1, c), lambda i, j: (0, 0))],
        out_specs=tok(0),
        out_shape=jax.ShapeDtypeStruct((b, 1, h1, n2, c), out_dtype),
        compiler_params=_params("parallel", "parallel"),
        name="hyena_conv_inv",
    )(m3k, bm, gates, skipin, skip_row)


def _hyena_mixer(gates, v, w1, b1, freq, w2, b2, w3, skip):
    b, _, seq, c = v.shape
    n1, n2, m1k, m2, m2t, m3k = _hyena_consts(seq)
    h1 = n1 // 2
    m1k, m2, m2t, m3k = (_mxu_const(a) for a in (m1k, m2, m2t, m3k))

    kr, ki = _hyena_filter_spectrum(seq, w1, b1, freq, w2, b2, w3, m1k, m2, min(h1, 8))

    gates = gates.reshape(b, 2, h1, n2, c)
    v = v.reshape(b, 1, h1, n2, c)
    bm = _conv_fwd(m1k, m2, m2t, v, 0, kr, ki, 0)
    z = _conv_inv(m3k, bm, gates, 0, v, 0, skip[0:1], F32)
    bm = _conv_fwd(m1k, m2, m2t, z, 0, kr, ki, 1)
    y = _conv_inv(m3k, bm, gates, 1, z, 0, skip[1:2], BF16)
    return y.reshape(b, seq, c)


def _fnet_body(m1k_ref, m2_ref, perm_ref, bdc_ref, bds_ref, g_ref, o_ref, a_scr):
    j = pl.program_id(1)
    fb = m2_ref.shape[0]
    n1, _, n2, c = a_scr.shape

    @pl.when(j == 0)
    def _():
        gi = BF16_SUBLANES
        _stage1_into(m1k_ref, lambda r0: g_ref[0, :, pl.ds(r0, gi), :], a_scr, gi)

    for i in range(fb):
        f1 = j * fb + i
        a_scr[f1] = _bdot(m2_ref[i], _stage2_rows(a_scr, f1)).reshape(2, n2, c)

    @pl.when(j == pl.num_programs(1) - 1)
    def _():
        go = F32_SUBLANES

        def body(s, carry):
            r0 = pl.multiple_of(s * go, go)
            re = a_scr[:, 0, pl.ds(r0, go), :].reshape(n1 * go, c).astype(BF16)
            im = a_scr[:, 1, pl.ds(r0, go), :].reshape(n1 * go, c).astype(BF16)
            xr = _bdot(perm_ref[...], re).astype(BF16)
            xi = _bdot(perm_ref[...], im).astype(BF16)
            out = _bdot(xr, bdc_ref[...]) + _bdot(xi, bds_ref[...])
            o_ref[0, pl.ds(pl.multiple_of(s * go * n1, go * n1), go * n1), :] = out.astype(o_ref.dtype)
            return carry

        lax.fori_loop(0, n2 // go, body, 0, unroll=True)


def _fourier_mix(g):
    b, seq, c = g.shape
    n1, n2, m1k, m2, perm, bdc, bds = _fnet_consts(seq)
    m1k, m2, perm, bdc, bds = (_mxu_const(a) for a in (m1k, m2, perm, bdc, bds))
    fb = min(n1, 8)
    const = lambda a: pl.BlockSpec(a.shape, lambda i, j: (0,) * a.ndim)
    return pl.pallas_call(
        _fnet_body,
        grid=(b, n1 // fb),
        in_specs=[const(m1k),
                  pl.BlockSpec((fb, 2 * n2, 2 * n2), lambda i, j: (j, 0, 0)),
                  const(perm), const(bdc), const(bds),
                  pl.BlockSpec((1, n1, n2, c), lambda i, j: (i, 0, 0, 0))],
        out_specs=pl.BlockSpec((1, seq, c), lambda i, j: (i, 0, 0)),
        out_shape=jax.ShapeDtypeStruct((b, seq, c), BF16),
        scratch_shapes=[pltpu.VMEM((n1, 2, n2, c), F32)],
        compiler_params=_params("parallel", "arbitrary"),
        name="fnet",
    )(m1k, m2, perm, bdc, bds, g.reshape(b, n1, n2, c))


def _dot_nt(a, b):
    return lax.dot_general(a, b, (((1,), (1,)), ((), ())), preferred_element_type=F32)


def _head_pair_attention(q2, keys, values, biases):
    lane = lax.broadcasted_iota(jnp.int32, q2.shape, 1)
    outs = []
    for hh in range(2):
        in_head = (lane >= hh * HEAD_DIM) & (lane < (hh + 1) * HEAD_DIM)
        qm = jnp.where(in_head, q2, jnp.zeros_like(q2))
        m = denom = acc = None
        for kk, vv, bias in zip(keys, values, biases[hh]):
            s = _dot_nt(qm, kk)
            if bias is not None:
                s = s + bias
            m_grp = s.max(axis=-1, keepdims=True)
            m_new = m_grp if m is None else jnp.maximum(m, m_grp)
            e = jnp.exp2(s - m_new)
            e_part = e[:, :LANES]
            for c0 in range(LANES, e.shape[1], LANES):
                e_part = e_part + e[:, c0:c0 + LANES]
            pv = _bdot(e.astype(BF16), vv)
            if m is None:
                denom, acc = e_part, pv
            else:
                alpha = jnp.exp2(m - m_new)
                denom = alpha * denom + e_part
                acc = alpha * acc + pv
            m = m_new
        outs.append(acc / denom.sum(axis=-1, keepdims=True))
    return jnp.where(lane < HEAD_DIM, outs[0], outs[1])


def _na_window_base(row_block, rows):
    return jnp.clip(row_block * NA_QROWS - NA_KH // 2, 0, rows - NA_KROWS)


def _na_body(rows, q_ref, k_ref, v_ref, kc_ref, vc_ref, bias_ref, o_ref):
    base = _na_window_base(pl.program_id(1), rows)
    start = pl.multiple_of(base * GRID_W, GRID_W)
    n_keys = NA_KROWS * GRID_W
    n_pairs = q_ref.shape[2] // LANES
    groups = [(c0, min(NA_KEY_GROUP, n_keys - c0)) for c0 in range(0, n_keys, NA_KEY_GROUP)]
    for p in range(n_pairs):
        sl = slice(p * LANES, (p + 1) * LANES)
        q2 = q_ref[0, :, sl]
        keys = [kc_ref[0, :, sl]] + [k_ref[0, pl.ds(start + c0, n), sl] for c0, n in groups]
        values = [vc_ref[0, :, sl]] + [v_ref[0, pl.ds(start + c0, n), sl] for c0, n in groups]
        biases = [[None] + [bias_ref[0, 2 * p + hh, 0, :, c0:c0 + n] for c0, n in groups] for hh in range(2)]
        o = _head_pair_attention(q2, keys, values, biases)
        o_ref[0, :, sl] = o.astype(o_ref.dtype)


def _na_bias_body(rows, p_ref, o_ref):
    kh, kw, w = NA_KH, NA_KW, GRID_W
    shape = (w, 2 * w)
    qc = lax.broadcasted_iota(jnp.int32, shape, 0)
    lane = lax.broadcasted_iota(jnp.int32, shape, 1)
    kc = lane & (w - 1)
    col_start = jnp.clip(qc - kw // 2, 0, w - kw)
    col_valid = (kc >= col_start) & (kc < col_start + kw)
    low_half = lane < w
    masked = jnp.full(shape, MASK_VALUE, F32)
    tiles = []
    for a in range(2 * kh - 1):
        row = jnp.broadcast_to(p_ref[0, 0, a:a + 1, :], shape)
        t = pltpu.roll(row, w + 1, 1, stride=1, stride_axis=0)
        t = jnp.where(low_half, t, pltpu.roll(t, w, 1))
        tiles.append(jnp.where(col_valid, t, masked))
    for case, r0 in enumerate((0, NA_QROWS, rows - NA_QROWS)):
        base = int(np.clip(r0 - kh // 2, 0, rows - NA_KROWS))
        for qr in range(NA_QROWS):
            q_abs = r0 + qr
            r_start = int(np.clip(q_abs - kh // 2, 0, rows - kh))

            def tile_for(kr):
                k_abs = base + kr
                if kr >= NA_KROWS or not (r_start <= k_abs < r_start + kh):
                    return masked
                return tiles[k_abs - q_abs + kh - 1]

            q_rows = slice(qr * w, (qr + 1) * w)
            for pair in range((NA_KROWS + 1) // 2):
                chunk = jnp.where(low_half, tile_for(2 * pair), tile_for(2 * pair + 1))
                width = min(2 * w, NA_KROWS * w - pair * 2 * w)
                o_ref[0, 0, case, q_rows, pair * 2 * w:pair * 2 * w + width] = chunk[:, :width]


def _na_bias_tables(rpb, rows):
    w = GRID_W
    depth, nh, na, nb = rpb.shape
    assert 2 * w == LANES and nb == 2 * NA_KW - 1
    lo = w - NA_KW
    padded = jnp.pad(rpb.astype(F32) * LOG2E, ((0, 0), (0, 0), (0, (-na) % 8), (lo, 2 * w - nb - lo)))
    return pl.pallas_call(
        functools.partial(_na_bias_body, rows),
        grid=(depth, nh),
        in_specs=[pl.BlockSpec((1, 1) + padded.shape[2:], lambda l, h: (l, h, 0, 0))],
        out_specs=pl.BlockSpec((1, 1, 3, NA_QROWS * w, NA_KROWS * w), lambda l, h: (l, h, 0, 0, 0)),
        out_shape=jax.ShapeDtypeStruct((depth, nh, 3, NA_QROWS * w, NA_KROWS * w), F32),
        compiler_params=_params("parallel", "parallel"),
        name="na_bias_tables",
    )(padded)


def _neighbourhood_attention(q, kv, kv_ctx, bias, layer):
    b, seq, _ = q.shape
    rows = seq // GRID_W
    lc = kv_ctx.shape[1]
    assert rows % NA_QROWS == 0 and rows >= NA_KROWS + 1 and NA_QROWS == NA_KH // 2
    n_blocks = rows // NA_QROWS
    nh = bias.shape[1]
    tq = NA_QROWS * GRID_W

    def bias_map(i, r):
        return (layer, 0, jnp.where(r == 0, 0, jnp.where(r == n_blocks - 1, 2, 1)), 0, 0)

    return pl.pallas_call(
        functools.partial(_na_body, rows),
        grid=(b, n_blocks),
        in_specs=[pl.BlockSpec((1, tq, D_NA), lambda i, r: (i, r, 0)),
                  pl.BlockSpec((1, seq, D_NA), lambda i, r: (i, 0, 0)),
                  pl.BlockSpec((1, seq, D_NA), lambda i, r: (i, 0, 1)),
                  pl.BlockSpec((1, lc, D_NA), lambda i, r: (i, 0, 0)),
                  pl.BlockSpec((1, lc, D_NA), lambda i, r: (i, 0, 1)),
                  pl.BlockSpec((1, nh, 1, tq, NA_KROWS * GRID_W), bias_map)],
        out_specs=pl.BlockSpec((1, tq, D_NA), lambda i, r: (i, r, 0)),
        out_shape=jax.ShapeDtypeStruct((b, seq, D_NA), BF16),
        compiler_params=_params("parallel", "arbitrary"),
        name="neighbourhood_attention",
    )(q, kv, kv, kv_ctx, kv_ctx, bias)


def _ctx_attn_body(q_ref, k_ref, v_ref, o_ref):
    o = _head_pair_attention(q_ref[0], [k_ref[0]], [v_ref[0]], [[None], [None]])
    o_ref[0] = o.astype(o_ref.dtype)


def _context_attention(q, kv):
    b, lc, _ = q.shape
    n_pairs = D_NA // LANES
    return pl.pallas_call(
        _ctx_attn_body,
        grid=(b, n_pairs),
        in_specs=[pl.BlockSpec((1, lc, LANES), lambda i, p: (i, 0, p)),
                  pl.BlockSpec((1, lc, LANES), lambda i, p: (i, 0, p)),
                  pl.BlockSpec((1, lc, LANES), lambda i, p: (i, 0, n_pairs + p))],
        out_specs=pl.BlockSpec((1, lc, LANES), lambda i, p: (i, 0, p)),
        out_shape=jax.ShapeDtypeStruct((b, lc, D_NA), BF16),
        compiler_params=_params("parallel", "parallel"),
        name="context_attention",
    )(q, kv, kv)


def kernel(x, c, ctx, c_ctx, w_mod, b_mod, norm1_g, norm2_g, w_in, w_out, hy_conv_w, hy_conv_b,
           filt_w1, filt_b1, filt_freq, filt_w2, filt_b2, filt_w3, hy_skip, na_rpb,
           w_gate, w_up, w_down, final_g):
    depth = w_mod.shape[0]
    b, seq, d = x.shape
    off_k = 3 * D_HYENA + D_FNET + D_NA
    qk_scale = LOG2E / math.sqrt(HEAD_DIM)
    splits = ((3 * D_HYENA, F32, None), (D_FNET, BF16, None), (D_NA, BF16, qk_scale), (2 * D_NA, BF16, None))
    tm_proj, tm = 1024, 512

    pad = (-(b + 1)) % 8
    c_all = jnp.concatenate([c, c_ctx[None], jnp.zeros((pad, d), F32)], axis=0)
    mod = _modulation(c_all, w_mod, b_mod).reshape(depth, b + 1 + pad, 6, 1, d)
    lat_row = lambda i: i
    ctx_row = lambda i: b

    w_in_b, w_out_b = w_in.astype(BF16), w_out.astype(BF16)
    w_gate_b, w_up_b, w_down_b = w_gate.astype(BF16), w_up.astype(BF16), w_down.astype(BF16)
    w_kv_last = w_in_b[depth - 1:, :, off_k:]
    n1g, n2g = norm1_g.reshape(depth, 1, d), norm2_g.reshape(depth, 1, d)
    fg = final_g.reshape(1, d)
    na_bias = _na_bias_tables(na_rpb, seq // GRID_W)

    for l in range(depth):
        last = l == depth - 1
        hy_params = (filt_w1[l], filt_b1[l], filt_freq[l], filt_w2[l], filt_b2[l], filt_w3[l], hy_skip[l])
        conv = (hy_conv_w[l], hy_conv_b[l].reshape(1, 3 * D_HYENA))

        gates, v, fn, q, kv = _norm_proj(x, n1g, mod, lat_row, l, w_in_b, l, splits, tm_proj, conv)
        if last:
            (kv_c,) = _norm_proj(ctx, n1g, mod, ctx_row, l, w_kv_last, 0, ((2 * D_NA, BF16, None),), tm_proj)
        else:
            cgates, cv, cfn, cq, kv_c = _norm_proj(ctx, n1g, mod, ctx_row, l, w_in_b, l, splits, tm_proj, conv)

        y_hy = _hyena_mixer(gates, v, *hy_params)
        y_fn = _fourier_mix(fn)
        y_na = _neighbourhood_attention(q, kv, kv_c, na_bias, l)

        x = _mix_ffn(x, y_hy, y_fn, y_na, w_out_b, n2g, mod, lat_row, l, w_gate_b, w_up_b, w_down_b,
                     fg, last, tm)

        if not last:
            cy_hy = _hyena_mixer(cgates, cv, *hy_params)
            cy_fn = _fourier_mix(cfn)
            cy_na = _context_attention(cq, kv_c)
            ctx = _mix_ffn(ctx, cy_hy, cy_fn, cy_na, w_out_b, n2g, mod, ctx_row, l, w_gate_b, w_up_b, w_down_b,
                           fg, False, tm)
    return x
```

```python
import functools
import math

import numpy as np
import jax
import jax.numpy as jnp
from jax import lax
from jax.experimental import pallas as pl
from jax.experimental.pallas import tpu as pltpu

F32 = jnp.float32
BF16 = jnp.bfloat16

EPS = 1e-6
GRID_W = 64
HEAD_DIM = 64
D_HYENA = 256
D_FNET = 256
D_NA = 512
FNET_GROUP_DIM = 64
NA_KH = 8
NA_KW = 16
NA_QROWS = 4
NA_KROWS = NA_QROWS + NA_KH
NA_KEY_GROUP = 256
POS_BANDS = 16
HYENA_DECAY_TARGET = 1e-2
HYENA_FAST_PCT = 0.3
HYENA_SLOW_PCT = 1.5
MASK_VALUE = -1e30
VMEM_LIMIT = 56 * 1024 * 1024
LANES = 128
MXU_WIDTH = 256
LOG2E = math.log2(math.e)
FFN_ROW_SPLIT = 2
F32_SUBLANES = 8
BF16_SUBLANES = 16


def _params(*sem):
    return pltpu.CompilerParams(dimension_semantics=sem, vmem_limit_bytes=VMEM_LIMIT)


def _bdot(a, b):
    return jnp.dot(a, b, preferred_element_type=F32)


def _split_bf16(a):
    hi = a.astype(BF16)
    lo = (a - hi.astype(F32)).astype(BF16)
    return hi, lo


def _dot3(a, b):
    a_hi, a_lo = _split_bf16(a)
    b_hi, b_lo = _split_bf16(b)
    return _bdot(a_hi, b_hi) + _bdot(a_lo, b_hi) + _bdot(a_hi, b_lo)


def _mod_body(c_ref, w_ref, b_ref, o_ref):
    c = c_ref[...]
    o_ref[0] = _dot3(c * jax.nn.sigmoid(c), w_ref[0]) + b_ref[0]


def _modulation(c_all, w_mod, b_mod):
    depth, d, n = w_mod.shape
    rows = c_all.shape[0]
    tn = 1536
    return pl.pallas_call(
        _mod_body,
        grid=(depth, n // tn),
        in_specs=[
            pl.BlockSpec((rows, d), lambda l, j: (0, 0)),
            pl.BlockSpec((1, d, tn), lambda l, j: (l, 0, j)),
            pl.BlockSpec((1, 1, tn), lambda l, j: (l, 0, j)),
        ],
        out_specs=pl.BlockSpec((1, rows, tn), lambda l, j: (l, 0, j)),
        out_shape=jax.ShapeDtypeStruct((depth, rows, n), F32),
        compiler_params=_params("parallel", "parallel"),
        name="modulation",
    )(c_all, w_mod, b_mod.reshape(depth, 1, n))


def _modulated_norm(x, g, sc, sh):
    ms = jnp.mean(x * x, axis=-1, keepdims=True)
    y = x * lax.rsqrt(ms + EPS) * g
    return y * (1.0 + sc) + sh


def _mod_spec(layer, row_of, chunk, d):
    return pl.BlockSpec((1, 1, 1, 1, d), lambda i, *_: (layer, row_of(i), chunk, 0, 0))


def _short_conv_tile(u, prev_row, next_row, cw_ref, cb_ref):
    rows = u.shape[0]
    row = lax.broadcasted_iota(jnp.int32, u.shape, 0)
    prev = jnp.where(row == 0, prev_row, pltpu.roll(u, 1, axis=0))
    nxt = jnp.where(row == rows - 1, next_row, pltpu.roll(u, rows - 1, axis=0))
    return prev * cw_ref[0:1] + u * cw_ref[1:2] + nxt * cw_ref[2:3] + cb_ref[...]


def _norm_proj_body(splits, conv, x_ref, g_ref, sc_ref, sh_ref, w_ref, *rest):
    o_refs = ((rest[4:6],) + rest[6:]) if conv else rest
    g, sc, sh = g_ref[0], sc_ref[0, 0, 0], sh_ref[0, 0, 0]
    p = _bdot(_modulated_norm(x_ref[0], g, sc, sh).astype(BF16), w_ref[0])
    off = 0
    for idx, (o_ref, (width, _, scale)) in enumerate(zip(o_refs, splits)):
        part = p[:, off:off + width]
        if conv and idx == 0:
            xp_ref, xn_ref, cw_ref, cb_ref = rest[:4]
            gate_ref, v_ref = o_ref
            j, nj = pl.program_id(1), pl.num_programs(1)
            wc = w_ref[0, :, off:off + width]
            edge = F32_SUBLANES
            pp = _bdot(_modulated_norm(xp_ref[0], g, sc, sh).astype(BF16), wc)
            pn = _bdot(_modulated_norm(xn_ref[0], g, sc, sh).astype(BF16), wc)
            prev_row = jnp.where(j == 0, 0.0, pp[edge - 1:edge])
            next_row = jnp.where(j == nj - 1, 0.0, pn[0:1])
            u = _short_conv_tile(part, prev_row, next_row, cw_ref, cb_ref)
            c = v_ref.shape[3]
            n_gates = gate_ref.shape[1]
            for gi in range(n_gates):
                gate_ref[0, gi] = u[:, gi * c:(gi + 1) * c].astype(gate_ref.dtype)
            v_ref[0, 0] = u[:, n_gates * c:]
        else:
            o_ref[0] = (part if scale is None else part * scale).astype(o_ref.dtype)
        off += width


def _norm_proj(x, norm_g, mod, row_of, layer, w, w_idx, splits, tm, conv=None):
    b, t, d = x.shape
    n = w.shape[2]
    tm = min(tm, t)
    edge = F32_SUBLANES
    tok = lambda wd: pl.BlockSpec((1, tm, wd), lambda i, j: (i, j, 0))
    in_specs = [
        tok(d),
        pl.BlockSpec((1, 1, d), lambda i, j: (layer, 0, 0)),
        _mod_spec(layer, row_of, 1, d),
        _mod_spec(layer, row_of, 0, d),
        pl.BlockSpec((1, d, n), lambda i, j: (w_idx, 0, 0)),
    ]
    args = [x, norm_g, mod, mod, w]
    out_specs = [tok(wd) for wd, _, _ in splits]
    out_shape = [jax.ShapeDtypeStruct((b, t, wd), dt) for wd, dt, _ in splits]
    if conv is not None:
        cw, cb = conv
        w0, c = splits[0][0], D_HYENA
        per_tile = tm // edge
        in_specs += [
            pl.BlockSpec((1, edge, d), lambda i, j: (i, jnp.maximum(j * per_tile - 1, 0), 0)),
            pl.BlockSpec((1, edge, d), lambda i, j: (i, jnp.minimum((j + 1) * per_tile, t // edge - 1), 0)),
            pl.BlockSpec(cw.shape, lambda i, j: (0, 0)),
            pl.BlockSpec(cb.shape, lambda i, j: (0, 0)),
        ]
        args += [x, x, cw, cb]
        n_gates = w0 // c - 1
        out_specs[0:1] = [pl.BlockSpec((1, n_gates, tm, c), lambda i, j: (i, 0, j, 0)),
                          pl.BlockSpec((1, 1, tm, c), lambda i, j: (i, 0, j, 0))]
        out_shape[0:1] = [jax.ShapeDtypeStruct((b, n_gates, t, c), BF16),
                          jax.ShapeDtypeStruct((b, 1, t, c), splits[0][1])]
    return pl.pallas_call(
        functools.partial(_norm_proj_body, tuple(splits), conv is not None),
        grid=(b, t // tm),
        in_specs=in_specs,
        out_specs=out_specs,
        out_shape=out_shape,
        compiler_params=_params("parallel", "parallel"),
        name="norm_proj",
    )(*args)


def _mix_ffn_body(final_norm, x_ref, hy_ref, fn_ref, na_ref, wo_ref, g1_ref, ng_ref, sc_ref, sh_ref, g2_ref,
                  wg_ref, wu_ref, wd_ref, fg_ref, o_ref):
    tm = x_ref.shape[1]
    sub = tm // FFN_ROW_SPLIT if tm % (FFN_ROW_SPLIT * MXU_WIDTH) == 0 else tm
    halves = [slice(r0, r0 + sub) for r0 in range(0, tm, sub)]
    x1s, hs = [], []
    for rows in halves:
        mix = (_bdot(hy_ref[0, rows], wo_ref[0, :D_HYENA])
               + _bdot(fn_ref[0, rows], wo_ref[0, D_HYENA:D_HYENA + D_FNET])
               + _bdot(na_ref[0, rows], wo_ref[0, D_HYENA + D_FNET:]))
        x1 = x_ref[0, rows] + g1_ref[0, 0, 0] * mix
        x1s.append(x1)
        hs.append(_modulated_norm(x1, ng_ref[0], sc_ref[0, 0, 0], sh_ref[0, 0, 0]).astype(BF16))
    accs = [None] * len(halves)
    for c0 in range(0, wg_ref.shape[2], MXU_WIDTH):
        cols = slice(c0, c0 + MXU_WIDTH)
        for s, h in enumerate(hs):
            a = _bdot(h, wg_ref[0, :, cols])
            u = _bdot(h, wu_ref[0, :, cols])
            act = (a * jax.nn.sigmoid(a) * u).astype(BF16)
            part = _bdot(act, wd_ref[0, cols, :])
            accs[s] = part if accs[s] is None else accs[s] + part
    for rows, x1, acc in zip(halves, x1s, accs):
        y = x1 + g2_ref[0, 0, 0] * acc
        if final_norm:
            ms = jnp.mean(y * y, axis=-1, keepdims=True)
            y = y * lax.rsqrt(ms + EPS) * fg_ref[...]
        o_ref[0, rows] = y


def _mix_ffn(x, y_hy, y_fn, y_na, w_out, norm_g, mod, row_of, layer, wg, wu, wd, final_g, final_norm, tm):
    b, t, d = x.shape
    dff = wg.shape[2]
    assert dff % MXU_WIDTH == 0
    tm = min(tm, t)
    tok = lambda wd_: pl.BlockSpec((1, tm, wd_), lambda i, j: (i, j, 0))
    resident = lambda shape: pl.BlockSpec((1,) + shape, lambda i, j: (layer, 0, 0), pipeline_mode=pl.Buffered(1))
    return pl.pallas_call(
        functools.partial(_mix_ffn_body, final_norm),
        grid=(b, t // tm),
        in_specs=[
            tok(d), tok(D_HYENA), tok(D_FNET), tok(D_NA),
            resident((d, d)),
            _mod_spec(layer, row_of, 2, d),
            pl.BlockSpec((1, 1, d), lambda i, j: (layer, 0, 0)),
            _mod_spec(layer, row_of, 4, d),
            _mod_spec(layer, row_of, 3, d),
            _mod_spec(layer, row_of, 5, d),
            resident((d, dff)), resident((d, dff)), resident((dff, d)),
            pl.BlockSpec((1, d), lambda i, j: (0, 0)),
        ],
        out_specs=tok(d),
        out_shape=jax.ShapeDtypeStruct(x.shape, F32),
        compiler_params=_params("parallel", "parallel"),
        name="mix_ffn",
    )(x, y_hy, y_fn, y_na, w_out, mod, norm_g, mod, mod, mod, wg, wu, wd, final_g)


def _split_seq(n_total):
    log = int(round(math.log2(n_total)))
    assert 2 ** log == n_total
    n1 = 2 ** (log // 2)
    return n1, n_total // n1


@functools.lru_cache(maxsize=None)
def _hyena_consts(seq):
    n = 2 * seq
    n1, n2 = _split_seq(n)
    h1 = n1 // 2
    f1 = np.arange(h1)[:, None]
    t1 = np.arange(h1)[None, :]
    th = np.pi * (2 * f1 + 1) * t1 / n1
    m1 = np.empty((n1, h1))
    m1[0::2] = np.cos(th)
    m1[1::2] = -np.sin(th)
    m3 = m1.T * (2.0 / n)
    f = (np.arange(h1)[:, None, None] + n1 * np.arange(n2)[None, :, None])
    t2 = np.arange(n2)[None, None, :]
    ph = np.pi * ((2 * f + 1) * t2 % (2 * n)) / n
    c, s = np.cos(ph), np.sin(ph)
    m2 = np.concatenate([np.concatenate([c, s], axis=2), np.concatenate([-s, c], axis=2)], axis=1)
    m2t = np.transpose(m2, (0, 2, 1))
    m1k = np.kron(m1, np.eye(F32_SUBLANES))
    m3k = np.kron(m3, np.eye(BF16_SUBLANES))
    return n1, n2, m1k, m2, m2t, m3k


@functools.lru_cache(maxsize=None)
def _fnet_consts(seq):
    n1, n2 = _split_seq(seq)
    n1, n2 = n1 // 2, n2 * 2
    f1 = np.arange(n1)[:, None]
    t1 = np.arange(n1)[None, :]
    th = 2 * np.pi * (f1 * t1 % n1) / n1
    m1 = np.empty((2 * n1, n1))
    m1[0::2] = np.cos(th)
    m1[1::2] = -np.sin(th)
    f = (np.arange(n1)[:, None, None] + n1 * np.arange(n2)[None, :, None])
    t2 = np.arange(n2)[None, None, :]
    ph = 2 * np.pi * (f * t2 % seq) / seq
    c, s = np.cos(ph), np.sin(ph)
    m2 = np.concatenate([np.concatenate([c, s], axis=2), np.concatenate([-s, c], axis=2)], axis=1)
    gd = FNET_GROUP_DIM
    cc = np.arange(gd)
    pg = 2 * np.pi * (np.outer(cc, cc) % gd) / gd
    scale = 1.0 / math.sqrt(gd * seq)
    eye = np.eye(D_FNET // gd)
    bdc = np.kron(eye, np.cos(pg)) * scale
    bds = np.kron(eye, np.sin(pg)) * scale
    m1k = np.kron(m1, np.eye(BF16_SUBLANES))
    g = F32_SUBLANES
    perm = np.zeros((g * n1, n1 * g))
    for f2 in range(g):
        for ff in range(n1):
            perm[f2 * n1 + ff, ff * g + f2] = 1.0
    return n1, n2, m1k, m2, perm, bdc, bds


def _mxu_const(a):
    return jnp.asarray(a, dtype=F32).astype(BF16)


def _stage1_into(mk_ref, x_at, a_scr, group):
    nf, _, n2, c = a_scr.shape

    def body(s, carry):
        r0 = pl.multiple_of(s * group, group)
        xs = x_at(r0)
        xs = xs.reshape(xs.shape[0] * group, c).astype(BF16)
        a_scr[:, :, pl.ds(r0, group), :] = _bdot(mk_ref[...], xs).reshape(nf, 2, group, c)
        return carry

    lax.fori_loop(0, n2 // group, body, 0, unroll=True)


def _stage2_rows(a_scr, f1):
    _, _, n2, c = a_scr.shape
    return a_scr[f1].reshape(2 * n2, c).astype(BF16)


def _filter_body(zt_ref, t_ref, w1t_ref, b1_ref, fr_ref, w2t_ref, b2_ref, w3_ref, dl_ref, m1k_ref, m2_ref,
                 kr_ref, ki_ref, h_scr, kt_scr, af_scr, ab_scr):
    o, j = pl.program_id(0), pl.program_id(1)
    fb = m2_ref.shape[0]
    h1, _, n2, c = af_scr.shape
    g = F32_SUBLANES

    @pl.when((o == 0) & (j == 0))
    def _():
        h = jnp.sin(fr_ref[:, 0:1] * (_dot3(w1t_ref[...], zt_ref[...]) + b1_ref[...]))
        h = jnp.sin(fr_ref[:, 1:2] * (_dot3(w2t_ref[...], h) + b2_ref[...]))
        h_scr[...] = h.T

    @pl.when(j == 0)
    def _():
        k = _dot3(h_scr[...], w3_ref[0])
        decay = jnp.exp(-t_ref[...] * jnp.abs(dl_ref[...]))
        kf = k[:, :c] * decay
        row = lax.broadcasted_iota(jnp.int32, kf.shape, 0)
        kb = jnp.where(row == 0, 0.0, k[:, c:] * decay)
        norm = jnp.sum(jnp.abs(kf), axis=0, keepdims=True) + jnp.sum(jnp.abs(kb), axis=0, keepdims=True)
        inv = 1.0 / norm
        kt_scr[0] = (kf * inv).reshape(h1, n2, c)
        kt_scr[1] = (kb * inv).reshape(h1, n2, c)
        _stage1_into(m1k_ref, lambda r0: kt_scr[0, :, pl.ds(r0, g), :], af_scr, g)
        _stage1_into(m1k_ref, lambda r0: kt_scr[1, :, pl.ds(r0, g), :], ab_scr, g)

    for i in range(fb):
        f1 = j * fb + i
        xf = _bdot(m2_ref[i], _stage2_rows(af_scr, f1))
        xb = _bdot(m2_ref[i], _stage2_rows(ab_scr, f1))
        kr_ref[0, i] = xf[:n2] + xb[:n2]
        ki_ref[0, i] = xf[n2:] - xb[n2:]


def _hyena_filter_spectrum(seq, w1, b1, freq, w2, b2, w3, m1k, m2, fb):
    c = D_HYENA
    h1, n2 = m2.shape[0], m2.shape[1] // 2
    t = np.linspace(0.0, 1.0, seq)[:, None]
    w = 2.0 * np.pi * np.arange(seq)[:, None] / seq
    f = np.linspace(1e-4, POS_BANDS - 1, POS_BANDS)[None]
    z = np.concatenate([t, np.cos(f * w), -np.sin(f * w)], axis=-1).astype(np.float32)
    pe = z.shape[1]
    zt = jnp.asarray(np.pad(z, ((0, 0), (0, LANES - pe))).T)
    tcol = jnp.asarray(t.astype(np.float32))
    w1t = jnp.pad(w1, ((0, LANES - pe), (0, 0))).T
    max_decay = math.log(HYENA_DECAY_TARGET) / HYENA_FAST_PCT
    min_decay = math.log(HYENA_DECAY_TARGET) / HYENA_SLOW_PCT
    deltas = jnp.asarray(np.linspace(min_decay, max_decay, c, dtype=np.float32)[None])
    hid = w1.shape[1]
    w3r = jnp.transpose(w3.reshape(hid, 2, 2 * c), (1, 0, 2))
    full = lambda a: pl.BlockSpec(a.shape, lambda o, j: (0,) * a.ndim)
    b1c, b2c, frt = b1.reshape(hid, 1), b2.reshape(hid, 1), freq.T
    out = jax.ShapeDtypeStruct((2, h1, n2, c), F32)
    return pl.pallas_call(
        _filter_body,
        grid=(2, h1 // fb),
        in_specs=[full(zt), full(tcol), full(w1t), full(b1c), full(frt), full(w2.T), full(b2c),
                  pl.BlockSpec((1, hid, 2 * c), lambda o, j: (o, 0, 0)), full(deltas), full(m1k),
                  pl.BlockSpec((fb, 2 * n2, 2 * n2), lambda o, j: (j, 0, 0))],
        out_specs=[pl.BlockSpec((1, fb, n2, c), lambda o, j: (o, j, 0, 0))] * 2,
        out_shape=[out, out],
        scratch_shapes=[pltpu.VMEM((seq, hid), F32), pltpu.VMEM((2, h1, n2, c), F32),
                        pltpu.VMEM((h1, 2, n2, c), F32), pltpu.VMEM((h1, 2, n2, c), F32)],
        compiler_params=_params("arbitrary", "arbitrary"),
        name="hyena_filter_spectrum",
    )(zt, tcol, w1t, b1c, frt, w2.T, b2c, w3r, deltas, m1k, m2)


def _conv_fwd_body(m1k_ref, m2_ref, m2t_ref, x_ref, kr_ref, ki_ref, o_ref, a_scr):
    h1, _, n2, c = a_scr.shape
    g = F32_SUBLANES
    _stage1_into(m1k_ref, lambda r0: x_ref[0, 0, :, pl.ds(r0, g), :], a_scr, g)

    def per_f1(f1, carry):
        x = _bdot(m2_ref[f1], _stage2_rows(a_scr, f1))
        xr, xi = x[:n2], x[n2:]
        kr, ki = kr_ref[0, f1], ki_ref[0, f1]
        y = jnp.concatenate([kr * xr - ki * xi, kr * xi + ki * xr], axis=0).astype(BF16)
        o_ref[0, pl.ds(2 * f1, 2)] = _bdot(m2t_ref[f1], y).astype(BF16).reshape(2, n2, c)
        return carry

    lax.fori_loop(0, h1, per_f1, 0, unroll=True)


def _conv_fwd(m1k, m2, m2t, x, sel, kr, ki, order):
    b, _, h1, n2, c = x.shape
    resident = lambda a, idx: pl.BlockSpec(a.shape, lambda i: idx, pipeline_mode=pl.Buffered(1))
    kspec = pl.BlockSpec((1, h1, n2, c), lambda i: (order, 0, 0, 0), pipeline_mode=pl.Buffered(1))
    return pl.pallas_call(
        _conv_fwd_body,
        grid=(b,),
        in_specs=[resident(m1k, (0, 0)), resident(m2, (0, 0, 0)), resident(m2t, (0, 0, 0)),
                  pl.BlockSpec((1, 1, h1, n2, c), lambda i: (i, sel, 0, 0, 0)),
                  kspec, kspec],
        out_specs=pl.BlockSpec((1, 2 * h1, n2, c), lambda i: (i, 0, 0, 0)),
        out_shape=jax.ShapeDtypeStruct((b, 2 * h1, n2, c), BF16),
        scratch_shapes=[pltpu.VMEM((h1, 2, n2, c), F32)],
        compiler_params=_params("parallel"),
        name="hyena_conv_fwd",
    )(m1k, m2, m2t, x, kr, ki)


def _conv_inv_body(m3k_ref, b_ref, gate_ref, skipin_ref, skip_ref, o_ref):
    n1, ts, c = b_ref.shape[1:]
    h1 = n1 // 2
    g = BF16_SUBLANES
    for s in range(ts // g):
        rows = slice(s * g, (s + 1) * g)
        bs = b_ref[0, :, rows, :].reshape(n1 * g, c)
        y = _bdot(m3k_ref[...], bs).reshape(h1, g, c)
        z = gate_ref[0, 0, :, rows, :] * (y + skipin_ref[0, 0, :, rows, :] * skip_ref[...])
        o_ref[0, 0, :, rows, :] = z.astype(o_ref.dtype)


def _conv_inv(m3k, bm, gates, gate_sel, skipin, skipin_sel, skip_row, out_dtype):
    b, n1, n2, c = bm.shape
    h1 = n1 // 2
    ts = min(n2, 4 * BF16_SUBLANES)
    tok = lambda sel: pl.BlockSpec((1, 1, h1, ts, c), lambda i, j: (i, sel, 0, j, 0))
    return pl.pallas_call(
        _conv_inv_body,
        grid=(b, n2 // ts),
        in_specs=[pl.BlockSpec(m3k.shape, lambda i, j: (0, 0)),
                  pl.BlockSpec((1, n1, ts, c), lambda i, j: (i, 0, j, 0)),
                  tok(gate_sel), tok(skipin_sel),
                  pl.BlockSpec((1, c), lambda i, j: (0, 0))],
        out_specs=tok(0),
        out_shape=jax.ShapeDtypeStruct((b, 1, h1, n2, c), out_dtype),
        compiler_params=_params("parallel", "parallel"),
        name="hyena_conv_inv",
    )(m3k, bm, gates, skipin, skip_row)


def _hyena_mixer(gates, v, w1, b1, freq, w2, b2, w3, skip):
    b, _, seq, c = v.shape
    n1, n2, m1k, m2, m2t, m3k = _hyena_consts(seq)
    h1 = n1 // 2
    m1k, m2, m2t, m3k = (_mxu_const(a) for a in (m1k, m2, m2t, m3k))

    kr, ki = _hyena_filter_spectrum(seq, w1, b1, freq, w2, b2, w3, m1k, m2, min(h1, 8))

    gates = gates.reshape(b, 2, h1, n2, c)
    v = v.reshape(b, 1, h1, n2, c)
    bm = _conv_fwd(m1k, m2, m2t, v, 0, kr, ki, 0)
    z = _conv_inv(m3k, bm, gates, 0, v, 0, skip[0:1], F32)
    bm = _conv_fwd(m1k, m2, m2t, z, 0, kr, ki, 1)
    y = _conv_inv(m3k, bm, gates, 1, z, 0, skip[1:2], BF16)
    return y.reshape(b, seq, c)


def _fnet_body(m1k_ref, m2_ref, perm_ref, bdc_ref, bds_ref, g_ref, o_ref, a_scr):
    j = pl.program_id(1)
    fb = m2_ref.shape[0]
    n1, _, n2, c = a_scr.shape

    @pl.when(j == 0)
    def _():
        gi = BF16_SUBLANES
        _stage1_into(m1k_ref, lambda r0: g_ref[0, :, pl.ds(r0, gi), :], a_scr, gi)

    for i in range(fb):
        f1 = j * fb + i
        a_scr[f1] = _bdot(m2_ref[i], _stage2_rows(a_scr, f1)).reshape(2, n2, c)

    @pl.when(j == pl.num_programs(1) - 1)
    def _():
        go = F32_SUBLANES

        def body(s, carry):
            r0 = pl.multiple_of(s * go, go)
            re = a_scr[:, 0, pl.ds(r0, go), :].reshape(n1 * go, c).astype(BF16)
            im = a_scr[:, 1, pl.ds(r0, go), :].reshape(n1 * go, c).astype(BF16)
            xr = _bdot(perm_ref[...], re).astype(BF16)
            xi = _bdot(perm_ref[...], im).astype(BF16)
            out = _bdot(xr, bdc_ref[...]) + _bdot(xi, bds_ref[...])
            o_ref[0, pl.ds(pl.multiple_of(s * go * n1, go * n1), go * n1), :] = out.astype(o_ref.dtype)
            return carry

        lax.fori_loop(0, n2 // go, body, 0, unroll=True)


def _fourier_mix(g):
    b, seq, c = g.shape
    n1, n2, m1k, m2, perm, bdc, bds = _fnet_consts(seq)
    m1k, m2, perm, bdc, bds = (_mxu_const(a) for a in (m1k, m2, perm, bdc, bds))
    fb = min(n1, 8)
    const = lambda a: pl.BlockSpec(a.shape, lambda i, j: (0,) * a.ndim)
    return pl.pallas_call(
        _fnet_body,
        grid=(b, n1 // fb),
        in_specs=[const(m1k),
                  pl.BlockSpec((fb, 2 * n2, 2 * n2), lambda i, j: (j, 0, 0)),
                  const(perm), const(bdc), const(bds),
                  pl.BlockSpec((1, n1, n2, c), lambda i, j: (i, 0, 0, 0))],
        out_specs=pl.BlockSpec((1, seq, c), lambda i, j: (i, 0, 0)),
        out_shape=jax.ShapeDtypeStruct((b, seq, c), BF16),
        scratch_shapes=[pltpu.VMEM((n1, 2, n2, c), F32)],
        compiler_params=_params("parallel", "arbitrary"),
        name="fnet",
    )(m1k, m2, perm, bdc, bds, g.reshape(b, n1, n2, c))


def _dot_nt(a, b):
    return lax.dot_general(a, b, (((1,), (1,)), ((), ())), preferred_element_type=F32)


def _head_pair_attention(q2, keys, values, biases):
    nq = q2.shape[0]
    lane = lax.broadcasted_iota(jnp.int32, q2.shape, 1)
    zero = jnp.zeros_like(q2)
    qs = jnp.concatenate([jnp.where(lane < HEAD_DIM, q2, zero), jnp.where(lane >= HEAD_DIM, q2, zero)], axis=0)
    m = denom = acc = None
    for gi, (kk, vv) in enumerate(zip(keys, values)):
        s = _dot_nt(qs, kk)
        if biases[0][gi] is not None:
            s = s + jnp.concatenate([biases[0][gi], biases[1][gi]], axis=0)
        m_grp = s.max(axis=-1, keepdims=True)
        m_new = m_grp if m is None else jnp.maximum(m, m_grp)
        e = jnp.exp2(s - m_new)
        e_part = e[:, :LANES]
        for c0 in range(LANES, e.shape[1], LANES):
            e_part = e_part + e[:, c0:c0 + LANES]
        pv = _bdot(e.astype(BF16), vv)
        if m is None:
            denom, acc = e_part, pv
        else:
            alpha = jnp.exp2(m - m_new)
            denom = alpha * denom + e_part
            acc = alpha * acc + pv
        m = m_new
    out = acc / denom.sum(axis=-1, keepdims=True)
    return jnp.where(lane < HEAD_DIM, out[:nq], out[nq:])


def _na_window_base(row_block, rows):
    return jnp.clip(row_block * NA_QROWS - NA_KH // 2, 0, rows - NA_KROWS)


def _na_body(rows, q_ref, k_ref, v_ref, kc_ref, vc_ref, bias_ref, o_ref):
    base = _na_window_base(pl.program_id(1), rows)
    start = pl.multiple_of(base * GRID_W, GRID_W)
    n_keys = NA_KROWS * GRID_W
    n_pairs = q_ref.shape[2] // LANES
    groups = [(c0, min(NA_KEY_GROUP, n_keys - c0)) for c0 in range(0, n_keys, NA_KEY_GROUP)]
    for p in range(n_pairs):
        sl = slice(p * LANES, (p + 1) * LANES)
        q2 = q_ref[0, :, sl]
        keys = [kc_ref[0, :, sl]] + [k_ref[0, pl.ds(start + c0, n), sl] for c0, n in groups]
        values = [vc_ref[0, :, sl]] + [v_ref[0, pl.ds(start + c0, n), sl] for c0, n in groups]
        biases = [[None] + [bias_ref[0, 2 * p + hh, 0, :, c0:c0 + n] for c0, n in groups] for hh in range(2)]
        o = _head_pair_attention(q2, keys, values, biases)
        o_ref[0, :, sl] = o.astype(o_ref.dtype)


def _na_bias_body(rows, p_ref, o_ref):
    kh, kw, w = NA_KH, NA_KW, GRID_W
    shape = (w, 2 * w)
    qc = lax.broadcasted_iota(jnp.int32, shape, 0)
    lane = lax.broadcasted_iota(jnp.int32, shape, 1)
    kc = lane & (w - 1)
    col_start = jnp.clip(qc - kw // 2, 0, w - kw)
    col_valid = (kc >= col_start) & (kc < col_start + kw)
    low_half = lane < w
    masked = jnp.full(shape, MASK_VALUE, F32)
    tiles = []
    for a in range(2 * kh - 1):
        row = jnp.broadcast_to(p_ref[0, 0, a:a + 1, :], shape)
        t = pltpu.roll(row, w + 1, 1, stride=1, stride_axis=0)
        t = jnp.where(low_half, t, pltpu.roll(t, w, 1))
        tiles.append(jnp.where(col_valid, t, masked))
    for case, r0 in enumerate((0, NA_QROWS, rows - NA_QROWS)):
        base = int(np.clip(r0 - kh // 2, 0, rows - NA_KROWS))
        for qr in range(NA_QROWS):
            q_abs = r0 + qr
            r_start = int(np.clip(q_abs - kh // 2, 0, rows - kh))

            def tile_for(kr):
                k_abs = base + kr
                if kr >= NA_KROWS or not (r_start <= k_abs < r_start + kh):
                    return masked
                return tiles[k_abs - q_abs + kh - 1]

            q_rows = slice(qr * w, (qr + 1) * w)
            for pair in range((NA_KROWS + 1) // 2):
                chunk = jnp.where(low_half, tile_for(2 * pair), tile_for(2 * pair + 1))
                width = min(2 * w, NA_KROWS * w - pair * 2 * w)
                o_ref[0, 0, case, q_rows, pair * 2 * w:pair * 2 * w + width] = chunk[:, :width]


def _na_bias_tables(rpb, rows):
    w = GRID_W
    depth, nh, na, nb = rpb.shape
    assert 2 * w == LANES and nb == 2 * NA_KW - 1
    lo = w - NA_KW
    padded = jnp.pad(rpb.astype(F32) * LOG2E, ((0, 0), (0, 0), (0, (-na) % 8), (lo, 2 * w - nb - lo)))
    return pl.pallas_call(
        functools.partial(_na_bias_body, rows),
        grid=(depth, nh),
        in_specs=[pl.BlockSpec((1, 1) + padded.shape[2:], lambda l, h: (l, h, 0, 0))],
        out_specs=pl.BlockSpec((1, 1, 3, NA_QROWS * w, NA_KROWS * w), lambda l, h: (l, h, 0, 0, 0)),
        out_shape=jax.ShapeDtypeStruct((depth, nh, 3, NA_QROWS * w, NA_KROWS * w), F32),
        compiler_params=_params("parallel", "parallel"),
        name="na_bias_tables",
    )(padded)


def _neighbourhood_attention(q, kv, kv_ctx, bias, layer):
    b, seq, _ = q.shape
    rows = seq // GRID_W
    lc = kv_ctx.shape[1]
    assert rows % NA_QROWS == 0 and rows >= NA_KROWS + 1 and NA_QROWS == NA_KH // 2
    n_blocks = rows // NA_QROWS
    nh = bias.shape[1]
    tq = NA_QROWS * GRID_W

    def bias_map(i, r):
        return (layer, 0, jnp.where(r == 0, 0, jnp.where(r == n_blocks - 1, 2, 1)), 0, 0)

    return pl.pallas_call(
        functools.partial(_na_body, rows),
        grid=(b, n_blocks),
        in_specs=[pl.BlockSpec((1, tq, D_NA), lambda i, r: (i, r, 0)),
                  pl.BlockSpec((1, seq, D_NA), lambda i, r: (i, 0, 0)),
                  pl.BlockSpec((1, seq, D_NA), lambda i, r: (i, 0, 1)),
                  pl.BlockSpec((1, lc, D_NA), lambda i, r: (i, 0, 0)),
                  pl.BlockSpec((1, lc, D_NA), lambda i, r: (i, 0, 1)),
                  pl.BlockSpec((1, nh, 1, tq, NA_KROWS * GRID_W), bias_map)],
        out_specs=pl.BlockSpec((1, tq, D_NA), lambda i, r: (i, r, 0)),
        out_shape=jax.ShapeDtypeStruct((b, seq, D_NA), BF16),
        compiler_params=_params("parallel", "arbitrary"),
        name="neighbourhood_attention",
    )(q, kv, kv, kv_ctx, kv_ctx, bias)


def _ctx_attn_body(q_ref, k_ref, v_ref, o_ref):
    o = _head_pair_attention(q_ref[0], [k_ref[0]], [v_ref[0]], [[None], [None]])
    o_ref[0] = o.astype(o_ref.dtype)


def _context_attention(q, kv):
    b, lc, _ = q.shape
    n_pairs = D_NA // LANES
    return pl.pallas_call(
        _ctx_attn_body,
        grid=(b, n_pairs),
        in_specs=[pl.BlockSpec((1, lc, LANES), lambda i, p: (i, 0, p)),
                  pl.BlockSpec((1, lc, LANES), lambda i, p: (i, 0, p)),
                  pl.BlockSpec((1, lc, LANES), lambda i, p: (i, 0, n_pairs + p))],
        out_specs=pl.BlockSpec((1, lc, LANES), lambda i, p: (i, 0, p)),
        out_shape=jax.ShapeDtypeStruct((b, lc, D_NA), BF16),
        compiler_params=_params("parallel", "parallel"),
        name="context_attention",
    )(q, kv, kv)


def kernel(x, c, ctx, c_ctx, w_mod, b_mod, norm1_g, norm2_g, w_in, w_out, hy_conv_w, hy_conv_b,
           filt_w1, filt_b1, filt_freq, filt_w2, filt_b2, filt_w3, hy_skip, na_rpb,
           w_gate, w_up, w_down, final_g):
    depth = w_mod.shape[0]
    b, seq, d = x.shape
    off_k = 3 * D_HYENA + D_FNET + D_NA
    qk_scale = LOG2E / math.sqrt(HEAD_DIM)
    splits = ((3 * D_HYENA, F32, None), (D_FNET, BF16, None), (D_NA, BF16, qk_scale), (2 * D_NA, BF16, None))
    tm_proj, tm = 1024, 1024

    pad = (-(b + 1)) % 8
    c_all = jnp.concatenate([c, c_ctx[None], jnp.zeros((pad, d), F32)], axis=0)
    mod = _modulation(c_all, w_mod, b_mod).reshape(depth, b + 1 + pad, 6, 1, d)
    lat_row = lambda i: i
    ctx_row = lambda i: b

    w_in_b, w_out_b = w_in.astype(BF16), w_out.astype(BF16)
    w_gate_b, w_up_b, w_down_b = w_gate.astype(BF16), w_up.astype(BF16), w_down.astype(BF16)
    w_kv_last = w_in_b[depth - 1:, :, off_k:]
    n1g, n2g = norm1_g.reshape(depth, 1, d), norm2_g.reshape(depth, 1, d)
    fg = final_g.reshape(1, d)
    na_bias = _na_bias_tables(na_rpb, seq // GRID_W)

    for l in range(depth):
        last = l == depth - 1
        hy_params = (filt_w1[l], filt_b1[l], filt_freq[l], filt_w2[l], filt_b2[l], filt_w3[l], hy_skip[l])
        conv = (hy_conv_w[l], hy_conv_b[l].reshape(1, 3 * D_HYENA))

        gates, v, fn, q, kv = _norm_proj(x, n1g, mod, lat_row, l, w_in_b, l, splits, tm_proj, conv)
        if last:
            (kv_c,) = _norm_proj(ctx, n1g, mod, ctx_row, l, w_kv_last, 0, ((2 * D_NA, BF16, None),), tm_proj)
        else:
            cgates, cv, cfn, cq, kv_c = _norm_proj(ctx, n1g, mod, ctx_row, l, w_in_b, l, splits, tm_proj, conv)

        y_hy = _hyena_mixer(gates, v, *hy_params)
        y_fn = _fourier_mix(fn)
        y_na = _neighbourhood_attention(q, kv, kv_c, na_bias, l)

        x = _mix_ffn(x, y_hy, y_fn, y_na, w_out_b, n2g, mod, lat_row, l, w_gate_b, w_up_b, w_down_b,
                     fg, last, tm)

        if not last:
            cy_hy = _hyena_mixer(cgates, cv, *hy_params)
            cy_fn = _fourier_mix(cfn)
            cy_na = _context_attention(cq, kv_c)
            ctx = _mix_ffn(ctx, cy_hy, cy_fn, cy_na, w_out_b, n2g, mod, ctx_row, l, w_gate_b, w_up_b, w_down_b,
                           fg, False, tm)
    return x
```

```python
import functools
import math

import numpy as np
import jax
import jax.numpy as jnp
from jax import lax
from jax.experimental import pallas as pl
from jax.experimental.pallas import tpu as pltpu

F32 = jnp.float32
BF16 = jnp.bfloat16

EPS = 1e-6
GRID_W = 64
HEAD_DIM = 64
D_HYENA = 256
D_FNET = 256
D_NA = 512
FNET_GROUP_DIM = 64
NA_KH = 8
NA_KW = 16
NA_QROWS = 4
NA_KROWS = NA_QROWS + NA_KH
NA_KEY_GROUP = 256
POS_BANDS = 16
HYENA_DECAY_TARGET = 1e-2
HYENA_FAST_PCT = 0.3
HYENA_SLOW_PCT = 1.5
MASK_VALUE = -1e30
VMEM_LIMIT = 56 * 1024 * 1024
LANES = 128
MXU_WIDTH = 256
LOG2E = math.log2(math.e)
FFN_ROW_SPLIT = 2
F32_SUBLANES = 8
BF16_SUBLANES = 16


def _params(*sem):
    return pltpu.CompilerParams(dimension_semantics=sem, vmem_limit_bytes=VMEM_LIMIT)


def _bdot(a, b):
    return jnp.dot(a, b, preferred_element_type=F32)


def _split_bf16(a):
    hi = a.astype(BF16)
    lo = (a - hi.astype(F32)).astype(BF16)
    return hi, lo


def _dot3(a, b):
    a_hi, a_lo = _split_bf16(a)
    b_hi, b_lo = _split_bf16(b)
    return _bdot(a_hi, b_hi) + _bdot(a_lo, b_hi) + _bdot(a_hi, b_lo)


def _mod_body(c_ref, w_ref, b_ref, o_ref):
    c = c_ref[...]
    rows = c.shape[0]
    s_hi, s_lo = _split_bf16(c * jax.nn.sigmoid(c))
    w_hi, w_lo = _split_bf16(w_ref[0])
    both = _bdot(jnp.concatenate([s_hi, s_lo], axis=0), w_hi)
    o_ref[0] = both[:rows] + both[rows:] + _bdot(s_hi, w_lo) + b_ref[0]


def _modulation(c_all, w_mod, b_mod):
    depth, d, n = w_mod.shape
    rows = c_all.shape[0]
    tn = 1536
    return pl.pallas_call(
        _mod_body,
        grid=(depth, n // tn),
        in_specs=[
            pl.BlockSpec((rows, d), lambda l, j: (0, 0)),
            pl.BlockSpec((1, d, tn), lambda l, j: (l, 0, j)),
            pl.BlockSpec((1, 1, tn), lambda l, j: (l, 0, j)),
        ],
        out_specs=pl.BlockSpec((1, rows, tn), lambda l, j: (l, 0, j)),
        out_shape=jax.ShapeDtypeStruct((depth, rows, n), F32),
        compiler_params=_params("parallel", "parallel"),
        name="modulation",
    )(c_all, w_mod, b_mod.reshape(depth, 1, n))


def _modulated_norm(x, g, sc, sh):
    ms = jnp.mean(x * x, axis=-1, keepdims=True)
    y = x * lax.rsqrt(ms + EPS) * g
    return y * (1.0 + sc) + sh


def _mod_spec(layer, row_of, chunk, d):
    return pl.BlockSpec((1, 1, 1, 1, d), lambda i, *_: (layer, row_of(i), chunk, 0, 0))


def _short_conv_tile(u, prev_row, next_row, cw_ref, cb_ref):
    rows = u.shape[0]
    row = lax.broadcasted_iota(jnp.int32, u.shape, 0)
    prev = jnp.where(row == 0, prev_row, pltpu.roll(u, 1, axis=0))
    nxt = jnp.where(row == rows - 1, next_row, pltpu.roll(u, rows - 1, axis=0))
    return prev * cw_ref[0:1] + u * cw_ref[1:2] + nxt * cw_ref[2:3] + cb_ref[...]


def _norm_proj_body(splits, conv, x_ref, g_ref, sc_ref, sh_ref, w_ref, *rest):
    o_refs = ((rest[4:6],) + rest[6:]) if conv else rest
    g, sc, sh = g_ref[0], sc_ref[0, 0, 0], sh_ref[0, 0, 0]
    p = _bdot(_modulated_norm(x_ref[0], g, sc, sh).astype(BF16), w_ref[0])
    off = 0
    for idx, (o_ref, (width, _, scale)) in enumerate(zip(o_refs, splits)):
        part = p[:, off:off + width]
        if conv and idx == 0:
            xp_ref, xn_ref, cw_ref, cb_ref = rest[:4]
            gate_ref, v_ref = o_ref
            j, nj = pl.program_id(1), pl.num_programs(1)
            wc = w_ref[0, :, off:off + width]
            edge = F32_SUBLANES
            pp = _bdot(_modulated_norm(xp_ref[0], g, sc, sh).astype(BF16), wc)
            pn = _bdot(_modulated_norm(xn_ref[0], g, sc, sh).astype(BF16), wc)
            prev_row = jnp.where(j == 0, 0.0, pp[edge - 1:edge])
            next_row = jnp.where(j == nj - 1, 0.0, pn[0:1])
            u = _short_conv_tile(part, prev_row, next_row, cw_ref, cb_ref)
            c = v_ref.shape[3]
            n_gates = gate_ref.shape[1]
            for gi in range(n_gates):
                gate_ref[0, gi] = u[:, gi * c:(gi + 1) * c].astype(gate_ref.dtype)
            v_ref[0, 0] = u[:, n_gates * c:]
        else:
            o_ref[0] = (part if scale is None else part * scale).astype(o_ref.dtype)
        off += width


def _norm_proj(x, norm_g, mod, row_of, layer, w, w_idx, splits, tm, conv=None):
    b, t, d = x.shape
    n = w.shape[2]
    tm = min(tm, t)
    edge = F32_SUBLANES
    tok = lambda wd: pl.BlockSpec((1, tm, wd), lambda i, j: (i, j, 0))
    in_specs = [
        tok(d),
        pl.BlockSpec((1, 1, d), lambda i, j: (layer, 0, 0)),
        _mod_spec(layer, row_of, 1, d),
        _mod_spec(layer, row_of, 0, d),
        pl.BlockSpec((1, d, n), lambda i, j: (w_idx, 0, 0)),
    ]
    args = [x, norm_g, mod, mod, w]
    out_specs = [tok(wd) for wd, _, _ in splits]
    out_shape = [jax.ShapeDtypeStruct((b, t, wd), dt) for wd, dt, _ in splits]
    if conv is not None:
        cw, cb = conv
        w0, c = splits[0][0], D_HYENA
        per_tile = tm // edge
        in_specs += [
            pl.BlockSpec((1, edge, d), lambda i, j: (i, jnp.maximum(j * per_tile - 1, 0), 0)),
            pl.BlockSpec((1, edge, d), lambda i, j: (i, jnp.minimum((j + 1) * per_tile, t // edge - 1), 0)),
            pl.BlockSpec(cw.shape, lambda i, j: (0, 0)),
            pl.BlockSpec(cb.shape, lambda i, j: (0, 0)),
        ]
        args += [x, x, cw, cb]
        n_gates = w0 // c - 1
        out_specs[0:1] = [pl.BlockSpec((1, n_gates, tm, c), lambda i, j: (i, 0, j, 0)),
                          pl.BlockSpec((1, 1, tm, c), lambda i, j: (i, 0, j, 0))]
        out_shape[0:1] = [jax.ShapeDtypeStruct((b, n_gates, t, c), BF16),
                          jax.ShapeDtypeStruct((b, 1, t, c), splits[0][1])]
    return pl.pallas_call(
        functools.partial(_norm_proj_body, tuple(splits), conv is not None),
        grid=(b, t // tm),
        in_specs=in_specs,
        out_specs=out_specs,
        out_shape=out_shape,
        compiler_params=_params("parallel", "parallel"),
        name="norm_proj",
    )(*args)


def _mix_ffn_body(final_norm, x_ref, hy_ref, fn_ref, na_ref, wo_ref, g1_ref, ng_ref, sc_ref, sh_ref, g2_ref,
                  wg_ref, wu_ref, wd_ref, fg_ref, o_ref):
    tm = x_ref.shape[1]
    sub = tm // FFN_ROW_SPLIT if tm % (FFN_ROW_SPLIT * MXU_WIDTH) == 0 else tm
    halves = [slice(r0, r0 + sub) for r0 in range(0, tm, sub)]
    x1s, hs = [], []
    for rows in halves:
        mix = (_bdot(hy_ref[0, rows], wo_ref[0, :D_HYENA])
               + _bdot(fn_ref[0, rows], wo_ref[0, D_HYENA:D_HYENA + D_FNET])
               + _bdot(na_ref[0, rows], wo_ref[0, D_HYENA + D_FNET:]))
        x1 = x_ref[0, rows] + g1_ref[0, 0, 0] * mix
        x1s.append(x1)
        hs.append(_modulated_norm(x1, ng_ref[0], sc_ref[0, 0, 0], sh_ref[0, 0, 0]).astype(BF16))
    accs = [None] * len(halves)
    for c0 in range(0, wg_ref.shape[2], MXU_WIDTH):
        cols = slice(c0, c0 + MXU_WIDTH)
        for s, h in enumerate(hs):
            a = _bdot(h, wg_ref[0, :, cols])
            u = _bdot(h, wu_ref[0, :, cols])
            act = (a * jax.nn.sigmoid(a) * u).astype(BF16)
            part = _bdot(act, wd_ref[0, cols, :])
            accs[s] = part if accs[s] is None else accs[s] + part
    for rows, x1, acc in zip(halves, x1s, accs):
        y = x1 + g2_ref[0, 0, 0] * acc
        if final_norm:
            ms = jnp.mean(y * y, axis=-1, keepdims=True)
            y = y * lax.rsqrt(ms + EPS) * fg_ref[...]
        o_ref[0, rows] = y


def _mix_ffn(x, y_hy, y_fn, y_na, w_out, norm_g, mod, row_of, layer, wg, wu, wd, final_g, final_norm, tm):
    b, t, d = x.shape
    dff = wg.shape[2]
    assert dff % MXU_WIDTH == 0
    tm = min(tm, t)
    tok = lambda wd_: pl.BlockSpec((1, tm, wd_), lambda i, j: (i, j, 0))
    resident = lambda shape: pl.BlockSpec((1,) + shape, lambda i, j: (layer, 0, 0), pipeline_mode=pl.Buffered(1))
    return pl.pallas_call(
        functools.partial(_mix_ffn_body, final_norm),
        grid=(b, t // tm),
        in_specs=[
            tok(d), tok(D_HYENA), tok(D_FNET), tok(D_NA),
            resident((d, d)),
            _mod_spec(layer, row_of, 2, d),
            pl.BlockSpec((1, 1, d), lambda i, j: (layer, 0, 0)),
            _mod_spec(layer, row_of, 4, d),
            _mod_spec(layer, row_of, 3, d),
            _mod_spec(layer, row_of, 5, d),
            resident((d, dff)), resident((d, dff)), resident((dff, d)),
            pl.BlockSpec((1, d), lambda i, j: (0, 0)),
        ],
        out_specs=tok(d),
        out_shape=jax.ShapeDtypeStruct(x.shape, F32),
        compiler_params=_params("parallel", "parallel"),
        name="mix_ffn",
    )(x, y_hy, y_fn, y_na, w_out, mod, norm_g, mod, mod, mod, wg, wu, wd, final_g)


def _split_seq(n_total):
    log = int(round(math.log2(n_total)))
    assert 2 ** log == n_total
    n1 = 2 ** (log // 2)
    return n1, n_total // n1


@functools.lru_cache(maxsize=None)
def _hyena_consts(seq):
    n = 2 * seq
    n1, n2 = _split_seq(n)
    h1 = n1 // 2
    f1 = np.arange(h1)[:, None]
    t1 = np.arange(h1)[None, :]
    th = np.pi * (2 * f1 + 1) * t1 / n1
    m1 = np.empty((n1, h1))
    m1[0::2] = np.cos(th)
    m1[1::2] = -np.sin(th)
    m3 = m1.T * (2.0 / n)
    f = (np.arange(h1)[:, None, None] + n1 * np.arange(n2)[None, :, None])
    t2 = np.arange(n2)[None, None, :]
    ph = np.pi * ((2 * f + 1) * t2 % (2 * n)) / n
    c, s = np.cos(ph), np.sin(ph)
    m2 = np.concatenate([np.concatenate([c, s], axis=2), np.concatenate([-s, c], axis=2)], axis=1)
    m2t = np.transpose(m2, (0, 2, 1))
    m1k = np.kron(m1, np.eye(F32_SUBLANES))
    m3k = np.kron(m3, np.eye(BF16_SUBLANES))
    return n1, n2, m1k, m2, m2t, m3k


@functools.lru_cache(maxsize=None)
def _fnet_consts(seq):
    n1, n2 = _split_seq(seq)
    n1, n2 = n1 // 2, n2 * 2
    f1 = np.arange(n1)[:, None]
    t1 = np.arange(n1)[None, :]
    th = 2 * np.pi * (f1 * t1 % n1) / n1
    m1 = np.empty((2 * n1, n1))
    m1[0::2] = np.cos(th)
    m1[1::2] = -np.sin(th)
    f = (np.arange(n1)[:, None, None] + n1 * np.arange(n2)[None, :, None])
    t2 = np.arange(n2)[None, None, :]
    ph = 2 * np.pi * (f * t2 % seq) / seq
    c, s = np.cos(ph), np.sin(ph)
    m2 = np.concatenate([np.concatenate([c, s], axis=2), np.concatenate([-s, c], axis=2)], axis=1)
    gd = FNET_GROUP_DIM
    cc = np.arange(gd)
    pg = 2 * np.pi * (np.outer(cc, cc) % gd) / gd
    scale = 1.0 / math.sqrt(gd * seq)
    eye = np.eye(D_FNET // gd)
    bdc = np.kron(eye, np.cos(pg)) * scale
    bds = np.kron(eye, np.sin(pg)) * scale
    m1k = np.kron(m1, np.eye(BF16_SUBLANES))
    g = F32_SUBLANES
    perm = np.zeros((g * n1, n1 * g))
    for f2 in range(g):
        for ff in range(n1):
            perm[f2 * n1 + ff, ff * g + f2] = 1.0
    return n1, n2, m1k, m2, perm, bdc, bds


def _mxu_const(a):
    return jnp.asarray(a, dtype=F32).astype(BF16)


def _stage1_into(mk_ref, x_at, a_scr, group):
    nf, _, n2, c = a_scr.shape

    def body(s, carry):
        r0 = pl.multiple_of(s * group, group)
        xs = x_at(r0)
        xs = xs.reshape(xs.shape[0] * group, c).astype(BF16)
        a_scr[:, :, pl.ds(r0, group), :] = _bdot(mk_ref[...], xs).reshape(nf, 2, group, c)
        return carry

    lax.fori_loop(0, n2 // group, body, 0, unroll=True)


def _stage2_rows(a_scr, f1):
    _, _, n2, c = a_scr.shape
    return a_scr[f1].reshape(2 * n2, c).astype(BF16)


def _filter_body(zt_ref, t_ref, w1t_ref, b1_ref, fr_ref, w2t_ref, b2_ref, w3_ref, dl_ref, m1k_ref, m2_ref,
                 kr_ref, ki_ref, h_scr, kt_scr, af_scr, ab_scr):
    o, j = pl.program_id(0), pl.program_id(1)
    fb = m2_ref.shape[0]
    h1, _, n2, c = af_scr.shape
    g = F32_SUBLANES

    @pl.when((o == 0) & (j == 0))
    def _():
        h = jnp.sin(fr_ref[:, 0:1] * (_dot3(w1t_ref[...], zt_ref[...]) + b1_ref[...]))
        h = jnp.sin(fr_ref[:, 1:2] * (_dot3(w2t_ref[...], h) + b2_ref[...]))
        h_scr[...] = h.T

    @pl.when(j == 0)
    def _():
        k = _dot3(h_scr[...], w3_ref[0])
        decay = jnp.exp(-t_ref[...] * jnp.abs(dl_ref[...]))
        kf = k[:, :c] * decay
        row = lax.broadcasted_iota(jnp.int32, kf.shape, 0)
        kb = jnp.where(row == 0, 0.0, k[:, c:] * decay)
        norm = jnp.sum(jnp.abs(kf), axis=0, keepdims=True) + jnp.sum(jnp.abs(kb), axis=0, keepdims=True)
        inv = 1.0 / norm
        kt_scr[0] = (kf * inv).reshape(h1, n2, c)
        kt_scr[1] = (kb * inv).reshape(h1, n2, c)
        _stage1_into(m1k_ref, lambda r0: kt_scr[0, :, pl.ds(r0, g), :], af_scr, g)
        _stage1_into(m1k_ref, lambda r0: kt_scr[1, :, pl.ds(r0, g), :], ab_scr, g)

    for i in range(fb):
        f1 = j * fb + i
        xf = _bdot(m2_ref[i], _stage2_rows(af_scr, f1))
        xb = _bdot(m2_ref[i], _stage2_rows(ab_scr, f1))
        kr_ref[0, i] = xf[:n2] + xb[:n2]
        ki_ref[0, i] = xf[n2:] - xb[n2:]


def _hyena_filter_spectrum(seq, w1, b1, freq, w2, b2, w3, m1k, m2, fb):
    c = D_HYENA
    h1, n2 = m2.shape[0], m2.shape[1] // 2
    t = np.linspace(0.0, 1.0, seq)[:, None]
    w = 2.0 * np.pi * np.arange(seq)[:, None] / seq
    f = np.linspace(1e-4, POS_BANDS - 1, POS_BANDS)[None]
    z = np.concatenate([t, np.cos(f * w), -np.sin(f * w)], axis=-1).astype(np.float32)
    pe = z.shape[1]
    zt = jnp.asarray(np.pad(z, ((0, 0), (0, LANES - pe))).T)
    tcol = jnp.asarray(t.astype(np.float32))
    w1t = jnp.pad(w1, ((0, LANES - pe), (0, 0))).T
    max_decay = math.log(HYENA_DECAY_TARGET) / HYENA_FAST_PCT
    min_decay = math.log(HYENA_DECAY_TARGET) / HYENA_SLOW_PCT
    deltas = jnp.asarray(np.linspace(min_decay, max_decay, c, dtype=np.float32)[None])
    hid = w1.shape[1]
    w3r = jnp.transpose(w3.reshape(hid, 2, 2 * c), (1, 0, 2))
    full = lambda a: pl.BlockSpec(a.shape, lambda o, j: (0,) * a.ndim)
    b1c, b2c, frt = b1.reshape(hid, 1), b2.reshape(hid, 1), freq.T
    out = jax.ShapeDtypeStruct((2, h1, n2, c), F32)
    return pl.pallas_call(
        _filter_body,
        grid=(2, h1 // fb),
        in_specs=[full(zt), full(tcol), full(w1t), full(b1c), full(frt), full(w2.T), full(b2c),
                  pl.BlockSpec((1, hid, 2 * c), lambda o, j: (o, 0, 0)), full(deltas), full(m1k),
                  pl.BlockSpec((fb, 2 * n2, 2 * n2), lambda o, j: (j, 0, 0))],
        out_specs=[pl.BlockSpec((1, fb, n2, c), lambda o, j: (o, j, 0, 0))] * 2,
        out_shape=[out, out],
        scratch_shapes=[pltpu.VMEM((seq, hid), F32), pltpu.VMEM((2, h1, n2, c), F32),
                        pltpu.VMEM((h1, 2, n2, c), F32), pltpu.VMEM((h1, 2, n2, c), F32)],
        compiler_params=_params("arbitrary", "arbitrary"),
        name="hyena_filter_spectrum",
    )(zt, tcol, w1t, b1c, frt, w2.T, b2c, w3r, deltas, m1k, m2)


def _conv_fwd_body(m1k_ref, m2_ref, m2t_ref, x_ref, kr_ref, ki_ref, o_ref, a_scr):
    h1, _, n2, c = a_scr.shape
    g = F32_SUBLANES
    _stage1_into(m1k_ref, lambda r0: x_ref[0, 0, :, pl.ds(r0, g), :], a_scr, g)

    def per_f1(f1, carry):
        x = _bdot(m2_ref[f1], _stage2_rows(a_scr, f1))
        xr, xi = x[:n2], x[n2:]
        kr, ki = kr_ref[0, f1], ki_ref[0, f1]
        y = jnp.concatenate([kr * xr - ki * xi, kr * xi + ki * xr], axis=0).astype(BF16)
        o_ref[0, pl.ds(2 * f1, 2)] = _bdot(m2t_ref[f1], y).astype(BF16).reshape(2, n2, c)
        return carry

    lax.fori_loop(0, h1, per_f1, 0, unroll=True)


def _conv_fwd(m1k, m2, m2t, x, sel, kr, ki, order):
    b, _, h1, n2, c = x.shape
    resident = lambda a, idx: pl.BlockSpec(a.shape, lambda i: idx, pipeline_mode=pl.Buffered(1))
    kspec = pl.BlockSpec((1, h1, n2, c), lambda i: (order, 0, 0, 0), pipeline_mode=pl.Buffered(1))
    return pl.pallas_call(
        _conv_fwd_body,
        grid=(b,),
        in_specs=[resident(m1k, (0, 0)), resident(m2, (0, 0, 0)), resident(m2t, (0, 0, 0)),
                  pl.BlockSpec((1, 1, h1, n2, c), lambda i: (i, sel, 0, 0, 0)),
                  kspec, kspec],
        out_specs=pl.BlockSpec((1, 2 * h1, n2, c), lambda i: (i, 0, 0, 0)),
        out_shape=jax.ShapeDtypeStruct((b, 2 * h1, n2, c), BF16),
        scratch_shapes=[pltpu.VMEM((h1, 2, n2, c), F32)],
        compiler_params=_params("parallel"),
        name="hyena_conv_fwd",
    )(m1k, m2, m2t, x, kr, ki)


def _conv_inv_body(m3k_ref, b_ref, gate_ref, skipin_ref, skip_ref, o_ref):
    n1, ts, c = b_ref.shape[1:]
    h1 = n1 // 2
    g = BF16_SUBLANES
    for s in range(ts // g):
        rows = slice(s * g, (s + 1) * g)
        bs = b_ref[0, :, rows, :].reshape(n1 * g, c)
        y = _bdot(m3k_ref[...], bs).reshape(h1, g, c)
        z = gate_ref[0, 0, :, rows, :] * (y + skipin_ref[0, 0, :, rows, :] * skip_ref[...])
        o_ref[0, 0, :, rows, :] = z.astype(o_ref.dtype)


def _conv_inv(m3k, bm, gates, gate_sel, skipin, skipin_sel, skip_row, out_dtype):
    b, n1, n2, c = bm.shape
    h1 = n1 // 2
    ts = min(n2, 4 * BF16_SUBLANES)
    tok = lambda sel: pl.BlockSpec((1, 1, h1, ts, c), lambda i, j: (i, sel, 0, j, 0))
    return pl.pallas_call(
        _conv_inv_body,
        grid=(b, n2 // ts),
        in_specs=[pl.BlockSpec(m3k.shape, lambda i, j: (0, 0)),
                  pl.BlockSpec((1, n1, ts, c), lambda i, j: (i, 0, j, 0)),
                  tok(gate_sel), tok(skipin_sel),
                  pl.BlockSpec((1, c), lambda i, j: (0, 0))],
        out_specs=tok(0),
        out_shape=jax.ShapeDtypeStruct((b, 1, h1, n2, c), out_dtype),
        compiler_params=_params("parallel", "parallel"),
        name="hyena_conv_inv",
    )(m3k, bm, gates, skipin, skip_row)


def _hyena_mixer(gates, v, w1, b1, freq, w2, b2, w3, skip):
    b, _, seq, c = v.shape
    n1, n2, m1k, m2, m2t, m3k = _hyena_consts(seq)
    h1 = n1 // 2
    m1k, m2, m2t, m3k = (_mxu_const(a) for a in (m1k, m2, m2t, m3k))

    kr, ki = _hyena_filter_spectrum(seq, w1, b1, freq, w2, b2, w3, m1k, m2, min(h1, 8))

    gates = gates.reshape(b, 2, h1, n2, c)
    v = v.reshape(b, 1, h1, n2, c)
    bm = _conv_fwd(m1k, m2, m2t, v, 0, kr, ki, 0)
    z = _conv_inv(m3k, bm, gates, 0, v, 0, skip[0:1], F32)
    bm = _conv_fwd(m1k, m2, m2t, z, 0, kr, ki, 1)
    y = _conv_inv(m3k, bm, gates, 1, z, 0, skip[1:2], BF16)
    return y.reshape(b, seq, c)


def _fnet_body(m1k_ref, m2_ref, perm_ref, bdc_ref, bds_ref, g_ref, o_ref, a_scr):
    j = pl.program_id(1)
    fb = m2_ref.shape[0]
    n1, _, n2, c = a_scr.shape

    @pl.when(j == 0)
    def _():
        gi = BF16_SUBLANES
        _stage1_into(m1k_ref, lambda r0: g_ref[0, :, pl.ds(r0, gi), :], a_scr, gi)

    for i in range(fb):
        f1 = j * fb + i
        a_scr[f1] = _bdot(m2_ref[i], _stage2_rows(a_scr, f1)).reshape(2, n2, c)

    @pl.when(j == pl.num_programs(1) - 1)
    def _():
        go = F32_SUBLANES

        def body(s, carry):
            r0 = pl.multiple_of(s * go, go)
            re = a_scr[:, 0, pl.ds(r0, go), :].reshape(n1 * go, c).astype(BF16)
            im = a_scr[:, 1, pl.ds(r0, go), :].reshape(n1 * go, c).astype(BF16)
            xr = _bdot(perm_ref[...], re).astype(BF16)
            xi = _bdot(perm_ref[...], im).astype(BF16)
            out = _bdot(xr, bdc_ref[...]) + _bdot(xi, bds_ref[...])
            o_ref[0, pl.ds(pl.multiple_of(s * go * n1, go * n1), go * n1), :] = out.astype(o_ref.dtype)
            return carry

        lax.fori_loop(0, n2 // go, body, 0, unroll=True)


def _fourier_mix(g):
    b, seq, c = g.shape
    n1, n2, m1k, m2, perm, bdc, bds = _fnet_consts(seq)
    m1k, m2, perm, bdc, bds = (_mxu_const(a) for a in (m1k, m2, perm, bdc, bds))
    fb = min(n1, 8)
    const = lambda a: pl.BlockSpec(a.shape, lambda i, j: (0,) * a.ndim)
    return pl.pallas_call(
        _fnet_body,
        grid=(b, n1 // fb),
        in_specs=[const(m1k),
                  pl.BlockSpec((fb, 2 * n2, 2 * n2), lambda i, j: (j, 0, 0)),
                  const(perm), const(bdc), const(bds),
                  pl.BlockSpec((1, n1, n2, c), lambda i, j: (i, 0, 0, 0))],
        out_specs=pl.BlockSpec((1, seq, c), lambda i, j: (i, 0, 0)),
        out_shape=jax.ShapeDtypeStruct((b, seq, c), BF16),
        scratch_shapes=[pltpu.VMEM((n1, 2, n2, c), F32)],
        compiler_params=_params("parallel", "arbitrary"),
        name="fnet",
    )(m1k, m2, perm, bdc, bds, g.reshape(b, n1, n2, c))


def _dot_nt(a, b):
    return lax.dot_general(a, b, (((1,), (1,)), ((), ())), preferred_element_type=F32)


def _head_pair_attention(q2, keys, values, biases):
    nq = q2.shape[0]
    lane = lax.broadcasted_iota(jnp.int32, q2.shape, 1)
    zero = jnp.zeros_like(q2)
    qs = jnp.concatenate([jnp.where(lane < HEAD_DIM, q2, zero), jnp.where(lane >= HEAD_DIM, q2, zero)], axis=0)
    m = denom = acc = None
    for gi, (kk, vv) in enumerate(zip(keys, values)):
        s = _dot_nt(qs, kk)
        if biases[0][gi] is not None:
            s = s + jnp.concatenate([biases[0][gi], biases[1][gi]], axis=0)
        m_grp = s.max(axis=-1, keepdims=True)
        m_new = m_grp if m is None else jnp.maximum(m, m_grp)
        e = jnp.exp2(s - m_new)
        e_part = e[:, :LANES]
        for c0 in range(LANES, e.shape[1], LANES):
            e_part = e_part + e[:, c0:c0 + LANES]
        pv = _bdot(e.astype(BF16), vv)
        if m is None:
            denom, acc = e_part, pv
        else:
            alpha = jnp.exp2(m - m_new)
            denom = alpha * denom + e_part
            acc = alpha * acc + pv
        m = m_new
    out = acc / denom.sum(axis=-1, keepdims=True)
    return jnp.where(lane < HEAD_DIM, out[:nq], out[nq:])


def _na_window_base(row_block, rows):
    return jnp.clip(row_block * NA_QROWS - NA_KH // 2, 0, rows - NA_KROWS)


def _na_body(rows, q_ref, k_ref, v_ref, kc_ref, vc_ref, bias_ref, o_ref):
    base = _na_window_base(pl.program_id(1), rows)
    start = pl.multiple_of(base * GRID_W, GRID_W)
    n_keys = NA_KROWS * GRID_W
    n_pairs = q_ref.shape[2] // LANES
    groups = [(c0, min(NA_KEY_GROUP, n_keys - c0)) for c0 in range(0, n_keys, NA_KEY_GROUP)]
    for p in range(n_pairs):
        sl = slice(p * LANES, (p + 1) * LANES)
        q2 = q_ref[0, :, sl]
        keys = [kc_ref[0, :, sl]] + [k_ref[0, pl.ds(start + c0, n), sl] for c0, n in groups]
        values = [vc_ref[0, :, sl]] + [v_ref[0, pl.ds(start + c0, n), sl] for c0, n in groups]
        biases = [[None] + [bias_ref[0, 2 * p + hh, 0, :, c0:c0 + n] for c0, n in groups] for hh in range(2)]
        o = _head_pair_attention(q2, keys, values, biases)
        o_ref[0, :, sl] = o.astype(o_ref.dtype)


def _na_bias_body(rows, p_ref, o_ref):
    kh, kw, w = NA_KH, NA_KW, GRID_W
    shape = (w, 2 * w)
    qc = lax.broadcasted_iota(jnp.int32, shape, 0)
    lane = lax.broadcasted_iota(jnp.int32, shape, 1)
    kc = lane & (w - 1)
    col_start = jnp.clip(qc - kw // 2, 0, w - kw)
    col_valid = (kc >= col_start) & (kc < col_start + kw)
    low_half = lane < w
    masked = jnp.full(shape, MASK_VALUE, F32)
    tiles = []
    for a in range(2 * kh - 1):
        row = jnp.broadcast_to(p_ref[0, 0, a:a + 1, :], shape)
        t = pltpu.roll(row, w + 1, 1, stride=1, stride_axis=0)
        t = jnp.where(low_half, t, pltpu.roll(t, w, 1))
        tiles.append(jnp.where(col_valid, t, masked))
    for case, r0 in enumerate((0, NA_QROWS, rows - NA_QROWS)):
        base = int(np.clip(r0 - kh // 2, 0, rows - NA_KROWS))
        for qr in range(NA_QROWS):
            q_abs = r0 + qr
            r_start = int(np.clip(q_abs - kh // 2, 0, rows - kh))

            def tile_for(kr):
                k_abs = base + kr
                if kr >= NA_KROWS or not (r_start <= k_abs < r_start + kh):
                    return masked
                return tiles[k_abs - q_abs + kh - 1]

            q_rows = slice(qr * w, (qr + 1) * w)
            for pair in range((NA_KROWS + 1) // 2):
                chunk = jnp.where(low_half, tile_for(2 * pair), tile_for(2 * pair + 1))
                width = min(2 * w, NA_KROWS * w - pair * 2 * w)
                o_ref[0, 0, case, q_rows, pair * 2 * w:pair * 2 * w + width] = chunk[:, :width]


def _na_bias_tables(rpb, rows):
    w = GRID_W
    depth, nh, na, nb = rpb.shape
    assert 2 * w == LANES and nb == 2 * NA_KW - 1
    lo = w - NA_KW
    padded = jnp.pad(rpb.astype(F32) * LOG2E, ((0, 0), (0, 0), (0, (-na) % 8), (lo, 2 * w - nb - lo)))
    return pl.pallas_call(
        functools.partial(_na_bias_body, rows),
        grid=(depth, nh),
        in_specs=[pl.BlockSpec((1, 1) + padded.shape[2:], lambda l, h: (l, h, 0, 0))],
        out_specs=pl.BlockSpec((1, 1, 3, NA_QROWS * w, NA_KROWS * w), lambda l, h: (l, h, 0, 0, 0)),
        out_shape=jax.ShapeDtypeStruct((depth, nh, 3, NA_QROWS * w, NA_KROWS * w), F32),
        compiler_params=_params("parallel", "parallel"),
        name="na_bias_tables",
    )(padded)


def _neighbourhood_attention(q, kv, kv_ctx, bias, layer):
    b, seq, _ = q.shape
    rows = seq // GRID_W
    lc = kv_ctx.shape[1]
    assert rows % NA_QROWS == 0 and rows >= NA_KROWS + 1 and NA_QROWS == NA_KH // 2
    n_blocks = rows // NA_QROWS
    nh = bias.shape[1]
    tq = NA_QROWS * GRID_W

    def bias_map(i, r):
        return (layer, 0, jnp.where(r == 0, 0, jnp.where(r == n_blocks - 1, 2, 1)), 0, 0)

    return pl.pallas_call(
        functools.partial(_na_body, rows),
        grid=(b, n_blocks),
        in_specs=[pl.BlockSpec((1, tq, D_NA), lambda i, r: (i, r, 0)),
                  pl.BlockSpec((1, seq, D_NA), lambda i, r: (i, 0, 0)),
                  pl.BlockSpec((1, seq, D_NA), lambda i, r: (i, 0, 1)),
                  pl.BlockSpec((1, lc, D_NA), lambda i, r: (i, 0, 0)),
                  pl.BlockSpec((1, lc, D_NA), lambda i, r: (i, 0, 1)),
                  pl.BlockSpec((1, nh, 1, tq, NA_KROWS * GRID_W), bias_map)],
        out_specs=pl.BlockSpec((1, tq, D_NA), lambda i, r: (i, r, 0)),
        out_shape=jax.ShapeDtypeStruct((b, seq, D_NA), BF16),
        compiler_params=_params("parallel", "arbitrary"),
        name="neighbourhood_attention",
    )(q, kv, kv, kv_ctx, kv_ctx, bias)


def _ctx_attn_body(q_ref, k_ref, v_ref, o_ref):
    o = _head_pair_attention(q_ref[0], [k_ref[0]], [v_ref[0]], [[None], [None]])
    o_ref[0] = o.astype(o_ref.dtype)


def _context_attention(q, kv):
    b, lc, _ = q.shape
    n_pairs = D_NA // LANES
    return pl.pallas_call(
        _ctx_attn_body,
        grid=(b, n_pairs),
        in_specs=[pl.BlockSpec((1, lc, LANES), lambda i, p: (i, 0, p)),
                  pl.BlockSpec((1, lc, LANES), lambda i, p: (i, 0, p)),
                  pl.BlockSpec((1, lc, LANES), lambda i, p: (i, 0, n_pairs + p))],
        out_specs=pl.BlockSpec((1, lc, LANES), lambda i, p: (i, 0, p)),
        out_shape=jax.ShapeDtypeStruct((b, lc, D_NA), BF16),
        compiler_params=_params("parallel", "parallel"),
        name="context_attention",
    )(q, kv, kv)


def kernel(x, c, ctx, c_ctx, w_mod, b_mod, norm1_g, norm2_g, w_in, w_out, hy_conv_w, hy_conv_b,
           filt_w1, filt_b1, filt_freq, filt_w2, filt_b2, filt_w3, hy_skip, na_rpb,
           w_gate, w_up, w_down, final_g):
    depth = w_mod.shape[0]
    b, seq, d = x.shape
    off_k = 3 * D_HYENA + D_FNET + D_NA
    qk_scale = LOG2E / math.sqrt(HEAD_DIM)
    splits = ((3 * D_HYENA, F32, None), (D_FNET, BF16, None), (D_NA, BF16, qk_scale), (2 * D_NA, BF16, None))
    tm_proj, tm = 1024, 1024

    pad = (-(b + 1)) % 8
    c_all = jnp.concatenate([c, c_ctx[None], jnp.zeros((pad, d), F32)], axis=0)
    mod = _modulation(c_all, w_mod, b_mod).reshape(depth, b + 1 + pad, 6, 1, d)
    lat_row = lambda i: i
    ctx_row = lambda i: b

    w_in_b, w_out_b = w_in.astype(BF16), w_out.astype(BF16)
    w_gate_b, w_up_b, w_down_b = w_gate.astype(BF16), w_up.astype(BF16), w_down.astype(BF16)
    w_kv_last = w_in_b[depth - 1:, :, off_k:]
    n1g, n2g = norm1_g.reshape(depth, 1, d), norm2_g.reshape(depth, 1, d)
    fg = final_g.reshape(1, d)
    na_bias = _na_bias_tables(na_rpb, seq // GRID_W)

    for l in range(depth):
        last = l == depth - 1
        hy_params = (filt_w1[l], filt_b1[l], filt_freq[l], filt_w2[l], filt_b2[l], filt_w3[l], hy_skip[l])
        conv = (hy_conv_w[l], hy_conv_b[l].reshape(1, 3 * D_HYENA))

        gates, v, fn, q, kv = _norm_proj(x, n1g, mod, lat_row, l, w_in_b, l, splits, tm_proj, conv)
        if last:
            (kv_c,) = _norm_proj(ctx.reshape(1, -1, d), n1g, mod, ctx_row, l, w_kv_last, 0,
                                 ((2 * D_NA, BF16, None),), tm_proj)
            kv_c = kv_c.reshape(b, -1, 2 * D_NA)
        else:
            cgates, cv, cfn, cq, kv_c = _norm_proj(ctx, n1g, mod, ctx_row, l, w_in_b, l, splits, tm_proj, conv)

        y_hy = _hyena_mixer(gates, v, *hy_params)
        y_fn = _fourier_mix(fn)
        y_na = _neighbourhood_attention(q, kv, kv_c, na_bias, l)

        x = _mix_ffn(x, y_hy, y_fn, y_na, w_out_b, n2g, mod, lat_row, l, w_gate_b, w_up_b, w_down_b,
                     fg, last, tm)

        if not last:
            cy_hy = _hyena_mixer(cgates, cv, *hy_params)
            cy_fn = _fourier_mix(cfn)
            cy_na = _context_attention(cq, kv_c)
            flat = lambda a: a.reshape(1, -1, a.shape[-1])
            ctx = _mix_ffn(flat(ctx), flat(cy_hy), flat(cy_fn), flat(cy_na), w_out_b, n2g, mod, ctx_row, l,
                           w_gate_b, w_up_b, w_down_b, fg, False, tm).reshape(ctx.shape)
    return x
```

```python
import functools
import math

import numpy as np
import jax
import jax.numpy as jnp
from jax import lax
from jax.experimental import pallas as pl
from jax.experimental.pallas import tpu as pltpu

F32 = jnp.float32
BF16 = jnp.bfloat16

EPS = 1e-6
GRID_W = 64
HEAD_DIM = 64
D_HYENA = 256
D_FNET = 256
D_NA = 512
FNET_GROUP_DIM = 64
NA_KH = 8
NA_KW = 16
NA_QROWS = 4
NA_KROWS = NA_QROWS + NA_KH
NA_KEY_GROUP = 256
POS_BANDS = 16
HYENA_DECAY_TARGET = 1e-2
HYENA_FAST_PCT = 0.3
HYENA_SLOW_PCT = 1.5
MASK_VALUE = -1e30
VMEM_LIMIT = 56 * 1024 * 1024
LANES = 128
MXU_WIDTH = 256
LOG2E = math.log2(math.e)
FFN_ROW_SPLIT = 2
F32_SUBLANES = 8
BF16_SUBLANES = 16


def _params(*sem):
    return pltpu.CompilerParams(dimension_semantics=sem, vmem_limit_bytes=VMEM_LIMIT)


def _bdot(a, b):
    return jnp.dot(a, b, preferred_element_type=F32)


def _split_bf16(a):
    hi = a.astype(BF16)
    lo = (a - hi.astype(F32)).astype(BF16)
    return hi, lo


def _dot3(a, b):
    a_hi, a_lo = _split_bf16(a)
    b_hi, b_lo = _split_bf16(b)
    return _bdot(a_hi, b_hi) + _bdot(a_lo, b_hi) + _bdot(a_hi, b_lo)


def _mod_body(c_ref, w_ref, b_ref, o_ref):
    c = c_ref[...]
    rows = c.shape[0]
    s_hi, s_lo = _split_bf16(c * jax.nn.sigmoid(c))
    w_hi, w_lo = _split_bf16(w_ref[0])
    both = _bdot(jnp.concatenate([s_hi, s_lo], axis=0), w_hi)
    o_ref[0] = both[:rows] + both[rows:] + _bdot(s_hi, w_lo) + b_ref[0]


def _modulation(c_all, w_mod, b_mod):
    depth, d, n = w_mod.shape
    rows = c_all.shape[0]
    tn = 1536
    return pl.pallas_call(
        _mod_body,
        grid=(depth, n // tn),
        in_specs=[
            pl.BlockSpec((rows, d), lambda l, j: (0, 0)),
            pl.BlockSpec((1, d, tn), lambda l, j: (l, 0, j)),
            pl.BlockSpec((1, 1, tn), lambda l, j: (l, 0, j)),
        ],
        out_specs=pl.BlockSpec((1, rows, tn), lambda l, j: (l, 0, j)),
        out_shape=jax.ShapeDtypeStruct((depth, rows, n), F32),
        compiler_params=_params("parallel", "parallel"),
        name="modulation",
    )(c_all, w_mod, b_mod.reshape(depth, 1, n))


def _modulated_norm(x, g, sc, sh):
    ms = jnp.mean(x * x, axis=-1, keepdims=True)
    y = x * lax.rsqrt(ms + EPS) * g
    return y * (1.0 + sc) + sh


def _mod_spec(layer, row_of, chunk, d):
    return pl.BlockSpec((1, 1, 1, 1, d), lambda i, *_: (layer, row_of(i), chunk, 0, 0))


def _short_conv_tile(u, prev_row, next_row, cw_ref, cb_ref):
    rows = u.shape[0]
    row = lax.broadcasted_iota(jnp.int32, u.shape, 0)
    prev = jnp.where(row == 0, prev_row, pltpu.roll(u, 1, axis=0))
    nxt = jnp.where(row == rows - 1, next_row, pltpu.roll(u, rows - 1, axis=0))
    return prev * cw_ref[0:1] + u * cw_ref[1:2] + nxt * cw_ref[2:3] + cb_ref[...]


def _norm_proj_body(splits, conv, x_ref, g_ref, sc_ref, sh_ref, w_ref, *rest):
    o_refs = ((rest[4:6],) + rest[6:]) if conv else rest
    g, sc, sh = g_ref[0], sc_ref[0, 0, 0], sh_ref[0, 0, 0]
    tm = x_ref.shape[1]
    edge = F32_SUBLANES if conv else 0
    if conv:
        xp_ref, xn_ref, cw_ref, cb_ref = rest[:4]
        x_all = jnp.concatenate([xp_ref[0], x_ref[0], xn_ref[0]], axis=0)
    else:
        x_all = x_ref[0]
    p_all = _bdot(_modulated_norm(x_all, g, sc, sh).astype(BF16), w_ref[0])
    p = p_all[edge:edge + tm]
    off = 0
    for idx, (o_ref, (width, _, scale)) in enumerate(zip(o_refs, splits)):
        part = p[:, off:off + width]
        if conv and idx == 0:
            gate_ref, v_ref = o_ref
            j, nj = pl.program_id(1), pl.num_programs(1)
            prev_row = jnp.where(j == 0, 0.0, p_all[edge - 1:edge, off:off + width])
            next_row = jnp.where(j == nj - 1, 0.0, p_all[edge + tm:edge + tm + 1, off:off + width])
            u = _short_conv_tile(part, prev_row, next_row, cw_ref, cb_ref)
            c = v_ref.shape[3]
            n_gates = gate_ref.shape[1]
            for gi in range(n_gates):
                gate_ref[0, gi] = u[:, gi * c:(gi + 1) * c].astype(gate_ref.dtype)
            v_ref[0, 0] = u[:, n_gates * c:]
        else:
            o_ref[0] = (part if scale is None else part * scale).astype(o_ref.dtype)
        off += width


def _norm_proj(x, norm_g, mod, row_of, layer, w, w_idx, splits, tm, conv=None):
    b, t, d = x.shape
    n = w.shape[2]
    tm = min(tm, t)
    edge = F32_SUBLANES
    tok = lambda wd: pl.BlockSpec((1, tm, wd), lambda i, j: (i, j, 0))
    in_specs = [
        tok(d),
        pl.BlockSpec((1, 1, d), lambda i, j: (layer, 0, 0)),
        _mod_spec(layer, row_of, 1, d),
        _mod_spec(layer, row_of, 0, d),
        pl.BlockSpec((1, d, n), lambda i, j: (w_idx, 0, 0)),
    ]
    args = [x, norm_g, mod, mod, w]
    out_specs = [tok(wd) for wd, _, _ in splits]
    out_shape = [jax.ShapeDtypeStruct((b, t, wd), dt) for wd, dt, _ in splits]
    if conv is not None:
        cw, cb = conv
        w0, c = splits[0][0], D_HYENA
        per_tile = tm // edge
        in_specs += [
            pl.BlockSpec((1, edge, d), lambda i, j: (i, jnp.maximum(j * per_tile - 1, 0), 0)),
            pl.BlockSpec((1, edge, d), lambda i, j: (i, jnp.minimum((j + 1) * per_tile, t // edge - 1), 0)),
            pl.BlockSpec(cw.shape, lambda i, j: (0, 0)),
            pl.BlockSpec(cb.shape, lambda i, j: (0, 0)),
        ]
        args += [x, x, cw, cb]
        n_gates = w0 // c - 1
        out_specs[0:1] = [pl.BlockSpec((1, n_gates, tm, c), lambda i, j: (i, 0, j, 0)),
                          pl.BlockSpec((1, 1, tm, c), lambda i, j: (i, 0, j, 0))]
        out_shape[0:1] = [jax.ShapeDtypeStruct((b, n_gates, t, c), BF16),
                          jax.ShapeDtypeStruct((b, 1, t, c), splits[0][1])]
    return pl.pallas_call(
        functools.partial(_norm_proj_body, tuple(splits), conv is not None),
        grid=(b, t // tm),
        in_specs=in_specs,
        out_specs=out_specs,
        out_shape=out_shape,
        compiler_params=_params("parallel", "parallel"),
        name="norm_proj",
    )(*args)


def _mix_ffn_body(final_norm, x_ref, hy_ref, fn_ref, na_ref, wo_ref, g1_ref, ng_ref, sc_ref, sh_ref, g2_ref,
                  wg_ref, wu_ref, wd_ref, fg_ref, o_ref):
    tm = x_ref.shape[1]
    sub = tm // FFN_ROW_SPLIT if tm % (FFN_ROW_SPLIT * MXU_WIDTH) == 0 else tm
    halves = [slice(r0, r0 + sub) for r0 in range(0, tm, sub)]
    x1s, hs = [], []
    for rows in halves:
        mix = (_bdot(hy_ref[0, rows], wo_ref[0, :D_HYENA])
               + _bdot(fn_ref[0, rows], wo_ref[0, D_HYENA:D_HYENA + D_FNET])
               + _bdot(na_ref[0, rows], wo_ref[0, D_HYENA + D_FNET:]))
        x1 = x_ref[0, rows] + g1_ref[0, 0, 0] * mix
        x1s.append(x1)
        hs.append(_modulated_norm(x1, ng_ref[0], sc_ref[0, 0, 0], sh_ref[0, 0, 0]).astype(BF16))
    accs = [None] * len(halves)
    for c0 in range(0, wg_ref.shape[2], MXU_WIDTH):
        cols = slice(c0, c0 + MXU_WIDTH)
        for s, h in enumerate(hs):
            a = _bdot(h, wg_ref[0, :, cols])
            u = _bdot(h, wu_ref[0, :, cols])
            act = (a * jax.nn.sigmoid(a) * u).astype(BF16)
            part = _bdot(act, wd_ref[0, cols, :])
            accs[s] = part if accs[s] is None else accs[s] + part
    for rows, x1, acc in zip(halves, x1s, accs):
        y = x1 + g2_ref[0, 0, 0] * acc
        if final_norm:
            ms = jnp.mean(y * y, axis=-1, keepdims=True)
            y = y * lax.rsqrt(ms + EPS) * fg_ref[...]
        o_ref[0, rows] = y


def _mix_ffn(x, y_hy, y_fn, y_na, w_out, norm_g, mod, row_of, layer, wg, wu, wd, final_g, final_norm, tm):
    b, t, d = x.shape
    dff = wg.shape[2]
    assert dff % MXU_WIDTH == 0
    tm = min(tm, t)
    tok = lambda wd_: pl.BlockSpec((1, tm, wd_), lambda i, j: (i, j, 0))
    resident = lambda shape: pl.BlockSpec((1,) + shape, lambda i, j: (layer, 0, 0), pipeline_mode=pl.Buffered(1))
    return pl.pallas_call(
        functools.partial(_mix_ffn_body, final_norm),
        grid=(b, t // tm),
        in_specs=[
            tok(d), tok(D_HYENA), tok(D_FNET), tok(D_NA),
            resident((d, d)),
            _mod_spec(layer, row_of, 2, d),
            pl.BlockSpec((1, 1, d), lambda i, j: (layer, 0, 0)),
            _mod_spec(layer, row_of, 4, d),
            _mod_spec(layer, row_of, 3, d),
            _mod_spec(layer, row_of, 5, d),
            resident((d, dff)), resident((d, dff)), resident((dff, d)),
            pl.BlockSpec((1, d), lambda i, j: (0, 0)),
        ],
        out_specs=tok(d),
        out_shape=jax.ShapeDtypeStruct(x.shape, F32),
        compiler_params=_params("parallel", "parallel"),
        name="mix_ffn",
    )(x, y_hy, y_fn, y_na, w_out, mod, norm_g, mod, mod, mod, wg, wu, wd, final_g)


def _split_seq(n_total):
    log = int(round(math.log2(n_total)))
    assert 2 ** log == n_total
    n1 = 2 ** (log // 2)
    return n1, n_total // n1


@functools.lru_cache(maxsize=None)
def _hyena_consts(seq):
    n = 2 * seq
    n1, n2 = _split_seq(n)
    h1 = n1 // 2
    f1 = np.arange(h1)[:, None]
    t1 = np.arange(h1)[None, :]
    th = np.pi * (2 * f1 + 1) * t1 / n1
    m1 = np.empty((n1, h1))
    m1[0::2] = np.cos(th)
    m1[1::2] = -np.sin(th)
    m3 = m1.T * (2.0 / n)
    f = (np.arange(h1)[:, None, None] + n1 * np.arange(n2)[None, :, None])
    t2 = np.arange(n2)[None, None, :]
    ph = np.pi * ((2 * f + 1) * t2 % (2 * n)) / n
    c, s = np.cos(ph), np.sin(ph)
    m2 = np.concatenate([np.concatenate([c, s], axis=2), np.concatenate([-s, c], axis=2)], axis=1)
    m2t = np.transpose(m2, (0, 2, 1))
    m1k = np.kron(m1, np.eye(F32_SUBLANES))
    m3k = np.kron(m3, np.eye(BF16_SUBLANES))
    return n1, n2, m1k, m2, m2t, m3k


@functools.lru_cache(maxsize=None)
def _fnet_consts(seq):
    n1, n2 = _split_seq(seq)
    n1, n2 = n1 // 2, n2 * 2
    f1 = np.arange(n1)[:, None]
    t1 = np.arange(n1)[None, :]
    th = 2 * np.pi * (f1 * t1 % n1) / n1
    m1 = np.empty((2 * n1, n1))
    m1[0::2] = np.cos(th)
    m1[1::2] = -np.sin(th)
    f = (np.arange(n1)[:, None, None] + n1 * np.arange(n2)[None, :, None])
    t2 = np.arange(n2)[None, None, :]
    ph = 2 * np.pi * (f * t2 % seq) / seq
    c, s = np.cos(ph), np.sin(ph)
    m2 = np.concatenate([np.concatenate([c, s], axis=2), np.concatenate([-s, c], axis=2)], axis=1)
    gd = FNET_GROUP_DIM
    cc = np.arange(gd)
    pg = 2 * np.pi * (np.outer(cc, cc) % gd) / gd
    scale = 1.0 / math.sqrt(gd * seq)
    eye = np.eye(D_FNET // gd)
    bdc = np.kron(eye, np.cos(pg)) * scale
    bds = np.kron(eye, np.sin(pg)) * scale
    m1k = np.kron(m1, np.eye(BF16_SUBLANES))
    g = F32_SUBLANES
    perm = np.zeros((g * n1, n1 * g))
    for f2 in range(g):
        for ff in range(n1):
            perm[f2 * n1 + ff, ff * g + f2] = 1.0
    return n1, n2, m1k, m2, perm, bdc, bds


def _mxu_const(a):
    return jnp.asarray(a, dtype=F32).astype(BF16)


def _stage1_into(mk_ref, x_at, a_scr, group):
    nf, _, n2, c = a_scr.shape

    def body(s, carry):
        r0 = pl.multiple_of(s * group, group)
        xs = x_at(r0)
        xs = xs.reshape(xs.shape[0] * group, c).astype(BF16)
        a_scr[:, :, pl.ds(r0, group), :] = _bdot(mk_ref[...], xs).reshape(nf, 2, group, c)
        return carry

    lax.fori_loop(0, n2 // group, body, 0, unroll=True)


def _stage2_rows(a_scr, f1):
    _, _, n2, c = a_scr.shape
    return a_scr[f1].reshape(2 * n2, c).astype(BF16)


def _filter_body(zt_ref, t_ref, w1t_ref, b1_ref, fr_ref, w2t_ref, b2_ref, w3_ref, dl_ref, m1k_ref, m2_ref,
                 kr_ref, ki_ref, h_scr, kt_scr, af_scr, ab_scr):
    o, j = pl.program_id(0), pl.program_id(1)
    fb = m2_ref.shape[0]
    h1, _, n2, c = af_scr.shape
    g = F32_SUBLANES

    @pl.when((o == 0) & (j == 0))
    def _():
        h = jnp.sin(fr_ref[:, 0:1] * (_dot3(w1t_ref[...], zt_ref[...]) + b1_ref[...]))
        h = jnp.sin(fr_ref[:, 1:2] * (_dot3(w2t_ref[...], h) + b2_ref[...]))
        h_scr[...] = h.T

    @pl.when(j == 0)
    def _():
        k = _dot3(h_scr[...], w3_ref[0])
        decay = jnp.exp(-t_ref[...] * jnp.abs(dl_ref[...]))
        kf = k[:, :c] * decay
        row = lax.broadcasted_iota(jnp.int32, kf.shape, 0)
        kb = jnp.where(row == 0, 0.0, k[:, c:] * decay)
        norm = jnp.sum(jnp.abs(kf), axis=0, keepdims=True) + jnp.sum(jnp.abs(kb), axis=0, keepdims=True)
        inv = 1.0 / norm
        kt_scr[0] = (kf * inv).reshape(h1, n2, c)
        kt_scr[1] = (kb * inv).reshape(h1, n2, c)
        _stage1_into(m1k_ref, lambda r0: kt_scr[0, :, pl.ds(r0, g), :], af_scr, g)
        _stage1_into(m1k_ref, lambda r0: kt_scr[1, :, pl.ds(r0, g), :], ab_scr, g)

    for i in range(fb):
        f1 = j * fb + i
        xf = _bdot(m2_ref[i], _stage2_rows(af_scr, f1))
        xb = _bdot(m2_ref[i], _stage2_rows(ab_scr, f1))
        kr_ref[0, i] = xf[:n2] + xb[:n2]
        ki_ref[0, i] = xf[n2:] - xb[n2:]


def _hyena_filter_spectrum(seq, w1, b1, freq, w2, b2, w3, m1k, m2, fb):
    c = D_HYENA
    h1, n2 = m2.shape[0], m2.shape[1] // 2
    t = np.linspace(0.0, 1.0, seq)[:, None]
    w = 2.0 * np.pi * np.arange(seq)[:, None] / seq
    f = np.linspace(1e-4, POS_BANDS - 1, POS_BANDS)[None]
    z = np.concatenate([t, np.cos(f * w), -np.sin(f * w)], axis=-1).astype(np.float32)
    pe = z.shape[1]
    zt = jnp.asarray(np.pad(z, ((0, 0), (0, LANES - pe))).T)
    tcol = jnp.asarray(t.astype(np.float32))
    w1t = jnp.pad(w1, ((0, LANES - pe), (0, 0))).T
    max_decay = math.log(HYENA_DECAY_TARGET) / HYENA_FAST_PCT
    min_decay = math.log(HYENA_DECAY_TARGET) / HYENA_SLOW_PCT
    deltas = jnp.asarray(np.linspace(min_decay, max_decay, c, dtype=np.float32)[None])
    hid = w1.shape[1]
    w3r = jnp.transpose(w3.reshape(hid, 2, 2 * c), (1, 0, 2))
    full = lambda a: pl.BlockSpec(a.shape, lambda o, j: (0,) * a.ndim)
    b1c, b2c, frt = b1.reshape(hid, 1), b2.reshape(hid, 1), freq.T
    out = jax.ShapeDtypeStruct((2, h1, n2, c), F32)
    return pl.pallas_call(
        _filter_body,
        grid=(2, h1 // fb),
        in_specs=[full(zt), full(tcol), full(w1t), full(b1c), full(frt), full(w2.T), full(b2c),
                  pl.BlockSpec((1, hid, 2 * c), lambda o, j: (o, 0, 0)), full(deltas), full(m1k),
                  pl.BlockSpec((fb, 2 * n2, 2 * n2), lambda o, j: (j, 0, 0))],
        out_specs=[pl.BlockSpec((1, fb, n2, c), lambda o, j: (o, j, 0, 0))] * 2,
        out_shape=[out, out],
        scratch_shapes=[pltpu.VMEM((seq, hid), F32), pltpu.VMEM((2, h1, n2, c), F32),
                        pltpu.VMEM((h1, 2, n2, c), F32), pltpu.VMEM((h1, 2, n2, c), F32)],
        compiler_params=_params("arbitrary", "arbitrary"),
        name="hyena_filter_spectrum",
    )(zt, tcol, w1t, b1c, frt, w2.T, b2c, w3r, deltas, m1k, m2)


def _conv_fwd_body(m1k_ref, m2_ref, m2t_ref, x_ref, kr_ref, ki_ref, o_ref, a_scr):
    h1, _, n2, c = a_scr.shape
    g = F32_SUBLANES
    _stage1_into(m1k_ref, lambda r0: x_ref[0, 0, :, pl.ds(r0, g), :], a_scr, g)

    def per_f1(f1, carry):
        x = _bdot(m2_ref[f1], _stage2_rows(a_scr, f1))
        xr, xi = x[:n2], x[n2:]
        kr, ki = kr_ref[0, f1], ki_ref[0, f1]
        y = jnp.concatenate([kr * xr - ki * xi, kr * xi + ki * xr], axis=0).astype(BF16)
        o_ref[0, pl.ds(2 * f1, 2)] = _bdot(m2t_ref[f1], y).astype(BF16).reshape(2, n2, c)
        return carry

    lax.fori_loop(0, h1, per_f1, 0, unroll=True)


def _conv_fwd(m1k, m2, m2t, x, sel, kr, ki, order):
    b, _, h1, n2, c = x.shape
    resident = lambda a, idx: pl.BlockSpec(a.shape, lambda i: idx, pipeline_mode=pl.Buffered(1))
    kspec = pl.BlockSpec((1, h1, n2, c), lambda i: (order, 0, 0, 0), pipeline_mode=pl.Buffered(1))
    return pl.pallas_call(
        _conv_fwd_body,
        grid=(b,),
        in_specs=[resident(m1k, (0, 0)), resident(m2, (0, 0, 0)), resident(m2t, (0, 0, 0)),
                  pl.BlockSpec((1, 1, h1, n2, c), lambda i: (i, sel, 0, 0, 0)),
                  kspec, kspec],
        out_specs=pl.BlockSpec((1, 2 * h1, n2, c), lambda i: (i, 0, 0, 0)),
        out_shape=jax.ShapeDtypeStruct((b, 2 * h1, n2, c), BF16),
        scratch_shapes=[pltpu.VMEM((h1, 2, n2, c), F32)],
        compiler_params=_params("parallel"),
        name="hyena_conv_fwd",
    )(m1k, m2, m2t, x, kr, ki)


def _conv_inv_body(m3k_ref, b_ref, gate_ref, skipin_ref, skip_ref, o_ref):
    n1, ts, c = b_ref.shape[1:]
    h1 = n1 // 2
    g = BF16_SUBLANES
    for s in range(ts // g):
        rows = slice(s * g, (s + 1) * g)
        bs = b_ref[0, :, rows, :].reshape(n1 * g, c)
        y = _bdot(m3k_ref[...], bs).reshape(h1, g, c)
        z = gate_ref[0, 0, :, rows, :] * (y + skipin_ref[0, 0, :, rows, :] * skip_ref[...])
        o_ref[0, 0, :, rows, :] = z.astype(o_ref.dtype)


def _conv_inv(m3k, bm, gates, gate_sel, skipin, skipin_sel, skip_row, out_dtype):
    b, n1, n2, c = bm.shape
    h1 = n1 // 2
    ts = min(n2, 4 * BF16_SUBLANES)
    tok = lambda sel: pl.BlockSpec((1, 1, h1, ts, c), lambda i, j: (i, sel, 0, j, 0))
    return pl.pallas_call(
        _conv_inv_body,
        grid=(b, n2 // ts),
        in_specs=[pl.BlockSpec(m3k.shape, lambda i, j: (0, 0)),
                  pl.BlockSpec((1, n1, ts, c), lambda i, j: (i, 0, j, 0)),
                  tok(gate_sel), tok(skipin_sel),
                  pl.BlockSpec((1, c), lambda i, j: (0, 0))],
        out_specs=tok(0),
        out_shape=jax.ShapeDtypeStruct((b, 1, h1, n2, c), out_dtype),
        compiler_params=_params("parallel", "parallel"),
        name="hyena_conv_inv",
    )(m3k, bm, gates, skipin, skip_row)


def _hyena_mixer(gates, v, w1, b1, freq, w2, b2, w3, skip):
    b, _, seq, c = v.shape
    n1, n2, m1k, m2, m2t, m3k = _hyena_consts(seq)
    h1 = n1 // 2
    m1k, m2, m2t, m3k = (_mxu_const(a) for a in (m1k, m2, m2t, m3k))

    kr, ki = _hyena_filter_spectrum(seq, w1, b1, freq, w2, b2, w3, m1k, m2, min(h1, 8))

    gates = gates.reshape(b, 2, h1, n2, c)
    v = v.reshape(b, 1, h1, n2, c)
    bm = _conv_fwd(m1k, m2, m2t, v, 0, kr, ki, 0)
    z = _conv_inv(m3k, bm, gates, 0, v, 0, skip[0:1], F32)
    bm = _conv_fwd(m1k, m2, m2t, z, 0, kr, ki, 1)
    y = _conv_inv(m3k, bm, gates, 1, z, 0, skip[1:2], BF16)
    return y.reshape(b, seq, c)


def _fnet_body(m1k_ref, m2_ref, perm_ref, bdc_ref, bds_ref, g_ref, o_ref, a_scr):
    j = pl.program_id(1)
    fb = m2_ref.shape[0]
    n1, _, n2, c = a_scr.shape

    @pl.when(j == 0)
    def _():
        gi = BF16_SUBLANES
        _stage1_into(m1k_ref, lambda r0: g_ref[0, :, pl.ds(r0, gi), :], a_scr, gi)

    for i in range(fb):
        f1 = j * fb + i
        a_scr[f1] = _bdot(m2_ref[i], _stage2_rows(a_scr, f1)).reshape(2, n2, c)

    @pl.when(j == pl.num_programs(1) - 1)
    def _():
        go = F32_SUBLANES

        def body(s, carry):
            r0 = pl.multiple_of(s * go, go)
            re = a_scr[:, 0, pl.ds(r0, go), :].reshape(n1 * go, c).astype(BF16)
            im = a_scr[:, 1, pl.ds(r0, go), :].reshape(n1 * go, c).astype(BF16)
            xr = _bdot(perm_ref[...], re).astype(BF16)
            xi = _bdot(perm_ref[...], im).astype(BF16)
            out = _bdot(xr, bdc_ref[...]) + _bdot(xi, bds_ref[...])
            o_ref[0, pl.ds(pl.multiple_of(s * go * n1, go * n1), go * n1), :] = out.astype(o_ref.dtype)
            return carry

        lax.fori_loop(0, n2 // go, body, 0, unroll=True)


def _fourier_mix(g):
    b, seq, c = g.shape
    n1, n2, m1k, m2, perm, bdc, bds = _fnet_consts(seq)
    m1k, m2, perm, bdc, bds = (_mxu_const(a) for a in (m1k, m2, perm, bdc, bds))
    fb = min(n1, 8)
    const = lambda a: pl.BlockSpec(a.shape, lambda i, j: (0,) * a.ndim)
    return pl.pallas_call(
        _fnet_body,
        grid=(b, n1 // fb),
        in_specs=[const(m1k),
                  pl.BlockSpec((fb, 2 * n2, 2 * n2), lambda i, j: (j, 0, 0)),
                  const(perm), const(bdc), const(bds),
                  pl.BlockSpec((1, n1, n2, c), lambda i, j: (i, 0, 0, 0))],
        out_specs=pl.BlockSpec((1, seq, c), lambda i, j: (i, 0, 0)),
        out_shape=jax.ShapeDtypeStruct((b, seq, c), BF16),
        scratch_shapes=[pltpu.VMEM((n1, 2, n2, c), F32)],
        compiler_params=_params("parallel", "arbitrary"),
        name="fnet",
    )(m1k, m2, perm, bdc, bds, g.reshape(b, n1, n2, c))


def _dot_nt(a, b):
    return lax.dot_general(a, b, (((1,), (1,)), ((), ())), preferred_element_type=F32)


def _head_pair_attention(q2, keys, values, biases):
    nq = q2.shape[0]
    lane = lax.broadcasted_iota(jnp.int32, q2.shape, 1)
    zero = jnp.zeros_like(q2)
    qs = jnp.concatenate([jnp.where(lane < HEAD_DIM, q2, zero), jnp.where(lane >= HEAD_DIM, q2, zero)], axis=0)
    m = denom = acc = None
    for gi, (kk, vv) in enumerate(zip(keys, values)):
        s = _dot_nt(qs, kk)
        if biases[0][gi] is not None:
            s = s + jnp.concatenate([biases[0][gi], biases[1][gi]], axis=0)
        m_grp = s.max(axis=-1, keepdims=True)
        m_new = m_grp if m is None else jnp.maximum(m, m_grp)
        e = jnp.exp2(s - m_new)
        e_part = e[:, :LANES]
        for c0 in range(LANES, e.shape[1], LANES):
            e_part = e_part + e[:, c0:c0 + LANES]
        pv = _bdot(e.astype(BF16), vv)
        if m is None:
            denom, acc = e_part, pv
        else:
            alpha = jnp.exp2(m - m_new)
            denom = alpha * denom + e_part
            acc = alpha * acc + pv
        m = m_new
    out = acc / denom.sum(axis=-1, keepdims=True)
    return jnp.where(lane < HEAD_DIM, out[:nq], out[nq:])


def _na_window_base(row_block, rows):
    return jnp.clip(row_block * NA_QROWS - NA_KH // 2, 0, rows - NA_KROWS)


def _na_body(rows, q_ref, k_ref, v_ref, kc_ref, vc_ref, bias_ref, o_ref):
    base = _na_window_base(pl.program_id(1), rows)
    start = pl.multiple_of(base * GRID_W, GRID_W)
    n_keys = NA_KROWS * GRID_W
    n_pairs = q_ref.shape[2] // LANES
    groups = [(c0, min(NA_KEY_GROUP, n_keys - c0)) for c0 in range(0, n_keys, NA_KEY_GROUP)]
    for p in range(n_pairs):
        sl = slice(p * LANES, (p + 1) * LANES)
        q2 = q_ref[0, :, sl]
        keys = [kc_ref[0, :, sl]] + [k_ref[0, pl.ds(start + c0, n), sl] for c0, n in groups]
        values = [vc_ref[0, :, sl]] + [v_ref[0, pl.ds(start + c0, n), sl] for c0, n in groups]
        biases = [[None] + [bias_ref[0, 2 * p + hh, 0, :, c0:c0 + n] for c0, n in groups] for hh in range(2)]
        o = _head_pair_attention(q2, keys, values, biases)
        o_ref[0, :, sl] = o.astype(o_ref.dtype)


def _na_bias_body(rows, p_ref, o_ref):
    kh, kw, w = NA_KH, NA_KW, GRID_W
    shape = (w, 2 * w)
    qc = lax.broadcasted_iota(jnp.int32, shape, 0)
    lane = lax.broadcasted_iota(jnp.int32, shape, 1)
    kc = lane & (w - 1)
    col_start = jnp.clip(qc - kw // 2, 0, w - kw)
    col_valid = (kc >= col_start) & (kc < col_start + kw)
    low_half = lane < w
    masked = jnp.full(shape, MASK_VALUE, F32)
    tiles = []
    for a in range(2 * kh - 1):
        row = jnp.broadcast_to(p_ref[0, 0, a:a + 1, :], shape)
        t = pltpu.roll(row, w + 1, 1, stride=1, stride_axis=0)
        t = jnp.where(low_half, t, pltpu.roll(t, w, 1))
        tiles.append(jnp.where(col_valid, t, masked))
    for case, r0 in enumerate((0, NA_QROWS, rows - NA_QROWS)):
        base = int(np.clip(r0 - kh // 2, 0, rows - NA_KROWS))
        for qr in range(NA_QROWS):
            q_abs = r0 + qr
            r_start = int(np.clip(q_abs - kh // 2, 0, rows - kh))

            def tile_for(kr):
                k_abs = base + kr
                if kr >= NA_KROWS or not (r_start <= k_abs < r_start + kh):
                    return masked
                return tiles[k_abs - q_abs + kh - 1]

            q_rows = slice(qr * w, (qr + 1) * w)
            for pair in range((NA_KROWS + 1) // 2):
                chunk = jnp.where(low_half, tile_for(2 * pair), tile_for(2 * pair + 1))
                width = min(2 * w, NA_KROWS * w - pair * 2 * w)
                o_ref[0, 0, case, q_rows, pair * 2 * w:pair * 2 * w + width] = chunk[:, :width]


def _na_bias_tables(rpb, rows):
    w = GRID_W
    depth, nh, na, nb = rpb.shape
    assert 2 * w == LANES and nb == 2 * NA_KW - 1
    lo = w - NA_KW
    padded = jnp.pad(rpb.astype(F32) * LOG2E, ((0, 0), (0, 0), (0, (-na) % 8), (lo, 2 * w - nb - lo)))
    return pl.pallas_call(
        functools.partial(_na_bias_body, rows),
        grid=(depth, nh),
        in_specs=[pl.BlockSpec((1, 1) + padded.shape[2:], lambda l, h: (l, h, 0, 0))],
        out_specs=pl.BlockSpec((1, 1, 3, NA_QROWS * w, NA_KROWS * w), lambda l, h: (l, h, 0, 0, 0)),
        out_shape=jax.ShapeDtypeStruct((depth, nh, 3, NA_QROWS * w, NA_KROWS * w), F32),
        compiler_params=_params("parallel", "parallel"),
        name="na_bias_tables",
    )(padded)


def _neighbourhood_attention(q, kv, kv_ctx, bias, layer):
    b, seq, _ = q.shape
    rows = seq // GRID_W
    lc = kv_ctx.shape[1]
    assert rows % NA_QROWS == 0 and rows >= NA_KROWS + 1 and NA_QROWS == NA_KH // 2
    n_blocks = rows // NA_QROWS
    nh = bias.shape[1]
    tq = NA_QROWS * GRID_W

    def bias_map(i, r):
        return (layer, 0, jnp.where(r == 0, 0, jnp.where(r == n_blocks - 1, 2, 1)), 0, 0)

    return pl.pallas_call(
        functools.partial(_na_body, rows),
        grid=(b, n_blocks),
        in_specs=[pl.BlockSpec((1, tq, D_NA), lambda i, r: (i, r, 0)),
                  pl.BlockSpec((1, seq, D_NA), lambda i, r: (i, 0, 0)),
                  pl.BlockSpec((1, seq, D_NA), lambda i, r: (i, 0, 1)),
                  pl.BlockSpec((1, lc, D_NA), lambda i, r: (i, 0, 0)),
                  pl.BlockSpec((1, lc, D_NA), lambda i, r: (i, 0, 1)),
                  pl.BlockSpec((1, nh, 1, tq, NA_KROWS * GRID_W), bias_map)],
        out_specs=pl.BlockSpec((1, tq, D_NA), lambda i, r: (i, r, 0)),
        out_shape=jax.ShapeDtypeStruct((b, seq, D_NA), BF16),
        compiler_params=_params("parallel", "arbitrary"),
        name="neighbourhood_attention",
    )(q, kv, kv, kv_ctx, kv_ctx, bias)


def _ctx_attn_body(q_ref, k_ref, v_ref, o_ref):
    o = _head_pair_attention(q_ref[0], [k_ref[0]], [v_ref[0]], [[None], [None]])
    o_ref[0] = o.astype(o_ref.dtype)


def _context_attention(q, kv):
    b, lc, _ = q.shape
    n_pairs = D_NA // LANES
    return pl.pallas_call(
        _ctx_attn_body,
        grid=(b, n_pairs),
        in_specs=[pl.BlockSpec((1, lc, LANES), lambda i, p: (i, 0, p)),
                  pl.BlockSpec((1, lc, LANES), lambda i, p: (i, 0, p)),
                  pl.BlockSpec((1, lc, LANES), lambda i, p: (i, 0, n_pairs + p))],
        out_specs=pl.BlockSpec((1, lc, LANES), lambda i, p: (i, 0, p)),
        out_shape=jax.ShapeDtypeStruct((b, lc, D_NA), BF16),
        compiler_params=_params("parallel", "parallel"),
        name="context_attention",
    )(q, kv, kv)


def kernel(x, c, ctx, c_ctx, w_mod, b_mod, norm1_g, norm2_g, w_in, w_out, hy_conv_w, hy_conv_b,
           filt_w1, filt_b1, filt_freq, filt_w2, filt_b2, filt_w3, hy_skip, na_rpb,
           w_gate, w_up, w_down, final_g):
    depth = w_mod.shape[0]
    b, seq, d = x.shape
    off_k = 3 * D_HYENA + D_FNET + D_NA
    qk_scale = LOG2E / math.sqrt(HEAD_DIM)
    splits = ((3 * D_HYENA, F32, None), (D_FNET, BF16, None), (D_NA, BF16, qk_scale), (2 * D_NA, BF16, None))
    tm_proj, tm = 1024, 1024

    pad = (-(b + 1)) % 8
    c_all = jnp.concatenate([c, c_ctx[None], jnp.zeros((pad, d), F32)], axis=0)
    mod = _modulation(c_all, w_mod, b_mod).reshape(depth, b + 1 + pad, 6, 1, d)
    lat_row = lambda i: i
    ctx_row = lambda i: b

    w_in_b, w_out_b = w_in.astype(BF16), w_out.astype(BF16)
    w_gate_b, w_up_b, w_down_b = w_gate.astype(BF16), w_up.astype(BF16), w_down.astype(BF16)
    w_kv_last = w_in_b[depth - 1:, :, off_k:]
    n1g, n2g = norm1_g.reshape(depth, 1, d), norm2_g.reshape(depth, 1, d)
    fg = final_g.reshape(1, d)
    na_bias = _na_bias_tables(na_rpb, seq // GRID_W)

    for l in range(depth):
        last = l == depth - 1
        hy_params = (filt_w1[l], filt_b1[l], filt_freq[l], filt_w2[l], filt_b2[l], filt_w3[l], hy_skip[l])
        conv = (hy_conv_w[l], hy_conv_b[l].reshape(1, 3 * D_HYENA))

        gates, v, fn, q, kv = _norm_proj(x, n1g, mod, lat_row, l, w_in_b, l, splits, tm_proj, conv)
        if last:
            (kv_c,) = _norm_proj(ctx.reshape(1, -1, d), n1g, mod, ctx_row, l, w_kv_last, 0,
                                 ((2 * D_NA, BF16, None),), tm_proj)
            kv_c = kv_c.reshape(b, -1, 2 * D_NA)
        else:
            cgates, cv, cfn, cq, kv_c = _norm_proj(ctx, n1g, mod, ctx_row, l, w_in_b, l, splits, tm_proj, conv)

        y_hy = _hyena_mixer(gates, v, *hy_params)
        y_fn = _fourier_mix(fn)
        y_na = _neighbourhood_attention(q, kv, kv_c, na_bias, l)

        x = _mix_ffn(x, y_hy, y_fn, y_na, w_out_b, n2g, mod, lat_row, l, w_gate_b, w_up_b, w_down_b,
                     fg, last, tm)

        if not last:
            cy_hy = _hyena_mixer(cgates, cv, *hy_params)
            cy_fn = _fourier_mix(cfn)
            cy_na = _context_attention(cq, kv_c)
            flat = lambda a: a.reshape(1, -1, a.shape[-1])
            ctx = _mix_ffn(flat(ctx), flat(cy_hy), flat(cy_fn), flat(cy_na), w_out_b, n2g, mod, ctx_row, l,
                           w_gate_b, w_up_b, w_down_b, fg, False, tm).reshape(ctx.shape)
    return x
```

```python
import functools
import math

import numpy as np
import jax
import jax.numpy as jnp
from jax import lax
from jax.experimental import pallas as pl
from jax.experimental.pallas import tpu as pltpu

F32 = jnp.float32
BF16 = jnp.bfloat16

EPS = 1e-6
GRID_W = 64
HEAD_DIM = 64
D_HYENA = 256
D_FNET = 256
D_NA = 512
FNET_GROUP_DIM = 64
NA_KH = 8
NA_KW = 16
NA_QROWS = 4
NA_KROWS = NA_QROWS + NA_KH
NA_KEY_GROUP = 256
POS_BANDS = 16
HYENA_DECAY_TARGET = 1e-2
HYENA_FAST_PCT = 0.3
HYENA_SLOW_PCT = 1.5
MASK_VALUE = -1e30
VMEM_LIMIT = 56 * 1024 * 1024
LANES = 128
MXU_WIDTH = 256
LOG2E = math.log2(math.e)
FFN_ROW_SPLIT = 2
F32_SUBLANES = 8
BF16_SUBLANES = 16


def _params(*sem):
    return pltpu.CompilerParams(dimension_semantics=sem, vmem_limit_bytes=VMEM_LIMIT)


def _bdot(a, b):
    return jnp.dot(a, b, preferred_element_type=F32)


def _split_bf16(a):
    hi = a.astype(BF16)
    lo = (a - hi.astype(F32)).astype(BF16)
    return hi, lo


def _dot3(a, b):
    a_hi, a_lo = _split_bf16(a)
    b_hi, b_lo = _split_bf16(b)
    return _bdot(a_hi, b_hi) + _bdot(a_lo, b_hi) + _bdot(a_hi, b_lo)


def _mod_body(c_ref, w_ref, b_ref, o_ref):
    c = c_ref[...]
    rows = c.shape[0]
    s_hi, s_lo = _split_bf16(c * jax.nn.sigmoid(c))
    w_hi, w_lo = _split_bf16(w_ref[0])
    both = _bdot(jnp.concatenate([s_hi, s_lo], axis=0), w_hi)
    o_ref[0] = both[:rows] + both[rows:] + _bdot(s_hi, w_lo) + b_ref[0]


def _modulation(c_all, w_mod, b_mod):
    depth, d, n = w_mod.shape
    rows = c_all.shape[0]
    tn = 1536
    return pl.pallas_call(
        _mod_body,
        grid=(depth, n // tn),
        in_specs=[
            pl.BlockSpec((rows, d), lambda l, j: (0, 0)),
            pl.BlockSpec((1, d, tn), lambda l, j: (l, 0, j)),
            pl.BlockSpec((1, 1, tn), lambda l, j: (l, 0, j)),
        ],
        out_specs=pl.BlockSpec((1, rows, tn), lambda l, j: (l, 0, j)),
        out_shape=jax.ShapeDtypeStruct((depth, rows, n), F32),
        compiler_params=_params("parallel", "parallel"),
        name="modulation",
    )(c_all, w_mod, b_mod.reshape(depth, 1, n))


def _modulated_norm(x, g, sc, sh):
    ms = jnp.mean(x * x, axis=-1, keepdims=True)
    y = x * lax.rsqrt(ms + EPS) * g
    return y * (1.0 + sc) + sh


def _mod_spec(layer, row_of, chunk, d):
    return pl.BlockSpec((1, 1, 1, 1, d), lambda i, *_: (layer, row_of(i), chunk, 0, 0))


def _short_conv_tile(u, prev_row, next_row, cw_ref, cb_ref):
    rows = u.shape[0]
    row = lax.broadcasted_iota(jnp.int32, u.shape, 0)
    prev = jnp.where(row == 0, prev_row, pltpu.roll(u, 1, axis=0))
    nxt = jnp.where(row == rows - 1, next_row, pltpu.roll(u, rows - 1, axis=0))
    return prev * cw_ref[0:1] + u * cw_ref[1:2] + nxt * cw_ref[2:3] + cb_ref[...]


def _norm_proj_body(splits, conv, x_ref, g_ref, sc_ref, sh_ref, w_ref, *rest):
    o_refs = ((rest[4:6],) + rest[6:]) if conv else rest
    g, sc, sh = g_ref[0], sc_ref[0, 0, 0], sh_ref[0, 0, 0]
    tm = x_ref.shape[1]
    edge = F32_SUBLANES if conv else 0
    if conv:
        xp_ref, xn_ref, cw_ref, cb_ref = rest[:4]
        x_all = jnp.concatenate([xp_ref[0], x_ref[0], xn_ref[0]], axis=0)
    else:
        x_all = x_ref[0]
    p_all = _bdot(_modulated_norm(x_all, g, sc, sh).astype(BF16), w_ref[0])
    p = p_all[edge:edge + tm]
    off = 0
    for idx, (o_ref, (width, _, scale)) in enumerate(zip(o_refs, splits)):
        part = p[:, off:off + width]
        if conv and idx == 0:
            gate_ref, v_ref = o_ref
            j, nj = pl.program_id(1), pl.num_programs(1)
            prev_row = jnp.where(j == 0, 0.0, p_all[edge - 1:edge, off:off + width])
            next_row = jnp.where(j == nj - 1, 0.0, p_all[edge + tm:edge + tm + 1, off:off + width])
            u = _short_conv_tile(part, prev_row, next_row, cw_ref, cb_ref)
            c = v_ref.shape[3]
            n_gates = gate_ref.shape[1]
            for gi in range(n_gates):
                gate_ref[0, gi] = u[:, gi * c:(gi + 1) * c].astype(gate_ref.dtype)
            v_ref[0, 0] = u[:, n_gates * c:]
        else:
            o_ref[0] = (part if scale is None else part * scale).astype(o_ref.dtype)
        off += width


def _norm_proj(x, norm_g, mod, row_of, layer, w, w_idx, splits, tm, conv=None):
    b, t, d = x.shape
    n = w.shape[2]
    tm = min(tm, t)
    edge = F32_SUBLANES
    tok = lambda wd: pl.BlockSpec((1, tm, wd), lambda i, j: (i, j, 0))
    in_specs = [
        tok(d),
        pl.BlockSpec((1, 1, d), lambda i, j: (layer, 0, 0)),
        _mod_spec(layer, row_of, 1, d),
        _mod_spec(layer, row_of, 0, d),
        pl.BlockSpec((1, d, n), lambda i, j: (w_idx, 0, 0)),
    ]
    args = [x, norm_g, mod, mod, w]
    out_specs = [tok(wd) for wd, _, _ in splits]
    out_shape = [jax.ShapeDtypeStruct((b, t, wd), dt) for wd, dt, _ in splits]
    if conv is not None:
        cw, cb = conv
        w0, c = splits[0][0], D_HYENA
        per_tile = tm // edge
        in_specs += [
            pl.BlockSpec((1, edge, d), lambda i, j: (i, jnp.maximum(j * per_tile - 1, 0), 0)),
            pl.BlockSpec((1, edge, d), lambda i, j: (i, jnp.minimum((j + 1) * per_tile, t // edge - 1), 0)),
            pl.BlockSpec(cw.shape, lambda i, j: (0, 0)),
            pl.BlockSpec(cb.shape, lambda i, j: (0, 0)),
        ]
        args += [x, x, cw, cb]
        n_gates = w0 // c - 1
        out_specs[0:1] = [pl.BlockSpec((1, n_gates, tm, c), lambda i, j: (i, 0, j, 0)),
                          pl.BlockSpec((1, 1, tm, c), lambda i, j: (i, 0, j, 0))]
        out_shape[0:1] = [jax.ShapeDtypeStruct((b, n_gates, t, c), BF16),
                          jax.ShapeDtypeStruct((b, 1, t, c), splits[0][1])]
    return pl.pallas_call(
        functools.partial(_norm_proj_body, tuple(splits), conv is not None),
        grid=(b, t // tm),
        in_specs=in_specs,
        out_specs=out_specs,
        out_shape=out_shape,
        compiler_params=_params("parallel", "parallel"),
        name="norm_proj",
    )(*args)


def _mix_ffn_body(final_norm, x_ref, hy_ref, fn_ref, na_ref, wo_ref, g1_ref, ng_ref, sc_ref, sh_ref, g2_ref,
                  wg_ref, wu_ref, wd_ref, fg_ref, o_ref):
    tm = x_ref.shape[1]
    sub = tm // FFN_ROW_SPLIT if tm % (FFN_ROW_SPLIT * MXU_WIDTH) == 0 else tm
    halves = [slice(r0, r0 + sub) for r0 in range(0, tm, sub)]
    x1s, hs = [], []
    for rows in halves:
        mix = (_bdot(hy_ref[0, rows], wo_ref[0, :D_HYENA])
               + _bdot(fn_ref[0, rows], wo_ref[0, D_HYENA:D_HYENA + D_FNET])
               + _bdot(na_ref[0, rows], wo_ref[0, D_HYENA + D_FNET:]))
        x1 = x_ref[0, rows] + g1_ref[0, 0, 0] * mix
        x1s.append(x1)
        hs.append(_modulated_norm(x1, ng_ref[0], sc_ref[0, 0, 0], sh_ref[0, 0, 0]).astype(BF16))
    accs = [None] * len(halves)
    for c0 in range(0, wg_ref.shape[2], MXU_WIDTH):
        cols = slice(c0, c0 + MXU_WIDTH)
        for s, h in enumerate(hs):
            a = _bdot(h, wg_ref[0, :, cols])
            u = _bdot(h, wu_ref[0, :, cols])
            act = (a * jax.nn.sigmoid(a) * u).astype(BF16)
            part = _bdot(act, wd_ref[0, cols, :])
            accs[s] = part if accs[s] is None else accs[s] + part
    for rows, x1, acc in zip(halves, x1s, accs):
        y = x1 + g2_ref[0, 0, 0] * acc
        if final_norm:
            ms = jnp.mean(y * y, axis=-1, keepdims=True)
            y = y * lax.rsqrt(ms + EPS) * fg_ref[...]
        o_ref[0, rows] = y


def _mix_ffn(x, y_hy, y_fn, y_na, w_out, norm_g, mod, row_of, layer, wg, wu, wd, final_g, final_norm, tm):
    b, t, d = x.shape
    dff = wg.shape[2]
    assert dff % MXU_WIDTH == 0
    tm = min(tm, t)
    tok = lambda wd_: pl.BlockSpec((1, tm, wd_), lambda i, j: (i, j, 0))
    resident = lambda shape: pl.BlockSpec((1,) + shape, lambda i, j: (layer, 0, 0), pipeline_mode=pl.Buffered(1))
    return pl.pallas_call(
        functools.partial(_mix_ffn_body, final_norm),
        grid=(b, t // tm),
        in_specs=[
            tok(d), tok(D_HYENA), tok(D_FNET), tok(D_NA),
            resident((d, d)),
            _mod_spec(layer, row_of, 2, d),
            pl.BlockSpec((1, 1, d), lambda i, j: (layer, 0, 0)),
            _mod_spec(layer, row_of, 4, d),
            _mod_spec(layer, row_of, 3, d),
            _mod_spec(layer, row_of, 5, d),
            resident((d, dff)), resident((d, dff)), resident((dff, d)),
            pl.BlockSpec((1, d), lambda i, j: (0, 0)),
        ],
        out_specs=tok(d),
        out_shape=jax.ShapeDtypeStruct(x.shape, F32),
        compiler_params=_params("parallel", "parallel"),
        name="mix_ffn",
    )(x, y_hy, y_fn, y_na, w_out, mod, norm_g, mod, mod, mod, wg, wu, wd, final_g)


def _split_seq(n_total):
    log = int(round(math.log2(n_total)))
    assert 2 ** log == n_total
    n1 = 2 ** (log // 2)
    return n1, n_total // n1


@functools.lru_cache(maxsize=None)
def _hyena_consts(seq):
    n = 2 * seq
    n1, n2 = _split_seq(n)
    h1 = n1 // 2
    f1 = np.arange(h1)[:, None]
    t1 = np.arange(h1)[None, :]
    th = np.pi * (2 * f1 + 1) * t1 / n1
    m1 = np.empty((n1, h1))
    m1[0::2] = np.cos(th)
    m1[1::2] = -np.sin(th)
    m3 = m1.T * (2.0 / n)
    f = (np.arange(h1)[:, None, None] + n1 * np.arange(n2)[None, :, None])
    t2 = np.arange(n2)[None, None, :]
    ph = np.pi * ((2 * f + 1) * t2 % (2 * n)) / n
    c, s = np.cos(ph), np.sin(ph)
    m2 = np.concatenate([np.concatenate([c, s], axis=2), np.concatenate([-s, c], axis=2)], axis=1)
    m2t = np.transpose(m2, (0, 2, 1))
    m1k = np.kron(m1, np.eye(F32_SUBLANES))
    m3k = np.kron(m3, np.eye(BF16_SUBLANES))
    return n1, n2, m1k, m2, m2t, m3k


@functools.lru_cache(maxsize=None)
def _fnet_consts(seq):
    n1, n2 = _split_seq(seq)
    n1, n2 = n1 // 2, n2 * 2
    f1 = np.arange(n1)[:, None]
    t1 = np.arange(n1)[None, :]
    th = 2 * np.pi * (f1 * t1 % n1) / n1
    m1 = np.empty((2 * n1, n1))
    m1[0::2] = np.cos(th)
    m1[1::2] = -np.sin(th)
    f = (np.arange(n1)[:, None, None] + n1 * np.arange(n2)[None, :, None])
    t2 = np.arange(n2)[None, None, :]
    ph = 2 * np.pi * (f * t2 % seq) / seq
    c, s = np.cos(ph), np.sin(ph)
    m2 = np.concatenate([np.concatenate([c, s], axis=2), np.concatenate([-s, c], axis=2)], axis=1)
    gd = FNET_GROUP_DIM
    cc = np.arange(gd)
    pg = 2 * np.pi * (np.outer(cc, cc) % gd) / gd
    scale = 1.0 / math.sqrt(gd * seq)
    eye = np.eye(D_FNET // gd)
    bdc = np.kron(eye, np.cos(pg)) * scale
    bds = np.kron(eye, np.sin(pg)) * scale
    m1k = np.kron(m1, np.eye(BF16_SUBLANES))
    g = F32_SUBLANES
    perm = np.zeros((g * n1, n1 * g))
    for f2 in range(g):
        for ff in range(n1):
            perm[f2 * n1 + ff, ff * g + f2] = 1.0
    return n1, n2, m1k, m2, perm, bdc, bds


def _mxu_const(a):
    return jnp.asarray(a, dtype=F32).astype(BF16)


def _stage1_into(mk_ref, x_at, a_scr, group):
    nf, _, n2, c = a_scr.shape

    def body(s, carry):
        r0 = pl.multiple_of(s * group, group)
        xs = x_at(r0)
        xs = xs.reshape(xs.shape[0] * group, c).astype(BF16)
        a_scr[:, :, pl.ds(r0, group), :] = _bdot(mk_ref[...], xs).reshape(nf, 2, group, c)
        return carry

    lax.fori_loop(0, n2 // group, body, 0, unroll=True)


def _stage2_rows(a_scr, f1):
    _, _, n2, c = a_scr.shape
    return a_scr[f1].reshape(2 * n2, c).astype(BF16)


def _filter_body(zt_ref, t_ref, w1t_ref, b1_ref, fr_ref, w2t_ref, b2_ref, w3_ref, dl_ref, m1k_ref, m2_ref,
                 kr_ref, ki_ref, h_scr, kt_scr, af_scr, ab_scr):
    o, j = pl.program_id(0), pl.program_id(1)
    fb = m2_ref.shape[0]
    h1, _, n2, c = af_scr.shape
    g = F32_SUBLANES

    @pl.when((o == 0) & (j == 0))
    def _():
        h = jnp.sin(fr_ref[:, 0:1] * (_dot3(w1t_ref[...], zt_ref[...]) + b1_ref[...]))
        h = jnp.sin(fr_ref[:, 1:2] * (_dot3(w2t_ref[...], h) + b2_ref[...]))
        h_scr[...] = h.T

    @pl.when(j == 0)
    def _():
        k = _dot3(h_scr[...], w3_ref[0])
        decay = jnp.exp(-t_ref[...] * jnp.abs(dl_ref[...]))
        kf = k[:, :c] * decay
        row = lax.broadcasted_iota(jnp.int32, kf.shape, 0)
        kb = jnp.where(row == 0, 0.0, k[:, c:] * decay)
        norm = jnp.sum(jnp.abs(kf), axis=0, keepdims=True) + jnp.sum(jnp.abs(kb), axis=0, keepdims=True)
        inv = 1.0 / norm
        kt_scr[0] = (kf * inv).reshape(h1, n2, c)
        kt_scr[1] = (kb * inv).reshape(h1, n2, c)
        _stage1_into(m1k_ref, lambda r0: kt_scr[0, :, pl.ds(r0, g), :], af_scr, g)
        _stage1_into(m1k_ref, lambda r0: kt_scr[1, :, pl.ds(r0, g), :], ab_scr, g)

    for i in range(fb):
        f1 = j * fb + i
        xf = _bdot(m2_ref[i], _stage2_rows(af_scr, f1))
        xb = _bdot(m2_ref[i], _stage2_rows(ab_scr, f1))
        kr_ref[0, i] = xf[:n2] + xb[:n2]
        ki_ref[0, i] = xf[n2:] - xb[n2:]


def _hyena_filter_spectrum(seq, w1, b1, freq, w2, b2, w3, m1k, m2, fb):
    c = D_HYENA
    h1, n2 = m2.shape[0], m2.shape[1] // 2
    t = np.linspace(0.0, 1.0, seq)[:, None]
    w = 2.0 * np.pi * np.arange(seq)[:, None] / seq
    f = np.linspace(1e-4, POS_BANDS - 1, POS_BANDS)[None]
    z = np.concatenate([t, np.cos(f * w), -np.sin(f * w)], axis=-1).astype(np.float32)
    pe = z.shape[1]
    zt = jnp.asarray(np.pad(z, ((0, 0), (0, LANES - pe))).T)
    tcol = jnp.asarray(t.astype(np.float32))
    w1t = jnp.pad(w1, ((0, LANES - pe), (0, 0))).T
    max_decay = math.log(HYENA_DECAY_TARGET) / HYENA_FAST_PCT
    min_decay = math.log(HYENA_DECAY_TARGET) / HYENA_SLOW_PCT
    deltas = jnp.asarray(np.linspace(min_decay, max_decay, c, dtype=np.float32)[None])
    hid = w1.shape[1]
    w3r = jnp.transpose(w3.reshape(hid, 2, 2 * c), (1, 0, 2))
    full = lambda a: pl.BlockSpec(a.shape, lambda o, j: (0,) * a.ndim)
    b1c, b2c, frt = b1.reshape(hid, 1), b2.reshape(hid, 1), freq.T
    out = jax.ShapeDtypeStruct((2, h1, n2, c), F32)
    return pl.pallas_call(
        _filter_body,
        grid=(2, h1 // fb),
        in_specs=[full(zt), full(tcol), full(w1t), full(b1c), full(frt), full(w2.T), full(b2c),
                  pl.BlockSpec((1, hid, 2 * c), lambda o, j: (o, 0, 0)), full(deltas), full(m1k),
                  pl.BlockSpec((fb, 2 * n2, 2 * n2), lambda o, j: (j, 0, 0))],
        out_specs=[pl.BlockSpec((1, fb, n2, c), lambda o, j: (o, j, 0, 0))] * 2,
        out_shape=[out, out],
        scratch_shapes=[pltpu.VMEM((seq, hid), F32), pltpu.VMEM((2, h1, n2, c), F32),
                        pltpu.VMEM((h1, 2, n2, c), F32), pltpu.VMEM((h1, 2, n2, c), F32)],
        compiler_params=_params("arbitrary", "arbitrary"),
        name="hyena_filter_spectrum",
    )(zt, tcol, w1t, b1c, frt, w2.T, b2c, w3r, deltas, m1k, m2)


def _long_conv_body(m1k_ref, m2_ref, m2t_ref, m3k_ref, x_ref, gate_ref, kr_ref, ki_ref, skip_ref, o_ref,
                    a_scr, b_scr):
    h1, _, n2, c = a_scr.shape
    g = F32_SUBLANES
    _stage1_into(m1k_ref, lambda r0: x_ref[0, 0, :, pl.ds(r0, g), :], a_scr, g)

    def per_f1(f1, carry):
        x = _bdot(m2_ref[f1], _stage2_rows(a_scr, f1))
        xr, xi = x[:n2], x[n2:]
        kr, ki = kr_ref[0, f1], ki_ref[0, f1]
        y = jnp.concatenate([kr * xr - ki * xi, kr * xi + ki * xr], axis=0).astype(BF16)
        b_scr[pl.ds(2 * f1, 2)] = _bdot(m2t_ref[f1], y).astype(BF16).reshape(2, n2, c)
        return carry

    lax.fori_loop(0, h1, per_f1, 0, unroll=True)

    g = BF16_SUBLANES
    for s in range(n2 // g):
        rows = slice(s * g, (s + 1) * g)
        bs = b_scr[:, rows, :].reshape(2 * h1 * g, c)
        y = _bdot(m3k_ref[...], bs).reshape(h1, g, c)
        z = gate_ref[0, 0, :, rows, :] * (y + x_ref[0, 0, :, rows, :] * skip_ref[...])
        o_ref[0, 0, :, rows, :] = z.astype(o_ref.dtype)


def _long_conv(consts, x, gates, gate_sel, kr, ki, order, skip_row, out_dtype):
    m1k, m2, m2t, m3k = consts
    b, _, h1, n2, c = x.shape
    resident = lambda a: pl.BlockSpec(a.shape, lambda i: (0,) * a.ndim, pipeline_mode=pl.Buffered(1))
    kspec = pl.BlockSpec((1, h1, n2, c), lambda i: (order, 0, 0, 0), pipeline_mode=pl.Buffered(1))
    tok = lambda sel: pl.BlockSpec((1, 1, h1, n2, c), lambda i: (i, sel, 0, 0, 0))
    return pl.pallas_call(
        _long_conv_body,
        grid=(b,),
        in_specs=[resident(m1k), resident(m2), resident(m2t), resident(m3k), tok(0), tok(gate_sel),
                  kspec, kspec, pl.BlockSpec((1, c), lambda i: (0, 0))],
        out_specs=tok(0),
        out_shape=jax.ShapeDtypeStruct((b, 1, h1, n2, c), out_dtype),
        scratch_shapes=[pltpu.VMEM((h1, 2, n2, c), F32), pltpu.VMEM((2 * h1, n2, c), BF16)],
        compiler_params=_params("parallel"),
        name="hyena_long_conv",
    )(m1k, m2, m2t, m3k, x, gates, kr, ki, skip_row)


def _hyena_mixer(gates, v, w1, b1, freq, w2, b2, w3, skip):
    b, _, seq, c = v.shape
    n1, n2, m1k, m2, m2t, m3k = _hyena_consts(seq)
    h1 = n1 // 2
    m1k, m2, m2t, m3k = (_mxu_const(a) for a in (m1k, m2, m2t, m3k))

    kr, ki = _hyena_filter_spectrum(seq, w1, b1, freq, w2, b2, w3, m1k, m2, min(h1, 8))

    gates = gates.reshape(b, 2, h1, n2, c)
    v = v.reshape(b, 1, h1, n2, c)
    consts = (m1k, m2, m2t, m3k)
    z = _long_conv(consts, v, gates, 0, kr, ki, 0, skip[0:1], F32)
    y = _long_conv(consts, z, gates, 1, kr, ki, 1, skip[1:2], BF16)
    return y.reshape(b, seq, c)


def _fnet_body(m1k_ref, m2_ref, perm_ref, bdc_ref, bds_ref, g_ref, o_ref, a_scr):
    j = pl.program_id(1)
    fb = m2_ref.shape[0]
    n1, _, n2, c = a_scr.shape

    @pl.when(j == 0)
    def _():
        gi = BF16_SUBLANES
        _stage1_into(m1k_ref, lambda r0: g_ref[0, :, pl.ds(r0, gi), :], a_scr, gi)

    for i in range(fb):
        f1 = j * fb + i
        a_scr[f1] = _bdot(m2_ref[i], _stage2_rows(a_scr, f1)).reshape(2, n2, c)

    @pl.when(j == pl.num_programs(1) - 1)
    def _():
        go = F32_SUBLANES

        def body(s, carry):
            r0 = pl.multiple_of(s * go, go)
            re = a_scr[:, 0, pl.ds(r0, go), :].reshape(n1 * go, c).astype(BF16)
            im = a_scr[:, 1, pl.ds(r0, go), :].reshape(n1 * go, c).astype(BF16)
            xr = _bdot(perm_ref[...], re).astype(BF16)
            xi = _bdot(perm_ref[...], im).astype(BF16)
            out = _bdot(xr, bdc_ref[...]) + _bdot(xi, bds_ref[...])
            o_ref[0, pl.ds(pl.multiple_of(s * go * n1, go * n1), go * n1), :] = out.astype(o_ref.dtype)
            return carry

        lax.fori_loop(0, n2 // go, body, 0, unroll=True)


def _fourier_mix(g):
    b, seq, c = g.shape
    n1, n2, m1k, m2, perm, bdc, bds = _fnet_consts(seq)
    m1k, m2, perm, bdc, bds = (_mxu_const(a) for a in (m1k, m2, perm, bdc, bds))
    fb = min(n1, 8)
    const = lambda a: pl.BlockSpec(a.shape, lambda i, j: (0,) * a.ndim)
    return pl.pallas_call(
        _fnet_body,
        grid=(b, n1 // fb),
        in_specs=[const(m1k),
                  pl.BlockSpec((fb, 2 * n2, 2 * n2), lambda i, j: (j, 0, 0)),
                  const(perm), const(bdc), const(bds),
                  pl.BlockSpec((1, n1, n2, c), lambda i, j: (i, 0, 0, 0))],
        out_specs=pl.BlockSpec((1, seq, c), lambda i, j: (i, 0, 0)),
        out_shape=jax.ShapeDtypeStruct((b, seq, c), BF16),
        scratch_shapes=[pltpu.VMEM((n1, 2, n2, c), F32)],
        compiler_params=_params("parallel", "arbitrary"),
        name="fnet",
    )(m1k, m2, perm, bdc, bds, g.reshape(b, n1, n2, c))


def _dot_nt(a, b):
    return lax.dot_general(a, b, (((1,), (1,)), ((), ())), preferred_element_type=F32)


def _head_pair_attention(q2, keys, values, biases):
    nq = q2.shape[0]
    lane = lax.broadcasted_iota(jnp.int32, q2.shape, 1)
    zero = jnp.zeros_like(q2)
    qs = jnp.concatenate([jnp.where(lane < HEAD_DIM, q2, zero), jnp.where(lane >= HEAD_DIM, q2, zero)], axis=0)
    m = denom = acc = None
    for gi, (kk, vv) in enumerate(zip(keys, values)):
        s = _dot_nt(qs, kk)
        if biases[0][gi] is not None:
            s = s + jnp.concatenate([biases[0][gi], biases[1][gi]], axis=0)
        m_grp = s.max(axis=-1, keepdims=True)
        m_new = m_grp if m is None else jnp.maximum(m, m_grp)
        e = jnp.exp2(s - m_new)
        e_part = e[:, :LANES]
        for c0 in range(LANES, e.shape[1], LANES):
            e_part = e_part + e[:, c0:c0 + LANES]
        pv = _bdot(e.astype(BF16), vv)
        if m is None:
            denom, acc = e_part, pv
        else:
            alpha = jnp.exp2(m - m_new)
            denom = alpha * denom + e_part
            acc = alpha * acc + pv
        m = m_new
    out = acc / denom.sum(axis=-1, keepdims=True)
    return jnp.where(lane < HEAD_DIM, out[:nq], out[nq:])


def _na_window_base(row_block, rows):
    return jnp.clip(row_block * NA_QROWS - NA_KH // 2, 0, rows - NA_KROWS)


def _na_body(rows, q_ref, k_ref, v_ref, kc_ref, vc_ref, bias_ref, o_ref):
    base = _na_window_base(pl.program_id(1), rows)
    start = pl.multiple_of(base * GRID_W, GRID_W)
    n_keys = NA_KROWS * GRID_W
    n_pairs = q_ref.shape[2] // LANES
    groups = [(c0, min(NA_KEY_GROUP, n_keys - c0)) for c0 in range(0, n_keys, NA_KEY_GROUP)]
    for p in range(n_pairs):
        sl = slice(p * LANES, (p + 1) * LANES)
        q2 = q_ref[0, :, sl]
        keys = [kc_ref[0, :, sl]] + [k_ref[0, pl.ds(start + c0, n), sl] for c0, n in groups]
        values = [vc_ref[0, :, sl]] + [v_ref[0, pl.ds(start + c0, n), sl] for c0, n in groups]
        biases = [[None] + [bias_ref[0, 2 * p + hh, 0, :, c0:c0 + n] for c0, n in groups] for hh in range(2)]
        o = _head_pair_attention(q2, keys, values, biases)
        o_ref[0, :, sl] = o.astype(o_ref.dtype)


def _na_bias_body(rows, p_ref, o_ref):
    kh, kw, w = NA_KH, NA_KW, GRID_W
    shape = (w, 2 * w)
    qc = lax.broadcasted_iota(jnp.int32, shape, 0)
    lane = lax.broadcasted_iota(jnp.int32, shape, 1)
    kc = lane & (w - 1)
    col_start = jnp.clip(qc - kw // 2, 0, w - kw)
    col_valid = (kc >= col_start) & (kc < col_start + kw)
    low_half = lane < w
    masked = jnp.full(shape, MASK_VALUE, F32)
    tiles = []
    for a in range(2 * kh - 1):
        row = jnp.broadcast_to(p_ref[0, 0, a:a + 1, :], shape)
        t = pltpu.roll(row, w + 1, 1, stride=1, stride_axis=0)
        t = jnp.where(low_half, t, pltpu.roll(t, w, 1))
        tiles.append(jnp.where(col_valid, t, masked))
    for case, r0 in enumerate((0, NA_QROWS, rows - NA_QROWS)):
        base = int(np.clip(r0 - kh // 2, 0, rows - NA_KROWS))
        for qr in range(NA_QROWS):
            q_abs = r0 + qr
            r_start = int(np.clip(q_abs - kh // 2, 0, rows - kh))

            def tile_for(kr):
                k_abs = base + kr
                if kr >= NA_KROWS or not (r_start <= k_abs < r_start + kh):
                    return masked
                return tiles[k_abs - q_abs + kh - 1]

            q_rows = slice(qr * w, (qr + 1) * w)
            for pair in range((NA_KROWS + 1) // 2):
                chunk = jnp.where(low_half, tile_for(2 * pair), tile_for(2 * pair + 1))
                width = min(2 * w, NA_KROWS * w - pair * 2 * w)
                o_ref[0, 0, case, q_rows, pair * 2 * w:pair * 2 * w + width] = chunk[:, :width]


def _na_bias_tables(rpb, rows):
    w = GRID_W
    depth, nh, na, nb = rpb.shape
    assert 2 * w == LANES and nb == 2 * NA_KW - 1
    lo = w - NA_KW
    padded = jnp.pad(rpb.astype(F32) * LOG2E, ((0, 0), (0, 0), (0, (-na) % 8), (lo, 2 * w - nb - lo)))
    return pl.pallas_call(
        functools.partial(_na_bias_body, rows),
        grid=(depth, nh),
        in_specs=[pl.BlockSpec((1, 1) + padded.shape[2:], lambda l, h: (l, h, 0, 0))],
        out_specs=pl.BlockSpec((1, 1, 3, NA_QROWS * w, NA_KROWS * w), lambda l, h: (l, h, 0, 0, 0)),
        out_shape=jax.ShapeDtypeStruct((depth, nh, 3, NA_QROWS * w, NA_KROWS * w), F32),
        compiler_params=_params("parallel", "parallel"),
        name="na_bias_tables",
    )(padded)


def _neighbourhood_attention(q, kv, kv_ctx, bias, layer):
    b, seq, _ = q.shape
    rows = seq // GRID_W
    lc = kv_ctx.shape[1]
    assert rows % NA_QROWS == 0 and rows >= NA_KROWS + 1 and NA_QROWS == NA_KH // 2
    n_blocks = rows // NA_QROWS
    nh = bias.shape[1]
    tq = NA_QROWS * GRID_W

    def bias_map(i, r):
        return (layer, 0, jnp.where(r == 0, 0, jnp.where(r == n_blocks - 1, 2, 1)), 0, 0)

    return pl.pallas_call(
        functools.partial(_na_body, rows),
        grid=(b, n_blocks),
        in_specs=[pl.BlockSpec((1, tq, D_NA), lambda i, r: (i, r, 0)),
                  pl.BlockSpec((1, seq, D_NA), lambda i, r: (i, 0, 0)),
                  pl.BlockSpec((1, seq, D_NA), lambda i, r: (i, 0, 1)),
                  pl.BlockSpec((1, lc, D_NA), lambda i, r: (i, 0, 0)),
                  pl.BlockSpec((1, lc, D_NA), lambda i, r: (i, 0, 1)),
                  pl.BlockSpec((1, nh, 1, tq, NA_KROWS * GRID_W), bias_map)],
        out_specs=pl.BlockSpec((1, tq, D_NA), lambda i, r: (i, r, 0)),
        out_shape=jax.ShapeDtypeStruct((b, seq, D_NA), BF16),
        compiler_params=_params("parallel", "arbitrary"),
        name="neighbourhood_attention",
    )(q, kv, kv, kv_ctx, kv_ctx, bias)


def _ctx_attn_body(q_ref, k_ref, v_ref, o_ref):
    o = _head_pair_attention(q_ref[0], [k_ref[0]], [v_ref[0]], [[None], [None]])
    o_ref[0] = o.astype(o_ref.dtype)


def _context_attention(q, kv):
    b, lc, _ = q.shape
    n_pairs = D_NA // LANES
    return pl.pallas_call(
        _ctx_attn_body,
        grid=(b, n_pairs),
        in_specs=[pl.BlockSpec((1, lc, LANES), lambda i, p: (i, 0, p)),
                  pl.BlockSpec((1, lc, LANES), lambda i, p: (i, 0, p)),
                  pl.BlockSpec((1, lc, LANES), lambda i, p: (i, 0, n_pairs + p))],
        out_specs=pl.BlockSpec((1, lc, LANES), lambda i, p: (i, 0, p)),
        out_shape=jax.ShapeDtypeStruct((b, lc, D_NA), BF16),
        compiler_params=_params("parallel", "parallel"),
        name="context_attention",
    )(q, kv, kv)


def kernel(x, c, ctx, c_ctx, w_mod, b_mod, norm1_g, norm2_g, w_in, w_out, hy_conv_w, hy_conv_b,
           filt_w1, filt_b1, filt_freq, filt_w2, filt_b2, filt_w3, hy_skip, na_rpb,
           w_gate, w_up, w_down, final_g):
    depth = w_mod.shape[0]
    b, seq, d = x.shape
    off_k = 3 * D_HYENA + D_FNET + D_NA
    qk_scale = LOG2E / math.sqrt(HEAD_DIM)
    splits = ((3 * D_HYENA, F32, None), (D_FNET, BF16, None), (D_NA, BF16, qk_scale), (2 * D_NA, BF16, None))
    tm_proj, tm = 1024, 1024

    pad = (-(b + 1)) % 8
    c_all = jnp.concatenate([c, c_ctx[None], jnp.zeros((pad, d), F32)], axis=0)
    mod = _modulation(c_all, w_mod, b_mod).reshape(depth, b + 1 + pad, 6, 1, d)
    lat_row = lambda i: i
    ctx_row = lambda i: b

    w_in_b, w_out_b = w_in.astype(BF16), w_out.astype(BF16)
    w_gate_b, w_up_b, w_down_b = w_gate.astype(BF16), w_up.astype(BF16), w_down.astype(BF16)
    w_kv_last = w_in_b[depth - 1:, :, off_k:]
    n1g, n2g = norm1_g.reshape(depth, 1, d), norm2_g.reshape(depth, 1, d)
    fg = final_g.reshape(1, d)
    na_bias = _na_bias_tables(na_rpb, seq // GRID_W)

    for l in range(depth):
        last = l == depth - 1
        hy_params = (filt_w1[l], filt_b1[l], filt_freq[l], filt_w2[l], filt_b2[l], filt_w3[l], hy_skip[l])
        conv = (hy_conv_w[l], hy_conv_b[l].reshape(1, 3 * D_HYENA))

        gates, v, fn, q, kv = _norm_proj(x, n1g, mod, lat_row, l, w_in_b, l, splits, tm_proj, conv)
        if last:
            (kv_c,) = _norm_proj(ctx.reshape(1, -1, d), n1g, mod, ctx_row, l, w_kv_last, 0,
                                 ((2 * D_NA, BF16, None),), tm_proj)
            kv_c = kv_c.reshape(b, -1, 2 * D_NA)
        else:
            cgates, cv, cfn, cq, kv_c = _norm_proj(ctx, n1g, mod, ctx_row, l, w_in_b, l, splits, tm_proj, conv)

        y_hy = _hyena_mixer(gates, v, *hy_params)
        y_fn = _fourier_mix(fn)
        y_na = _neighbourhood_attention(q, kv, kv_c, na_bias, l)

        x = _mix_ffn(x, y_hy, y_fn, y_na, w_out_b, n2g, mod, lat_row, l, w_gate_b, w_up_b, w_down_b,
                     fg, last, tm)

        if not last:
            cy_hy = _hyena_mixer(cgates, cv, *hy_params)
            cy_fn = _fourier_mix(cfn)
            cy_na = _context_attention(cq, kv_c)
            flat = lambda a: a.reshape(1, -1, a.shape[-1])
            ctx = _mix_ffn(flat(ctx), flat(cy_hy), flat(cy_fn), flat(cy_na), w_out_b, n2g, mod, ctx_row, l,
                           w_gate_b, w_up_b, w_down_b, fg, False, tm).reshape(ctx.shape)
    return x
```

```python
import functools
import math

import numpy as np
import jax
import jax.numpy as jnp
from jax import lax
from jax.experimental import pallas as pl
from jax.experimental.pallas import tpu as pltpu

F32 = jnp.float32
BF16 = jnp.bfloat16

EPS = 1e-6
GRID_W = 64
HEAD_DIM = 64
D_HYENA = 256
D_FNET = 256
D_NA = 512
FNET_GROUP_DIM = 64
NA_KH = 8
NA_KW = 16
NA_QROWS = 4
NA_KROWS = NA_QROWS + NA_KH
NA_KEY_GROUP = 256
POS_BANDS = 16
HYENA_DECAY_TARGET = 1e-2
HYENA_FAST_PCT = 0.3
HYENA_SLOW_PCT = 1.5
MASK_VALUE = -1e30
VMEM_LIMIT = 56 * 1024 * 1024
LONG_CONV_VMEM = 60 * 1024 * 1024
LANES = 128
MXU_WIDTH = 256
LOG2E = math.log2(math.e)
FFN_ROW_SPLIT = 2
F32_SUBLANES = 8
BF16_SUBLANES = 16


def _params(*sem):
    return pltpu.CompilerParams(dimension_semantics=sem, vmem_limit_bytes=VMEM_LIMIT)


def _bdot(a, b):
    return jnp.dot(a, b, preferred_element_type=F32)


def _split_bf16(a):
    hi = a.astype(BF16)
    lo = (a - hi.astype(F32)).astype(BF16)
    return hi, lo


def _dot3(a, b):
    a_hi, a_lo = _split_bf16(a)
    b_hi, b_lo = _split_bf16(b)
    return _bdot(a_hi, b_hi) + _bdot(a_lo, b_hi) + _bdot(a_hi, b_lo)


def _mod_body(c_ref, w_ref, b_ref, o_ref):
    c = c_ref[...]
    rows = c.shape[0]
    s_hi, s_lo = _split_bf16(c * jax.nn.sigmoid(c))
    w_hi, w_lo = _split_bf16(w_ref[0])
    both = _bdot(jnp.concatenate([s_hi, s_lo], axis=0), w_hi)
    o_ref[0] = both[:rows] + both[rows:] + _bdot(s_hi, w_lo) + b_ref[0]


def _modulation(c_all, w_mod, b_mod):
    depth, d, n = w_mod.shape
    rows = c_all.shape[0]
    tn = 1536
    return pl.pallas_call(
        _mod_body,
        grid=(depth, n // tn),
        in_specs=[
            pl.BlockSpec((rows, d), lambda l, j: (0, 0)),
            pl.BlockSpec((1, d, tn), lambda l, j: (l, 0, j)),
            pl.BlockSpec((1, 1, tn), lambda l, j: (l, 0, j)),
        ],
        out_specs=pl.BlockSpec((1, rows, tn), lambda l, j: (l, 0, j)),
        out_shape=jax.ShapeDtypeStruct((depth, rows, n), F32),
        compiler_params=_params("parallel", "parallel"),
        name="modulation",
    )(c_all, w_mod, b_mod.reshape(depth, 1, n))


def _modulated_norm(x, g, sc, sh):
    ms = jnp.mean(x * x, axis=-1, keepdims=True)
    y = x * lax.rsqrt(ms + EPS) * g
    return y * (1.0 + sc) + sh


def _mod_spec(layer, row_of, chunk, d):
    return pl.BlockSpec((1, 1, 1, 1, d), lambda i, *_: (layer, row_of(i), chunk, 0, 0))


def _short_conv_tile(u, prev_row, next_row, cw_ref, cb_ref):
    rows = u.shape[0]
    row = lax.broadcasted_iota(jnp.int32, u.shape, 0)
    prev = jnp.where(row == 0, prev_row, pltpu.roll(u, 1, axis=0))
    nxt = jnp.where(row == rows - 1, next_row, pltpu.roll(u, rows - 1, axis=0))
    return prev * cw_ref[0:1] + u * cw_ref[1:2] + nxt * cw_ref[2:3] + cb_ref[...]


def _norm_proj_body(splits, conv, x_ref, g_ref, sc_ref, sh_ref, w_ref, *rest):
    o_refs = ((rest[4:6],) + rest[6:]) if conv else rest
    g, sc, sh = g_ref[0], sc_ref[0, 0, 0], sh_ref[0, 0, 0]
    tm = x_ref.shape[1]
    edge = F32_SUBLANES if conv else 0
    if conv:
        xp_ref, xn_ref, cw_ref, cb_ref = rest[:4]
        x_all = jnp.concatenate([xp_ref[0], x_ref[0], xn_ref[0]], axis=0)
    else:
        x_all = x_ref[0]
    p_all = _bdot(_modulated_norm(x_all, g, sc, sh).astype(BF16), w_ref[0])
    p = p_all[edge:edge + tm]
    off = 0
    for idx, (o_ref, (width, _, scale)) in enumerate(zip(o_refs, splits)):
        part = p[:, off:off + width]
        if conv and idx == 0:
            gate_ref, v_ref = o_ref
            j, nj = pl.program_id(1), pl.num_programs(1)
            prev_row = jnp.where(j == 0, 0.0, p_all[edge - 1:edge, off:off + width])
            next_row = jnp.where(j == nj - 1, 0.0, p_all[edge + tm:edge + tm + 1, off:off + width])
            u = _short_conv_tile(part, prev_row, next_row, cw_ref, cb_ref)
            c = v_ref.shape[3]
            n_gates = gate_ref.shape[1]
            for gi in range(n_gates):
                gate_ref[0, gi] = u[:, gi * c:(gi + 1) * c].astype(gate_ref.dtype)
            v_ref[0, 0] = u[:, n_gates * c:]
        else:
            o_ref[0] = (part if scale is None else part * scale).astype(o_ref.dtype)
        off += width


def _norm_proj(x, norm_g, mod, row_of, layer, w, w_idx, splits, tm, conv=None):
    b, t, d = x.shape
    n = w.shape[2]
    tm = min(tm, t)
    edge = F32_SUBLANES
    tok = lambda wd: pl.BlockSpec((1, tm, wd), lambda i, j: (i, j, 0))
    in_specs = [
        tok(d),
        pl.BlockSpec((1, 1, d), lambda i, j: (layer, 0, 0)),
        _mod_spec(layer, row_of, 1, d),
        _mod_spec(layer, row_of, 0, d),
        pl.BlockSpec((1, d, n), lambda i, j: (w_idx, 0, 0)),
    ]
    args = [x, norm_g, mod, mod, w]
    out_specs = [tok(wd) for wd, _, _ in splits]
    out_shape = [jax.ShapeDtypeStruct((b, t, wd), dt) for wd, dt, _ in splits]
    if conv is not None:
        cw, cb = conv
        w0, c = splits[0][0], D_HYENA
        per_tile = tm // edge
        in_specs += [
            pl.BlockSpec((1, edge, d), lambda i, j: (i, jnp.maximum(j * per_tile - 1, 0), 0)),
            pl.BlockSpec((1, edge, d), lambda i, j: (i, jnp.minimum((j + 1) * per_tile, t // edge - 1), 0)),
            pl.BlockSpec(cw.shape, lambda i, j: (0, 0)),
            pl.BlockSpec(cb.shape, lambda i, j: (0, 0)),
        ]
        args += [x, x, cw, cb]
        n_gates = w0 // c - 1
        out_specs[0:1] = [pl.BlockSpec((1, n_gates, tm, c), lambda i, j: (i, 0, j, 0)),
                          pl.BlockSpec((1, 1, tm, c), lambda i, j: (i, 0, j, 0))]
        out_shape[0:1] = [jax.ShapeDtypeStruct((b, n_gates, t, c), BF16),
                          jax.ShapeDtypeStruct((b, 1, t, c), splits[0][1])]
    return pl.pallas_call(
        functools.partial(_norm_proj_body, tuple(splits), conv is not None),
        grid=(b, t // tm),
        in_specs=in_specs,
        out_specs=out_specs,
        out_shape=out_shape,
        compiler_params=_params("parallel", "parallel"),
        name="norm_proj",
    )(*args)


def _mix_ffn_body(final_norm, x_ref, hy_ref, fn_ref, na_ref, wo_ref, g1_ref, ng_ref, sc_ref, sh_ref, g2_ref,
                  wg_ref, wu_ref, wd_ref, fg_ref, o_ref):
    tm = x_ref.shape[1]
    sub = tm // FFN_ROW_SPLIT if tm % (FFN_ROW_SPLIT * MXU_WIDTH) == 0 else tm
    halves = [slice(r0, r0 + sub) for r0 in range(0, tm, sub)]
    x1s, hs = [], []
    for rows in halves:
        mix = (_bdot(hy_ref[0, rows], wo_ref[0, :D_HYENA])
               + _bdot(fn_ref[0, rows], wo_ref[0, D_HYENA:D_HYENA + D_FNET])
               + _bdot(na_ref[0, rows], wo_ref[0, D_HYENA + D_FNET:]))
        x1 = x_ref[0, rows] + g1_ref[0, 0, 0] * mix
        x1s.append(x1)
        hs.append(_modulated_norm(x1, ng_ref[0], sc_ref[0, 0, 0], sh_ref[0, 0, 0]).astype(BF16))
    accs = [None] * len(halves)
    for c0 in range(0, wg_ref.shape[2], MXU_WIDTH):
        cols = slice(c0, c0 + MXU_WIDTH)
        for s, h in enumerate(hs):
            a = _bdot(h, wg_ref[0, :, cols])
            u = _bdot(h, wu_ref[0, :, cols])
            act = (a * jax.nn.sigmoid(a) * u).astype(BF16)
            part = _bdot(act, wd_ref[0, cols, :])
            accs[s] = part if accs[s] is None else accs[s] + part
    for rows, x1, acc in zip(halves, x1s, accs):
        y = x1 + g2_ref[0, 0, 0] * acc
        if final_norm:
            ms = jnp.mean(y * y, axis=-1, keepdims=True)
            y = y * lax.rsqrt(ms + EPS) * fg_ref[...]
        o_ref[0, rows] = y


def _mix_ffn(x, y_hy, y_fn, y_na, w_out, norm_g, mod, row_of, layer, wg, wu, wd, final_g, final_norm, tm):
    b, t, d = x.shape
    dff = wg.shape[2]
    assert dff % MXU_WIDTH == 0
    tm = min(tm, t)
    tok = lambda wd_: pl.BlockSpec((1, tm, wd_), lambda i, j: (i, j, 0))
    resident = lambda shape: pl.BlockSpec((1,) + shape, lambda i, j: (layer, 0, 0), pipeline_mode=pl.Buffered(1))
    return pl.pallas_call(
        functools.partial(_mix_ffn_body, final_norm),
        grid=(b, t // tm),
        in_specs=[
            tok(d), tok(D_HYENA), tok(D_FNET), tok(D_NA),
            resident((d, d)),
            _mod_spec(layer, row_of, 2, d),
            pl.BlockSpec((1, 1, d), lambda i, j: (layer, 0, 0)),
            _mod_spec(layer, row_of, 4, d),
            _mod_spec(layer, row_of, 3, d),
            _mod_spec(layer, row_of, 5, d),
            resident((d, dff)), resident((d, dff)), resident((dff, d)),
            pl.BlockSpec((1, d), lambda i, j: (0, 0)),
        ],
        out_specs=tok(d),
        out_shape=jax.ShapeDtypeStruct(x.shape, F32),
        compiler_params=_params("parallel", "parallel"),
        name="mix_ffn",
    )(x, y_hy, y_fn, y_na, w_out, mod, norm_g, mod, mod, mod, wg, wu, wd, final_g)


def _split_seq(n_total):
    log = int(round(math.log2(n_total)))
    assert 2 ** log == n_total
    n1 = 2 ** (log // 2)
    return n1, n_total // n1


@functools.lru_cache(maxsize=None)
def _hyena_consts(seq):
    n = 2 * seq
    n1, n2 = _split_seq(n)
    h1 = n1 // 2
    f1 = np.arange(h1)[:, None]
    t1 = np.arange(h1)[None, :]
    th = np.pi * (2 * f1 + 1) * t1 / n1
    m1 = np.empty((n1, h1))
    m1[0::2] = np.cos(th)
    m1[1::2] = -np.sin(th)
    m3 = m1.T * (2.0 / n)
    f = (np.arange(h1)[:, None, None] + n1 * np.arange(n2)[None, :, None])
    t2 = np.arange(n2)[None, None, :]
    ph = np.pi * ((2 * f + 1) * t2 % (2 * n)) / n
    c, s = np.cos(ph), np.sin(ph)
    m2 = np.concatenate([np.concatenate([c, s], axis=2), np.concatenate([-s, c], axis=2)], axis=1)
    m2t = np.transpose(m2, (0, 2, 1))
    m1k = np.kron(m1, np.eye(F32_SUBLANES))
    m3k = np.kron(m3, np.eye(BF16_SUBLANES))
    return n1, n2, m1k, m2, m2t, m3k


@functools.lru_cache(maxsize=None)
def _fnet_consts(seq):
    n1, n2 = _split_seq(seq)
    n1, n2 = n1 // 2, n2 * 2
    f1 = np.arange(n1)[:, None]
    t1 = np.arange(n1)[None, :]
    th = 2 * np.pi * (f1 * t1 % n1) / n1
    m1 = np.empty((2 * n1, n1))
    m1[0::2] = np.cos(th)
    m1[1::2] = -np.sin(th)
    f = (np.arange(n1)[:, None, None] + n1 * np.arange(n2)[None, :, None])
    t2 = np.arange(n2)[None, None, :]
    ph = 2 * np.pi * (f * t2 % seq) / seq
    c, s = np.cos(ph), np.sin(ph)
    m2 = np.concatenate([np.concatenate([c, s], axis=2), np.concatenate([-s, c], axis=2)], axis=1)
    gd = FNET_GROUP_DIM
    cc = np.arange(gd)
    pg = 2 * np.pi * (np.outer(cc, cc) % gd) / gd
    scale = 1.0 / math.sqrt(gd * seq)
    eye = np.eye(D_FNET // gd)
    bdc = np.kron(eye, np.cos(pg)) * scale
    bds = np.kron(eye, np.sin(pg)) * scale
    m1k = np.kron(m1, np.eye(BF16_SUBLANES))
    g = F32_SUBLANES
    perm = np.zeros((g * n1, n1 * g))
    for f2 in range(g):
        for ff in range(n1):
            perm[f2 * n1 + ff, ff * g + f2] = 1.0
    return n1, n2, m1k, m2, perm, bdc, bds


def _mxu_const(a):
    return jnp.asarray(a, dtype=F32).astype(BF16)


def _stage1_into(mk_ref, x_at, a_scr, group):
    nf, _, n2, c = a_scr.shape

    def body(s, carry):
        r0 = pl.multiple_of(s * group, group)
        xs = x_at(r0)
        xs = xs.reshape(xs.shape[0] * group, c).astype(BF16)
        a_scr[:, :, pl.ds(r0, group), :] = _bdot(mk_ref[...], xs).reshape(nf, 2, group, c)
        return carry

    lax.fori_loop(0, n2 // group, body, 0, unroll=True)


def _stage2_rows(a_scr, f1):
    _, _, n2, c = a_scr.shape
    return a_scr[f1].reshape(2 * n2, c).astype(BF16)


def _filter_body(zt_ref, t_ref, w1t_ref, b1_ref, fr_ref, w2t_ref, b2_ref, w3_ref, dl_ref, m1k_ref, m2_ref,
                 kr_ref, ki_ref, h_scr, kt_scr, af_scr, ab_scr):
    o, j = pl.program_id(0), pl.program_id(1)
    fb = m2_ref.shape[0]
    h1, _, n2, c = af_scr.shape
    g = F32_SUBLANES

    @pl.when((o == 0) & (j == 0))
    def _():
        h = jnp.sin(fr_ref[:, 0:1] * (_dot3(w1t_ref[...], zt_ref[...]) + b1_ref[...]))
        h = jnp.sin(fr_ref[:, 1:2] * (_dot3(w2t_ref[...], h) + b2_ref[...]))
        h_scr[...] = h.T

    @pl.when(j == 0)
    def _():
        k = _dot3(h_scr[...], w3_ref[0])
        decay = jnp.exp(-t_ref[...] * jnp.abs(dl_ref[...]))
        kf = k[:, :c] * decay
        row = lax.broadcasted_iota(jnp.int32, kf.shape, 0)
        kb = jnp.where(row == 0, 0.0, k[:, c:] * decay)
        norm = jnp.sum(jnp.abs(kf), axis=0, keepdims=True) + jnp.sum(jnp.abs(kb), axis=0, keepdims=True)
        inv = 1.0 / norm
        kt_scr[0] = (kf * inv).reshape(h1, n2, c)
        kt_scr[1] = (kb * inv).reshape(h1, n2, c)
        _stage1_into(m1k_ref, lambda r0: kt_scr[0, :, pl.ds(r0, g), :], af_scr, g)
        _stage1_into(m1k_ref, lambda r0: kt_scr[1, :, pl.ds(r0, g), :], ab_scr, g)

    for i in range(fb):
        f1 = j * fb + i
        xf = _bdot(m2_ref[i], _stage2_rows(af_scr, f1))
        xb = _bdot(m2_ref[i], _stage2_rows(ab_scr, f1))
        kr_ref[0, i] = (xf[:n2] + xb[:n2]).astype(kr_ref.dtype)
        ki_ref[0, i] = (xf[n2:] - xb[n2:]).astype(ki_ref.dtype)


def _hyena_filter_spectrum(seq, w1, b1, freq, w2, b2, w3, m1k, m2, fb):
    c = D_HYENA
    h1, n2 = m2.shape[0], m2.shape[1] // 2
    t = np.linspace(0.0, 1.0, seq)[:, None]
    w = 2.0 * np.pi * np.arange(seq)[:, None] / seq
    f = np.linspace(1e-4, POS_BANDS - 1, POS_BANDS)[None]
    z = np.concatenate([t, np.cos(f * w), -np.sin(f * w)], axis=-1).astype(np.float32)
    pe = z.shape[1]
    zt = jnp.asarray(np.pad(z, ((0, 0), (0, LANES - pe))).T)
    tcol = jnp.asarray(t.astype(np.float32))
    w1t = jnp.pad(w1, ((0, LANES - pe), (0, 0))).T
    max_decay = math.log(HYENA_DECAY_TARGET) / HYENA_FAST_PCT
    min_decay = math.log(HYENA_DECAY_TARGET) / HYENA_SLOW_PCT
    deltas = jnp.asarray(np.linspace(min_decay, max_decay, c, dtype=np.float32)[None])
    hid = w1.shape[1]
    w3r = jnp.transpose(w3.reshape(hid, 2, 2 * c), (1, 0, 2))
    full = lambda a: pl.BlockSpec(a.shape, lambda o, j: (0,) * a.ndim)
    b1c, b2c, frt = b1.reshape(hid, 1), b2.reshape(hid, 1), freq.T
    out = jax.ShapeDtypeStruct((2, h1, n2, c), BF16)
    return pl.pallas_call(
        _filter_body,
        grid=(2, h1 // fb),
        in_specs=[full(zt), full(tcol), full(w1t), full(b1c), full(frt), full(w2.T), full(b2c),
                  pl.BlockSpec((1, hid, 2 * c), lambda o, j: (o, 0, 0)), full(deltas), full(m1k),
                  pl.BlockSpec((fb, 2 * n2, 2 * n2), lambda o, j: (j, 0, 0))],
        out_specs=[pl.BlockSpec((1, fb, n2, c), lambda o, j: (o, j, 0, 0))] * 2,
        out_shape=[out, out],
        scratch_shapes=[pltpu.VMEM((seq, hid), F32), pltpu.VMEM((2, h1, n2, c), F32),
                        pltpu.VMEM((h1, 2, n2, c), F32), pltpu.VMEM((h1, 2, n2, c), F32)],
        compiler_params=_params("arbitrary", "arbitrary"),
        name="hyena_filter_spectrum",
    )(zt, tcol, w1t, b1c, frt, w2.T, b2c, w3r, deltas, m1k, m2)


def _long_conv_body(m1k_ref, m2_ref, m2t_ref, m3k_ref, v_ref, gate_ref, kr_ref, ki_ref, skip_ref, o_ref,
                    a_scr, b_scr, z_scr):
    h1, _, n2, c = a_scr.shape

    def gated_conv(order, src, store):
        g = F32_SUBLANES
        _stage1_into(m1k_ref, lambda r0: src(pl.ds(r0, g)), a_scr, g)

        def per_f1(f1, carry):
            x = _bdot(m2_ref[f1], _stage2_rows(a_scr, f1))
            xr, xi = x[:n2], x[n2:]
            kr, ki = kr_ref[order, f1].astype(F32), ki_ref[order, f1].astype(F32)
            y = jnp.concatenate([kr * xr - ki * xi, kr * xi + ki * xr], axis=0).astype(BF16)
            b_scr[pl.ds(2 * f1, 2)] = _bdot(m2t_ref[f1], y).astype(BF16).reshape(2, n2, c)
            return carry

        lax.fori_loop(0, h1, per_f1, 0, unroll=True)

        g = BF16_SUBLANES
        for s in range(n2 // g):
            rows = slice(s * g, (s + 1) * g)
            bs = b_scr[:, rows, :].reshape(2 * h1 * g, c)
            y = _bdot(m3k_ref[...], bs).reshape(h1, g, c)
            store(rows, gate_ref[0, order, :, rows, :] * (y + src(rows) * skip_ref[order:order + 1]))

    def store_z(rows, z):
        z_scr[:, rows, :] = z

    def store_out(rows, y):
        o_ref[0, :, rows, :] = y.astype(o_ref.dtype)

    gated_conv(0, lambda rows: v_ref[0, 0, :, rows, :], store_z)
    gated_conv(1, lambda rows: z_scr[:, rows, :], store_out)


def _hyena_mixer(gates, v, w1, b1, freq, w2, b2, w3, skip):
    b, _, seq, c = v.shape
    n1, n2, m1k, m2, m2t, m3k = _hyena_consts(seq)
    h1 = n1 // 2
    m1k, m2, m2t, m3k = (_mxu_const(a) for a in (m1k, m2, m2t, m3k))

    kr, ki = _hyena_filter_spectrum(seq, w1, b1, freq, w2, b2, w3, m1k, m2, min(h1, 8))

    resident = lambda a: pl.BlockSpec(a.shape, lambda i: (0,) * a.ndim, pipeline_mode=pl.Buffered(1))
    y = pl.pallas_call(
        _long_conv_body,
        grid=(b,),
        in_specs=[resident(m1k), resident(m2), resident(m2t), resident(m3k),
                  pl.BlockSpec((1, 1, h1, n2, c), lambda i: (i, 0, 0, 0, 0)),
                  pl.BlockSpec((1, 2, h1, n2, c), lambda i: (i, 0, 0, 0, 0)),
                  resident(kr), resident(ki), resident(skip)],
        out_specs=pl.BlockSpec((1, h1, n2, c), lambda i: (i, 0, 0, 0)),
        out_shape=jax.ShapeDtypeStruct((b, h1, n2, c), BF16),
        scratch_shapes=[pltpu.VMEM((h1, 2, n2, c), F32), pltpu.VMEM((2 * h1, n2, c), BF16),
                        pltpu.VMEM((h1, n2, c), F32)],
        compiler_params=pltpu.CompilerParams(dimension_semantics=("parallel",), vmem_limit_bytes=LONG_CONV_VMEM),
        name="hyena_long_conv",
    )(m1k, m2, m2t, m3k, v.reshape(b, 1, h1, n2, c), gates.reshape(b, 2, h1, n2, c), kr, ki, skip)
    return y.reshape(b, seq, c)


def _fnet_body(m1k_ref, m2_ref, perm_ref, bdc_ref, bds_ref, g_ref, o_ref, a_scr):
    j = pl.program_id(1)
    fb = m2_ref.shape[0]
    n1, _, n2, c = a_scr.shape

    @pl.when(j == 0)
    def _():
        gi = BF16_SUBLANES
        _stage1_into(m1k_ref, lambda r0: g_ref[0, :, pl.ds(r0, gi), :], a_scr, gi)

    for i in range(fb):
        f1 = j * fb + i
        a_scr[f1] = _bdot(m2_ref[i], _stage2_rows(a_scr, f1)).reshape(2, n2, c)

    @pl.when(j == pl.num_programs(1) - 1)
    def _():
        go = F32_SUBLANES

        def body(s, carry):
            r0 = pl.multiple_of(s * go, go)
            re = a_scr[:, 0, pl.ds(r0, go), :].reshape(n1 * go, c).astype(BF16)
            im = a_scr[:, 1, pl.ds(r0, go), :].reshape(n1 * go, c).astype(BF16)
            xr = _bdot(perm_ref[...], re).astype(BF16)
            xi = _bdot(perm_ref[...], im).astype(BF16)
            out = _bdot(xr, bdc_ref[...]) + _bdot(xi, bds_ref[...])
            o_ref[0, pl.ds(pl.multiple_of(s * go * n1, go * n1), go * n1), :] = out.astype(o_ref.dtype)
            return carry

        lax.fori_loop(0, n2 // go, body, 0, unroll=True)


def _fourier_mix(g):
    b, seq, c = g.shape
    n1, n2, m1k, m2, perm, bdc, bds = _fnet_consts(seq)
    m1k, m2, perm, bdc, bds = (_mxu_const(a) for a in (m1k, m2, perm, bdc, bds))
    fb = min(n1, 8)
    const = lambda a: pl.BlockSpec(a.shape, lambda i, j: (0,) * a.ndim)
    return pl.pallas_call(
        _fnet_body,
        grid=(b, n1 // fb),
        in_specs=[const(m1k),
                  pl.BlockSpec((fb, 2 * n2, 2 * n2), lambda i, j: (j, 0, 0)),
                  const(perm), const(bdc), const(bds),
                  pl.BlockSpec((1, n1, n2, c), lambda i, j: (i, 0, 0, 0))],
        out_specs=pl.BlockSpec((1, seq, c), lambda i, j: (i, 0, 0)),
        out_shape=jax.ShapeDtypeStruct((b, seq, c), BF16),
        scratch_shapes=[pltpu.VMEM((n1, 2, n2, c), F32)],
        compiler_params=_params("parallel", "arbitrary"),
        name="fnet",
    )(m1k, m2, perm, bdc, bds, g.reshape(b, n1, n2, c))


def _dot_nt(a, b):
    return lax.dot_general(a, b, (((1,), (1,)), ((), ())), preferred_element_type=F32)


def _head_pair_attention(q2, keys, values, biases):
    nq = q2.shape[0]
    lane = lax.broadcasted_iota(jnp.int32, q2.shape, 1)
    zero = jnp.zeros_like(q2)
    qs = jnp.concatenate([jnp.where(lane < HEAD_DIM, q2, zero), jnp.where(lane >= HEAD_DIM, q2, zero)], axis=0)
    m = denom = acc = None
    for gi, (kk, vv) in enumerate(zip(keys, values)):
        s = _dot_nt(qs, kk)
        if biases[0][gi] is not None:
            s = s + jnp.concatenate([biases[0][gi], biases[1][gi]], axis=0)
        m_grp = s.max(axis=-1, keepdims=True)
        m_new = m_grp if m is None else jnp.maximum(m, m_grp)
        e = jnp.exp2(s - m_new)
        e_part = e[:, :LANES]
        for c0 in range(LANES, e.shape[1], LANES):
            e_part = e_part + e[:, c0:c0 + LANES]
        pv = _bdot(e.astype(BF16), vv)
        if m is None:
            denom, acc = e_part, pv
        else:
            alpha = jnp.exp2(m - m_new)
            denom = alpha * denom + e_part
            acc = alpha * acc + pv
        m = m_new
    out = acc / denom.sum(axis=-1, keepdims=True)
    return jnp.where(lane < HEAD_DIM, out[:nq], out[nq:])


def _na_window_base(row_block, rows):
    return jnp.clip(row_block * NA_QROWS - NA_KH // 2, 0, rows - NA_KROWS)


def _na_body(rows, q_ref, k_ref, v_ref, kc_ref, vc_ref, bias_ref, o_ref):
    base = _na_window_base(pl.program_id(1), rows)
    start = pl.multiple_of(base * GRID_W, GRID_W)
    n_keys = NA_KROWS * GRID_W
    n_pairs = q_ref.shape[2] // LANES
    groups = [(c0, min(NA_KEY_GROUP, n_keys - c0)) for c0 in range(0, n_keys, NA_KEY_GROUP)]
    for p in range(n_pairs):
        sl = slice(p * LANES, (p + 1) * LANES)
        q2 = q_ref[0, :, sl]
        keys = [kc_ref[0, :, sl]] + [k_ref[0, pl.ds(start + c0, n), sl] for c0, n in groups]
        values = [vc_ref[0, :, sl]] + [v_ref[0, pl.ds(start + c0, n), sl] for c0, n in groups]
        biases = [[None] + [bias_ref[0, 2 * p + hh, 0, :, c0:c0 + n] for c0, n in groups] for hh in range(2)]
        o = _head_pair_attention(q2, keys, values, biases)
        o_ref[0, :, sl] = o.astype(o_ref.dtype)


def _na_bias_body(rows, p_ref, o_ref):
    kh, kw, w = NA_KH, NA_KW, GRID_W
    shape = (w, 2 * w)
    qc = lax.broadcasted_iota(jnp.int32, shape, 0)
    lane = lax.broadcasted_iota(jnp.int32, shape, 1)
    kc = lane & (w - 1)
    col_start = jnp.clip(qc - kw // 2, 0, w - kw)
    col_valid = (kc >= col_start) & (kc < col_start + kw)
    low_half = lane < w
    masked = jnp.full(shape, MASK_VALUE, F32)
    tiles = []
    for a in range(2 * kh - 1):
        row = jnp.broadcast_to(p_ref[0, 0, a:a + 1, :], shape)
        t = pltpu.roll(row, w + 1, 1, stride=1, stride_axis=0)
        t = jnp.where(low_half, t, pltpu.roll(t, w, 1))
        tiles.append(jnp.where(col_valid, t, masked))
    for case, r0 in enumerate((0, NA_QROWS, rows - NA_QROWS)):
        base = int(np.clip(r0 - kh // 2, 0, rows - NA_KROWS))
        for qr in range(NA_QROWS):
            q_abs = r0 + qr
            r_start = int(np.clip(q_abs - kh // 2, 0, rows - kh))

            def tile_for(kr):
                k_abs = base + kr
                if kr >= NA_KROWS or not (r_start <= k_abs < r_start + kh):
                    return masked
                return tiles[k_abs - q_abs + kh - 1]

            q_rows = slice(qr * w, (qr + 1) * w)
            for pair in range((NA_KROWS + 1) // 2):
                chunk = jnp.where(low_half, tile_for(2 * pair), tile_for(2 * pair + 1))
                width = min(2 * w, NA_KROWS * w - pair * 2 * w)
                o_ref[0, 0, case, q_rows, pair * 2 * w:pair * 2 * w + width] = chunk[:, :width]


def _na_bias_tables(rpb, rows):
    w = GRID_W
    depth, nh, na, nb = rpb.shape
    assert 2 * w == LANES and nb == 2 * NA_KW - 1
    lo = w - NA_KW
    padded = jnp.pad(rpb.astype(F32) * LOG2E, ((0, 0), (0, 0), (0, (-na) % 8), (lo, 2 * w - nb - lo)))
    return pl.pallas_call(
        functools.partial(_na_bias_body, rows),
        grid=(depth, nh),
        in_specs=[pl.BlockSpec((1, 1) + padded.shape[2:], lambda l, h: (l, h, 0, 0))],
        out_specs=pl.BlockSpec((1, 1, 3, NA_QROWS * w, NA_KROWS * w), lambda l, h: (l, h, 0, 0, 0)),
        out_shape=jax.ShapeDtypeStruct((depth, nh, 3, NA_QROWS * w, NA_KROWS * w), F32),
        compiler_params=_params("parallel", "parallel"),
        name="na_bias_tables",
    )(padded)


def _neighbourhood_attention(q, kv, kv_ctx, bias, layer):
    b, seq, _ = q.shape
    rows = seq // GRID_W
    lc = kv_ctx.shape[1]
    assert rows % NA_QROWS == 0 and rows >= NA_KROWS + 1 and NA_QROWS == NA_KH // 2
    n_blocks = rows // NA_QROWS
    nh = bias.shape[1]
    tq = NA_QROWS * GRID_W

    def bias_map(i, r):
        return (layer, 0, jnp.where(r == 0, 0, jnp.where(r == n_blocks - 1, 2, 1)), 0, 0)

    return pl.pallas_call(
        functools.partial(_na_body, rows),
        grid=(b, n_blocks),
        in_specs=[pl.BlockSpec((1, tq, D_NA), lambda i, r: (i, r, 0)),
                  pl.BlockSpec((1, seq, D_NA), lambda i, r: (i, 0, 0)),
                  pl.BlockSpec((1, seq, D_NA), lambda i, r: (i, 0, 1)),
                  pl.BlockSpec((1, lc, D_NA), lambda i, r: (i, 0, 0)),
                  pl.BlockSpec((1, lc, D_NA), lambda i, r: (i, 0, 1)),
                  pl.BlockSpec((1, nh, 1, tq, NA_KROWS * GRID_W), bias_map)],
        out_specs=pl.BlockSpec((1, tq, D_NA), lambda i, r: (i, r, 0)),
        out_shape=jax.ShapeDtypeStruct((b, seq, D_NA), BF16),
        compiler_params=_params("parallel", "arbitrary"),
        name="neighbourhood_attention",
    )(q, kv, kv, kv_ctx, kv_ctx, bias)


def _ctx_attn_body(q_ref, k_ref, v_ref, o_ref):
    o = _head_pair_attention(q_ref[0], [k_ref[0]], [v_ref[0]], [[None], [None]])
    o_ref[0] = o.astype(o_ref.dtype)


def _context_attention(q, kv):
    b, lc, _ = q.shape
    n_pairs = D_NA // LANES
    return pl.pallas_call(
        _ctx_attn_body,
        grid=(b, n_pairs),
        in_specs=[pl.BlockSpec((1, lc, LANES), lambda i, p: (i, 0, p)),
                  pl.BlockSpec((1, lc, LANES), lambda i, p: (i, 0, p)),
                  pl.BlockSpec((1, lc, LANES), lambda i, p: (i, 0, n_pairs + p))],
        out_specs=pl.BlockSpec((1, lc, LANES), lambda i, p: (i, 0, p)),
        out_shape=jax.ShapeDtypeStruct((b, lc, D_NA), BF16),
        compiler_params=_params("parallel", "parallel"),
        name="context_attention",
    )(q, kv, kv)


def kernel(x, c, ctx, c_ctx, w_mod, b_mod, norm1_g, norm2_g, w_in, w_out, hy_conv_w, hy_conv_b,
           filt_w1, filt_b1, filt_freq, filt_w2, filt_b2, filt_w3, hy_skip, na_rpb,
           w_gate, w_up, w_down, final_g):
    depth = w_mod.shape[0]
    b, seq, d = x.shape
    off_k = 3 * D_HYENA + D_FNET + D_NA
    qk_scale = LOG2E / math.sqrt(HEAD_DIM)
    splits = ((3 * D_HYENA, F32, None), (D_FNET, BF16, None), (D_NA, BF16, qk_scale), (2 * D_NA, BF16, None))
    tm_proj, tm = 1024, 1024

    pad = (-(b + 1)) % 8
    c_all = jnp.concatenate([c, c_ctx[None], jnp.zeros((pad, d), F32)], axis=0)
    mod = _modulation(c_all, w_mod, b_mod).reshape(depth, b + 1 + pad, 6, 1, d)
    lat_row = lambda i: i
    ctx_row = lambda i: b

    w_in_b, w_out_b = w_in.astype(BF16), w_out.astype(BF16)
    w_gate_b, w_up_b, w_down_b = w_gate.astype(BF16), w_up.astype(BF16), w_down.astype(BF16)
    w_kv_last = w_in_b[depth - 1:, :, off_k:]
    n1g, n2g = norm1_g.reshape(depth, 1, d), norm2_g.reshape(depth, 1, d)
    fg = final_g.reshape(1, d)
    na_bias = _na_bias_tables(na_rpb, seq // GRID_W)

    for l in range(depth):
        last = l == depth - 1
        hy_params = (filt_w1[l], filt_b1[l], filt_freq[l], filt_w2[l], filt_b2[l], filt_w3[l], hy_skip[l])
        conv = (hy_conv_w[l], hy_conv_b[l].reshape(1, 3 * D_HYENA))

        gates, v, fn, q, kv = _norm_proj(x, n1g, mod, lat_row, l, w_in_b, l, splits, tm_proj, conv)
        if last:
            (kv_c,) = _norm_proj(ctx.reshape(1, -1, d), n1g, mod, ctx_row, l, w_kv_last, 0,
                                 ((2 * D_NA, BF16, None),), tm_proj)
            kv_c = kv_c.reshape(b, -1, 2 * D_NA)
        else:
            cgates, cv, cfn, cq, kv_c = _norm_proj(ctx, n1g, mod, ctx_row, l, w_in_b, l, splits, tm_proj, conv)

        y_hy = _hyena_mixer(gates, v, *hy_params)
        y_fn = _fourier_mix(fn)
        y_na = _neighbourhood_attention(q, kv, kv_c, na_bias, l)

        x = _mix_ffn(x, y_hy, y_fn, y_na, w_out_b, n2g, mod, lat_row, l, w_gate_b, w_up_b, w_down_b,
                     fg, last, tm)

        if not last:
            cy_hy = _hyena_mixer(cgates, cv, *hy_params)
            cy_fn = _fourier_mix(cfn)
            cy_na = _context_attention(cq, kv_c)
            flat = lambda a: a.reshape(1, -1, a.shape[-1])
            ctx = _mix_ffn(flat(ctx), flat(cy_hy), flat(cy_fn), flat(cy_na), w_out_b, n2g, mod, ctx_row, l,
                           w_gate_b, w_up_b, w_down_b, fg, False, tm).reshape(ctx.shape)
    return x
```

```python
import functools
import math

import numpy as np
import jax
import jax.numpy as jnp
from jax import lax
from jax.experimental import pallas as pl
from jax.experimental.pallas import tpu as pltpu

F32 = jnp.float32
BF16 = jnp.bfloat16

EPS = 1e-6
GRID_W = 64
HEAD_DIM = 64
D_HYENA = 256
D_FNET = 256
D_NA = 512
FNET_GROUP_DIM = 64
NA_KH = 8
NA_KW = 16
NA_QROWS = 4
NA_KROWS = NA_QROWS + NA_KH
NA_KEY_GROUP = 256
POS_BANDS = 16
HYENA_DECAY_TARGET = 1e-2
HYENA_FAST_PCT = 0.3
HYENA_SLOW_PCT = 1.5
MASK_VALUE = -1e30
VMEM_LIMIT = 56 * 1024 * 1024
LONG_CONV_VMEM = 60 * 1024 * 1024
LANES = 128
MXU_WIDTH = 256
LOG2E = math.log2(math.e)
FFN_ROW_SPLIT = 2
F32_SUBLANES = 8
BF16_SUBLANES = 16


def _params(*sem):
    return pltpu.CompilerParams(dimension_semantics=sem, vmem_limit_bytes=VMEM_LIMIT)


def _bdot(a, b):
    return jnp.dot(a, b, preferred_element_type=F32)


def _split_bf16(a):
    hi = a.astype(BF16)
    lo = (a - hi.astype(F32)).astype(BF16)
    return hi, lo


def _dot3(a, b):
    a_hi, a_lo = _split_bf16(a)
    b_hi, b_lo = _split_bf16(b)
    return _bdot(a_hi, b_hi) + _bdot(a_lo, b_hi) + _bdot(a_hi, b_lo)


def _mod_body(c_ref, w_ref, b_ref, o_ref):
    c = c_ref[...]
    rows = c.shape[0]
    s_hi, s_lo = _split_bf16(c * jax.nn.sigmoid(c))
    w_hi, w_lo = _split_bf16(w_ref[0])
    both = _bdot(jnp.concatenate([s_hi, s_lo], axis=0), w_hi)
    o_ref[0] = both[:rows] + both[rows:] + _bdot(s_hi, w_lo) + b_ref[0]


def _modulation(c_all, w_mod, b_mod):
    depth, d, n = w_mod.shape
    rows = c_all.shape[0]
    tn = 1536
    return pl.pallas_call(
        _mod_body,
        grid=(depth, n // tn),
        in_specs=[
            pl.BlockSpec((rows, d), lambda l, j: (0, 0)),
            pl.BlockSpec((1, d, tn), lambda l, j: (l, 0, j)),
            pl.BlockSpec((1, 1, tn), lambda l, j: (l, 0, j)),
        ],
        out_specs=pl.BlockSpec((1, rows, tn), lambda l, j: (l, 0, j)),
        out_shape=jax.ShapeDtypeStruct((depth, rows, n), F32),
        compiler_params=_params("parallel", "parallel"),
        name="modulation",
    )(c_all, w_mod, b_mod.reshape(depth, 1, n))


def _modulated_norm(x, g, sc, sh):
    ms = jnp.mean(x * x, axis=-1, keepdims=True)
    return x * lax.rsqrt(ms + EPS) * (g * (1.0 + sc)) + sh


def _mod_spec(layer, row_of, chunk, d):
    return pl.BlockSpec((1, 1, 1, 1, d), lambda i, *_: (layer, row_of(i), chunk, 0, 0))


def _short_conv_tile(u, prev_row, next_row, cw_ref, cb_ref):
    rows = u.shape[0]
    row = lax.broadcasted_iota(jnp.int32, u.shape, 0)
    prev = jnp.where(row == 0, prev_row, pltpu.roll(u, 1, axis=0))
    nxt = jnp.where(row == rows - 1, next_row, pltpu.roll(u, rows - 1, axis=0))
    return prev * cw_ref[0:1] + u * cw_ref[1:2] + nxt * cw_ref[2:3] + cb_ref[...]


def _norm_proj_body(splits, conv, x_ref, g_ref, sc_ref, sh_ref, w_ref, *rest):
    o_refs = ((rest[4:6],) + rest[6:]) if conv else rest
    g, sc, sh = g_ref[0], sc_ref[0, 0, 0], sh_ref[0, 0, 0]
    tm = x_ref.shape[1]
    edge = F32_SUBLANES if conv else 0
    if conv:
        xp_ref, xn_ref, cw_ref, cb_ref = rest[:4]
        x_all = jnp.concatenate([xp_ref[0], x_ref[0], xn_ref[0]], axis=0)
    else:
        x_all = x_ref[0]
    p_all = _bdot(_modulated_norm(x_all, g, sc, sh).astype(BF16), w_ref[0])
    p = p_all[edge:edge + tm]
    off = 0
    for idx, (o_ref, (width, _, scale)) in enumerate(zip(o_refs, splits)):
        part = p[:, off:off + width]
        if conv and idx == 0:
            gate_ref, v_ref = o_ref
            j, nj = pl.program_id(1), pl.num_programs(1)
            prev_row = jnp.where(j == 0, 0.0, p_all[edge - 1:edge, off:off + width])
            next_row = jnp.where(j == nj - 1, 0.0, p_all[edge + tm:edge + tm + 1, off:off + width])
            u = _short_conv_tile(part, prev_row, next_row, cw_ref, cb_ref)
            c = v_ref.shape[3]
            n_gates = gate_ref.shape[1]
            for gi in range(n_gates):
                gate_ref[0, gi] = u[:, gi * c:(gi + 1) * c].astype(gate_ref.dtype)
            v_ref[0, 0] = u[:, n_gates * c:]
        else:
            o_ref[0] = (part if scale is None else part * scale).astype(o_ref.dtype)
        off += width


def _norm_proj(x, norm_g, mod, row_of, layer, w, w_idx, splits, tm, conv=None):
    b, t, d = x.shape
    n = w.shape[2]
    tm = min(tm, t)
    edge = F32_SUBLANES
    tok = lambda wd: pl.BlockSpec((1, tm, wd), lambda i, j: (i, j, 0))
    in_specs = [
        tok(d),
        pl.BlockSpec((1, 1, d), lambda i, j: (layer, 0, 0)),
        _mod_spec(layer, row_of, 1, d),
        _mod_spec(layer, row_of, 0, d),
        pl.BlockSpec((1, d, n), lambda i, j: (w_idx, 0, 0)),
    ]
    args = [x, norm_g, mod, mod, w]
    out_specs = [tok(wd) for wd, _, _ in splits]
    out_shape = [jax.ShapeDtypeStruct((b, t, wd), dt) for wd, dt, _ in splits]
    if conv is not None:
        cw, cb = conv
        w0, c = splits[0][0], D_HYENA
        per_tile = tm // edge
        in_specs += [
            pl.BlockSpec((1, edge, d), lambda i, j: (i, jnp.maximum(j * per_tile - 1, 0), 0)),
            pl.BlockSpec((1, edge, d), lambda i, j: (i, jnp.minimum((j + 1) * per_tile, t // edge - 1), 0)),
            pl.BlockSpec(cw.shape, lambda i, j: (0, 0)),
            pl.BlockSpec(cb.shape, lambda i, j: (0, 0)),
        ]
        args += [x, x, cw, cb]
        n_gates = w0 // c - 1
        out_specs[0:1] = [pl.BlockSpec((1, n_gates, tm, c), lambda i, j: (i, 0, j, 0)),
                          pl.BlockSpec((1, 1, tm, c), lambda i, j: (i, 0, j, 0))]
        out_shape[0:1] = [jax.ShapeDtypeStruct((b, n_gates, t, c), BF16),
                          jax.ShapeDtypeStruct((b, 1, t, c), splits[0][1])]
    return pl.pallas_call(
        functools.partial(_norm_proj_body, tuple(splits), conv is not None),
        grid=(b, t // tm),
        in_specs=in_specs,
        out_specs=out_specs,
        out_shape=out_shape,
        compiler_params=_params("parallel", "parallel"),
        name="norm_proj",
    )(*args)


def _mix_ffn_body(final_norm, x_ref, hy_ref, fn_ref, na_ref, wo_ref, g1_ref, ng_ref, sc_ref, sh_ref, g2_ref,
                  wg_ref, wu_ref, wd_ref, fg_ref, o_ref):
    tm = x_ref.shape[1]
    sub = tm // FFN_ROW_SPLIT if tm % (FFN_ROW_SPLIT * MXU_WIDTH) == 0 else tm
    halves = [slice(r0, r0 + sub) for r0 in range(0, tm, sub)]
    x1s, hs = [], []
    for rows in halves:
        mix = (_bdot(hy_ref[0, rows], wo_ref[0, :D_HYENA])
               + _bdot(fn_ref[0, rows], wo_ref[0, D_HYENA:D_HYENA + D_FNET])
               + _bdot(na_ref[0, rows], wo_ref[0, D_HYENA + D_FNET:]))
        x1 = x_ref[0, rows] + g1_ref[0, 0, 0] * mix
        x1s.append(x1)
        hs.append(_modulated_norm(x1, ng_ref[0], sc_ref[0, 0, 0], sh_ref[0, 0, 0]).astype(BF16))
    accs = [None] * len(halves)
    for c0 in range(0, wg_ref.shape[2], MXU_WIDTH):
        cols = slice(c0, c0 + MXU_WIDTH)
        for s, h in enumerate(hs):
            a = _bdot(h, wg_ref[0, :, cols])
            u = _bdot(h, wu_ref[0, :, cols])
            act = (a * jax.nn.sigmoid(a) * u).astype(BF16)
            part = _bdot(act, wd_ref[0, cols, :])
            accs[s] = part if accs[s] is None else accs[s] + part
    for rows, x1, acc in zip(halves, x1s, accs):
        y = x1 + g2_ref[0, 0, 0] * acc
        if final_norm:
            ms = jnp.mean(y * y, axis=-1, keepdims=True)
            y = y * lax.rsqrt(ms + EPS) * fg_ref[...]
        o_ref[0, rows] = y


def _mix_ffn(x, y_hy, y_fn, y_na, w_out, norm_g, mod, row_of, layer, wg, wu, wd, final_g, final_norm, tm):
    b, t, d = x.shape
    dff = wg.shape[2]
    assert dff % MXU_WIDTH == 0
    tm = min(tm, t)
    tok = lambda wd_: pl.BlockSpec((1, tm, wd_), lambda i, j: (i, j, 0))
    resident = lambda shape: pl.BlockSpec((1,) + shape, lambda i, j: (layer, 0, 0), pipeline_mode=pl.Buffered(1))
    return pl.pallas_call(
        functools.partial(_mix_ffn_body, final_norm),
        grid=(b, t // tm),
        in_specs=[
            tok(d), tok(D_HYENA), tok(D_FNET), tok(D_NA),
            resident((d, d)),
            _mod_spec(layer, row_of, 2, d),
            pl.BlockSpec((1, 1, d), lambda i, j: (layer, 0, 0)),
            _mod_spec(layer, row_of, 4, d),
            _mod_spec(layer, row_of, 3, d),
            _mod_spec(layer, row_of, 5, d),
            resident((d, dff)), resident((d, dff)), resident((dff, d)),
            pl.BlockSpec((1, d), lambda i, j: (0, 0)),
        ],
        out_specs=tok(d),
        out_shape=jax.ShapeDtypeStruct(x.shape, F32),
        compiler_params=_params("parallel", "parallel"),
        name="mix_ffn",
    )(x, y_hy, y_fn, y_na, w_out, mod, norm_g, mod, mod, mod, wg, wu, wd, final_g)


def _split_seq(n_total):
    log = int(round(math.log2(n_total)))
    assert 2 ** log == n_total
    n1 = 2 ** (log // 2)
    return n1, n_total // n1


@functools.lru_cache(maxsize=None)
def _hyena_consts(seq):
    n = 2 * seq
    n1, n2 = _split_seq(n)
    h1 = n1 // 2
    f1 = np.arange(h1)[:, None]
    t1 = np.arange(h1)[None, :]
    th = np.pi * (2 * f1 + 1) * t1 / n1
    m1 = np.empty((n1, h1))
    m1[0::2] = np.cos(th)
    m1[1::2] = -np.sin(th)
    m3 = m1.T * (2.0 / n)
    f = (np.arange(h1)[:, None, None] + n1 * np.arange(n2)[None, :, None])
    t2 = np.arange(n2)[None, None, :]
    ph = np.pi * ((2 * f + 1) * t2 % (2 * n)) / n
    c, s = np.cos(ph), np.sin(ph)
    m2 = np.concatenate([np.concatenate([c, s], axis=2), np.concatenate([-s, c], axis=2)], axis=1)
    m2t = np.transpose(m2, (0, 2, 1))
    m1k = np.kron(m1, np.eye(F32_SUBLANES))
    m3k = np.kron(m3, np.eye(BF16_SUBLANES))
    return n1, n2, m1k, m2, m2t, m3k


@functools.lru_cache(maxsize=None)
def _fnet_consts(seq):
    n1, n2 = _split_seq(seq)
    n1, n2 = n1 // 2, n2 * 2
    f1 = np.arange(n1)[:, None]
    t1 = np.arange(n1)[None, :]
    th = 2 * np.pi * (f1 * t1 % n1) / n1
    m1 = np.empty((2 * n1, n1))
    m1[0::2] = np.cos(th)
    m1[1::2] = -np.sin(th)
    f = (np.arange(n1)[:, None, None] + n1 * np.arange(n2)[None, :, None])
    t2 = np.arange(n2)[None, None, :]
    ph = 2 * np.pi * (f * t2 % seq) / seq
    c, s = np.cos(ph), np.sin(ph)
    m2 = np.concatenate([np.concatenate([c, s], axis=2), np.concatenate([-s, c], axis=2)], axis=1)
    gd = FNET_GROUP_DIM
    cc = np.arange(gd)
    pg = 2 * np.pi * (np.outer(cc, cc) % gd) / gd
    scale = 1.0 / math.sqrt(gd * seq)
    eye = np.eye(D_FNET // gd)
    bdc = np.kron(eye, np.cos(pg)) * scale
    bds = np.kron(eye, np.sin(pg)) * scale
    m1k = np.kron(m1, np.eye(BF16_SUBLANES))
    g = F32_SUBLANES
    perm = np.zeros((g * n1, n1 * g))
    for f2 in range(g):
        for ff in range(n1):
            perm[f2 * n1 + ff, ff * g + f2] = 1.0
    return n1, n2, m1k, m2, perm, bdc, bds


def _mxu_const(a):
    return jnp.asarray(a, dtype=F32).astype(BF16)


def _stage1_into(mk_ref, x_at, a_scr, group):
    nf, _, n2, c = a_scr.shape

    def body(s, carry):
        r0 = pl.multiple_of(s * group, group)
        xs = x_at(r0)
        xs = xs.reshape(xs.shape[0] * group, c).astype(BF16)
        a_scr[:, :, pl.ds(r0, group), :] = _bdot(mk_ref[...], xs).reshape(nf, 2, group, c)
        return carry

    lax.fori_loop(0, n2 // group, body, 0, unroll=True)


def _stage2_rows(a_scr, f1):
    _, _, n2, c = a_scr.shape
    return a_scr[f1].reshape(2 * n2, c).astype(BF16)


def _filter_body(zt_ref, t_ref, w1t_ref, b1_ref, fr_ref, w2t_ref, b2_ref, w3_ref, dl_ref, m1k_ref, m2_ref,
                 kr_ref, ki_ref, h_scr, kt_scr, af_scr, ab_scr):
    o, j = pl.program_id(0), pl.program_id(1)
    fb = m2_ref.shape[0]
    h1, _, n2, c = af_scr.shape
    g = F32_SUBLANES

    @pl.when((o == 0) & (j == 0))
    def _():
        h = jnp.sin(fr_ref[:, 0:1] * (_dot3(w1t_ref[...], zt_ref[...]) + b1_ref[...]))
        h = jnp.sin(fr_ref[:, 1:2] * (_dot3(w2t_ref[...], h) + b2_ref[...]))
        h_scr[...] = h.T

    @pl.when(j == 0)
    def _():
        k = _dot3(h_scr[...], w3_ref[0])
        decay = jnp.exp(-t_ref[...] * jnp.abs(dl_ref[...]))
        kf = k[:, :c] * decay
        row = lax.broadcasted_iota(jnp.int32, kf.shape, 0)
        kb = jnp.where(row == 0, 0.0, k[:, c:] * decay)
        norm = jnp.sum(jnp.abs(kf), axis=0, keepdims=True) + jnp.sum(jnp.abs(kb), axis=0, keepdims=True)
        inv = 1.0 / norm
        kt_scr[0] = (kf * inv).reshape(h1, n2, c)
        kt_scr[1] = (kb * inv).reshape(h1, n2, c)
        _stage1_into(m1k_ref, lambda r0: kt_scr[0, :, pl.ds(r0, g), :], af_scr, g)
        _stage1_into(m1k_ref, lambda r0: kt_scr[1, :, pl.ds(r0, g), :], ab_scr, g)

    for i in range(fb):
        f1 = j * fb + i
        xf = _bdot(m2_ref[i], _stage2_rows(af_scr, f1))
        xb = _bdot(m2_ref[i], _stage2_rows(ab_scr, f1))
        kr_ref[0, i] = (xf[:n2] + xb[:n2]).astype(kr_ref.dtype)
        ki_ref[0, i] = (xf[n2:] - xb[n2:]).astype(ki_ref.dtype)


def _hyena_filter_spectrum(seq, w1, b1, freq, w2, b2, w3, m1k, m2, fb):
    c = D_HYENA
    h1, n2 = m2.shape[0], m2.shape[1] // 2
    t = np.linspace(0.0, 1.0, seq)[:, None]
    w = 2.0 * np.pi * np.arange(seq)[:, None] / seq
    f = np.linspace(1e-4, POS_BANDS - 1, POS_BANDS)[None]
    z = np.concatenate([t, np.cos(f * w), -np.sin(f * w)], axis=-1).astype(np.float32)
    pe = z.shape[1]
    zt = jnp.asarray(np.pad(z, ((0, 0), (0, LANES - pe))).T)
    tcol = jnp.asarray(t.astype(np.float32))
    w1t = jnp.pad(w1, ((0, LANES - pe), (0, 0))).T
    max_decay = math.log(HYENA_DECAY_TARGET) / HYENA_FAST_PCT
    min_decay = math.log(HYENA_DECAY_TARGET) / HYENA_SLOW_PCT
    deltas = jnp.asarray(np.linspace(min_decay, max_decay, c, dtype=np.float32)[None])
    hid = w1.shape[1]
    w3r = jnp.transpose(w3.reshape(hid, 2, 2 * c), (1, 0, 2))
    full = lambda a: pl.BlockSpec(a.shape, lambda o, j: (0,) * a.ndim)
    b1c, b2c, frt = b1.reshape(hid, 1), b2.reshape(hid, 1), freq.T
    out = jax.ShapeDtypeStruct((2, h1, n2, c), BF16)
    return pl.pallas_call(
        _filter_body,
        grid=(2, h1 // fb),
        in_specs=[full(zt), full(tcol), full(w1t), full(b1c), full(frt), full(w2.T), full(b2c),
                  pl.BlockSpec((1, hid, 2 * c), lambda o, j: (o, 0, 0)), full(deltas), full(m1k),
                  pl.BlockSpec((fb, 2 * n2, 2 * n2), lambda o, j: (j, 0, 0))],
        out_specs=[pl.BlockSpec((1, fb, n2, c), lambda o, j: (o, j, 0, 0))] * 2,
        out_shape=[out, out],
        scratch_shapes=[pltpu.VMEM((seq, hid), F32), pltpu.VMEM((2, h1, n2, c), F32),
                        pltpu.VMEM((h1, 2, n2, c), F32), pltpu.VMEM((h1, 2, n2, c), F32)],
        compiler_params=_params("arbitrary", "arbitrary"),
        name="hyena_filter_spectrum",
    )(zt, tcol, w1t, b1c, frt, w2.T, b2c, w3r, deltas, m1k, m2)


def _long_conv_body(m1k_ref, m2_ref, m2t_ref, m3k_ref, v_ref, gate_ref, kr_ref, ki_ref, skip_ref, o_ref,
                    a_scr, b_scr, z_scr):
    h1, _, n2, c = a_scr.shape

    def gated_conv(order, src, store):
        g = F32_SUBLANES
        _stage1_into(m1k_ref, lambda r0: src(pl.ds(r0, g)), a_scr, g)

        def per_f1(f1, carry):
            x = _bdot(m2_ref[f1], _stage2_rows(a_scr, f1))
            xr, xi = x[:n2], x[n2:]
            kr, ki = kr_ref[order, f1].astype(F32), ki_ref[order, f1].astype(F32)
            y = jnp.concatenate([kr * xr - ki * xi, kr * xi + ki * xr], axis=0).astype(BF16)
            b_scr[pl.ds(2 * f1, 2)] = _bdot(m2t_ref[f1], y).astype(BF16).reshape(2, n2, c)
            return carry

        lax.fori_loop(0, h1, per_f1, 0, unroll=True)

        g = BF16_SUBLANES
        for s in range(n2 // g):
            rows = slice(s * g, (s + 1) * g)
            bs = b_scr[:, rows, :].reshape(2 * h1 * g, c)
            y = _bdot(m3k_ref[...], bs).reshape(h1, g, c)
            store(rows, gate_ref[0, order, :, rows, :] * (y + src(rows) * skip_ref[order:order + 1]))

    def store_z(rows, z):
        z_scr[:, rows, :] = z

    def store_out(rows, y):
        o_ref[0, :, rows, :] = y.astype(o_ref.dtype)

    gated_conv(0, lambda rows: v_ref[0, 0, :, rows, :], store_z)
    gated_conv(1, lambda rows: z_scr[:, rows, :], store_out)


def _hyena_mixer(gates, v, w1, b1, freq, w2, b2, w3, skip):
    b, _, seq, c = v.shape
    n1, n2, m1k, m2, m2t, m3k = _hyena_consts(seq)
    h1 = n1 // 2
    m1k, m2, m2t, m3k = (_mxu_const(a) for a in (m1k, m2, m2t, m3k))

    kr, ki = _hyena_filter_spectrum(seq, w1, b1, freq, w2, b2, w3, m1k, m2, min(h1, 8))

    resident = lambda a: pl.BlockSpec(a.shape, lambda i: (0,) * a.ndim, pipeline_mode=pl.Buffered(1))
    y = pl.pallas_call(
        _long_conv_body,
        grid=(b,),
        in_specs=[resident(m1k), resident(m2), resident(m2t), resident(m3k),
                  pl.BlockSpec((1, 1, h1, n2, c), lambda i: (i, 0, 0, 0, 0)),
                  pl.BlockSpec((1, 2, h1, n2, c), lambda i: (i, 0, 0, 0, 0)),
                  resident(kr), resident(ki), resident(skip)],
        out_specs=pl.BlockSpec((1, h1, n2, c), lambda i: (i, 0, 0, 0)),
        out_shape=jax.ShapeDtypeStruct((b, h1, n2, c), BF16),
        scratch_shapes=[pltpu.VMEM((h1, 2, n2, c), F32), pltpu.VMEM((2 * h1, n2, c), BF16),
                        pltpu.VMEM((h1, n2, c), F32)],
        compiler_params=pltpu.CompilerParams(dimension_semantics=("parallel",), vmem_limit_bytes=LONG_CONV_VMEM),
        name="hyena_long_conv",
    )(m1k, m2, m2t, m3k, v.reshape(b, 1, h1, n2, c), gates.reshape(b, 2, h1, n2, c), kr, ki, skip)
    return y.reshape(b, seq, c)


def _fnet_body(m1k_ref, m2_ref, perm_ref, bdc_ref, bds_ref, g_ref, o_ref, a_scr):
    j = pl.program_id(1)
    fb = m2_ref.shape[0]
    n1, _, n2, c = a_scr.shape

    @pl.when(j == 0)
    def _():
        gi = BF16_SUBLANES
        _stage1_into(m1k_ref, lambda r0: g_ref[0, :, pl.ds(r0, gi), :], a_scr, gi)

    for i in range(fb):
        f1 = j * fb + i
        a_scr[f1] = _bdot(m2_ref[i], _stage2_rows(a_scr, f1)).reshape(2, n2, c)

    @pl.when(j == pl.num_programs(1) - 1)
    def _():
        go = F32_SUBLANES

        def body(s, carry):
            r0 = pl.multiple_of(s * go, go)
            re = a_scr[:, 0, pl.ds(r0, go), :].reshape(n1 * go, c).astype(BF16)
            im = a_scr[:, 1, pl.ds(r0, go), :].reshape(n1 * go, c).astype(BF16)
            xr = _bdot(perm_ref[...], re).astype(BF16)
            xi = _bdot(perm_ref[...], im).astype(BF16)
            out = _bdot(xr, bdc_ref[...]) + _bdot(xi, bds_ref[...])
            o_ref[0, pl.ds(pl.multiple_of(s * go * n1, go * n1), go * n1), :] = out.astype(o_ref.dtype)
            return carry

        lax.fori_loop(0, n2 // go, body, 0, unroll=True)


def _fourier_mix(g):
    b, seq, c = g.shape
    n1, n2, m1k, m2, perm, bdc, bds = _fnet_consts(seq)
    m1k, m2, perm, bdc, bds = (_mxu_const(a) for a in (m1k, m2, perm, bdc, bds))
    fb = min(n1, 8)
    const = lambda a: pl.BlockSpec(a.shape, lambda i, j: (0,) * a.ndim)
    return pl.pallas_call(
        _fnet_body,
        grid=(b, n1 // fb),
        in_specs=[const(m1k),
                  pl.BlockSpec((fb, 2 * n2, 2 * n2), lambda i, j: (j, 0, 0)),
                  const(perm), const(bdc), const(bds),
                  pl.BlockSpec((1, n1, n2, c), lambda i, j: (i, 0, 0, 0))],
        out_specs=pl.BlockSpec((1, seq, c), lambda i, j: (i, 0, 0)),
        out_shape=jax.ShapeDtypeStruct((b, seq, c), BF16),
        scratch_shapes=[pltpu.VMEM((n1, 2, n2, c), F32)],
        compiler_params=_params("parallel", "arbitrary"),
        name="fnet",
    )(m1k, m2, perm, bdc, bds, g.reshape(b, n1, n2, c))


def _dot_nt(a, b):
    return lax.dot_general(a, b, (((1,), (1,)), ((), ())), preferred_element_type=F32)


def _head_pair_attention(q2, keys, values, biases):
    nq = q2.shape[0]
    lane = lax.broadcasted_iota(jnp.int32, q2.shape, 1)
    zero = jnp.zeros_like(q2)
    qs = jnp.concatenate([jnp.where(lane < HEAD_DIM, q2, zero), jnp.where(lane >= HEAD_DIM, q2, zero)], axis=0)
    m = denom = acc = None
    for gi, (kk, vv) in enumerate(zip(keys, values)):
        s = _dot_nt(qs, kk)
        if biases[0][gi] is not None:
            s = s + jnp.concatenate([biases[0][gi], biases[1][gi]], axis=0)
        m_grp = s.max(axis=-1, keepdims=True)
        m_new = m_grp if m is None else jnp.maximum(m, m_grp)
        e = jnp.exp2(s - m_new)
        e_part = e[:, :LANES]
        for c0 in range(LANES, e.shape[1], LANES):
            e_part = e_part + e[:, c0:c0 + LANES]
        pv = _bdot(e.astype(BF16), vv)
        if m is None:
            denom, acc = e_part, pv
        else:
            alpha = jnp.exp2(m - m_new)
            denom = alpha * denom + e_part
            acc = alpha * acc + pv
        m = m_new
    out = acc / denom.sum(axis=-1, keepdims=True)
    return jnp.where(lane < HEAD_DIM, out[:nq], out[nq:])


def _na_window_base(row_block, rows):
    return jnp.clip(row_block * NA_QROWS - NA_KH // 2, 0, rows - NA_KROWS)


def _na_body(rows, q_ref, k_ref, v_ref, kc_ref, vc_ref, bias_ref, o_ref):
    base = _na_window_base(pl.program_id(1), rows)
    start = pl.multiple_of(base * GRID_W, GRID_W)
    n_keys = NA_KROWS * GRID_W
    n_pairs = q_ref.shape[2] // LANES
    groups = [(c0, min(NA_KEY_GROUP, n_keys - c0)) for c0 in range(0, n_keys, NA_KEY_GROUP)]
    for p in range(n_pairs):
        sl = slice(p * LANES, (p + 1) * LANES)
        q2 = q_ref[0, :, sl]
        keys = [kc_ref[0, :, sl]] + [k_ref[0, pl.ds(start + c0, n), sl] for c0, n in groups]
        values = [vc_ref[0, :, sl]] + [v_ref[0, pl.ds(start + c0, n), sl] for c0, n in groups]
        biases = [[None] + [bias_ref[0, 2 * p + hh, 0, :, c0:c0 + n] for c0, n in groups] for hh in range(2)]
        o = _head_pair_attention(q2, keys, values, biases)
        o_ref[0, :, sl] = o.astype(o_ref.dtype)


def _na_bias_body(rows, p_ref, o_ref):
    kh, kw, w = NA_KH, NA_KW, GRID_W
    shape = (w, 2 * w)
    qc = lax.broadcasted_iota(jnp.int32, shape, 0)
    lane = lax.broadcasted_iota(jnp.int32, shape, 1)
    kc = lane & (w - 1)
    col_start = jnp.clip(qc - kw // 2, 0, w - kw)
    col_valid = (kc >= col_start) & (kc < col_start + kw)
    low_half = lane < w
    masked = jnp.full(shape, MASK_VALUE, F32)
    tiles = []
    for a in range(2 * kh - 1):
        row = jnp.broadcast_to(p_ref[0, 0, a:a + 1, :], shape)
        t = pltpu.roll(row, w + 1, 1, stride=1, stride_axis=0)
        t = jnp.where(low_half, t, pltpu.roll(t, w, 1))
        tiles.append(jnp.where(col_valid, t, masked))
    for case, r0 in enumerate((0, NA_QROWS, rows - NA_QROWS)):
        base = int(np.clip(r0 - kh // 2, 0, rows - NA_KROWS))
        for qr in range(NA_QROWS):
            q_abs = r0 + qr
            r_start = int(np.clip(q_abs - kh // 2, 0, rows - kh))

            def tile_for(kr):
                k_abs = base + kr
                if kr >= NA_KROWS or not (r_start <= k_abs < r_start + kh):
                    return masked
                return tiles[k_abs - q_abs + kh - 1]

            q_rows = slice(qr * w, (qr + 1) * w)
            for pair in range((NA_KROWS + 1) // 2):
                chunk = jnp.where(low_half, tile_for(2 * pair), tile_for(2 * pair + 1))
                width = min(2 * w, NA_KROWS * w - pair * 2 * w)
                o_ref[0, 0, case, q_rows, pair * 2 * w:pair * 2 * w + width] = chunk[:, :width]


def _na_bias_tables(rpb, rows):
    w = GRID_W
    depth, nh, na, nb = rpb.shape
    assert 2 * w == LANES and nb == 2 * NA_KW - 1
    lo = w - NA_KW
    padded = jnp.pad(rpb.astype(F32) * LOG2E, ((0, 0), (0, 0), (0, (-na) % 8), (lo, 2 * w - nb - lo)))
    return pl.pallas_call(
        functools.partial(_na_bias_body, rows),
        grid=(depth, nh),
        in_specs=[pl.BlockSpec((1, 1) + padded.shape[2:], lambda l, h: (l, h, 0, 0))],
        out_specs=pl.BlockSpec((1, 1, 3, NA_QROWS * w, NA_KROWS * w), lambda l, h: (l, h, 0, 0, 0)),
        out_shape=jax.ShapeDtypeStruct((depth, nh, 3, NA_QROWS * w, NA_KROWS * w), F32),
        compiler_params=_params("parallel", "parallel"),
        name="na_bias_tables",
    )(padded)


def _neighbourhood_attention(q, kv, kv_ctx, bias, layer):
    b, seq, _ = q.shape
    rows = seq // GRID_W
    lc = kv_ctx.shape[1]
    assert rows % NA_QROWS == 0 and rows >= NA_KROWS + 1 and NA_QROWS == NA_KH // 2
    n_blocks = rows // NA_QROWS
    nh = bias.shape[1]
    tq = NA_QROWS * GRID_W

    def bias_map(i, r):
        return (layer, 0, jnp.where(r == 0, 0, jnp.where(r == n_blocks - 1, 2, 1)), 0, 0)

    return pl.pallas_call(
        functools.partial(_na_body, rows),
        grid=(b, n_blocks),
        in_specs=[pl.BlockSpec((1, tq, D_NA), lambda i, r: (i, r, 0)),
                  pl.BlockSpec((1, seq, D_NA), lambda i, r: (i, 0, 0)),
                  pl.BlockSpec((1, seq, D_NA), lambda i, r: (i, 0, 1)),
                  pl.BlockSpec((1, lc, D_NA), lambda i, r: (i, 0, 0)),
                  pl.BlockSpec((1, lc, D_NA), lambda i, r: (i, 0, 1)),
                  pl.BlockSpec((1, nh, 1, tq, NA_KROWS * GRID_W), bias_map)],
        out_specs=pl.BlockSpec((1, tq, D_NA), lambda i, r: (i, r, 0)),
        out_shape=jax.ShapeDtypeStruct((b, seq, D_NA), BF16),
        compiler_params=_params("parallel", "arbitrary"),
        name="neighbourhood_attention",
    )(q, kv, kv, kv_ctx, kv_ctx, bias)


def _ctx_attn_body(q_ref, k_ref, v_ref, o_ref):
    for p in range(q_ref.shape[2] // LANES):
        sl = slice(p * LANES, (p + 1) * LANES)
        o = _head_pair_attention(q_ref[0, :, sl], [k_ref[0, :, sl]], [v_ref[0, :, sl]], [[None], [None]])
        o_ref[0, :, sl] = o.astype(o_ref.dtype)


def _context_attention(q, kv):
    b, lc, _ = q.shape
    return pl.pallas_call(
        _ctx_attn_body,
        grid=(b,),
        in_specs=[pl.BlockSpec((1, lc, D_NA), lambda i: (i, 0, 0)),
                  pl.BlockSpec((1, lc, D_NA), lambda i: (i, 0, 0)),
                  pl.BlockSpec((1, lc, D_NA), lambda i: (i, 0, 1))],
        out_specs=pl.BlockSpec((1, lc, D_NA), lambda i: (i, 0, 0)),
        out_shape=jax.ShapeDtypeStruct((b, lc, D_NA), BF16),
        compiler_params=_params("parallel"),
        name="context_attention",
    )(q, kv, kv)


def kernel(x, c, ctx, c_ctx, w_mod, b_mod, norm1_g, norm2_g, w_in, w_out, hy_conv_w, hy_conv_b,
           filt_w1, filt_b1, filt_freq, filt_w2, filt_b2, filt_w3, hy_skip, na_rpb,
           w_gate, w_up, w_down, final_g):
    depth = w_mod.shape[0]
    b, seq, d = x.shape
    off_k = 3 * D_HYENA + D_FNET + D_NA
    qk_scale = LOG2E / math.sqrt(HEAD_DIM)
    splits = ((3 * D_HYENA, F32, None), (D_FNET, BF16, None), (D_NA, BF16, qk_scale), (2 * D_NA, BF16, None))
    tm_proj, tm = 1024, 1024

    pad = (-(b + 1)) % 8
    c_all = jnp.concatenate([c, c_ctx[None], jnp.zeros((pad, d), F32)], axis=0)
    mod = _modulation(c_all, w_mod, b_mod).reshape(depth, b + 1 + pad, 6, 1, d)
    lat_row = lambda i: i
    ctx_row = lambda i: b

    w_in_b, w_out_b = w_in.astype(BF16), w_out.astype(BF16)
    w_gate_b, w_up_b, w_down_b = w_gate.astype(BF16), w_up.astype(BF16), w_down.astype(BF16)
    w_kv_last = w_in_b[depth - 1:, :, off_k:]
    n1g, n2g = norm1_g.reshape(depth, 1, d), norm2_g.reshape(depth, 1, d)
    fg = final_g.reshape(1, d)
    na_bias = _na_bias_tables(na_rpb, seq // GRID_W)

    for l in range(depth):
        last = l == depth - 1
        hy_params = (filt_w1[l], filt_b1[l], filt_freq[l], filt_w2[l], filt_b2[l], filt_w3[l], hy_skip[l])
        conv = (hy_conv_w[l], hy_conv_b[l].reshape(1, 3 * D_HYENA))

        gates, v, fn, q, kv = _norm_proj(x, n1g, mod, lat_row, l, w_in_b, l, splits, tm_proj, conv)
        if last:
            (kv_c,) = _norm_proj(ctx.reshape(1, -1, d), n1g, mod, ctx_row, l, w_kv_last, 0,
                                 ((2 * D_NA, BF16, None),), tm_proj)
            kv_c = kv_c.reshape(b, -1, 2 * D_NA)
        else:
            cgates, cv, cfn, cq, kv_c = _norm_proj(ctx, n1g, mod, ctx_row, l, w_in_b, l, splits, tm_proj, conv)

        y_hy = _hyena_mixer(gates, v, *hy_params)
        y_fn = _fourier_mix(fn)
        y_na = _neighbourhood_attention(q, kv, kv_c, na_bias, l)

        x = _mix_ffn(x, y_hy, y_fn, y_na, w_out_b, n2g, mod, lat_row, l, w_gate_b, w_up_b, w_down_b,
                     fg, last, tm)

        if not last:
            cy_hy = _hyena_mixer(cgates, cv, *hy_params)
            cy_fn = _fourier_mix(cfn)
            cy_na = _context_attention(cq, kv_c)
            flat = lambda a: a.reshape(1, -1, a.shape[-1])
            ctx = _mix_ffn(flat(ctx), flat(cy_hy), flat(cy_fn), flat(cy_na), w_out_b, n2g, mod, ctx_row, l,
                           w_gate_b, w_up_b, w_down_b, fg, False, tm).reshape(ctx.shape)
    return x
```

```python
import functools
import math

import numpy as np
import jax
import jax.numpy as jnp
from jax import lax
from jax.experimental import pallas as pl
from jax.experimental.pallas import tpu as pltpu

F32 = jnp.float32
BF16 = jnp.bfloat16

EPS = 1e-6
GRID_W = 64
HEAD_DIM = 64
D_HYENA = 256
D_FNET = 256
D_NA = 512
FNET_GROUP_DIM = 64
NA_KH = 8
NA_KW = 16
NA_QROWS = 4
NA_KROWS = NA_QROWS + NA_KH
NA_KEY_GROUP = 256
POS_BANDS = 16
HYENA_DECAY_TARGET = 1e-2
HYENA_FAST_PCT = 0.3
HYENA_SLOW_PCT = 1.5
MASK_VALUE = -1e30
VMEM_LIMIT = 56 * 1024 * 1024
LONG_CONV_VMEM = 60 * 1024 * 1024
LANES = 128
MXU_WIDTH = 256
LOG2E = math.log2(math.e)
FFN_ROW_SPLIT = 2
F32_SUBLANES = 8
BF16_SUBLANES = 16


def _params(*sem):
    return pltpu.CompilerParams(dimension_semantics=sem, vmem_limit_bytes=VMEM_LIMIT)


def _bdot(a, b):
    return jnp.dot(a, b, preferred_element_type=F32)


def _split_bf16(a):
    hi = a.astype(BF16)
    lo = (a - hi.astype(F32)).astype(BF16)
    return hi, lo


def _dot3(a, b):
    a_hi, a_lo = _split_bf16(a)
    b_hi, b_lo = _split_bf16(b)
    return _bdot(a_hi, b_hi) + _bdot(a_lo, b_hi) + _bdot(a_hi, b_lo)


def _mod_body(c_ref, w_ref, b_ref, o_ref):
    c = c_ref[...]
    rows = c.shape[0]
    s_hi, s_lo = _split_bf16(c * jax.nn.sigmoid(c))
    w_hi, w_lo = _split_bf16(w_ref[0])
    both = _bdot(jnp.concatenate([s_hi, s_lo], axis=0), w_hi)
    o_ref[0] = both[:rows] + both[rows:] + _bdot(s_hi, w_lo) + b_ref[0]


def _modulation(c_all, w_mod, b_mod):
    depth, d, n = w_mod.shape
    rows = c_all.shape[0]
    tn = 1536
    return pl.pallas_call(
        _mod_body,
        grid=(depth, n // tn),
        in_specs=[
            pl.BlockSpec((rows, d), lambda l, j: (0, 0)),
            pl.BlockSpec((1, d, tn), lambda l, j: (l, 0, j)),
            pl.BlockSpec((1, 1, tn), lambda l, j: (l, 0, j)),
        ],
        out_specs=pl.BlockSpec((1, rows, tn), lambda l, j: (l, 0, j)),
        out_shape=jax.ShapeDtypeStruct((depth, rows, n), F32),
        compiler_params=_params("parallel", "parallel"),
        name="modulation",
    )(c_all, w_mod, b_mod.reshape(depth, 1, n))


def _modulated_norm(x, g, sc, sh):
    ms = jnp.mean(x * x, axis=-1, keepdims=True)
    return x * lax.rsqrt(ms + EPS) * (g * (1.0 + sc)) + sh


def _mod_spec(mod, layer, chunk, d):
    return pl.BlockSpec((1, mod.shape[1], d), lambda i, *_: (layer, 0, chunk))


def _mod_row(ref, row_of):
    return ref[0, pl.ds(row_of(pl.program_id(0)), 1), :]


def _short_conv_tile(u, prev_row, next_row, cw_ref, cb_ref):
    rows = u.shape[0]
    row = lax.broadcasted_iota(jnp.int32, u.shape, 0)
    prev = jnp.where(row == 0, prev_row, pltpu.roll(u, 1, axis=0))
    nxt = jnp.where(row == rows - 1, next_row, pltpu.roll(u, rows - 1, axis=0))
    return prev * cw_ref[0:1] + u * cw_ref[1:2] + nxt * cw_ref[2:3] + cb_ref[...]


def _norm_proj_body(splits, conv, row_of, x_ref, g_ref, sc_ref, sh_ref, w_ref, *rest):
    o_refs = ((rest[4:6],) + rest[6:]) if conv else rest
    g, sc, sh = g_ref[0], _mod_row(sc_ref, row_of), _mod_row(sh_ref, row_of)
    tm = x_ref.shape[1]
    edge = F32_SUBLANES if conv else 0
    if conv:
        xp_ref, xn_ref, cw_ref, cb_ref = rest[:4]
        x_all = jnp.concatenate([xp_ref[0], x_ref[0], xn_ref[0]], axis=0)
    else:
        x_all = x_ref[0]
    p_all = _bdot(_modulated_norm(x_all, g, sc, sh).astype(BF16), w_ref[0])
    p = p_all[edge:edge + tm]
    off = 0
    for idx, (o_ref, (width, _, scale)) in enumerate(zip(o_refs, splits)):
        part = p[:, off:off + width]
        if conv and idx == 0:
            gate_ref, v_ref = o_ref
            j, nj = pl.program_id(1), pl.num_programs(1)
            prev_row = jnp.where(j == 0, 0.0, p_all[edge - 1:edge, off:off + width])
            next_row = jnp.where(j == nj - 1, 0.0, p_all[edge + tm:edge + tm + 1, off:off + width])
            u = _short_conv_tile(part, prev_row, next_row, cw_ref, cb_ref)
            c = v_ref.shape[3]
            n_gates = gate_ref.shape[1]
            for gi in range(n_gates):
                gate_ref[0, gi] = u[:, gi * c:(gi + 1) * c].astype(gate_ref.dtype)
            v_ref[0, 0] = u[:, n_gates * c:]
        else:
            o_ref[0] = (part if scale is None else part * scale).astype(o_ref.dtype)
        off += width


def _norm_proj(x, norm_g, mod, row_of, layer, w, w_idx, splits, tm, conv=None):
    b, t, d = x.shape
    n = w.shape[2]
    tm = min(tm, t)
    edge = F32_SUBLANES
    tok = lambda wd: pl.BlockSpec((1, tm, wd), lambda i, j: (i, j, 0))
    in_specs = [
        tok(d),
        pl.BlockSpec((1, 1, d), lambda i, j: (layer, 0, 0)),
        _mod_spec(mod, layer, 1, d),
        _mod_spec(mod, layer, 0, d),
        pl.BlockSpec((1, d, n), lambda i, j: (w_idx, 0, 0)),
    ]
    args = [x, norm_g, mod, mod, w]
    out_specs = [tok(wd) for wd, _, _ in splits]
    out_shape = [jax.ShapeDtypeStruct((b, t, wd), dt) for wd, dt, _ in splits]
    if conv is not None:
        cw, cb = conv
        w0, c = splits[0][0], D_HYENA
        per_tile = tm // edge
        in_specs += [
            pl.BlockSpec((1, edge, d), lambda i, j: (i, jnp.maximum(j * per_tile - 1, 0), 0)),
            pl.BlockSpec((1, edge, d), lambda i, j: (i, jnp.minimum((j + 1) * per_tile, t // edge - 1), 0)),
            pl.BlockSpec(cw.shape, lambda i, j: (0, 0)),
            pl.BlockSpec(cb.shape, lambda i, j: (0, 0)),
        ]
        args += [x, x, cw, cb]
        n_gates = w0 // c - 1
        out_specs[0:1] = [pl.BlockSpec((1, n_gates, tm, c), lambda i, j: (i, 0, j, 0)),
                          pl.BlockSpec((1, 1, tm, c), lambda i, j: (i, 0, j, 0))]
        out_shape[0:1] = [jax.ShapeDtypeStruct((b, n_gates, t, c), BF16),
                          jax.ShapeDtypeStruct((b, 1, t, c), splits[0][1])]
    return pl.pallas_call(
        functools.partial(_norm_proj_body, tuple(splits), conv is not None, row_of),
        grid=(b, t // tm),
        in_specs=in_specs,
        out_specs=out_specs,
        out_shape=out_shape,
        compiler_params=_params("parallel", "parallel"),
        name="norm_proj",
    )(*args)


def _mix_ffn_body(final_norm, row_of, x_ref, hy_ref, fn_ref, na_ref, wo_ref, g1_ref, ng_ref, sc_ref, sh_ref,
                  g2_ref, wg_ref, wu_ref, wd_ref, fg_ref, o_ref):
    g1, g2 = _mod_row(g1_ref, row_of), _mod_row(g2_ref, row_of)
    sc, sh = _mod_row(sc_ref, row_of), _mod_row(sh_ref, row_of)
    tm = x_ref.shape[1]
    sub = tm // FFN_ROW_SPLIT if tm % (FFN_ROW_SPLIT * MXU_WIDTH) == 0 else tm
    halves = [slice(r0, r0 + sub) for r0 in range(0, tm, sub)]
    x1s, hs = [], []
    for rows in halves:
        mix = (_bdot(hy_ref[0, rows], wo_ref[0, :D_HYENA])
               + _bdot(fn_ref[0, rows], wo_ref[0, D_HYENA:D_HYENA + D_FNET])
               + _bdot(na_ref[0, rows], wo_ref[0, D_HYENA + D_FNET:]))
        x1 = x_ref[0, rows] + g1 * mix
        x1s.append(x1)
        hs.append(_modulated_norm(x1, ng_ref[0], sc, sh).astype(BF16))
    accs = [None] * len(halves)
    for c0 in range(0, wg_ref.shape[2], MXU_WIDTH):
        cols = slice(c0, c0 + MXU_WIDTH)
        for s, h in enumerate(hs):
            a = _bdot(h, wg_ref[0, :, cols])
            u = _bdot(h, wu_ref[0, :, cols])
            act = (a * jax.nn.sigmoid(a) * u).astype(BF16)
            part = _bdot(act, wd_ref[0, cols, :])
            accs[s] = part if accs[s] is None else accs[s] + part
    for rows, x1, acc in zip(halves, x1s, accs):
        y = x1 + g2 * acc
        if final_norm:
            ms = jnp.mean(y * y, axis=-1, keepdims=True)
            y = y * lax.rsqrt(ms + EPS) * fg_ref[...]
        o_ref[0, rows] = y


def _mix_ffn(x, y_hy, y_fn, y_na, w_out, norm_g, mod, row_of, layer, wg, wu, wd, final_g, final_norm, tm):
    b, t, d = x.shape
    dff = wg.shape[2]
    assert dff % MXU_WIDTH == 0
    tm = min(tm, t)
    tok = lambda wd_: pl.BlockSpec((1, tm, wd_), lambda i, j: (i, j, 0))
    resident = lambda shape: pl.BlockSpec((1,) + shape, lambda i, j: (layer, 0, 0), pipeline_mode=pl.Buffered(1))
    return pl.pallas_call(
        functools.partial(_mix_ffn_body, final_norm, row_of),
        grid=(b, t // tm),
        in_specs=[
            tok(d), tok(D_HYENA), tok(D_FNET), tok(D_NA),
            resident((d, d)),
            _mod_spec(mod, layer, 2, d),
            pl.BlockSpec((1, 1, d), lambda i, j: (layer, 0, 0)),
            _mod_spec(mod, layer, 4, d),
            _mod_spec(mod, layer, 3, d),
            _mod_spec(mod, layer, 5, d),
            resident((d, dff)), resident((d, dff)), resident((dff, d)),
            pl.BlockSpec((1, d), lambda i, j: (0, 0)),
        ],
        out_specs=tok(d),
        out_shape=jax.ShapeDtypeStruct(x.shape, F32),
        compiler_params=_params("parallel", "parallel"),
        name="mix_ffn",
    )(x, y_hy, y_fn, y_na, w_out, mod, norm_g, mod, mod, mod, wg, wu, wd, final_g)


def _split_seq(n_total):
    log = int(round(math.log2(n_total)))
    assert 2 ** log == n_total
    n1 = 2 ** (log // 2)
    return n1, n_total // n1


@functools.lru_cache(maxsize=None)
def _hyena_consts(seq):
    n = 2 * seq
    n1, n2 = _split_seq(n)
    h1 = n1 // 2
    f1 = np.arange(h1)[:, None]
    t1 = np.arange(h1)[None, :]
    th = np.pi * (2 * f1 + 1) * t1 / n1
    m1 = np.empty((n1, h1))
    m1[0::2] = np.cos(th)
    m1[1::2] = -np.sin(th)
    m3 = m1.T * (2.0 / n)
    f = (np.arange(h1)[:, None, None] + n1 * np.arange(n2)[None, :, None])
    t2 = np.arange(n2)[None, None, :]
    ph = np.pi * ((2 * f + 1) * t2 % (2 * n)) / n
    c, s = np.cos(ph), np.sin(ph)
    m2 = np.concatenate([np.concatenate([c, s], axis=2), np.concatenate([-s, c], axis=2)], axis=1)
    m2t = np.transpose(m2, (0, 2, 1))
    m1k = np.kron(m1, np.eye(F32_SUBLANES))
    m3k = np.kron(m3, np.eye(BF16_SUBLANES))
    return n1, n2, m1k, m2, m2t, m3k


@functools.lru_cache(maxsize=None)
def _fnet_consts(seq):
    n1, n2 = _split_seq(seq)
    n1, n2 = n1 // 2, n2 * 2
    f1 = np.arange(n1)[:, None]
    t1 = np.arange(n1)[None, :]
    th = 2 * np.pi * (f1 * t1 % n1) / n1
    m1 = np.empty((2 * n1, n1))
    m1[0::2] = np.cos(th)
    m1[1::2] = -np.sin(th)
    f = (np.arange(n1)[:, None, None] + n1 * np.arange(n2)[None, :, None])
    t2 = np.arange(n2)[None, None, :]
    ph = 2 * np.pi * (f * t2 % seq) / seq
    c, s = np.cos(ph), np.sin(ph)
    m2 = np.concatenate([np.concatenate([c, s], axis=2), np.concatenate([-s, c], axis=2)], axis=1)
    gd = FNET_GROUP_DIM
    cc = np.arange(gd)
    pg = 2 * np.pi * (np.outer(cc, cc) % gd) / gd
    scale = 1.0 / math.sqrt(gd * seq)
    eye = np.eye(D_FNET // gd)
    bdc = np.kron(eye, np.cos(pg)) * scale
    bds = np.kron(eye, np.sin(pg)) * scale
    m1k = np.kron(m1, np.eye(BF16_SUBLANES))
    g = F32_SUBLANES
    perm = np.zeros((g * n1, n1 * g))
    for f2 in range(g):
        for ff in range(n1):
            perm[f2 * n1 + ff, ff * g + f2] = 1.0
    return n1, n2, m1k, m2, perm, bdc, bds


def _mxu_const(a):
    return jnp.asarray(a, dtype=F32).astype(BF16)


def _stage1_into(mk_ref, x_at, a_scr, group):
    nf, _, n2, c = a_scr.shape

    def body(s, carry):
        r0 = pl.multiple_of(s * group, group)
        xs = x_at(r0)
        xs = xs.reshape(xs.shape[0] * group, c).astype(BF16)
        a_scr[:, :, pl.ds(r0, group), :] = _bdot(mk_ref[...], xs).reshape(nf, 2, group, c)
        return carry

    lax.fori_loop(0, n2 // group, body, 0, unroll=True)


def _stage2_rows(a_scr, f1):
    _, _, n2, c = a_scr.shape
    return a_scr[f1].reshape(2 * n2, c).astype(BF16)


def _filter_body(zt_ref, t_ref, w1t_ref, b1_ref, fr_ref, w2t_ref, b2_ref, w3_ref, dl_ref, m1k_ref, m2_ref,
                 kr_ref, ki_ref, h_scr, kt_scr, af_scr, ab_scr):
    o, j = pl.program_id(0), pl.program_id(1)
    fb = m2_ref.shape[0]
    h1, _, n2, c = af_scr.shape
    g = F32_SUBLANES

    @pl.when((o == 0) & (j == 0))
    def _():
        h = jnp.sin(fr_ref[:, 0:1] * (_dot3(w1t_ref[...], zt_ref[...]) + b1_ref[...]))
        h = jnp.sin(fr_ref[:, 1:2] * (_dot3(w2t_ref[...], h) + b2_ref[...]))
        h_scr[...] = h.T

    @pl.when(j == 0)
    def _():
        k = _dot3(h_scr[...], w3_ref[0])
        decay = jnp.exp(-t_ref[...] * jnp.abs(dl_ref[...]))
        kf = k[:, :c] * decay
        row = lax.broadcasted_iota(jnp.int32, kf.shape, 0)
        kb = jnp.where(row == 0, 0.0, k[:, c:] * decay)
        norm = jnp.sum(jnp.abs(kf), axis=0, keepdims=True) + jnp.sum(jnp.abs(kb), axis=0, keepdims=True)
        inv = 1.0 / norm
        kt_scr[0] = (kf * inv).reshape(h1, n2, c)
        kt_scr[1] = (kb * inv).reshape(h1, n2, c)
        _stage1_into(m1k_ref, lambda r0: kt_scr[0, :, pl.ds(r0, g), :], af_scr, g)
        _stage1_into(m1k_ref, lambda r0: kt_scr[1, :, pl.ds(r0, g), :], ab_scr, g)

    for i in range(fb):
        f1 = j * fb + i
        xf = _bdot(m2_ref[i], _stage2_rows(af_scr, f1))
        xb = _bdot(m2_ref[i], _stage2_rows(ab_scr, f1))
        kr_ref[0, i] = (xf[:n2] + xb[:n2]).astype(kr_ref.dtype)
        ki_ref[0, i] = (xf[n2:] - xb[n2:]).astype(ki_ref.dtype)


def _hyena_filter_spectrum(seq, w1, b1, freq, w2, b2, w3, m1k, m2, fb):
    c = D_HYENA
    h1, n2 = m2.shape[0], m2.shape[1] // 2
    t = np.linspace(0.0, 1.0, seq)[:, None]
    w = 2.0 * np.pi * np.arange(seq)[:, None] / seq
    f = np.linspace(1e-4, POS_BANDS - 1, POS_BANDS)[None]
    z = np.concatenate([t, np.cos(f * w), -np.sin(f * w)], axis=-1).astype(np.float32)
    pe = z.shape[1]
    zt = jnp.asarray(np.pad(z, ((0, 0), (0, LANES - pe))).T)
    tcol = jnp.asarray(t.astype(np.float32))
    w1t = jnp.pad(w1, ((0, LANES - pe), (0, 0))).T
    max_decay = math.log(HYENA_DECAY_TARGET) / HYENA_FAST_PCT
    min_decay = math.log(HYENA_DECAY_TARGET) / HYENA_SLOW_PCT
    deltas = jnp.asarray(np.linspace(min_decay, max_decay, c, dtype=np.float32)[None])
    hid = w1.shape[1]
    w3r = jnp.transpose(w3.reshape(hid, 2, 2 * c), (1, 0, 2))
    full = lambda a: pl.BlockSpec(a.shape, lambda o, j: (0,) * a.ndim)
    b1c, b2c, frt = b1.reshape(hid, 1), b2.reshape(hid, 1), freq.T
    out = jax.ShapeDtypeStruct((2, h1, n2, c), BF16)
    return pl.pallas_call(
        _filter_body,
        grid=(2, h1 // fb),
        in_specs=[full(zt), full(tcol), full(w1t), full(b1c), full(frt), full(w2.T), full(b2c),
                  pl.BlockSpec((1, hid, 2 * c), lambda o, j: (o, 0, 0)), full(deltas), full(m1k),
                  pl.BlockSpec((fb, 2 * n2, 2 * n2), lambda o, j: (j, 0, 0))],
        out_specs=[pl.BlockSpec((1, fb, n2, c), lambda o, j: (o, j, 0, 0))] * 2,
        out_shape=[out, out],
        scratch_shapes=[pltpu.VMEM((seq, hid), F32), pltpu.VMEM((2, h1, n2, c), F32),
                        pltpu.VMEM((h1, 2, n2, c), F32), pltpu.VMEM((h1, 2, n2, c), F32)],
        compiler_params=_params("arbitrary", "arbitrary"),
        name="hyena_filter_spectrum",
    )(zt, tcol, w1t, b1c, frt, w2.T, b2c, w3r, deltas, m1k, m2)


def _long_conv_body(m1k_ref, m2_ref, m2t_ref, m3k_ref, v_ref, gate_ref, kr_ref, ki_ref, skip_ref, o_ref,
                    a_scr, b_scr, z_scr):
    h1, _, n2, c = a_scr.shape

    def gated_conv(order, src, store):
        g = F32_SUBLANES
        _stage1_into(m1k_ref, lambda r0: src(pl.ds(r0, g)), a_scr, g)

        def per_f1(f1, carry):
            x = _bdot(m2_ref[f1], _stage2_rows(a_scr, f1))
            xr, xi = x[:n2], x[n2:]
            kr, ki = kr_ref[order, f1].astype(F32), ki_ref[order, f1].astype(F32)
            y = jnp.concatenate([kr * xr - ki * xi, kr * xi + ki * xr], axis=0).astype(BF16)
            b_scr[pl.ds(2 * f1, 2)] = _bdot(m2t_ref[f1], y).astype(BF16).reshape(2, n2, c)
            return carry

        lax.fori_loop(0, h1, per_f1, 0, unroll=True)

        g = BF16_SUBLANES
        for s in range(n2 // g):
            rows = slice(s * g, (s + 1) * g)
            bs = b_scr[:, rows, :].reshape(2 * h1 * g, c)
            y = _bdot(m3k_ref[...], bs).reshape(h1, g, c)
            store(rows, gate_ref[0, order, :, rows, :] * (y + src(rows) * skip_ref[order:order + 1]))

    def store_z(rows, z):
        z_scr[:, rows, :] = z

    def store_out(rows, y):
        o_ref[0, :, rows, :] = y.astype(o_ref.dtype)

    gated_conv(0, lambda rows: v_ref[0, 0, :, rows, :], store_z)
    gated_conv(1, lambda rows: z_scr[:, rows, :], store_out)


def _hyena_mixer(gates, v, w1, b1, freq, w2, b2, w3, skip):
    b, _, seq, c = v.shape
    n1, n2, m1k, m2, m2t, m3k = _hyena_consts(seq)
    h1 = n1 // 2
    m1k, m2, m2t, m3k = (_mxu_const(a) for a in (m1k, m2, m2t, m3k))

    kr, ki = _hyena_filter_spectrum(seq, w1, b1, freq, w2, b2, w3, m1k, m2, min(h1, 8))

    resident = lambda a: pl.BlockSpec(a.shape, lambda i: (0,) * a.ndim, pipeline_mode=pl.Buffered(1))
    y = pl.pallas_call(
        _long_conv_body,
        grid=(b,),
        in_specs=[resident(m1k), resident(m2), resident(m2t), resident(m3k),
                  pl.BlockSpec((1, 1, h1, n2, c), lambda i: (i, 0, 0, 0, 0)),
                  pl.BlockSpec((1, 2, h1, n2, c), lambda i: (i, 0, 0, 0, 0)),
                  resident(kr), resident(ki), resident(skip)],
        out_specs=pl.BlockSpec((1, h1, n2, c), lambda i: (i, 0, 0, 0)),
        out_shape=jax.ShapeDtypeStruct((b, h1, n2, c), BF16),
        scratch_shapes=[pltpu.VMEM((h1, 2, n2, c), F32), pltpu.VMEM((2 * h1, n2, c), BF16),
                        pltpu.VMEM((h1, n2, c), F32)],
        compiler_params=pltpu.CompilerParams(dimension_semantics=("parallel",), vmem_limit_bytes=LONG_CONV_VMEM),
        name="hyena_long_conv",
    )(m1k, m2, m2t, m3k, v.reshape(b, 1, h1, n2, c), gates.reshape(b, 2, h1, n2, c), kr, ki, skip)
    return y.reshape(b, seq, c)


def _fnet_body(m1k_ref, m2_ref, perm_ref, bdc_ref, bds_ref, g_ref, o_ref, a_scr):
    j = pl.program_id(1)
    fb = m2_ref.shape[0]
    n1, _, n2, c = a_scr.shape

    @pl.when(j == 0)
    def _():
        gi = BF16_SUBLANES
        _stage1_into(m1k_ref, lambda r0: g_ref[0, :, pl.ds(r0, gi), :], a_scr, gi)

    for i in range(fb):
        f1 = j * fb + i
        a_scr[f1] = _bdot(m2_ref[i], _stage2_rows(a_scr, f1)).reshape(2, n2, c)

    @pl.when(j == pl.num_programs(1) - 1)
    def _():
        go = F32_SUBLANES

        def body(s, carry):
            r0 = pl.multiple_of(s * go, go)
            re = a_scr[:, 0, pl.ds(r0, go), :].reshape(n1 * go, c).astype(BF16)
            im = a_scr[:, 1, pl.ds(r0, go), :].reshape(n1 * go, c).astype(BF16)
            xr = _bdot(perm_ref[...], re).astype(BF16)
            xi = _bdot(perm_ref[...], im).astype(BF16)
            out = _bdot(xr, bdc_ref[...]) + _bdot(xi, bds_ref[...])
            o_ref[0, pl.ds(pl.multiple_of(s * go * n1, go * n1), go * n1), :] = out.astype(o_ref.dtype)
            return carry

        lax.fori_loop(0, n2 // go, body, 0, unroll=True)


def _fourier_mix(g):
    b, seq, c = g.shape
    n1, n2, m1k, m2, perm, bdc, bds = _fnet_consts(seq)
    m1k, m2, perm, bdc, bds = (_mxu_const(a) for a in (m1k, m2, perm, bdc, bds))
    fb = min(n1, 8)
    const = lambda a: pl.BlockSpec(a.shape, lambda i, j: (0,) * a.ndim)
    return pl.pallas_call(
        _fnet_body,
        grid=(b, n1 // fb),
        in_specs=[const(m1k),
                  pl.BlockSpec((fb, 2 * n2, 2 * n2), lambda i, j: (j, 0, 0)),
                  const(perm), const(bdc), const(bds),
                  pl.BlockSpec((1, n1, n2, c), lambda i, j: (i, 0, 0, 0))],
        out_specs=pl.BlockSpec((1, seq, c), lambda i, j: (i, 0, 0)),
        out_shape=jax.ShapeDtypeStruct((b, seq, c), BF16),
        scratch_shapes=[pltpu.VMEM((n1, 2, n2, c), F32)],
        compiler_params=_params("parallel", "arbitrary"),
        name="fnet",
    )(m1k, m2, perm, bdc, bds, g.reshape(b, n1, n2, c))


def _dot_nt(a, b):
    return lax.dot_general(a, b, (((1,), (1,)), ((), ())), preferred_element_type=F32)


def _head_pair_attention(q2, keys, values, biases):
    nq = q2.shape[0]
    lane = lax.broadcasted_iota(jnp.int32, q2.shape, 1)
    zero = jnp.zeros_like(q2)
    qs = jnp.concatenate([jnp.where(lane < HEAD_DIM, q2, zero), jnp.where(lane >= HEAD_DIM, q2, zero)], axis=0)
    m = denom = acc = None
    for gi, (kk, vv) in enumerate(zip(keys, values)):
        s = _dot_nt(qs, kk)
        if biases[0][gi] is not None:
            s = s + jnp.concatenate([biases[0][gi], biases[1][gi]], axis=0)
        m_grp = s.max(axis=-1, keepdims=True)
        m_new = m_grp if m is None else jnp.maximum(m, m_grp)
        e = jnp.exp2(s - m_new)
        e_part = e[:, :LANES]
        for c0 in range(LANES, e.shape[1], LANES):
            e_part = e_part + e[:, c0:c0 + LANES]
        pv = _bdot(e.astype(BF16), vv)
        if m is None:
            denom, acc = e_part, pv
        else:
            alpha = jnp.exp2(m - m_new)
            denom = alpha * denom + e_part
            acc = alpha * acc + pv
        m = m_new
    out = acc / denom.sum(axis=-1, keepdims=True)
    return jnp.where(lane < HEAD_DIM, out[:nq], out[nq:])


def _na_window_base(row_block, rows):
    return jnp.clip(row_block * NA_QROWS - NA_KH // 2, 0, rows - NA_KROWS)


def _na_body(rows, q_ref, k_ref, v_ref, kc_ref, vc_ref, bias_ref, o_ref):
    base = _na_window_base(pl.program_id(1), rows)
    start = pl.multiple_of(base * GRID_W, GRID_W)
    n_keys = NA_KROWS * GRID_W
    n_pairs = q_ref.shape[2] // LANES
    groups = [(c0, min(NA_KEY_GROUP, n_keys - c0)) for c0 in range(0, n_keys, NA_KEY_GROUP)]
    for p in range(n_pairs):
        sl = slice(p * LANES, (p + 1) * LANES)
        q2 = q_ref[0, :, sl]
        keys = [kc_ref[0, :, sl]] + [k_ref[0, pl.ds(start + c0, n), sl] for c0, n in groups]
        values = [vc_ref[0, :, sl]] + [v_ref[0, pl.ds(start + c0, n), sl] for c0, n in groups]
        biases = [[None] + [bias_ref[0, 2 * p + hh, 0, :, c0:c0 + n] for c0, n in groups] for hh in range(2)]
        o = _head_pair_attention(q2, keys, values, biases)
        o_ref[0, :, sl] = o.astype(o_ref.dtype)


def _na_bias_body(rows, p_ref, o_ref):
    kh, kw, w = NA_KH, NA_KW, GRID_W
    shape = (w, 2 * w)
    qc = lax.broadcasted_iota(jnp.int32, shape, 0)
    lane = lax.broadcasted_iota(jnp.int32, shape, 1)
    kc = lane & (w - 1)
    col_start = jnp.clip(qc - kw // 2, 0, w - kw)
    col_valid = (kc >= col_start) & (kc < col_start + kw)
    low_half = lane < w
    masked = jnp.full(shape, MASK_VALUE, F32)
    tiles = []
    for a in range(2 * kh - 1):
        row = jnp.broadcast_to(p_ref[0, 0, a:a + 1, :], shape)
        t = pltpu.roll(row, w + 1, 1, stride=1, stride_axis=0)
        t = jnp.where(low_half, t, pltpu.roll(t, w, 1))
        tiles.append(jnp.where(col_valid, t, masked))
    for case, r0 in enumerate((0, NA_QROWS, rows - NA_QROWS)):
        base = int(np.clip(r0 - kh // 2, 0, rows - NA_KROWS))
        for qr in range(NA_QROWS):
            q_abs = r0 + qr
            r_start = int(np.clip(q_abs - kh // 2, 0, rows - kh))

            def tile_for(kr):
                k_abs = base + kr
                if kr >= NA_KROWS or not (r_start <= k_abs < r_start + kh):
                    return masked
                return tiles[k_abs - q_abs + kh - 1]

            q_rows = slice(qr * w, (qr + 1) * w)
            for pair in range((NA_KROWS + 1) // 2):
                chunk = jnp.where(low_half, tile_for(2 * pair), tile_for(2 * pair + 1))
                width = min(2 * w, NA_KROWS * w - pair * 2 * w)
                o_ref[0, 0, case, q_rows, pair * 2 * w:pair * 2 * w + width] = chunk[:, :width]


def _na_bias_tables(rpb, rows):
    w = GRID_W
    depth, nh, na, nb = rpb.shape
    assert 2 * w == LANES and nb == 2 * NA_KW - 1
    lo = w - NA_KW
    padded = jnp.pad(rpb.astype(F32) * LOG2E, ((0, 0), (0, 0), (0, (-na) % 8), (lo, 2 * w - nb - lo)))
    return pl.pallas_call(
        functools.partial(_na_bias_body, rows),
        grid=(depth, nh),
        in_specs=[pl.BlockSpec((1, 1) + padded.shape[2:], lambda l, h: (l, h, 0, 0))],
        out_specs=pl.BlockSpec((1, 1, 3, NA_QROWS * w, NA_KROWS * w), lambda l, h: (l, h, 0, 0, 0)),
        out_shape=jax.ShapeDtypeStruct((depth, nh, 3, NA_QROWS * w, NA_KROWS * w), F32),
        compiler_params=_params("parallel", "parallel"),
        name="na_bias_tables",
    )(padded)


def _neighbourhood_attention(q, kv, kv_ctx, bias, layer):
    b, seq, _ = q.shape
    rows = seq // GRID_W
    lc = kv_ctx.shape[1]
    assert rows % NA_QROWS == 0 and rows >= NA_KROWS + 1 and NA_QROWS == NA_KH // 2
    n_blocks = rows // NA_QROWS
    nh = bias.shape[1]
    tq = NA_QROWS * GRID_W

    def bias_map(i, r):
        return (layer, 0, jnp.where(r == 0, 0, jnp.where(r == n_blocks - 1, 2, 1)), 0, 0)

    return pl.pallas_call(
        functools.partial(_na_body, rows),
        grid=(b, n_blocks),
        in_specs=[pl.BlockSpec((1, tq, D_NA), lambda i, r: (i, r, 0)),
                  pl.BlockSpec((1, seq, D_NA), lambda i, r: (i, 0, 0)),
                  pl.BlockSpec((1, seq, D_NA), lambda i, r: (i, 0, 1)),
                  pl.BlockSpec((1, lc, D_NA), lambda i, r: (i, 0, 0)),
                  pl.BlockSpec((1, lc, D_NA), lambda i, r: (i, 0, 1)),
                  pl.BlockSpec((1, nh, 1, tq, NA_KROWS * GRID_W), bias_map)],
        out_specs=pl.BlockSpec((1, tq, D_NA), lambda i, r: (i, r, 0)),
        out_shape=jax.ShapeDtypeStruct((b, seq, D_NA), BF16),
        compiler_params=_params("parallel", "arbitrary"),
        name="neighbourhood_attention",
    )(q, kv, kv, kv_ctx, kv_ctx, bias)


def _ctx_attn_body(q_ref, k_ref, v_ref, o_ref):
    for p in range(q_ref.shape[2] // LANES):
        sl = slice(p * LANES, (p + 1) * LANES)
        o = _head_pair_attention(q_ref[0, :, sl], [k_ref[0, :, sl]], [v_ref[0, :, sl]], [[None], [None]])
        o_ref[0, :, sl] = o.astype(o_ref.dtype)


def _context_attention(q, kv):
    b, lc, _ = q.shape
    return pl.pallas_call(
        _ctx_attn_body,
        grid=(b,),
        in_specs=[pl.BlockSpec((1, lc, D_NA), lambda i: (i, 0, 0)),
                  pl.BlockSpec((1, lc, D_NA), lambda i: (i, 0, 0)),
                  pl.BlockSpec((1, lc, D_NA), lambda i: (i, 0, 1))],
        out_specs=pl.BlockSpec((1, lc, D_NA), lambda i: (i, 0, 0)),
        out_shape=jax.ShapeDtypeStruct((b, lc, D_NA), BF16),
        compiler_params=_params("parallel"),
        name="context_attention",
    )(q, kv, kv)


def kernel(x, c, ctx, c_ctx, w_mod, b_mod, norm1_g, norm2_g, w_in, w_out, hy_conv_w, hy_conv_b,
           filt_w1, filt_b1, filt_freq, filt_w2, filt_b2, filt_w3, hy_skip, na_rpb,
           w_gate, w_up, w_down, final_g):
    depth = w_mod.shape[0]
    b, seq, d = x.shape
    off_k = 3 * D_HYENA + D_FNET + D_NA
    qk_scale = LOG2E / math.sqrt(HEAD_DIM)
    splits = ((3 * D_HYENA, F32, None), (D_FNET, BF16, None), (D_NA, BF16, qk_scale), (2 * D_NA, BF16, None))
    tm_proj, tm = 1024, 1024

    pad = (-(b + 1)) % 8
    c_all = jnp.concatenate([c, c_ctx[None], jnp.zeros((pad, d), F32)], axis=0)
    mod = _modulation(c_all, w_mod, b_mod)
    lat_row = lambda i: i
    ctx_row = lambda i: b

    w_in_b, w_out_b = w_in.astype(BF16), w_out.astype(BF16)
    w_gate_b, w_up_b, w_down_b = w_gate.astype(BF16), w_up.astype(BF16), w_down.astype(BF16)
    w_kv_last = w_in_b[depth - 1:, :, off_k:]
    n1g, n2g = norm1_g.reshape(depth, 1, d), norm2_g.reshape(depth, 1, d)
    fg = final_g.reshape(1, d)
    na_bias = _na_bias_tables(na_rpb, seq // GRID_W)

    for l in range(depth):
        last = l == depth - 1
        hy_params = (filt_w1[l], filt_b1[l], filt_freq[l], filt_w2[l], filt_b2[l], filt_w3[l], hy_skip[l])
        conv = (hy_conv_w[l], hy_conv_b[l].reshape(1, 3 * D_HYENA))

        gates, v, fn, q, kv = _norm_proj(x, n1g, mod, lat_row, l, w_in_b, l, splits, tm_proj, conv)
        if last:
            (kv_c,) = _norm_proj(ctx.reshape(1, -1, d), n1g, mod, ctx_row, l, w_kv_last, 0,
                                 ((2 * D_NA, BF16, None),), tm_proj)
            kv_c = kv_c.reshape(b, -1, 2 * D_NA)
        else:
            cgates, cv, cfn, cq, kv_c = _norm_proj(ctx, n1g, mod, ctx_row, l, w_in_b, l, splits, tm_proj, conv)

        y_hy = _hyena_mixer(gates, v, *hy_params)
        y_fn = _fourier_mix(fn)
        y_na = _neighbourhood_attention(q, kv, kv_c, na_bias, l)

        x = _mix_ffn(x, y_hy, y_fn, y_na, w_out_b, n2g, mod, lat_row, l, w_gate_b, w_up_b, w_down_b,
                     fg, last, tm)

        if not last:
            cy_hy = _hyena_mixer(cgates, cv, *hy_params)
            cy_fn = _fourier_mix(cfn)
            cy_na = _context_attention(cq, kv_c)
            flat = lambda a: a.reshape(1, -1, a.shape[-1])
            ctx = _mix_ffn(flat(ctx), flat(cy_hy), flat(cy_fn), flat(cy_na), w_out_b, n2g, mod, ctx_row, l,
                           w_gate_b, w_up_b, w_down_b, fg, False, tm).reshape(ctx.shape)
    return x
```

```python
import functools
import math

import numpy as np
import jax
import jax.numpy as jnp
from jax import lax
from jax.experimental import pallas as pl
from jax.experimental.pallas import tpu as pltpu

F32 = jnp.float32
BF16 = jnp.bfloat16

EPS = 1e-6
GRID_W = 64
HEAD_DIM = 64
D_HYENA = 256
D_FNET = 256
D_NA = 512
FNET_GROUP_DIM = 64
NA_KH = 8
NA_KW = 16
NA_QROWS = 4
NA_KROWS = NA_QROWS + NA_KH
NA_KEY_GROUP = 256
POS_BANDS = 16
HYENA_DECAY_TARGET = 1e-2
HYENA_FAST_PCT = 0.3
HYENA_SLOW_PCT = 1.5
MASK_VALUE = -1e30
VMEM_LIMIT = 56 * 1024 * 1024
LONG_CONV_VMEM = 60 * 1024 * 1024
LANES = 128
MXU_WIDTH = 256
LOG2E = math.log2(math.e)
FFN_ROW_SPLIT = 2
F32_SUBLANES = 8
BF16_SUBLANES = 16


def _params(*sem):
    return pltpu.CompilerParams(dimension_semantics=sem, vmem_limit_bytes=VMEM_LIMIT)


def _bdot(a, b):
    return jnp.dot(a, b, preferred_element_type=F32)


def _split_bf16(a):
    hi = a.astype(BF16)
    lo = (a - hi.astype(F32)).astype(BF16)
    return hi, lo


def _dot3(a, b):
    a_hi, a_lo = _split_bf16(a)
    b_hi, b_lo = _split_bf16(b)
    return _bdot(a_hi, b_hi) + _bdot(a_lo, b_hi) + _bdot(a_hi, b_lo)


def _mod_body(c_ref, w_ref, b_ref, o_ref):
    c = c_ref[...]
    rows = c.shape[0]
    s_hi, s_lo = _split_bf16(c * jax.nn.sigmoid(c))
    w_hi, w_lo = _split_bf16(w_ref[0])
    both = _bdot(jnp.concatenate([s_hi, s_lo], axis=0), w_hi)
    o_ref[0] = both[:rows] + both[rows:] + _bdot(s_hi, w_lo) + b_ref[0]


def _modulation(c_all, w_mod, b_mod):
    depth, d, n = w_mod.shape
    rows = c_all.shape[0]
    tn = n // 2
    return pl.pallas_call(
        _mod_body,
        grid=(depth, n // tn),
        in_specs=[
            pl.BlockSpec((rows, d), lambda l, j: (0, 0)),
            pl.BlockSpec((1, d, tn), lambda l, j: (l, 0, j)),
            pl.BlockSpec((1, 1, tn), lambda l, j: (l, 0, j)),
        ],
        out_specs=pl.BlockSpec((1, rows, tn), lambda l, j: (l, 0, j)),
        out_shape=jax.ShapeDtypeStruct((depth, rows, n), F32),
        compiler_params=_params("parallel", "parallel"),
        name="modulation",
    )(c_all, w_mod, b_mod.reshape(depth, 1, n))


def _modulated_norm(x, g, sc, sh):
    ms = jnp.mean(x * x, axis=-1, keepdims=True)
    return x * lax.rsqrt(ms + EPS) * (g * (1.0 + sc)) + sh


def _mod_spec(mod, layer, chunk, d):
    return pl.BlockSpec((1, mod.shape[1], d), lambda i, *_: (layer, 0, chunk))


def _mod_row(ref, row_of):
    return ref[0, pl.ds(row_of(pl.program_id(0)), 1), :]


def _short_conv_tile(u, prev_row, next_row, cw_ref, cb_ref):
    rows = u.shape[0]
    row = lax.broadcasted_iota(jnp.int32, u.shape, 0)
    prev = jnp.where(row == 0, prev_row, pltpu.roll(u, 1, axis=0))
    nxt = jnp.where(row == rows - 1, next_row, pltpu.roll(u, rows - 1, axis=0))
    return prev * cw_ref[0:1] + u * cw_ref[1:2] + nxt * cw_ref[2:3] + cb_ref[...]


def _norm_proj_body(splits, conv, row_of, x_ref, g_ref, sc_ref, sh_ref, w_ref, *rest):
    o_refs = ((rest[4:6],) + rest[6:]) if conv else rest
    g, sc, sh = g_ref[0], _mod_row(sc_ref, row_of), _mod_row(sh_ref, row_of)
    tm = x_ref.shape[1]
    edge = F32_SUBLANES if conv else 0
    if conv:
        xp_ref, xn_ref, cw_ref, cb_ref = rest[:4]
        x_all = jnp.concatenate([xp_ref[0], x_ref[0], xn_ref[0]], axis=0)
    else:
        x_all = x_ref[0]
    p_all = _bdot(_modulated_norm(x_all, g, sc, sh).astype(BF16), w_ref[0])
    p = p_all[edge:edge + tm]
    off = 0
    for idx, (o_ref, (width, _, scale)) in enumerate(zip(o_refs, splits)):
        part = p[:, off:off + width]
        if conv and idx == 0:
            gate_ref, v_ref = o_ref
            j, nj = pl.program_id(1), pl.num_programs(1)
            prev_row = jnp.where(j == 0, 0.0, p_all[edge - 1:edge, off:off + width])
            next_row = jnp.where(j == nj - 1, 0.0, p_all[edge + tm:edge + tm + 1, off:off + width])
            u = _short_conv_tile(part, prev_row, next_row, cw_ref, cb_ref)
            c = v_ref.shape[3]
            n_gates = gate_ref.shape[1]
            for gi in range(n_gates):
                gate_ref[0, gi] = u[:, gi * c:(gi + 1) * c].astype(gate_ref.dtype)
            v_ref[0, 0] = u[:, n_gates * c:]
        else:
            o_ref[0] = (part if scale is None else part * scale).astype(o_ref.dtype)
        off += width


def _norm_proj(x, norm_g, mod, row_of, layer, w, w_idx, splits, tm, conv=None):
    b, t, d = x.shape
    n = w.shape[2]
    tm = min(tm, t)
    edge = F32_SUBLANES
    tok = lambda wd: pl.BlockSpec((1, tm, wd), lambda i, j: (i, j, 0))
    in_specs = [
        tok(d),
        pl.BlockSpec((1, 1, d), lambda i, j: (layer, 0, 0)),
        _mod_spec(mod, layer, 1, d),
        _mod_spec(mod, layer, 0, d),
        pl.BlockSpec((1, d, n), lambda i, j: (w_idx, 0, 0)),
    ]
    args = [x, norm_g, mod, mod, w]
    out_specs = [tok(wd) for wd, _, _ in splits]
    out_shape = [jax.ShapeDtypeStruct((b, t, wd), dt) for wd, dt, _ in splits]
    if conv is not None:
        cw, cb = conv
        w0, c = splits[0][0], D_HYENA
        per_tile = tm // edge
        in_specs += [
            pl.BlockSpec((1, edge, d), lambda i, j: (i, jnp.maximum(j * per_tile - 1, 0), 0)),
            pl.BlockSpec((1, edge, d), lambda i, j: (i, jnp.minimum((j + 1) * per_tile, t // edge - 1), 0)),
            pl.BlockSpec(cw.shape, lambda i, j: (0, 0)),
            pl.BlockSpec(cb.shape, lambda i, j: (0, 0)),
        ]
        args += [x, x, cw, cb]
        n_gates = w0 // c - 1
        out_specs[0:1] = [pl.BlockSpec((1, n_gates, tm, c), lambda i, j: (i, 0, j, 0)),
                          pl.BlockSpec((1, 1, tm, c), lambda i, j: (i, 0, j, 0))]
        out_shape[0:1] = [jax.ShapeDtypeStruct((b, n_gates, t, c), BF16),
                          jax.ShapeDtypeStruct((b, 1, t, c), splits[0][1])]
    return pl.pallas_call(
        functools.partial(_norm_proj_body, tuple(splits), conv is not None, row_of),
        grid=(b, t // tm),
        in_specs=in_specs,
        out_specs=out_specs,
        out_shape=out_shape,
        compiler_params=_params("parallel", "parallel"),
        name="norm_proj",
    )(*args)


def _mix_ffn_body(final_norm, row_of, x_ref, hy_ref, fn_ref, na_ref, wo_ref, g1_ref, ng_ref, sc_ref, sh_ref,
                  g2_ref, wg_ref, wu_ref, wd_ref, fg_ref, o_ref):
    g1, g2 = _mod_row(g1_ref, row_of), _mod_row(g2_ref, row_of)
    sc, sh = _mod_row(sc_ref, row_of), _mod_row(sh_ref, row_of)
    tm = x_ref.shape[1]
    sub = tm // FFN_ROW_SPLIT if tm % (FFN_ROW_SPLIT * MXU_WIDTH) == 0 else tm
    halves = [slice(r0, r0 + sub) for r0 in range(0, tm, sub)]
    x1s, hs = [], []
    for rows in halves:
        mix = (_bdot(hy_ref[0, rows], wo_ref[0, :D_HYENA])
               + _bdot(fn_ref[0, rows], wo_ref[0, D_HYENA:D_HYENA + D_FNET])
               + _bdot(na_ref[0, rows], wo_ref[0, D_HYENA + D_FNET:]))
        x1 = x_ref[0, rows] + g1 * mix
        x1s.append(x1)
        hs.append(_modulated_norm(x1, ng_ref[0], sc, sh).astype(BF16))
    accs = [None] * len(halves)
    for c0 in range(0, wg_ref.shape[2], MXU_WIDTH):
        cols = slice(c0, c0 + MXU_WIDTH)
        for s, h in enumerate(hs):
            a = _bdot(h, wg_ref[0, :, cols])
            u = _bdot(h, wu_ref[0, :, cols])
            act = (a * jax.nn.sigmoid(a) * u).astype(BF16)
            part = _bdot(act, wd_ref[0, cols, :])
            accs[s] = part if accs[s] is None else accs[s] + part
    for rows, x1, acc in zip(halves, x1s, accs):
        y = x1 + g2 * acc
        if final_norm:
            ms = jnp.mean(y * y, axis=-1, keepdims=True)
            y = y * lax.rsqrt(ms + EPS) * fg_ref[...]
        o_ref[0, rows] = y


def _mix_ffn(x, y_hy, y_fn, y_na, w_out, norm_g, mod, row_of, layer, wg, wu, wd, final_g, final_norm, tm):
    b, t, d = x.shape
    dff = wg.shape[2]
    assert dff % MXU_WIDTH == 0
    tm = min(tm, t)
    tok = lambda wd_: pl.BlockSpec((1, tm, wd_), lambda i, j: (i, j, 0))
    resident = lambda shape: pl.BlockSpec((1,) + shape, lambda i, j: (layer, 0, 0), pipeline_mode=pl.Buffered(1))
    return pl.pallas_call(
        functools.partial(_mix_ffn_body, final_norm, row_of),
        grid=(b, t // tm),
        in_specs=[
            tok(d), tok(D_HYENA), tok(D_FNET), tok(D_NA),
            resident((d, d)),
            _mod_spec(mod, layer, 2, d),
            pl.BlockSpec((1, 1, d), lambda i, j: (layer, 0, 0)),
            _mod_spec(mod, layer, 4, d),
            _mod_spec(mod, layer, 3, d),
            _mod_spec(mod, layer, 5, d),
            resident((d, dff)), resident((d, dff)), resident((dff, d)),
            pl.BlockSpec((1, d), lambda i, j: (0, 0)),
        ],
        out_specs=tok(d),
        out_shape=jax.ShapeDtypeStruct(x.shape, F32),
        compiler_params=_params("parallel", "parallel"),
        name="mix_ffn",
    )(x, y_hy, y_fn, y_na, w_out, mod, norm_g, mod, mod, mod, wg, wu, wd, final_g)


def _split_seq(n_total):
    log = int(round(math.log2(n_total)))
    assert 2 ** log == n_total
    n1 = 2 ** (log // 2)
    return n1, n_total // n1


@functools.lru_cache(maxsize=None)
def _hyena_consts(seq):
    n = 2 * seq
    n1, n2 = _split_seq(n)
    h1 = n1 // 2
    f1 = np.arange(h1)[:, None]
    t1 = np.arange(h1)[None, :]
    th = np.pi * (2 * f1 + 1) * t1 / n1
    m1 = np.empty((n1, h1))
    m1[0::2] = np.cos(th)
    m1[1::2] = -np.sin(th)
    m3 = m1.T * (2.0 / n)
    f = (np.arange(h1)[:, None, None] + n1 * np.arange(n2)[None, :, None])
    t2 = np.arange(n2)[None, None, :]
    ph = np.pi * ((2 * f + 1) * t2 % (2 * n)) / n
    c, s = np.cos(ph), np.sin(ph)
    m2 = np.concatenate([np.concatenate([c, s], axis=2), np.concatenate([-s, c], axis=2)], axis=1)
    m2t = np.transpose(m2, (0, 2, 1))
    m1k = np.kron(m1, np.eye(F32_SUBLANES))
    m3k = np.kron(m3, np.eye(BF16_SUBLANES))
    return n1, n2, m1k, m2, m2t, m3k


@functools.lru_cache(maxsize=None)
def _fnet_consts(seq):
    n1, n2 = _split_seq(seq)
    n1, n2 = n1 // 2, n2 * 2
    f1 = np.arange(n1)[:, None]
    t1 = np.arange(n1)[None, :]
    th = 2 * np.pi * (f1 * t1 % n1) / n1
    m1 = np.empty((2 * n1, n1))
    m1[0::2] = np.cos(th)
    m1[1::2] = -np.sin(th)
    f = (np.arange(n1)[:, None, None] + n1 * np.arange(n2)[None, :, None])
    t2 = np.arange(n2)[None, None, :]
    ph = 2 * np.pi * (f * t2 % seq) / seq
    c, s = np.cos(ph), np.sin(ph)
    m2 = np.concatenate([np.concatenate([c, s], axis=2), np.concatenate([-s, c], axis=2)], axis=1)
    gd = FNET_GROUP_DIM
    cc = np.arange(gd)
    pg = 2 * np.pi * (np.outer(cc, cc) % gd) / gd
    scale = 1.0 / math.sqrt(gd * seq)
    eye = np.eye(D_FNET // gd)
    bdc = np.kron(eye, np.cos(pg)) * scale
    bds = np.kron(eye, np.sin(pg)) * scale
    m1k = np.kron(m1, np.eye(BF16_SUBLANES))
    g = F32_SUBLANES
    perm = np.zeros((g * n1, n1 * g))
    for f2 in range(g):
        for ff in range(n1):
            perm[f2 * n1 + ff, ff * g + f2] = 1.0
    return n1, n2, m1k, m2, perm, bdc, bds


def _mxu_const(a):
    return jnp.asarray(a, dtype=F32).astype(BF16)


def _stage1_into(mk_ref, x_at, a_scr, group):
    nf, _, n2, c = a_scr.shape

    def body(s, carry):
        r0 = pl.multiple_of(s * group, group)
        xs = x_at(r0)
        xs = xs.reshape(xs.shape[0] * group, c).astype(BF16)
        a_scr[:, :, pl.ds(r0, group), :] = _bdot(mk_ref[...], xs).reshape(nf, 2, group, c)
        return carry

    lax.fori_loop(0, n2 // group, body, 0, unroll=True)


def _stage2_rows(a_scr, f1):
    _, _, n2, c = a_scr.shape
    return a_scr[f1].reshape(2 * n2, c).astype(BF16)


def _filter_body(zt_ref, t_ref, w1t_ref, b1_ref, fr_ref, w2t_ref, b2_ref, w3_ref, dl_ref, m1k_ref, m2_ref,
                 kr_ref, ki_ref, h_scr, kt_scr, af_scr, ab_scr):
    o, j = pl.program_id(0), pl.program_id(1)
    fb = m2_ref.shape[0]
    h1, _, n2, c = af_scr.shape
    g = F32_SUBLANES

    @pl.when((o == 0) & (j == 0))
    def _():
        h = jnp.sin(fr_ref[:, 0:1] * (_dot3(w1t_ref[...], zt_ref[...]) + b1_ref[...]))
        h = jnp.sin(fr_ref[:, 1:2] * (_dot3(w2t_ref[...], h) + b2_ref[...]))
        h_scr[...] = h.T

    @pl.when(j == 0)
    def _():
        k = _dot3(h_scr[...], w3_ref[...])
        decay = jnp.exp(-t_ref[...] * jnp.abs(dl_ref[...]))
        kf = k[:, :c] * decay
        row = lax.broadcasted_iota(jnp.int32, kf.shape, 0)
        kb = jnp.where(row == 0, 0.0, k[:, c:] * decay)
        norm = jnp.sum(jnp.abs(kf), axis=0, keepdims=True) + jnp.sum(jnp.abs(kb), axis=0, keepdims=True)
        inv = 1.0 / norm
        kt_scr[0] = (kf * inv).reshape(h1, n2, c)
        kt_scr[1] = (kb * inv).reshape(h1, n2, c)
        _stage1_into(m1k_ref, lambda r0: kt_scr[0, :, pl.ds(r0, g), :], af_scr, g)
        _stage1_into(m1k_ref, lambda r0: kt_scr[1, :, pl.ds(r0, g), :], ab_scr, g)

    for i in range(fb):
        f1 = j * fb + i
        xf = _bdot(m2_ref[i], _stage2_rows(af_scr, f1))
        xb = _bdot(m2_ref[i], _stage2_rows(ab_scr, f1))
        kr_ref[0, i] = (xf[:n2] + xb[:n2]).astype(kr_ref.dtype)
        ki_ref[0, i] = (xf[n2:] - xb[n2:]).astype(ki_ref.dtype)


def _hyena_filter_spectrum(seq, w1, b1, freq, w2, b2, w3, m1k, m2, fb):
    c = D_HYENA
    h1, n2 = m2.shape[0], m2.shape[1] // 2
    t = np.linspace(0.0, 1.0, seq)[:, None]
    w = 2.0 * np.pi * np.arange(seq)[:, None] / seq
    f = np.linspace(1e-4, POS_BANDS - 1, POS_BANDS)[None]
    z = np.concatenate([t, np.cos(f * w), -np.sin(f * w)], axis=-1).astype(np.float32)
    pe = z.shape[1]
    zt = jnp.asarray(np.pad(z, ((0, 0), (0, LANES - pe))).T)
    tcol = jnp.asarray(t.astype(np.float32))
    w1t = jnp.pad(w1, ((0, LANES - pe), (0, 0))).T
    max_decay = math.log(HYENA_DECAY_TARGET) / HYENA_FAST_PCT
    min_decay = math.log(HYENA_DECAY_TARGET) / HYENA_SLOW_PCT
    deltas = jnp.asarray(np.linspace(min_decay, max_decay, c, dtype=np.float32)[None])
    hid = w1.shape[1]
    full = lambda a: pl.BlockSpec(a.shape, lambda o, j: (0,) * a.ndim)
    b1c, b2c, frt = b1.reshape(hid, 1), b2.reshape(hid, 1), freq.T
    out = jax.ShapeDtypeStruct((2, h1, n2, c), BF16)
    return pl.pallas_call(
        _filter_body,
        grid=(2, h1 // fb),
        in_specs=[full(zt), full(tcol), full(w1t), full(b1c), full(frt), full(w2.T), full(b2c),
                  pl.BlockSpec((hid, 2 * c), lambda o, j: (0, o)), full(deltas), full(m1k),
                  pl.BlockSpec((fb, 2 * n2, 2 * n2), lambda o, j: (j, 0, 0))],
        out_specs=[pl.BlockSpec((1, fb, n2, c), lambda o, j: (o, j, 0, 0))] * 2,
        out_shape=[out, out],
        scratch_shapes=[pltpu.VMEM((seq, hid), F32), pltpu.VMEM((2, h1, n2, c), F32),
                        pltpu.VMEM((h1, 2, n2, c), F32), pltpu.VMEM((h1, 2, n2, c), F32)],
        compiler_params=_params("arbitrary", "arbitrary"),
        name="hyena_filter_spectrum",
    )(zt, tcol, w1t, b1c, frt, w2.T, b2c, w3, deltas, m1k, m2)


def _long_conv_body(m1k_ref, m2_ref, m2t_ref, m3k_ref, v_ref, gate_ref, kr_ref, ki_ref, skip_ref, o_ref,
                    a_scr, b_scr, z_scr):
    h1, _, n2, c = a_scr.shape

    def gated_conv(order, src, store):
        g = F32_SUBLANES
        _stage1_into(m1k_ref, lambda r0: src(pl.ds(r0, g)), a_scr, g)

        def per_f1(f1, carry):
            x = _bdot(m2_ref[f1], _stage2_rows(a_scr, f1))
            xr, xi = x[:n2], x[n2:]
            kr, ki = kr_ref[order, f1].astype(F32), ki_ref[order, f1].astype(F32)
            y = jnp.concatenate([kr * xr - ki * xi, kr * xi + ki * xr], axis=0).astype(BF16)
            b_scr[pl.ds(2 * f1, 2)] = _bdot(m2t_ref[f1], y).astype(BF16).reshape(2, n2, c)
            return carry

        lax.fori_loop(0, h1, per_f1, 0, unroll=True)

        g = BF16_SUBLANES
        for s in range(n2 // g):
            rows = slice(s * g, (s + 1) * g)
            bs = b_scr[:, rows, :].reshape(2 * h1 * g, c)
            y = _bdot(m3k_ref[...], bs).reshape(h1, g, c)
            store(rows, gate_ref[0, order, :, rows, :] * (y + src(rows) * skip_ref[order:order + 1]))

    def store_z(rows, z):
        z_scr[:, rows, :] = z

    def store_out(rows, y):
        o_ref[0, :, rows, :] = y.astype(o_ref.dtype)

    gated_conv(0, lambda rows: v_ref[0, 0, :, rows, :], store_z)
    gated_conv(1, lambda rows: z_scr[:, rows, :], store_out)


def _hyena_mixer(gates, v, w1, b1, freq, w2, b2, w3, skip):
    b, _, seq, c = v.shape
    n1, n2, m1k, m2, m2t, m3k = _hyena_consts(seq)
    h1 = n1 // 2
    m1k, m2, m2t, m3k = (_mxu_const(a) for a in (m1k, m2, m2t, m3k))

    kr, ki = _hyena_filter_spectrum(seq, w1, b1, freq, w2, b2, w3, m1k, m2, min(h1, 8))

    resident = lambda a: pl.BlockSpec(a.shape, lambda i: (0,) * a.ndim, pipeline_mode=pl.Buffered(1))
    y = pl.pallas_call(
        _long_conv_body,
        grid=(b,),
        in_specs=[resident(m1k), resident(m2), resident(m2t), resident(m3k),
                  pl.BlockSpec((1, 1, h1, n2, c), lambda i: (i, 0, 0, 0, 0)),
                  pl.BlockSpec((1, 2, h1, n2, c), lambda i: (i, 0, 0, 0, 0)),
                  resident(kr), resident(ki), resident(skip)],
        out_specs=pl.BlockSpec((1, h1, n2, c), lambda i: (i, 0, 0, 0)),
        out_shape=jax.ShapeDtypeStruct((b, h1, n2, c), BF16),
        scratch_shapes=[pltpu.VMEM((h1, 2, n2, c), F32), pltpu.VMEM((2 * h1, n2, c), BF16),
                        pltpu.VMEM((h1, n2, c), F32)],
        compiler_params=pltpu.CompilerParams(dimension_semantics=("parallel",), vmem_limit_bytes=LONG_CONV_VMEM),
        name="hyena_long_conv",
    )(m1k, m2, m2t, m3k, v.reshape(b, 1, h1, n2, c), gates.reshape(b, 2, h1, n2, c), kr, ki, skip)
    return y.reshape(b, seq, c)


def _fnet_body(m1k_ref, m2_ref, perm_ref, bdc_ref, bds_ref, g_ref, o_ref, a_scr):
    j = pl.program_id(1)
    fb = m2_ref.shape[0]
    n1, _, n2, c = a_scr.shape

    @pl.when(j == 0)
    def _():
        gi = BF16_SUBLANES
        _stage1_into(m1k_ref, lambda r0: g_ref[0, :, pl.ds(r0, gi), :], a_scr, gi)

    for i in range(fb):
        f1 = j * fb + i
        a_scr[f1] = _bdot(m2_ref[i], _stage2_rows(a_scr, f1)).reshape(2, n2, c)

    @pl.when(j == pl.num_programs(1) - 1)
    def _():
        go = F32_SUBLANES

        def body(s, carry):
            r0 = pl.multiple_of(s * go, go)
            re = a_scr[:, 0, pl.ds(r0, go), :].reshape(n1 * go, c).astype(BF16)
            im = a_scr[:, 1, pl.ds(r0, go), :].reshape(n1 * go, c).astype(BF16)
            xr = _bdot(perm_ref[...], re).astype(BF16)
            xi = _bdot(perm_ref[...], im).astype(BF16)
            out = _bdot(xr, bdc_ref[...]) + _bdot(xi, bds_ref[...])
            o_ref[0, pl.ds(pl.multiple_of(s * go * n1, go * n1), go * n1), :] = out.astype(o_ref.dtype)
            return carry

        lax.fori_loop(0, n2 // go, body, 0, unroll=True)


def _fourier_mix(g):
    b, seq, c = g.shape
    n1, n2, m1k, m2, perm, bdc, bds = _fnet_consts(seq)
    m1k, m2, perm, bdc, bds = (_mxu_const(a) for a in (m1k, m2, perm, bdc, bds))
    fb = min(n1, 8)
    const = lambda a: pl.BlockSpec(a.shape, lambda i, j: (0,) * a.ndim)
    return pl.pallas_call(
        _fnet_body,
        grid=(b, n1 // fb),
        in_specs=[const(m1k),
                  pl.BlockSpec((fb, 2 * n2, 2 * n2), lambda i, j: (j, 0, 0)),
                  const(perm), const(bdc), const(bds),
                  pl.BlockSpec((1, n1, n2, c), lambda i, j: (i, 0, 0, 0))],
        out_specs=pl.BlockSpec((1, seq, c), lambda i, j: (i, 0, 0)),
        out_shape=jax.ShapeDtypeStruct((b, seq, c), BF16),
        scratch_shapes=[pltpu.VMEM((n1, 2, n2, c), F32)],
        compiler_params=_params("parallel", "arbitrary"),
        name="fnet",
    )(m1k, m2, perm, bdc, bds, g.reshape(b, n1, n2, c))


def _dot_nt(a, b):
    return lax.dot_general(a, b, (((1,), (1,)), ((), ())), preferred_element_type=F32)


def _head_pair_attention(q2, keys, values, biases):
    nq = q2.shape[0]
    lane = lax.broadcasted_iota(jnp.int32, q2.shape, 1)
    zero = jnp.zeros_like(q2)
    qs = jnp.concatenate([jnp.where(lane < HEAD_DIM, q2, zero), jnp.where(lane >= HEAD_DIM, q2, zero)], axis=0)
    m = denom = acc = None
    for gi, (kk, vv) in enumerate(zip(keys, values)):
        s = _dot_nt(qs, kk)
        if biases[0][gi] is not None:
            s = s + jnp.concatenate([biases[0][gi], biases[1][gi]], axis=0)
        m_grp = s.max(axis=-1, keepdims=True)
        m_new = m_grp if m is None else jnp.maximum(m, m_grp)
        e = jnp.exp2(s - m_new)
        e_part = e[:, :LANES]
        for c0 in range(LANES, e.shape[1], LANES):
            e_part = e_part + e[:, c0:c0 + LANES]
        pv = _bdot(e.astype(BF16), vv)
        if m is None:
            denom, acc = e_part, pv
        else:
            alpha = jnp.exp2(m - m_new)
            denom = alpha * denom + e_part
            acc = alpha * acc + pv
        m = m_new
    out = acc / denom.sum(axis=-1, keepdims=True)
    return jnp.where(lane < HEAD_DIM, out[:nq], out[nq:])


def _na_window_base(row_block, rows):
    return jnp.clip(row_block * NA_QROWS - NA_KH // 2, 0, rows - NA_KROWS)


def _na_body(rows, q_ref, k_ref, v_ref, kc_ref, vc_ref, bias_ref, o_ref):
    base = _na_window_base(pl.program_id(1), rows)
    start = pl.multiple_of(base * GRID_W, GRID_W)
    n_keys = NA_KROWS * GRID_W
    n_pairs = q_ref.shape[2] // LANES
    groups = [(c0, min(NA_KEY_GROUP, n_keys - c0)) for c0 in range(0, n_keys, NA_KEY_GROUP)]
    for p in range(n_pairs):
        sl = slice(p * LANES, (p + 1) * LANES)
        q2 = q_ref[0, :, sl]
        keys = [kc_ref[0, :, sl]] + [k_ref[0, pl.ds(start + c0, n), sl] for c0, n in groups]
        values = [vc_ref[0, :, sl]] + [v_ref[0, pl.ds(start + c0, n), sl] for c0, n in groups]
        biases = [[None] + [bias_ref[0, 2 * p + hh, 0, :, c0:c0 + n] for c0, n in groups] for hh in range(2)]
        o = _head_pair_attention(q2, keys, values, biases)
        o_ref[0, :, sl] = o.astype(o_ref.dtype)


def _na_bias_body(rows, p_ref, o_ref):
    kh, kw, w = NA_KH, NA_KW, GRID_W
    shape = (w, 2 * w)
    qc = lax.broadcasted_iota(jnp.int32, shape, 0)
    lane = lax.broadcasted_iota(jnp.int32, shape, 1)
    kc = lane & (w - 1)
    col_start = jnp.clip(qc - kw // 2, 0, w - kw)
    col_valid = (kc >= col_start) & (kc < col_start + kw)
    low_half = lane < w
    masked = jnp.full(shape, MASK_VALUE, F32)
    tiles = []
    for a in range(2 * kh - 1):
        row = jnp.broadcast_to(p_ref[0, 0, a:a + 1, :], shape)
        t = pltpu.roll(row, w + 1, 1, stride=1, stride_axis=0)
        t = jnp.where(low_half, t, pltpu.roll(t, w, 1))
        tiles.append(jnp.where(col_valid, t, masked))
    for case, r0 in enumerate((0, NA_QROWS, rows - NA_QROWS)):
        base = int(np.clip(r0 - kh // 2, 0, rows - NA_KROWS))
        for qr in range(NA_QROWS):
            q_abs = r0 + qr
            r_start = int(np.clip(q_abs - kh // 2, 0, rows - kh))

            def tile_for(kr):
                k_abs = base + kr
                if kr >= NA_KROWS or not (r_start <= k_abs < r_start + kh):
                    return masked
                return tiles[k_abs - q_abs + kh - 1]

            q_rows = slice(qr * w, (qr + 1) * w)
            for pair in range((NA_KROWS + 1) // 2):
                chunk = jnp.where(low_half, tile_for(2 * pair), tile_for(2 * pair + 1))
                width = min(2 * w, NA_KROWS * w - pair * 2 * w)
                o_ref[0, 0, case, q_rows, pair * 2 * w:pair * 2 * w + width] = chunk[:, :width]


def _na_bias_tables(rpb, rows):
    w = GRID_W
    depth, nh, na, nb = rpb.shape
    assert 2 * w == LANES and nb == 2 * NA_KW - 1
    lo = w - NA_KW
    padded = jnp.pad(rpb.astype(F32) * LOG2E, ((0, 0), (0, 0), (0, (-na) % 8), (lo, 2 * w - nb - lo)))
    return pl.pallas_call(
        functools.partial(_na_bias_body, rows),
        grid=(depth, nh),
        in_specs=[pl.BlockSpec((1, 1) + padded.shape[2:], lambda l, h: (l, h, 0, 0))],
        out_specs=pl.BlockSpec((1, 1, 3, NA_QROWS * w, NA_KROWS * w), lambda l, h: (l, h, 0, 0, 0)),
        out_shape=jax.ShapeDtypeStruct((depth, nh, 3, NA_QROWS * w, NA_KROWS * w), F32),
        compiler_params=_params("parallel", "parallel"),
        name="na_bias_tables",
    )(padded)


def _neighbourhood_attention(q, kv, kv_ctx, bias, layer):
    b, seq, _ = q.shape
    rows = seq // GRID_W
    lc = kv_ctx.shape[1]
    assert rows % NA_QROWS == 0 and rows >= NA_KROWS + 1 and NA_QROWS == NA_KH // 2
    n_blocks = rows // NA_QROWS
    nh = bias.shape[1]
    tq = NA_QROWS * GRID_W

    def bias_map(i, r):
        return (layer, 0, jnp.where(r == 0, 0, jnp.where(r == n_blocks - 1, 2, 1)), 0, 0)

    return pl.pallas_call(
        functools.partial(_na_body, rows),
        grid=(b, n_blocks),
        in_specs=[pl.BlockSpec((1, tq, D_NA), lambda i, r: (i, r, 0)),
                  pl.BlockSpec((1, seq, D_NA), lambda i, r: (i, 0, 0)),
                  pl.BlockSpec((1, seq, D_NA), lambda i, r: (i, 0, 1)),
                  pl.BlockSpec((1, lc, D_NA), lambda i, r: (i, 0, 0)),
                  pl.BlockSpec((1, lc, D_NA), lambda i, r: (i, 0, 1)),
                  pl.BlockSpec((1, nh, 1, tq, NA_KROWS * GRID_W), bias_map)],
        out_specs=pl.BlockSpec((1, tq, D_NA), lambda i, r: (i, r, 0)),
        out_shape=jax.ShapeDtypeStruct((b, seq, D_NA), BF16),
        compiler_params=_params("parallel", "arbitrary"),
        name="neighbourhood_attention",
    )(q, kv, kv, kv_ctx, kv_ctx, bias)


def _ctx_attn_body(q_ref, k_ref, v_ref, o_ref):
    for p in range(q_ref.shape[2] // LANES):
        sl = slice(p * LANES, (p + 1) * LANES)
        o = _head_pair_attention(q_ref[0, :, sl], [k_ref[0, :, sl]], [v_ref[0, :, sl]], [[None], [None]])
        o_ref[0, :, sl] = o.astype(o_ref.dtype)


def _context_attention(q, kv):
    b, lc, _ = q.shape
    return pl.pallas_call(
        _ctx_attn_body,
        grid=(b,),
        in_specs=[pl.BlockSpec((1, lc, D_NA), lambda i: (i, 0, 0)),
                  pl.BlockSpec((1, lc, D_NA), lambda i: (i, 0, 0)),
                  pl.BlockSpec((1, lc, D_NA), lambda i: (i, 0, 1))],
        out_specs=pl.BlockSpec((1, lc, D_NA), lambda i: (i, 0, 0)),
        out_shape=jax.ShapeDtypeStruct((b, lc, D_NA), BF16),
        compiler_params=_params("parallel"),
        name="context_attention",
    )(q, kv, kv)


def kernel(x, c, ctx, c_ctx, w_mod, b_mod, norm1_g, norm2_g, w_in, w_out, hy_conv_w, hy_conv_b,
           filt_w1, filt_b1, filt_freq, filt_w2, filt_b2, filt_w3, hy_skip, na_rpb,
           w_gate, w_up, w_down, final_g):
    depth = w_mod.shape[0]
    b, seq, d = x.shape
    off_k = 3 * D_HYENA + D_FNET + D_NA
    qk_scale = LOG2E / math.sqrt(HEAD_DIM)
    splits = ((3 * D_HYENA, F32, None), (D_FNET, BF16, None), (D_NA, BF16, qk_scale), (2 * D_NA, BF16, None))
    tm_proj, tm = 1024, 1024

    pad = (-(b + 1)) % 8
    c_all = jnp.concatenate([c, c_ctx[None], jnp.zeros((pad, d), F32)], axis=0)
    mod = _modulation(c_all, w_mod, b_mod)
    lat_row = lambda i: i
    ctx_row = lambda i: b

    w_in_b, w_out_b = w_in.astype(BF16), w_out.astype(BF16)
    w_gate_b, w_up_b, w_down_b = w_gate.astype(BF16), w_up.astype(BF16), w_down.astype(BF16)
    w_kv_last = w_in_b[depth - 1:, :, off_k:]
    n1g, n2g = norm1_g.reshape(depth, 1, d), norm2_g.reshape(depth, 1, d)
    fg = final_g.reshape(1, d)
    na_bias = _na_bias_tables(na_rpb, seq // GRID_W)

    for l in range(depth):
        last = l == depth - 1
        hy_params = (filt_w1[l], filt_b1[l], filt_freq[l], filt_w2[l], filt_b2[l], filt_w3[l], hy_skip[l])
        conv = (hy_conv_w[l], hy_conv_b[l].reshape(1, 3 * D_HYENA))

        gates, v, fn, q, kv = _norm_proj(x, n1g, mod, lat_row, l, w_in_b, l, splits, tm_proj, conv)
        if last:
            (kv_c,) = _norm_proj(ctx.reshape(1, -1, d), n1g, mod, ctx_row, l, w_kv_last, 0,
                                 ((2 * D_NA, BF16, None),), tm_proj)
            kv_c = kv_c.reshape(b, -1, 2 * D_NA)
        else:
            cgates, cv, cfn, cq, kv_c = _norm_proj(ctx, n1g, mod, ctx_row, l, w_in_b, l, splits, tm_proj, conv)

        y_hy = _hyena_mixer(gates, v, *hy_params)
        y_fn = _fourier_mix(fn)
        y_na = _neighbourhood_attention(q, kv, kv_c, na_bias, l)

        x = _mix_ffn(x, y_hy, y_fn, y_na, w_out_b, n2g, mod, lat_row, l, w_gate_b, w_up_b, w_down_b,
                     fg, last, tm)

        if not last:
            cy_hy = _hyena_mixer(cgates, cv, *hy_params)
            cy_fn = _fourier_mix(cfn)
            cy_na = _context_attention(cq, kv_c)
            flat = lambda a: a.reshape(1, -1, a.shape[-1])
            ctx = _mix_ffn(flat(ctx), flat(cy_hy), flat(cy_fn), flat(cy_na), w_out_b, n2g, mod, ctx_row, l,
                           w_gate_b, w_up_b, w_down_b, fg, False, tm).reshape(ctx.shape)
    return x
```

```python
import functools
import math

import numpy as np
import jax
import jax.numpy as jnp
from jax import lax
from jax.experimental import pallas as pl
from jax.experimental.pallas import tpu as pltpu

F32 = jnp.float32
BF16 = jnp.bfloat16

EPS = 1e-6
GRID_W = 64
HEAD_DIM = 64
D_HYENA = 256
D_FNET = 256
D_NA = 512
FNET_GROUP_DIM = 64
NA_KH = 8
NA_KW = 16
NA_QROWS = 4
NA_KROWS = NA_QROWS + NA_KH
NA_KEY_GROUP = 256
POS_BANDS = 16
HYENA_DECAY_TARGET = 1e-2
HYENA_FAST_PCT = 0.3
HYENA_SLOW_PCT = 1.5
MASK_VALUE = -1e30
VMEM_LIMIT = 56 * 1024 * 1024
LONG_CONV_VMEM = 60 * 1024 * 1024
LANES = 128
MXU_WIDTH = 256
LOG2E = math.log2(math.e)
FFN_ROW_SPLIT = 2
F32_SUBLANES = 8
BF16_SUBLANES = 16


def _params(*sem):
    return pltpu.CompilerParams(dimension_semantics=sem, vmem_limit_bytes=VMEM_LIMIT)


def _bdot(a, b):
    return jnp.dot(a, b, preferred_element_type=F32)


def _split_bf16(a):
    hi = a.astype(BF16)
    lo = (a - hi.astype(F32)).astype(BF16)
    return hi, lo


def _dot3(a, b):
    a_hi, a_lo = _split_bf16(a)
    b_hi, b_lo = _split_bf16(b)
    return _bdot(a_hi, b_hi) + _bdot(a_lo, b_hi) + _bdot(a_hi, b_lo)


def _mod_body(c_ref, w_ref, b_ref, o_ref):
    c = c_ref[...]
    rows = c.shape[0]
    s_hi, s_lo = _split_bf16(c * jax.nn.sigmoid(c))
    w_hi, w_lo = _split_bf16(w_ref[0])
    both = _bdot(jnp.concatenate([s_hi, s_lo], axis=0), w_hi)
    o_ref[0] = both[:rows] + both[rows:] + _bdot(s_hi, w_lo) + b_ref[0]


def _modulation(c_all, w_mod, b_mod):
    depth, d, n = w_mod.shape
    rows = c_all.shape[0]
    tn = 1536
    return pl.pallas_call(
        _mod_body,
        grid=(depth, n // tn),
        in_specs=[
            pl.BlockSpec((rows, d), lambda l, j: (0, 0)),
            pl.BlockSpec((1, d, tn), lambda l, j: (l, 0, j)),
            pl.BlockSpec((1, 1, tn), lambda l, j: (l, 0, j)),
        ],
        out_specs=pl.BlockSpec((1, rows, tn), lambda l, j: (l, 0, j)),
        out_shape=jax.ShapeDtypeStruct((depth, rows, n), F32),
        compiler_params=_params("parallel", "parallel"),
        name="modulation",
    )(c_all, w_mod, b_mod.reshape(depth, 1, n))


def _modulated_norm(x, g, sc, sh):
    ms = jnp.mean(x * x, axis=-1, keepdims=True)
    return x * lax.rsqrt(ms + EPS) * (g * (1.0 + sc)) + sh


def _mod_spec(mod, layer, chunk, d):
    return pl.BlockSpec((1, mod.shape[1], d), lambda i, *_: (layer, 0, chunk))


def _mod_row(ref, row_of):
    return ref[0, pl.ds(row_of(pl.program_id(0)), 1), :]


def _short_conv_tile(u, prev_row, next_row, cw_ref, cb_ref):
    rows = u.shape[0]
    row = lax.broadcasted_iota(jnp.int32, u.shape, 0)
    prev = jnp.where(row == 0, prev_row, pltpu.roll(u, 1, axis=0))
    nxt = jnp.where(row == rows - 1, next_row, pltpu.roll(u, rows - 1, axis=0))
    return prev * cw_ref[0:1] + u * cw_ref[1:2] + nxt * cw_ref[2:3] + cb_ref[...]


def _norm_proj_body(splits, conv, row_of, x_ref, g_ref, sc_ref, sh_ref, w_ref, *rest):
    o_refs = ((rest[4:6],) + rest[6:]) if conv else rest
    g, sc, sh = g_ref[0], _mod_row(sc_ref, row_of), _mod_row(sh_ref, row_of)
    tm = x_ref.shape[1]
    edge = F32_SUBLANES if conv else 0
    if conv:
        xp_ref, xn_ref, cw_ref, cb_ref = rest[:4]
        x_all = jnp.concatenate([xp_ref[0], x_ref[0], xn_ref[0]], axis=0)
    else:
        x_all = x_ref[0]
    p_all = _bdot(_modulated_norm(x_all, g, sc, sh).astype(BF16), w_ref[0])
    p = p_all[edge:edge + tm]
    off = 0
    for idx, (o_ref, (width, _, scale)) in enumerate(zip(o_refs, splits)):
        part = p[:, off:off + width]
        if conv and idx == 0:
            gate_ref, v_ref = o_ref
            j, nj = pl.program_id(1), pl.num_programs(1)
            prev_row = jnp.where(j == 0, 0.0, p_all[edge - 1:edge, off:off + width])
            next_row = jnp.where(j == nj - 1, 0.0, p_all[edge + tm:edge + tm + 1, off:off + width])
            u = _short_conv_tile(part, prev_row, next_row, cw_ref, cb_ref)
            c = v_ref.shape[3]
            n_gates = gate_ref.shape[1]
            for gi in range(n_gates):
                gate_ref[0, gi] = u[:, gi * c:(gi + 1) * c].astype(gate_ref.dtype)
            v_ref[0, 0] = u[:, n_gates * c:]
        else:
            o_ref[0] = (part if scale is None else part * scale).astype(o_ref.dtype)
        off += width


def _norm_proj(x, norm_g, mod, row_of, layer, w, w_idx, splits, tm, conv=None):
    b, t, d = x.shape
    n = w.shape[2]
    tm = min(tm, t)
    edge = F32_SUBLANES
    tok = lambda wd: pl.BlockSpec((1, tm, wd), lambda i, j: (i, j, 0))
    in_specs = [
        tok(d),
        pl.BlockSpec((1, 1, d), lambda i, j: (layer, 0, 0)),
        _mod_spec(mod, layer, 1, d),
        _mod_spec(mod, layer, 0, d),
        pl.BlockSpec((1, d, n), lambda i, j: (w_idx, 0, 0)),
    ]
    args = [x, norm_g, mod, mod, w]
    out_specs = [tok(wd) for wd, _, _ in splits]
    out_shape = [jax.ShapeDtypeStruct((b, t, wd), dt) for wd, dt, _ in splits]
    if conv is not None:
        cw, cb = conv
        w0, c = splits[0][0], D_HYENA
        per_tile = tm // edge
        in_specs += [
            pl.BlockSpec((1, edge, d), lambda i, j: (i, jnp.maximum(j * per_tile - 1, 0), 0)),
            pl.BlockSpec((1, edge, d), lambda i, j: (i, jnp.minimum((j + 1) * per_tile, t // edge - 1), 0)),
            pl.BlockSpec(cw.shape, lambda i, j: (0, 0)),
            pl.BlockSpec(cb.shape, lambda i, j: (0, 0)),
        ]
        args += [x, x, cw, cb]
        n_gates = w0 // c - 1
        out_specs[0:1] = [pl.BlockSpec((1, n_gates, tm, c), lambda i, j: (i, 0, j, 0)),
                          pl.BlockSpec((1, 1, tm, c), lambda i, j: (i, 0, j, 0))]
        out_shape[0:1] = [jax.ShapeDtypeStruct((b, n_gates, t, c), BF16),
                          jax.ShapeDtypeStruct((b, 1, t, c), splits[0][1])]
    return pl.pallas_call(
        functools.partial(_norm_proj_body, tuple(splits), conv is not None, row_of),
        grid=(b, t // tm),
        in_specs=in_specs,
        out_specs=out_specs,
        out_shape=out_shape,
        compiler_params=_params("parallel", "parallel"),
        name="norm_proj",
    )(*args)


def _mix_ffn_body(final_norm, row_of, x_ref, hy_ref, fn_ref, na_ref, wo_ref, g1_ref, ng_ref, sc_ref, sh_ref,
                  g2_ref, wg_ref, wu_ref, wd_ref, fg_ref, o_ref):
    g1, g2 = _mod_row(g1_ref, row_of), _mod_row(g2_ref, row_of)
    sc, sh = _mod_row(sc_ref, row_of), _mod_row(sh_ref, row_of)
    tm = x_ref.shape[1]
    sub = tm // FFN_ROW_SPLIT if tm % (FFN_ROW_SPLIT * MXU_WIDTH) == 0 else tm
    halves = [slice(r0, r0 + sub) for r0 in range(0, tm, sub)]
    x1s, hs = [], []
    for rows in halves:
        mix = (_bdot(hy_ref[0, rows], wo_ref[0, :D_HYENA])
               + _bdot(fn_ref[0, rows], wo_ref[0, D_HYENA:D_HYENA + D_FNET])
               + _bdot(na_ref[0, rows], wo_ref[0, D_HYENA + D_FNET:]))
        x1 = x_ref[0, rows] + g1 * mix
        x1s.append(x1)
        hs.append(_modulated_norm(x1, ng_ref[0], sc, sh).astype(BF16))
    accs = [None] * len(halves)
    for c0 in range(0, wg_ref.shape[2], MXU_WIDTH):
        cols = slice(c0, c0 + MXU_WIDTH)
        for s, h in enumerate(hs):
            a = _bdot(h, wg_ref[0, :, cols])
            u = _bdot(h, wu_ref[0, :, cols])
            act = (a * jax.nn.sigmoid(a) * u).astype(BF16)
            part = _bdot(act, wd_ref[0, cols, :])
            accs[s] = part if accs[s] is None else accs[s] + part
    for rows, x1, acc in zip(halves, x1s, accs):
        y = x1 + g2 * acc
        if final_norm:
            ms = jnp.mean(y * y, axis=-1, keepdims=True)
            y = y * lax.rsqrt(ms + EPS) * fg_ref[...]
        o_ref[0, rows] = y


def _mix_ffn(x, y_hy, y_fn, y_na, w_out, norm_g, mod, row_of, layer, wg, wu, wd, final_g, final_norm, tm):
    b, t, d = x.shape
    dff = wg.shape[2]
    assert dff % MXU_WIDTH == 0
    tm = min(tm, t)
    tok = lambda wd_: pl.BlockSpec((1, tm, wd_), lambda i, j: (i, j, 0))
    resident = lambda shape: pl.BlockSpec((1,) + shape, lambda i, j: (layer, 0, 0), pipeline_mode=pl.Buffered(1))
    return pl.pallas_call(
        functools.partial(_mix_ffn_body, final_norm, row_of),
        grid=(b, t // tm),
        in_specs=[
            tok(d), tok(D_HYENA), tok(D_FNET), tok(D_NA),
            resident((d, d)),
            _mod_spec(mod, layer, 2, d),
            pl.BlockSpec((1, 1, d), lambda i, j: (layer, 0, 0)),
            _mod_spec(mod, layer, 4, d),
            _mod_spec(mod, layer, 3, d),
            _mod_spec(mod, layer, 5, d),
            resident((d, dff)), resident((d, dff)), resident((dff, d)),
            pl.BlockSpec((1, d), lambda i, j: (0, 0)),
        ],
        out_specs=tok(d),
        out_shape=jax.ShapeDtypeStruct(x.shape, F32),
        compiler_params=_params("parallel", "parallel"),
        name="mix_ffn",
    )(x, y_hy, y_fn, y_na, w_out, mod, norm_g, mod, mod, mod, wg, wu, wd, final_g)


def _split_seq(n_total):
    log = int(round(math.log2(n_total)))
    assert 2 ** log == n_total
    n1 = 2 ** (log // 2)
    return n1, n_total // n1


@functools.lru_cache(maxsize=None)
def _hyena_consts(seq):
    n = 2 * seq
    n1, n2 = _split_seq(n)
    h1 = n1 // 2
    f1 = np.arange(h1)[:, None]
    t1 = np.arange(h1)[None, :]
    th = np.pi * (2 * f1 + 1) * t1 / n1
    m1 = np.empty((n1, h1))
    m1[0::2] = np.cos(th)
    m1[1::2] = -np.sin(th)
    m3 = m1.T * (2.0 / n)
    f = (np.arange(h1)[:, None, None] + n1 * np.arange(n2)[None, :, None])
    t2 = np.arange(n2)[None, None, :]
    ph = np.pi * ((2 * f + 1) * t2 % (2 * n)) / n
    c, s = np.cos(ph), np.sin(ph)
    m2 = np.concatenate([np.concatenate([c, s], axis=2), np.concatenate([-s, c], axis=2)], axis=1)
    m2t = np.transpose(m2, (0, 2, 1))
    m1k = np.kron(m1, np.eye(F32_SUBLANES))
    m3k = np.kron(m3, np.eye(BF16_SUBLANES))
    return n1, n2, m1k, m2, m2t, m3k


@functools.lru_cache(maxsize=None)
def _fnet_consts(seq):
    n1, n2 = _split_seq(seq)
    n1, n2 = n1 // 2, n2 * 2
    f1 = np.arange(n1)[:, None]
    t1 = np.arange(n1)[None, :]
    th = 2 * np.pi * (f1 * t1 % n1) / n1
    m1 = np.empty((2 * n1, n1))
    m1[0::2] = np.cos(th)
    m1[1::2] = -np.sin(th)
    f = (np.arange(n1)[:, None, None] + n1 * np.arange(n2)[None, :, None])
    t2 = np.arange(n2)[None, None, :]
    ph = 2 * np.pi * (f * t2 % seq) / seq
    c, s = np.cos(ph), np.sin(ph)
    m2 = np.concatenate([np.concatenate([c, s], axis=2), np.concatenate([-s, c], axis=2)], axis=1)
    gd = FNET_GROUP_DIM
    cc = np.arange(gd)
    pg = 2 * np.pi * (np.outer(cc, cc) % gd) / gd
    scale = 1.0 / math.sqrt(gd * seq)
    eye = np.eye(D_FNET // gd)
    bdc = np.kron(eye, np.cos(pg)) * scale
    bds = np.kron(eye, np.sin(pg)) * scale
    m1k = np.kron(m1, np.eye(BF16_SUBLANES))
    g = F32_SUBLANES
    perm = np.zeros((g * n1, n1 * g))
    for f2 in range(g):
        for ff in range(n1):
            perm[f2 * n1 + ff, ff * g + f2] = 1.0
    return n1, n2, m1k, m2, perm, bdc, bds


def _mxu_const(a):
    return jnp.asarray(a, dtype=F32).astype(BF16)


def _stage1_into(mk_ref, x_at, a_scr, group):
    nf, _, n2, c = a_scr.shape

    def body(s, carry):
        r0 = pl.multiple_of(s * group, group)
        xs = x_at(r0)
        xs = xs.reshape(xs.shape[0] * group, c).astype(BF16)
        a_scr[:, :, pl.ds(r0, group), :] = _bdot(mk_ref[...], xs).reshape(nf, 2, group, c)
        return carry

    lax.fori_loop(0, n2 // group, body, 0, unroll=True)


def _stage2_rows(a_scr, f1):
    _, _, n2, c = a_scr.shape
    return a_scr[f1].reshape(2 * n2, c).astype(BF16)


def _filter_body(zt_ref, t_ref, w1t_ref, b1_ref, fr_ref, w2t_ref, b2_ref, w3_ref, dl_ref, m1k_ref, m2_ref,
                 kr_ref, ki_ref, h_scr, kt_scr, af_scr, ab_scr):
    o, j = pl.program_id(0), pl.program_id(1)
    fb = m2_ref.shape[0]
    h1, _, n2, c = af_scr.shape
    g = F32_SUBLANES

    @pl.when((o == 0) & (j == 0))
    def _():
        h = jnp.sin(fr_ref[:, 0:1] * (_dot3(w1t_ref[...], zt_ref[...]) + b1_ref[...]))
        h = jnp.sin(fr_ref[:, 1:2] * (_dot3(w2t_ref[...], h) + b2_ref[...]))
        h_scr[...] = h.T

    @pl.when(j == 0)
    def _():
        k = _dot3(h_scr[...], w3_ref[0])
        decay = jnp.exp(-t_ref[...] * jnp.abs(dl_ref[...]))
        kf = k[:, :c] * decay
        row = lax.broadcasted_iota(jnp.int32, kf.shape, 0)
        kb = jnp.where(row == 0, 0.0, k[:, c:] * decay)
        norm = jnp.sum(jnp.abs(kf), axis=0, keepdims=True) + jnp.sum(jnp.abs(kb), axis=0, keepdims=True)
        inv = 1.0 / norm
        kt_scr[0] = (kf * inv).reshape(h1, n2, c)
        kt_scr[1] = (kb * inv).reshape(h1, n2, c)
        _stage1_into(m1k_ref, lambda r0: kt_scr[0, :, pl.ds(r0, g), :], af_scr, g)
        _stage1_into(m1k_ref, lambda r0: kt_scr[1, :, pl.ds(r0, g), :], ab_scr, g)

    for i in range(fb):
        f1 = j * fb + i
        xf = _bdot(m2_ref[i], _stage2_rows(af_scr, f1))
        xb = _bdot(m2_ref[i], _stage2_rows(ab_scr, f1))
        kr_ref[0, i] = (xf[:n2] + xb[:n2]).astype(kr_ref.dtype)
        ki_ref[0, i] = (xf[n2:] - xb[n2:]).astype(ki_ref.dtype)


def _hyena_filter_spectrum(seq, w1, b1, freq, w2, b2, w3, m1k, m2, fb):
    c = D_HYENA
    h1, n2 = m2.shape[0], m2.shape[1] // 2
    t = np.linspace(0.0, 1.0, seq)[:, None]
    w = 2.0 * np.pi * np.arange(seq)[:, None] / seq
    f = np.linspace(1e-4, POS_BANDS - 1, POS_BANDS)[None]
    z = np.concatenate([t, np.cos(f * w), -np.sin(f * w)], axis=-1).astype(np.float32)
    pe = z.shape[1]
    zt = jnp.asarray(np.pad(z, ((0, 0), (0, LANES - pe))).T)
    tcol = jnp.asarray(t.astype(np.float32))
    w1t = jnp.pad(w1, ((0, LANES - pe), (0, 0))).T
    max_decay = math.log(HYENA_DECAY_TARGET) / HYENA_FAST_PCT
    min_decay = math.log(HYENA_DECAY_TARGET) / HYENA_SLOW_PCT
    deltas = jnp.asarray(np.linspace(min_decay, max_decay, c, dtype=np.float32)[None])
    hid = w1.shape[1]
    w3r = jnp.transpose(w3.reshape(hid, 2, 2 * c), (1, 0, 2))
    full = lambda a: pl.BlockSpec(a.shape, lambda o, j: (0,) * a.ndim)
    b1c, b2c, frt = b1.reshape(hid, 1), b2.reshape(hid, 1), freq.T
    out = jax.ShapeDtypeStruct((2, h1, n2, c), BF16)
    return pl.pallas_call(
        _filter_body,
        grid=(2, h1 // fb),
        in_specs=[full(zt), full(tcol), full(w1t), full(b1c), full(frt), full(w2.T), full(b2c),
                  pl.BlockSpec((1, hid, 2 * c), lambda o, j: (o, 0, 0)), full(deltas), full(m1k),
                  pl.BlockSpec((fb, 2 * n2, 2 * n2), lambda o, j: (j, 0, 0))],
        out_specs=[pl.BlockSpec((1, fb, n2, c), lambda o, j: (o, j, 0, 0))] * 2,
        out_shape=[out, out],
        scratch_shapes=[pltpu.VMEM((seq, hid), F32), pltpu.VMEM((2, h1, n2, c), F32),
                        pltpu.VMEM((h1, 2, n2, c), F32), pltpu.VMEM((h1, 2, n2, c), F32)],
        compiler_params=_params("arbitrary", "arbitrary"),
        name="hyena_filter_spectrum",
    )(zt, tcol, w1t, b1c, frt, w2.T, b2c, w3r, deltas, m1k, m2)


def _long_conv_body(m1k_ref, m2_ref, m2t_ref, m3k_ref, v_ref, gate_ref, kr_ref, ki_ref, skip_ref, o_ref,
                    a_scr, b_scr, z_scr):
    h1, _, n2, c = a_scr.shape

    def gated_conv(order, src, store):
        g = F32_SUBLANES
        _stage1_into(m1k_ref, lambda r0: src(pl.ds(r0, g)), a_scr, g)

        def per_f1(f1, carry):
            x = _bdot(m2_ref[f1], _stage2_rows(a_scr, f1))
            xr, xi = x[:n2], x[n2:]
            kr, ki = kr_ref[order, f1].astype(F32), ki_ref[order, f1].astype(F32)
            y = jnp.concatenate([kr * xr - ki * xi, kr * xi + ki * xr], axis=0).astype(BF16)
            b_scr[pl.ds(2 * f1, 2)] = _bdot(m2t_ref[f1], y).astype(BF16).reshape(2, n2, c)
            return carry

        lax.fori_loop(0, h1, per_f1, 0, unroll=True)

        g = BF16_SUBLANES
        for s in range(n2 // g):
            rows = slice(s * g, (s + 1) * g)
            bs = b_scr[:, rows, :].reshape(2 * h1 * g, c)
            y = _bdot(m3k_ref[...], bs).reshape(h1, g, c)
            store(rows, gate_ref[0, order, :, rows, :] * (y + src(rows) * skip_ref[order:order + 1]))

    def store_z(rows, z):
        z_scr[:, rows, :] = z

    def store_out(rows, y):
        o_ref[0, :, rows, :] = y.astype(o_ref.dtype)

    gated_conv(0, lambda rows: v_ref[0, 0, :, rows, :], store_z)
    gated_conv(1, lambda rows: z_scr[:, rows, :], store_out)


def _hyena_mixer(gates, v, w1, b1, freq, w2, b2, w3, skip):
    b, _, seq, c = v.shape
    n1, n2, m1k, m2, m2t, m3k = _hyena_consts(seq)
    h1 = n1 // 2
    m1k, m2, m2t, m3k = (_mxu_const(a) for a in (m1k, m2, m2t, m3k))

    kr, ki = _hyena_filter_spectrum(seq, w1, b1, freq, w2, b2, w3, m1k, m2, min(h1, 8))

    resident = lambda a: pl.BlockSpec(a.shape, lambda i: (0,) * a.ndim, pipeline_mode=pl.Buffered(1))
    y = pl.pallas_call(
        _long_conv_body,
        grid=(b,),
        in_specs=[resident(m1k), resident(m2), resident(m2t), resident(m3k),
                  pl.BlockSpec((1, 1, h1, n2, c), lambda i: (i, 0, 0, 0, 0)),
                  pl.BlockSpec((1, 2, h1, n2, c), lambda i: (i, 0, 0, 0, 0)),
                  resident(kr), resident(ki), resident(skip)],
        out_specs=pl.BlockSpec((1, h1, n2, c), lambda i: (i, 0, 0, 0)),
        out_shape=jax.ShapeDtypeStruct((b, h1, n2, c), BF16),
        scratch_shapes=[pltpu.VMEM((h1, 2, n2, c), F32), pltpu.VMEM((2 * h1, n2, c), BF16),
                        pltpu.VMEM((h1, n2, c), F32)],
        compiler_params=pltpu.CompilerParams(dimension_semantics=("parallel",), vmem_limit_bytes=LONG_CONV_VMEM),
        name="hyena_long_conv",
    )(m1k, m2, m2t, m3k, v.reshape(b, 1, h1, n2, c), gates.reshape(b, 2, h1, n2, c), kr, ki, skip)
    return y.reshape(b, seq, c)


def _fnet_body(m1k_ref, m2_ref, perm_ref, bdc_ref, bds_ref, g_ref, o_ref, a_scr):
    j = pl.program_id(1)
    fb = m2_ref.shape[0]
    n1, _, n2, c = a_scr.shape

    @pl.when(j == 0)
    def _():
        gi = BF16_SUBLANES
        _stage1_into(m1k_ref, lambda r0: g_ref[0, :, pl.ds(r0, gi), :], a_scr, gi)

    for i in range(fb):
        f1 = j * fb + i
        a_scr[f1] = _bdot(m2_ref[i], _stage2_rows(a_scr, f1)).reshape(2, n2, c)

    @pl.when(j == pl.num_programs(1) - 1)
    def _():
        go = F32_SUBLANES

        def body(s, carry):
            r0 = pl.multiple_of(s * go, go)
            re = a_scr[:, 0, pl.ds(r0, go), :].reshape(n1 * go, c).astype(BF16)
            im = a_scr[:, 1, pl.ds(r0, go), :].reshape(n1 * go, c).astype(BF16)
            xr = _bdot(perm_ref[...], re).astype(BF16)
            xi = _bdot(perm_ref[...], im).astype(BF16)
            out = _bdot(xr, bdc_ref[...]) + _bdot(xi, bds_ref[...])
            o_ref[0, pl.ds(pl.multiple_of(s * go * n1, go * n1), go * n1), :] = out.astype(o_ref.dtype)
            return carry

        lax.fori_loop(0, n2 // go, body, 0, unroll=True)


def _fourier_mix(g):
    b, seq, c = g.shape
    n1, n2, m1k, m2, perm, bdc, bds = _fnet_consts(seq)
    m1k, m2, perm, bdc, bds = (_mxu_const(a) for a in (m1k, m2, perm, bdc, bds))
    fb = min(n1, 8)
    const = lambda a: pl.BlockSpec(a.shape, lambda i, j: (0,) * a.ndim)
    return pl.pallas_call(
        _fnet_body,
        grid=(b, n1 // fb),
        in_specs=[const(m1k),
                  pl.BlockSpec((fb, 2 * n2, 2 * n2), lambda i, j: (j, 0, 0)),
                  const(perm), const(bdc), const(bds),
                  pl.BlockSpec((1, n1, n2, c), lambda i, j: (i, 0, 0, 0))],
        out_specs=pl.BlockSpec((1, seq, c), lambda i, j: (i, 0, 0)),
        out_shape=jax.ShapeDtypeStruct((b, seq, c), BF16),
        scratch_shapes=[pltpu.VMEM((n1, 2, n2, c), F32)],
        compiler_params=_params("parallel", "arbitrary"),
        name="fnet",
    )(m1k, m2, perm, bdc, bds, g.reshape(b, n1, n2, c))


def _dot_nt(a, b):
    return lax.dot_general(a, b, (((1,), (1,)), ((), ())), preferred_element_type=F32)


def _head_pairs_attention(q2s, keys, values, biases):
    nq = q2s[0].shape[0]
    lane = lax.broadcasted_iota(jnp.int32, q2s[0].shape, 1)
    qs = [jnp.concatenate([jnp.where(lane < HEAD_DIM, q2, jnp.zeros_like(q2)),
                           jnp.where(lane >= HEAD_DIM, q2, jnp.zeros_like(q2))], axis=0) for q2 in q2s]
    m, denom, acc = ([None] * len(q2s) for _ in range(3))
    for gi in range(len(keys[0])):
        for p in range(len(q2s)):
            s = _dot_nt(qs[p], keys[p][gi])
            if biases[p][0][gi] is not None:
                s = s + jnp.concatenate([biases[p][0][gi], biases[p][1][gi]], axis=0)
            m_grp = s.max(axis=-1, keepdims=True)
            m_new = m_grp if m[p] is None else jnp.maximum(m[p], m_grp)
            e = jnp.exp2(s - m_new)
            e_part = e[:, :LANES]
            for c0 in range(LANES, e.shape[1], LANES):
                e_part = e_part + e[:, c0:c0 + LANES]
            pv = _bdot(e.astype(BF16), values[p][gi])
            if m[p] is None:
                denom[p], acc[p] = e_part, pv
            else:
                alpha = jnp.exp2(m[p] - m_new)
                denom[p] = alpha * denom[p] + e_part
                acc[p] = alpha * acc[p] + pv
            m[p] = m_new
    outs = []
    for p in range(len(q2s)):
        out = acc[p] / denom[p].sum(axis=-1, keepdims=True)
        outs.append(jnp.where(lane < HEAD_DIM, out[:nq], out[nq:]))
    return outs


def _na_window_base(row_block, rows):
    return jnp.clip(row_block * NA_QROWS - NA_KH // 2, 0, rows - NA_KROWS)


def _na_body(rows, q_ref, k_ref, v_ref, kc_ref, vc_ref, bias_ref, o_ref):
    base = _na_window_base(pl.program_id(1), rows)
    start = pl.multiple_of(base * GRID_W, GRID_W)
    n_keys = NA_KROWS * GRID_W
    n_pairs = q_ref.shape[2] // LANES
    groups = [(c0, min(NA_KEY_GROUP, n_keys - c0)) for c0 in range(0, n_keys, NA_KEY_GROUP)]
    lanes = [slice(p * LANES, (p + 1) * LANES) for p in range(n_pairs)]
    keys = [[kc_ref[0, :, sl]] + [k_ref[0, pl.ds(start + c0, n), sl] for c0, n in groups] for sl in lanes]
    values = [[vc_ref[0, :, sl]] + [v_ref[0, pl.ds(start + c0, n), sl] for c0, n in groups] for sl in lanes]
    biases = [[[None] + [bias_ref[0, 2 * p + hh, 0, :, c0:c0 + n] for c0, n in groups] for hh in range(2)]
              for p in range(n_pairs)]
    outs = _head_pairs_attention([q_ref[0, :, sl] for sl in lanes], keys, values, biases)
    for sl, o in zip(lanes, outs):
        o_ref[0, :, sl] = o.astype(o_ref.dtype)


def _na_bias_body(rows, p_ref, o_ref):
    kh, kw, w = NA_KH, NA_KW, GRID_W
    shape = (w, 2 * w)
    qc = lax.broadcasted_iota(jnp.int32, shape, 0)
    lane = lax.broadcasted_iota(jnp.int32, shape, 1)
    kc = lane & (w - 1)
    col_start = jnp.clip(qc - kw // 2, 0, w - kw)
    col_valid = (kc >= col_start) & (kc < col_start + kw)
    low_half = lane < w
    masked = jnp.full(shape, MASK_VALUE, F32)
    tiles = []
    for a in range(2 * kh - 1):
        row = jnp.broadcast_to(p_ref[0, 0, a:a + 1, :], shape)
        t = pltpu.roll(row, w + 1, 1, stride=1, stride_axis=0)
        t = jnp.where(low_half, t, pltpu.roll(t, w, 1))
        tiles.append(jnp.where(col_valid, t, masked))
    for case, r0 in enumerate((0, NA_QROWS, rows - NA_QROWS)):
        base = int(np.clip(r0 - kh // 2, 0, rows - NA_KROWS))
        for qr in range(NA_QROWS):
            q_abs = r0 + qr
            r_start = int(np.clip(q_abs - kh // 2, 0, rows - kh))

            def tile_for(kr):
                k_abs = base + kr
                if kr >= NA_KROWS or not (r_start <= k_abs < r_start + kh):
                    return masked
                return tiles[k_abs - q_abs + kh - 1]

            q_rows = slice(qr * w, (qr + 1) * w)
            for pair in range((NA_KROWS + 1) // 2):
                chunk = jnp.where(low_half, tile_for(2 * pair), tile_for(2 * pair + 1))
                width = min(2 * w, NA_KROWS * w - pair * 2 * w)
                o_ref[0, 0, case, q_rows, pair * 2 * w:pair * 2 * w + width] = chunk[:, :width]


def _na_bias_tables(rpb, rows):
    w = GRID_W
    depth, nh, na, nb = rpb.shape
    assert 2 * w == LANES and nb == 2 * NA_KW - 1
    lo = w - NA_KW
    padded = jnp.pad(rpb.astype(F32) * LOG2E, ((0, 0), (0, 0), (0, (-na) % 8), (lo, 2 * w - nb - lo)))
    return pl.pallas_call(
        functools.partial(_na_bias_body, rows),
        grid=(depth, nh),
        in_specs=[pl.BlockSpec((1, 1) + padded.shape[2:], lambda l, h: (l, h, 0, 0))],
        out_specs=pl.BlockSpec((1, 1, 3, NA_QROWS * w, NA_KROWS * w), lambda l, h: (l, h, 0, 0, 0)),
        out_shape=jax.ShapeDtypeStruct((depth, nh, 3, NA_QROWS * w, NA_KROWS * w), F32),
        compiler_params=_params("parallel", "parallel"),
        name="na_bias_tables",
    )(padded)


def _neighbourhood_attention(q, kv, kv_ctx, bias, layer):
    b, seq, _ = q.shape
    rows = seq // GRID_W
    lc = kv_ctx.shape[1]
    assert rows % NA_QROWS == 0 and rows >= NA_KROWS + 1 and NA_QROWS == NA_KH // 2
    n_blocks = rows // NA_QROWS
    nh = bias.shape[1]
    tq = NA_QROWS * GRID_W

    def bias_map(i, r):
        return (layer, 0, jnp.where(r == 0, 0, jnp.where(r == n_blocks - 1, 2, 1)), 0, 0)

    return pl.pallas_call(
        functools.partial(_na_body, rows),
        grid=(b, n_blocks),
        in_specs=[pl.BlockSpec((1, tq, D_NA), lambda i, r: (i, r, 0)),
                  pl.BlockSpec((1, seq, D_NA), lambda i, r: (i, 0, 0)),
                  pl.BlockSpec((1, seq, D_NA), lambda i, r: (i, 0, 1)),
                  pl.BlockSpec((1, lc, D_NA), lambda i, r: (i, 0, 0)),
                  pl.BlockSpec((1, lc, D_NA), lambda i, r: (i, 0, 1)),
                  pl.BlockSpec((1, nh, 1, tq, NA_KROWS * GRID_W), bias_map)],
        out_specs=pl.BlockSpec((1, tq, D_NA), lambda i, r: (i, r, 0)),
        out_shape=jax.ShapeDtypeStruct((b, seq, D_NA), BF16),
        compiler_params=_params("parallel", "arbitrary"),
        name="neighbourhood_attention",
    )(q, kv, kv, kv_ctx, kv_ctx, bias)


def _ctx_attn_body(q_ref, k_ref, v_ref, o_ref):
    lanes = [slice(p * LANES, (p + 1) * LANES) for p in range(q_ref.shape[2] // LANES)]
    outs = _head_pairs_attention([q_ref[0, :, sl] for sl in lanes], [[k_ref[0, :, sl]] for sl in lanes],
                                 [[v_ref[0, :, sl]] for sl in lanes], [[[None], [None]] for _ in lanes])
    for sl, o in zip(lanes, outs):
        o_ref[0, :, sl] = o.astype(o_ref.dtype)


def _context_attention(q, kv):
    b, lc, _ = q.shape
    return pl.pallas_call(
        _ctx_attn_body,
        grid=(b,),
        in_specs=[pl.BlockSpec((1, lc, D_NA), lambda i: (i, 0, 0)),
                  pl.BlockSpec((1, lc, D_NA), lambda i: (i, 0, 0)),
                  pl.BlockSpec((1, lc, D_NA), lambda i: (i, 0, 1))],
        out_specs=pl.BlockSpec((1, lc, D_NA), lambda i: (i, 0, 0)),
        out_shape=jax.ShapeDtypeStruct((b, lc, D_NA), BF16),
        compiler_params=_params("parallel"),
        name="context_attention",
    )(q, kv, kv)


def kernel(x, c, ctx, c_ctx, w_mod, b_mod, norm1_g, norm2_g, w_in, w_out, hy_conv_w, hy_conv_b,
           filt_w1, filt_b1, filt_freq, filt_w2, filt_b2, filt_w3, hy_skip, na_rpb,
           w_gate, w_up, w_down, final_g):
    depth = w_mod.shape[0]
    b, seq, d = x.shape
    off_k = 3 * D_HYENA + D_FNET + D_NA
    qk_scale = LOG2E / math.sqrt(HEAD_DIM)
    splits = ((3 * D_HYENA, F32, None), (D_FNET, BF16, None), (D_NA, BF16, qk_scale), (2 * D_NA, BF16, None))
    tm_proj, tm = 1024, 1024

    pad = (-(b + 1)) % 8
    c_all = jnp.concatenate([c, c_ctx[None], jnp.zeros((pad, d), F32)], axis=0)
    mod = _modulation(c_all, w_mod, b_mod)
    lat_row = lambda i: i
    ctx_row = lambda i: b

    w_in_b, w_out_b = w_in.astype(BF16), w_out.astype(BF16)
    w_gate_b, w_up_b, w_down_b = w_gate.astype(BF16), w_up.astype(BF16), w_down.astype(BF16)
    w_kv_last = w_in_b[depth - 1:, :, off_k:]
    n1g, n2g = norm1_g.reshape(depth, 1, d), norm2_g.reshape(depth, 1, d)
    fg = final_g.reshape(1, d)
    na_bias = _na_bias_tables(na_rpb, seq // GRID_W)

    for l in range(depth):
        last = l == depth - 1
        hy_params = (filt_w1[l], filt_b1[l], filt_freq[l], filt_w2[l], filt_b2[l], filt_w3[l], hy_skip[l])
        conv = (hy_conv_w[l], hy_conv_b[l].reshape(1, 3 * D_HYENA))

        gates, v, fn, q, kv = _norm_proj(x, n1g, mod, lat_row, l, w_in_b, l, splits, tm_proj, conv)
        if last:
            (kv_c,) = _norm_proj(ctx.reshape(1, -1, d), n1g, mod, ctx_row, l, w_kv_last, 0,
                                 ((2 * D_NA, BF16, None),), tm_proj)
            kv_c = kv_c.reshape(b, -1, 2 * D_NA)
        else:
            cgates, cv, cfn, cq, kv_c = _norm_proj(ctx, n1g, mod, ctx_row, l, w_in_b, l, splits, tm_proj, conv)

        y_hy = _hyena_mixer(gates, v, *hy_params)
        y_fn = _fourier_mix(fn)
        y_na = _neighbourhood_attention(q, kv, kv_c, na_bias, l)

        x = _mix_ffn(x, y_hy, y_fn, y_na, w_out_b, n2g, mod, lat_row, l, w_gate_b, w_up_b, w_down_b,
                     fg, last, tm)

        if not last:
            cy_hy = _hyena_mixer(cgates, cv, *hy_params)
            cy_fn = _fourier_mix(cfn)
            cy_na = _context_attention(cq, kv_c)
            flat = lambda a: a.reshape(1, -1, a.shape[-1])
            ctx = _mix_ffn(flat(ctx), flat(cy_hy), flat(cy_fn), flat(cy_na), w_out_b, n2g, mod, ctx_row, l,
                           w_gate_b, w_up_b, w_down_b, fg, False, tm).reshape(ctx.shape)
    return x
```
